```python
import math
import jax, jax.numpy as jnp
from jax import lax
import numpy as np

D_MODEL = 2048
BATCH = 1
SEQ = 8192
DEPTH = 1

D_MIX = D_MODEL
HEAD_DIM = 64
D_ATTN = D_MIX // 2
D_CONV = D_MIX - D_ATTN
N_Q_HEADS = D_ATTN // HEAD_DIM
N_KV_HEADS = 4
GQA_GROUP = N_Q_HEADS // N_KV_HEADS
D_KV = N_KV_HEADS * HEAD_DIM
WINDOW = 128
BLOCK = 128
CONV_WIDTH = 31
CONV_GROUPS = D_CONV // HEAD_DIM
N_BUCKETS = 32
MAX_DISTANCE = 128
LN_EPS = 1e-5
ALPHA = (2.0 * DEPTH) ** 0.25
BETA = (8.0 * DEPTH) ** -0.25

SPLIT_SIZES = (D_ATTN, D_KV, D_KV, D_ATTN, D_CONV, D_CONV, D_CONV)
D_IN = sum(SPLIT_SIZES)
SPLIT_POINTS = [int(s) for s in np.cumsum(SPLIT_SIZES)[:-1]]

kernel_name = "hybrid_conformer_swa_sink_deepnorm_adaln"


def layer_norm(x, g, b):
    xf = x.astype(jnp.float32)
    mu = jnp.mean(xf, axis=-1, keepdims=True)
    var = jnp.mean(jnp.square(xf - mu), axis=-1, keepdims=True)
    y = (xf - mu) * lax.rsqrt(var + LN_EPS)
    return (y * g.astype(jnp.float32) + b.astype(jnp.float32)).astype(x.dtype)


def t5_bucket(dist):
    max_exact = N_BUCKETS // 2
    d = jnp.maximum(dist, 1).astype(jnp.float32)
    large = max_exact + (jnp.log(d / max_exact) / math.log(MAX_DISTANCE / max_exact)
                         * (N_BUCKETS - max_exact)).astype(jnp.int32)
    large = jnp.minimum(large, N_BUCKETS - 1)
    return jnp.where(dist < max_exact, dist, large)


def banded_sink_attention(q, k, v, rel_bias, sinks):
    B, S = q.shape[0], q.shape[1]
    nb = S // BLOCK
    f32 = jnp.float32
    qb = q.astype(f32).reshape(B, nb, BLOCK, N_KV_HEADS, GQA_GROUP, HEAD_DIM)
    kb = k.astype(f32).reshape(B, nb, BLOCK, N_KV_HEADS, HEAD_DIM)
    vb = v.astype(f32).reshape(B, nb, BLOCK, N_KV_HEADS, HEAD_DIM)
    prev = lambda t: jnp.concatenate([jnp.zeros_like(t[:, :1]), t[:, :-1]], axis=1)
    kw = jnp.concatenate([prev(kb), kb], axis=2)
    vw = jnp.concatenate([prev(vb), vb], axis=2)
    scores = jnp.einsum('bnqhgd,bnkhd->bnhgqk', qb, kw) * (HEAD_DIM ** -0.5)

    qi = jnp.arange(BLOCK, dtype=jnp.int32)[:, None]
    kj = jnp.arange(2 * BLOCK, dtype=jnp.int32)[None, :]
    dist = qi + BLOCK - kj
    in_window = (dist >= 0) & (dist < WINDOW)
    bias = rel_bias.astype(f32)[t5_bucket(jnp.maximum(dist, 0))]
    bias = bias.transpose(2, 0, 1).reshape(N_KV_HEADS, GQA_GROUP, BLOCK, 2 * BLOCK)
    key_pos = jnp.arange(nb, dtype=jnp.int32)[:, None] * BLOCK - BLOCK + kj
    valid = in_window[None] & (key_pos[:, None, :] >= 0)

    scores = jnp.where(valid[None, :, None, None], scores + bias, jnp.finfo(f32).min)
    sink = jnp.broadcast_to(sinks.astype(f32).reshape(1, 1, N_KV_HEADS, GQA_GROUP, 1, 1),
                            scores.shape[:-1] + (1,))
    probs = jax.nn.softmax(jnp.concatenate([scores, sink], axis=-1), axis=-1)[..., :-1]
    out = jnp.einsum('bnhgqk,bnkhd->bnqhgd', probs, vw)
    return out.reshape(B, S, N_Q_HEADS * HEAD_DIM).astype(q.dtype)


def conformer_conv(glu_a, glu_b, conv_w, conv_b, ln_g, ln_b, w_pw, b_pw):
    u = glu_a * jax.nn.sigmoid(glu_b)
    u = lax.conv_general_dilated(
        u, conv_w.reshape(CONV_WIDTH, 1, D_CONV).astype(u.dtype),
        window_strides=(1,), padding=[(CONV_WIDTH - 1, 0)],
        dimension_numbers=('NWC', 'WIO', 'NWC'), feature_group_count=D_CONV) + conv_b
    u = jax.nn.silu(layer_norm(u, ln_g, ln_b))
    return u @ w_pw + b_pw


def setup_inputs(seed: int = 0) -> dict:
    key = jax.random.key(seed)
    ks = jax.random.split(key, 20)
    n = jax.random.normal
    f32 = jnp.float32
    x = n(ks[0], (BATCH, SEQ, D_MODEL), f32)
    c = n(ks[1], (BATCH, D_MODEL), f32)
    w_ada = 0.5 * D_MODEL ** -0.5 * n(ks[2], (DEPTH, D_MODEL, 3 * D_MODEL), f32)
    b_ada = 0.01 * n(ks[3], (DEPTH, 3 * D_MODEL), f32)
    col_scale = jnp.ones((D_IN,), f32).at[D_ATTN + D_KV:D_ATTN + 2 * D_KV].set(BETA)
    w_in = D_MODEL ** -0.5 * n(ks[4], (DEPTH, D_MODEL, D_IN), f32) * col_scale
    rel_bias = 0.5 * n(ks[5], (N_BUCKETS, N_Q_HEADS), f32)
    sinks = n(ks[6], (DEPTH, N_Q_HEADS), f32)
    conv_w = CONV_WIDTH ** -0.5 * n(ks[7], (DEPTH, CONV_WIDTH, D_CONV), f32)
    conv_b = 0.01 * n(ks[8], (DEPTH, D_CONV), f32)
    conv_ln_g = 1.0 + 0.01 * n(ks[9], (DEPTH, D_CONV), f32)
    conv_ln_b = 0.01 * n(ks[10], (DEPTH, D_CONV), f32)
    w_pw = BETA * D_CONV ** -0.5 * n(ks[11], (DEPTH, D_CONV, D_CONV), f32)
    b_pw = 0.01 * n(ks[12], (DEPTH, D_CONV), f32)
    w_out = BETA * D_MIX ** -0.5 * n(ks[13], (DEPTH, D_MIX, D_MODEL), f32)
    ln_g = 1.0 + 0.01 * n(ks[14], (DEPTH, D_MODEL), f32)
    ln_b = 0.01 * n(ks[15], (DEPTH, D_MODEL), f32)
    return {"x": x, "c": c, "w_ada": w_ada, "b_ada": b_ada, "w_in": w_in,
            "rel_bias": rel_bias, "sinks": sinks, "conv_w": conv_w, "conv_b": conv_b,
            "conv_ln_g": conv_ln_g, "conv_ln_b": conv_ln_b, "w_pw": w_pw, "b_pw": b_pw,
            "w_out": w_out, "ln_g": ln_g, "ln_b": ln_b}


def reference(x, c, w_ada, b_ada, w_in, rel_bias, sinks, conv_w, conv_b, conv_ln_g,
              conv_ln_b, w_pw, b_pw, w_out, ln_g, ln_b):
    c_act = jax.nn.silu(c)
    for l in range(DEPTH):
        mod = (c_act @ w_ada[l] + b_ada[l])[:, None, :]
        shift, scale, gate = jnp.split(mod, 3, axis=-1)
        h = x * (1.0 + scale) + shift
        proj = h @ w_in[l]
        q, k, v, g_attn, glu_a, glu_b, g_conv = jnp.split(proj, SPLIT_POINTS, axis=-1)
        y_attn = banded_sink_attention(q, k, v, rel_bias, sinks[l]) * jax.nn.silu(g_attn)
        y_conv = conformer_conv(glu_a, glu_b, conv_w[l], conv_b[l], conv_ln_g[l], conv_ln_b[l],
                                w_pw[l], b_pw[l]) * jax.nn.silu(g_conv)
        y = jnp.concatenate([y_attn, y_conv], axis=-1) @ w_out[l]
        x = layer_norm(ALPHA * x + gate * y, ln_g[l], ln_b[l])
    return x
```

```python
import functools
import math

import jax
import jax.numpy as jnp
import numpy as np
from jax.experimental import pallas as pl
from jax.experimental.pallas import tpu as pltpu

F32 = jnp.float32
BF16 = jnp.bfloat16

D_MODEL = 2048
SEQ = 8192
HEAD_DIM = 64
D_ATTN = 1024
D_CONV = 1024
N_Q_HEADS = 16
N_KV_HEADS = 4
GQA_GROUP = 4
D_KV = 256
WINDOW = 128
BLOCK = 128
CONV_WIDTH = 31
N_BUCKETS = 32
MAX_DISTANCE = 128
LN_EPS = 1e-5
DEPTH = 1
ALPHA = (2.0 * DEPTH) ** 0.25
D_IN = 2 * D_ATTN + 2 * D_KV + 3 * D_CONV

LANES = 128
SUBLANES = 8

COL_Q = 0
COL_G_ATTN = 1024
COL_GLU_A = 2048
COL_GLU_B = 3072
COL_G_CONV = 4096
COL_K = 5120
COL_V = 5376

VMEM_LIMIT = 56 * 1024 * 1024


def _silu(v):
    return v * jax.nn.sigmoid(v)


def _t5_bucket_table():
    qi = np.arange(BLOCK)[:, None]
    kj = np.arange(2 * BLOCK)[None, :]
    dist = np.maximum(qi + BLOCK - kj, 0)
    max_exact = N_BUCKETS // 2
    d = np.maximum(dist, 1).astype(np.float64)
    large = max_exact + (np.log(d / max_exact) / math.log(MAX_DISTANCE / max_exact)
                         * (N_BUCKETS - max_exact)).astype(np.int32)
    large = np.minimum(large, N_BUCKETS - 1)
    return np.where(dist < max_exact, dist, large).astype(np.int32)


def _mod_kernel(c_ref, w_ref, b_ref, o_ref):
    c = c_ref[...]
    ca = _silu(c)
    o_ref[...] = jnp.sum(ca * w_ref[...], axis=0, keepdims=True) + b_ref[...]


def _mod_call(c_col, w_ada, b_ada):
    tn = 1024
    n = w_ada.shape[1]
    return pl.pallas_call(
        _mod_kernel,
        grid=(n // tn,),
        in_specs=[
            pl.BlockSpec((D_MODEL, 1), lambda j: (0, 0)),
            pl.BlockSpec((D_MODEL, tn), lambda j: (0, j)),
            pl.BlockSpec((1, tn), lambda j: (0, j)),
        ],
        out_specs=pl.BlockSpec((1, tn), lambda j: (0, j)),
        out_shape=jax.ShapeDtypeStruct((1, n), F32),
        compiler_params=pltpu.CompilerParams(
            dimension_semantics=("arbitrary",), vmem_limit_bytes=VMEM_LIMIT),
        name="mod",
    )(c_col, w_ada, b_ada)


def _bias_kernel(rb_ref, bucket_ref, o_ref):
    bucket = bucket_ref[...]
    for h in range(N_Q_HEADS):
        acc = jnp.zeros((BLOCK, 2 * BLOCK), F32)
        for b in range(N_BUCKETS):
            acc = jnp.where(bucket == b, rb_ref[b, h], acc)
        o_ref[h] = acc


def _bias_call(rel_bias, bucket):
    return pl.pallas_call(
        _bias_kernel,
        in_specs=[
            pl.BlockSpec(memory_space=pltpu.SMEM),
            pl.BlockSpec((BLOCK, 2 * BLOCK), lambda: (0, 0)),
        ],
        out_specs=pl.BlockSpec((N_Q_HEADS, BLOCK, 2 * BLOCK), lambda: (0, 0, 0)),
        out_shape=jax.ShapeDtypeStruct((N_Q_HEADS, BLOCK, 2 * BLOCK), F32),
        name="bias",
    )(rel_bias, bucket)


def _inproj_kernel(x_ref, shift_ref, scale_ref, w_ref, o_ref, h_ref):
    @pl.when(pl.program_id(1) == 0)
    def _():
        h = x_ref[...] * (1.0 + scale_ref[...]) + shift_ref[...]
        h_ref[...] = h.astype(BF16)

    o_ref[...] = jnp.dot(h_ref[...], w_ref[...],
                         preferred_element_type=F32).astype(BF16)


def _inproj_call(x2d, mod, w_in_bf16):
    tm, tn = 1024, 512
    return pl.pallas_call(
        _inproj_kernel,
        grid=(SEQ // tm, D_IN // tn),
        in_specs=[
            pl.BlockSpec((tm, D_MODEL), lambda i, j: (i, 0)),
            pl.BlockSpec((1, D_MODEL), lambda i, j: (0, 0)),
            pl.BlockSpec((1, D_MODEL), lambda i, j: (0, 1)),
            pl.BlockSpec((D_MODEL, tn), lambda i, j: (0, j)),
        ],
        out_specs=pl.BlockSpec((tm, tn), lambda i, j: (i, j)),
        out_shape=jax.ShapeDtypeStruct((SEQ, D_IN), BF16),
        scratch_shapes=[pltpu.VMEM((tm, D_MODEL), BF16)],
        compiler_params=pltpu.CompilerParams(
            dimension_semantics=("arbitrary", "arbitrary"),
            vmem_limit_bytes=VMEM_LIMIT),
        name="inproj",
    )(x2d, mod, mod, w_in_bf16)


def _swap_halves(t):
    return jnp.concatenate([t[:, HEAD_DIM:], t[:, :HEAD_DIM]], axis=1)


def _attn_kernel(sinks_ref, q_ref, g_ref, kc_ref, vc_ref, kp_ref, vp_ref,
                 bias_ref, o_ref):
    i = pl.program_id(0)
    row = jax.lax.broadcasted_iota(jnp.int32, (BLOCK, 2 * BLOCK), 0)
    col = jax.lax.broadcasted_iota(jnp.int32, (BLOCK, 2 * BLOCK), 1)
    first_key = jnp.where(i > 0, 0, BLOCK)
    valid = (col > row) & (col <= row + WINDOW) & (col >= first_key)
    lane = jax.lax.broadcasted_iota(jnp.int32, (BLOCK, LANES), 1)
    low_half = lane < HEAD_DIM
    neg = jnp.finfo(F32).min

    k_tiles, v_tiles = [], []
    for t in range(D_KV // LANES):
        sl = slice(t * LANES, (t + 1) * LANES)
        kt = jnp.concatenate([kp_ref[:, sl], kc_ref[:, sl]], axis=0)
        vt = jnp.concatenate([vp_ref[:, sl], vc_ref[:, sl]], axis=0)
        k_tiles.append((kt, _swap_halves(kt)))
        v_tiles.append((vt, _swap_halves(vt)))

    for tile in range(D_ATTN // LANES):
        sl = slice(tile * LANES, (tile + 1) * LANES)
        q_tile = q_ref[:, sl] * (HEAD_DIM ** -0.5)
        outs = []
        for p in range(2):
            head = 2 * tile + p
            kv = head // GQA_GROUP
            same = (kv % 2) == p
            kt = k_tiles[kv // 2][0 if same else 1]
            vt = v_tiles[kv // 2][0 if same else 1]
            in_half = low_half if p == 0 else jnp.logical_not(low_half)
            qm = jnp.where(in_half, q_tile, jnp.zeros_like(q_tile))
            s = jax.lax.dot_general(qm, kt, (((1,), (1,)), ((), ())),
                                    preferred_element_type=F32)
            s = jnp.where(valid, s + bias_ref[head], neg)
            sink = sinks_ref[head]
            m = jnp.maximum(jnp.max(s, axis=1, keepdims=True), sink)
            e = jnp.exp(s - m)
            denom = jnp.sum(e, axis=1, keepdims=True) + jnp.exp(sink - m)
            o = jnp.dot(e.astype(BF16), vt, preferred_element_type=F32)
            outs.append(o / denom)
        y = jnp.where(low_half, outs[0], outs[1])
        gate = _silu(g_ref[:, sl].astype(F32))
        o_ref[:, sl] = (y * gate).astype(BF16)


def _attn_call(proj, bias, sinks):
    nb = SEQ // BLOCK
    prev = lambda i: jnp.maximum(i - 1, 0)
    return pl.pallas_call(
        _attn_kernel,
        grid=(nb,),
        in_specs=[
            pl.BlockSpec(memory_space=pltpu.SMEM),
            pl.BlockSpec((BLOCK, D_ATTN), lambda i: (i, COL_Q // D_ATTN)),
            pl.BlockSpec((BLOCK, D_ATTN), lambda i: (i, COL_G_ATTN // D_ATTN)),
            pl.BlockSpec((BLOCK, D_KV), lambda i: (i, COL_K // D_KV)),
            pl.BlockSpec((BLOCK, D_KV), lambda i: (i, COL_V // D_KV)),
            pl.BlockSpec((BLOCK, D_KV), lambda i: (prev(i), COL_K // D_KV)),
            pl.BlockSpec((BLOCK, D_KV), lambda i: (prev(i), COL_V // D_KV)),
            pl.BlockSpec((N_Q_HEADS, BLOCK, 2 * BLOCK), lambda i: (0, 0, 0)),
        ],
        out_specs=pl.BlockSpec((BLOCK, D_ATTN), lambda i: (i, 0)),
        out_shape=jax.ShapeDtypeStruct((SEQ, D_ATTN), BF16),
        compiler_params=pltpu.CompilerParams(
            dimension_semantics=("arbitrary",), vmem_limit_bytes=VMEM_LIMIT),
        name="attn",
    )(sinks, proj, proj, proj, proj, proj, proj, bias)


CONV_TM = 256
CONV_HALO = 32
CONV_RC = 32


def _conv_kernel(a_ref, b_ref, g_ref, cw_ref, cb_ref, lng_ref, lnb_ref,
                 wpw_ref, bpw_ref, o_ref, u_ref, acc_ref):
    i = pl.program_id(0)
    tm = CONV_TM

    @pl.when(i == 0)
    def _():
        u_ref[0:CONV_HALO, :] = jnp.zeros((CONV_HALO, D_CONV), F32)

    @pl.when(i > 0)
    def _():
        u_ref[0:CONV_HALO, :] = u_ref[tm:tm + CONV_HALO, :]

    a = a_ref[...].astype(F32)
    b = b_ref[...].astype(F32)
    u_ref[CONV_HALO:CONV_HALO + tm, :] = a * jax.nn.sigmoid(b)

    off = CONV_HALO - (CONV_WIDTH - 1)
    for c in range(D_CONV // LANES):
        cs = slice(c * LANES, (c + 1) * LANES)
        w_rows = [cw_ref[k:k + 1, cs] for k in range(CONV_WIDTH)]
        bias_row = cb_ref[:, cs]
        for r in range(tm // CONV_RC):
            r0 = r * CONV_RC
            acc = jnp.broadcast_to(bias_row, (CONV_RC, LANES))
            for k in range(CONV_WIDTH):
                acc = acc + w_rows[k] * u_ref[r0 + off + k:r0 + off + k + CONV_RC, cs]
            acc_ref[r0:r0 + CONV_RC, cs] = acc

    v = acc_ref[...]
    mu = jnp.mean(v, axis=1, keepdims=True)
    d = v - mu
    var = jnp.mean(d * d, axis=1, keepdims=True)
    y = d * jax.lax.rsqrt(var + LN_EPS) * lng_ref[...] + lnb_ref[...]
    s = _silu(y).astype(BF16)
    z = jnp.dot(s, wpw_ref[...], preferred_element_type=F32) + bpw_ref[...]
    gate = _silu(g_ref[...].astype(F32))
    o_ref[...] = (z * gate).astype(BF16)


def _conv_call(proj, conv_w, conv_b, ln_g, ln_b, w_pw_bf16, b_pw):
    tm = CONV_TM
    row = lambda i: (0, 0)
    return pl.pallas_call(
        _conv_kernel,
        grid=(SEQ // tm,),
        in_specs=[
            pl.BlockSpec((tm, D_CONV), lambda i: (i, COL_GLU_A // D_CONV)),
            pl.BlockSpec((tm, D_CONV), lambda i: (i, COL_GLU_B // D_CONV)),
            pl.BlockSpec((tm, D_CONV), lambda i: (i, COL_G_CONV // D_CONV)),
            pl.BlockSpec((CONV_WIDTH, D_CONV), row),
            pl.BlockSpec((1, D_CONV), row),
            pl.BlockSpec((1, D_CONV), row),
            pl.BlockSpec((1, D_CONV), row),
            pl.BlockSpec((D_CONV, D_CONV), row),
            pl.BlockSpec((1, D_CONV), row),
        ],
        out_specs=pl.BlockSpec((tm, D_CONV), lambda i: (i, 0)),
        out_shape=jax.ShapeDtypeStruct((SEQ, D_CONV), BF16),
        scratch_shapes=[
            pltpu.VMEM((CONV_HALO + tm, D_CONV), F32),
            pltpu.VMEM((tm, D_CONV), F32),
        ],
        compiler_params=pltpu.CompilerParams(
            dimension_semantics=("arbitrary",), vmem_limit_bytes=VMEM_LIMIT),
        name="conv",
    )(proj, proj, proj, conv_w, conv_b, ln_g, ln_b, w_pw_bf16, b_pw)


def _outproj_kernel(ya_ref, yc_ref, x_ref, gate_ref, wa_ref, wc_ref,
                    lng_ref, lnb_ref, o_ref):
    y = jnp.dot(ya_ref[...], wa_ref[...], preferred_element_type=F32)
    y = y + jnp.dot(yc_ref[...], wc_ref[...], preferred_element_type=F32)
    z = ALPHA * x_ref[...] + gate_ref[...] * y
    mu = jnp.mean(z, axis=1, keepdims=True)
    d = z - mu
    var = jnp.mean(d * d, axis=1, keepdims=True)
    o_ref[...] = d * jax.lax.rsqrt(var + LN_EPS) * lng_ref[...] + lnb_ref[...]


def _outproj_call(y_attn, y_conv, x2d, mod, w_out_bf16, ln_g, ln_b):
    tm = 256
    row = lambda i: (0, 0)
    return pl.pallas_call(
        _outproj_kernel,
        grid=(SEQ // tm,),
        in_specs=[
            pl.BlockSpec((tm, D_ATTN), lambda i: (i, 0)),
            pl.BlockSpec((tm, D_CONV), lambda i: (i, 0)),
            pl.BlockSpec((tm, D_MODEL), lambda i: (i, 0)),
            pl.BlockSpec((1, D_MODEL), lambda i: (0, 2)),
            pl.BlockSpec((D_ATTN, D_MODEL), lambda i: (0, 0)),
            pl.BlockSpec((D_CONV, D_MODEL), lambda i: (1, 0)),
            pl.BlockSpec((1, D_MODEL), row),
            pl.BlockSpec((1, D_MODEL), row),
        ],
        out_specs=pl.BlockSpec((tm, D_MODEL), lambda i: (i, 0)),
        out_shape=jax.ShapeDtypeStruct((SEQ, D_MODEL), F32),
        compiler_params=pltpu.CompilerParams(
            dimension_semantics=("arbitrary",), vmem_limit_bytes=VMEM_LIMIT),
        name="outproj",
    )(y_attn, y_conv, x2d, mod, w_out_bf16, w_out_bf16, ln_g, ln_b)


def kernel(x, c, w_ada, b_ada, w_in, rel_bias, sinks, conv_w, conv_b, conv_ln_g,
           conv_ln_b, w_pw, b_pw, w_out, ln_g, ln_b):
    assert x.shape == (1, SEQ, D_MODEL) and w_in.shape == (DEPTH, D_MODEL, D_IN)
    x2d = x.reshape(SEQ, D_MODEL)

    w = w_in[0]
    q0, k0, v0, ga0 = 0, D_ATTN, D_ATTN + D_KV, D_ATTN + 2 * D_KV
    a0 = ga0 + D_ATTN
    w_perm = jnp.concatenate(
        [w[:, q0:k0], w[:, ga0:], w[:, k0:v0], w[:, v0:ga0]], axis=1).astype(BF16)
    assert (COL_Q, COL_G_ATTN, COL_GLU_A, COL_K, COL_V) == (
        0, D_ATTN, D_ATTN + a0 - ga0, D_IN - 2 * D_KV, D_IN - D_KV)

    mod = _mod_call(c.reshape(D_MODEL, 1), w_ada[0], b_ada[0].reshape(1, -1))
    bias = _bias_call(rel_bias, jnp.asarray(_t5_bucket_table()))
    proj = _inproj_call(x2d, mod, w_perm)
    y_attn = _attn_call(proj, bias, sinks[0])
    y_conv = _conv_call(proj, conv_w[0], conv_b[0].reshape(1, -1),
                        conv_ln_g[0].reshape(1, -1), conv_ln_b[0].reshape(1, -1),
                        w_pw[0].astype(BF16), b_pw[0].reshape(1, -1))
    out = _outproj_call(y_attn, y_conv, x2d, mod, w_out[0].astype(BF16),
                        ln_g[0].reshape(1, -1), ln_b[0].reshape(1, -1))
    return out.reshape(1, SEQ, D_MODEL)
```

```python
import functools
import math

import jax
import jax.numpy as jnp
import numpy as np
from jax.experimental import pallas as pl
from jax.experimental.pallas import tpu as pltpu

F32 = jnp.float32
BF16 = jnp.bfloat16

D_MODEL = 2048
SEQ = 8192
HEAD_DIM = 64
D_ATTN = 1024
D_CONV = 1024
N_Q_HEADS = 16
N_KV_HEADS = 4
GQA_GROUP = 4
D_KV = 256
WINDOW = 128
BLOCK = 128
CONV_WIDTH = 31
N_BUCKETS = 32
MAX_DISTANCE = 128
LN_EPS = 1e-5
DEPTH = 1
ALPHA = (2.0 * DEPTH) ** 0.25
D_IN = 2 * D_ATTN + 2 * D_KV + 3 * D_CONV

LANES = 128
SUBLANES = 8

COL_Q = 0
COL_K = COL_Q + D_ATTN
COL_V = COL_K + D_KV
COL_G_ATTN = COL_V + D_KV
COL_GLU_A = COL_G_ATTN + D_ATTN
COL_GLU_B = COL_GLU_A + D_CONV
COL_G_CONV = COL_GLU_B + D_CONV
HALF = 512

VMEM_LIMIT = 56 * 1024 * 1024


def _silu(v):
    return v * jax.nn.sigmoid(v)


def _t5_bucket_table():
    qi = np.arange(BLOCK)[:, None]
    kj = np.arange(2 * BLOCK)[None, :]
    dist = np.maximum(qi + BLOCK - kj, 0)
    max_exact = N_BUCKETS // 2
    d = np.maximum(dist, 1).astype(np.float64)
    large = max_exact + (np.log(d / max_exact) / math.log(MAX_DISTANCE / max_exact)
                         * (N_BUCKETS - max_exact)).astype(np.int32)
    large = np.minimum(large, N_BUCKETS - 1)
    return np.where(dist < max_exact, dist, large).astype(np.int32)


def _mod_kernel(c_ref, w_ref, b_ref, o_ref):
    c = c_ref[...]
    ca = _silu(c)
    o_ref[...] = jnp.sum(ca * w_ref[...], axis=0, keepdims=True) + b_ref[...]


def _mod_call(c_col, w_ada, b_ada):
    tn = 1024
    n = w_ada.shape[1]
    return pl.pallas_call(
        _mod_kernel,
        grid=(n // tn,),
        in_specs=[
            pl.BlockSpec((D_MODEL, 1), lambda j: (0, 0)),
            pl.BlockSpec((D_MODEL, tn), lambda j: (0, j)),
            pl.BlockSpec((1, tn), lambda j: (0, j)),
        ],
        out_specs=pl.BlockSpec((1, tn), lambda j: (0, j)),
        out_shape=jax.ShapeDtypeStruct((1, n), F32),
        compiler_params=pltpu.CompilerParams(
            dimension_semantics=("arbitrary",), vmem_limit_bytes=VMEM_LIMIT),
        name="mod",
    )(c_col, w_ada, b_ada)


def _bias_kernel(rb_ref, bucket_ref, o_ref):
    bucket = bucket_ref[...]
    for h in range(N_Q_HEADS):
        acc = jnp.zeros((BLOCK, 2 * BLOCK), F32)
        for b in range(N_BUCKETS):
            acc = jnp.where(bucket == b, rb_ref[b, h], acc)
        o_ref[h] = acc


def _bias_call(rel_bias, bucket):
    return pl.pallas_call(
        _bias_kernel,
        in_specs=[
            pl.BlockSpec(memory_space=pltpu.SMEM),
            pl.BlockSpec((BLOCK, 2 * BLOCK), lambda: (0, 0)),
        ],
        out_specs=pl.BlockSpec((N_Q_HEADS, BLOCK, 2 * BLOCK), lambda: (0, 0, 0)),
        out_shape=jax.ShapeDtypeStruct((N_Q_HEADS, BLOCK, 2 * BLOCK), F32),
        name="bias",
    )(rel_bias, bucket)


def _inproj_kernel(x_ref, shift_ref, scale_ref, w_ref, o_ref, h_ref):
    @pl.when(pl.program_id(1) == 0)
    def _():
        h = x_ref[...] * (1.0 + scale_ref[...]) + shift_ref[...]
        h_ref[...] = h.astype(BF16)

    o_ref[...] = jnp.dot(h_ref[...], w_ref[...],
                         preferred_element_type=F32).astype(BF16)


def _inproj_call(x2d, mod, w_in_bf16):
    tm, tn = 1024, 512
    return pl.pallas_call(
        _inproj_kernel,
        grid=(SEQ // tm, D_IN // tn),
        in_specs=[
            pl.BlockSpec((tm, D_MODEL), lambda i, j: (i, 0)),
            pl.BlockSpec((1, D_MODEL), lambda i, j: (0, 0)),
            pl.BlockSpec((1, D_MODEL), lambda i, j: (0, 1)),
            pl.BlockSpec((D_MODEL, tn), lambda i, j: (0, j)),
        ],
        out_specs=pl.BlockSpec((tm, tn), lambda i, j: (i, j)),
        out_shape=jax.ShapeDtypeStruct((SEQ, D_IN), BF16),
        scratch_shapes=[pltpu.VMEM((tm, D_MODEL), BF16)],
        compiler_params=pltpu.CompilerParams(
            dimension_semantics=("arbitrary", "arbitrary"),
            vmem_limit_bytes=VMEM_LIMIT),
        name="inproj",
    )(x2d, mod, mod, w_in_bf16)


def _swap_halves(t):
    return jnp.concatenate([t[:, HEAD_DIM:], t[:, :HEAD_DIM]], axis=1)


def _attn_kernel(sinks_ref, q_ref, g0_ref, g1_ref, kc_ref, vc_ref, kp_ref, vp_ref,
                 bias_ref, o_ref):
    i = pl.program_id(0)
    row = jax.lax.broadcasted_iota(jnp.int32, (BLOCK, 2 * BLOCK), 0)
    col = jax.lax.broadcasted_iota(jnp.int32, (BLOCK, 2 * BLOCK), 1)
    first_key = jnp.where(i > 0, 0, BLOCK)
    valid = (col > row) & (col <= row + WINDOW) & (col >= first_key)
    lane = jax.lax.broadcasted_iota(jnp.int32, (BLOCK, LANES), 1)
    low_half = lane < HEAD_DIM
    neg = jnp.finfo(F32).min

    k_tiles, v_tiles = [], []
    for t in range(D_KV // LANES):
        sl = slice(t * LANES, (t + 1) * LANES)
        kt = jnp.concatenate([kp_ref[:, sl], kc_ref[:, sl]], axis=0)
        vt = jnp.concatenate([vp_ref[:, sl], vc_ref[:, sl]], axis=0)
        k_tiles.append((kt, _swap_halves(kt)))
        v_tiles.append((vt, _swap_halves(vt)))

    for tile in range(D_ATTN // LANES):
        sl = slice(tile * LANES, (tile + 1) * LANES)
        q_tile = q_ref[:, sl] * (HEAD_DIM ** -0.5)
        outs = []
        for p in range(2):
            head = 2 * tile + p
            kv = head // GQA_GROUP
            same = (kv % 2) == p
            kt = k_tiles[kv // 2][0 if same else 1]
            vt = v_tiles[kv // 2][0 if same else 1]
            in_half = low_half if p == 0 else jnp.logical_not(low_half)
            qm = jnp.where(in_half, q_tile, jnp.zeros_like(q_tile))
            s = jax.lax.dot_general(qm, kt, (((1,), (1,)), ((), ())),
                                    preferred_element_type=F32)
            s = jnp.where(valid, s + bias_ref[head], neg)
            sink = sinks_ref[head]
            m = jnp.maximum(jnp.max(s, axis=1, keepdims=True), sink)
            e = jnp.exp(s - m)
            denom = jnp.sum(e, axis=1, keepdims=True) + jnp.exp(sink - m)
            o = jnp.dot(e.astype(BF16), vt, preferred_element_type=F32)
            outs.append(o / denom)
        y = jnp.where(low_half, outs[0], outs[1])
        g_ref = g0_ref if tile < HALF // LANES else g1_ref
        gl = (tile * LANES) % HALF
        gate = _silu(g_ref[:, gl:gl + LANES].astype(F32))
        o_ref[:, sl] = (y * gate).astype(BF16)


def _attn_call(proj, bias, sinks):
    nb = SEQ // BLOCK
    prev = lambda i: jnp.maximum(i - 1, 0)
    return pl.pallas_call(
        _attn_kernel,
        grid=(nb,),
        in_specs=[
            pl.BlockSpec(memory_space=pltpu.SMEM),
            pl.BlockSpec((BLOCK, D_ATTN), lambda i: (i, COL_Q // D_ATTN)),
            pl.BlockSpec((BLOCK, HALF), lambda i: (i, COL_G_ATTN // HALF)),
            pl.BlockSpec((BLOCK, HALF), lambda i: (i, COL_G_ATTN // HALF + 1)),
            pl.BlockSpec((BLOCK, D_KV), lambda i: (i, COL_K // D_KV)),
            pl.BlockSpec((BLOCK, D_KV), lambda i: (i, COL_V // D_KV)),
            pl.BlockSpec((BLOCK, D_KV), lambda i: (prev(i), COL_K // D_KV)),
            pl.BlockSpec((BLOCK, D_KV), lambda i: (prev(i), COL_V // D_KV)),
            pl.BlockSpec((N_Q_HEADS, BLOCK, 2 * BLOCK), lambda i: (0, 0, 0)),
        ],
        out_specs=pl.BlockSpec((BLOCK, D_ATTN), lambda i: (i, 0)),
        out_shape=jax.ShapeDtypeStruct((SEQ, D_ATTN), BF16),
        compiler_params=pltpu.CompilerParams(
            dimension_semantics=("arbitrary",), vmem_limit_bytes=VMEM_LIMIT),
        name="attn",
    )(sinks, proj, proj, proj, proj, proj, proj, proj, bias)


CONV_TM = 256
CONV_HALO = 32
CONV_RC = 64


def _conv_taps(u_ref, cw_ref, cb_ref, acc_ref, tm):
    off = CONV_HALO - (CONV_WIDTH - 1)
    for c in range(D_CONV // LANES):
        cs = slice(c * LANES, (c + 1) * LANES)
        w_rows = [cw_ref[k:k + 1, cs] for k in range(CONV_WIDTH)]
        for r0 in range(0, tm, CONV_RC):
            acc = jnp.broadcast_to(cb_ref[:, cs], (CONV_RC, LANES))
            for k in range(CONV_WIDTH):
                acc = acc + w_rows[k] * u_ref[c, r0 + off + k:r0 + off + k + CONV_RC, :]
            acc_ref[r0:r0 + CONV_RC, cs] = acc


def _conv_kernel(a0_ref, a1_ref, b0_ref, b1_ref, g0_ref, g1_ref, cw_ref, cb_ref,
                 lng_ref, lnb_ref, wpw_ref, bpw_ref, o_ref, u_ref, acc_ref):
    i = pl.program_id(0)
    tm = CONV_TM

    @pl.when(i == 0)
    def _():
        u_ref[:, 0:CONV_HALO, :] = jnp.zeros((D_CONV // LANES, CONV_HALO, LANES), F32)

    @pl.when(i > 0)
    def _():
        u_ref[:, 0:CONV_HALO, :] = u_ref[:, tm:tm + CONV_HALO, :]

    for a_ref, b_ref, c0 in ((a0_ref, b0_ref, 0), (a1_ref, b1_ref, HALF)):
        a = a_ref[...].astype(F32)
        b = b_ref[...].astype(F32)
        u = a * jax.nn.sigmoid(b)
        for c in range(HALF // LANES):
            u_ref[(c0 // LANES) + c, CONV_HALO:CONV_HALO + tm, :] = u[:, c * LANES:(c + 1) * LANES]

    _conv_taps(u_ref, cw_ref, cb_ref, acc_ref, tm)

    v = acc_ref[...]
    mu = jnp.mean(v, axis=1, keepdims=True)
    d = v - mu
    var = jnp.mean(d * d, axis=1, keepdims=True)
    y = d * jax.lax.rsqrt(var + LN_EPS) * lng_ref[...] + lnb_ref[...]
    s = _silu(y).astype(BF16)
    z = jnp.dot(s, wpw_ref[...], preferred_element_type=F32) + bpw_ref[...]
    for g_ref, c0 in ((g0_ref, 0), (g1_ref, HALF)):
        gate = _silu(g_ref[...].astype(F32))
        o_ref[:, c0:c0 + HALF] = (z[:, c0:c0 + HALF] * gate).astype(BF16)


def _conv_call(proj, conv_w, conv_b, ln_g, ln_b, w_pw_bf16, b_pw):
    tm = CONV_TM
    row = lambda i: (0, 0)
    half = lambda col, h: pl.BlockSpec((tm, HALF), lambda i: (i, col // HALF + h))
    return pl.pallas_call(
        _conv_kernel,
        grid=(SEQ // tm,),
        in_specs=[
            half(COL_GLU_A, 0), half(COL_GLU_A, 1),
            half(COL_GLU_B, 0), half(COL_GLU_B, 1),
            half(COL_G_CONV, 0), half(COL_G_CONV, 1),
            pl.BlockSpec((CONV_WIDTH, D_CONV), row),
            pl.BlockSpec((1, D_CONV), row),
            pl.BlockSpec((1, D_CONV), row),
            pl.BlockSpec((1, D_CONV), row),
            pl.BlockSpec((D_CONV, D_CONV), row),
            pl.BlockSpec((1, D_CONV), row),
        ],
        out_specs=pl.BlockSpec((tm, D_CONV), lambda i: (i, 0)),
        out_shape=jax.ShapeDtypeStruct((SEQ, D_CONV), BF16),
        scratch_shapes=[
            pltpu.VMEM((D_CONV // LANES, CONV_HALO + tm, LANES), F32),
            pltpu.VMEM((tm, D_CONV), F32),
        ],
        compiler_params=pltpu.CompilerParams(
            dimension_semantics=("arbitrary",), vmem_limit_bytes=VMEM_LIMIT),
        name="conv",
    )(proj, proj, proj, proj, proj, proj, conv_w, conv_b, ln_g, ln_b, w_pw_bf16, b_pw)


def _outproj_kernel(ya_ref, yc_ref, x_ref, gate_ref, wa_ref, wc_ref,
                    lng_ref, lnb_ref, o_ref):
    y = jnp.dot(ya_ref[...], wa_ref[...], preferred_element_type=F32)
    y = y + jnp.dot(yc_ref[...], wc_ref[...], preferred_element_type=F32)
    z = ALPHA * x_ref[...] + gate_ref[...] * y
    mu = jnp.mean(z, axis=1, keepdims=True)
    d = z - mu
    var = jnp.mean(d * d, axis=1, keepdims=True)
    o_ref[...] = d * jax.lax.rsqrt(var + LN_EPS) * lng_ref[...] + lnb_ref[...]


def _outproj_call(y_attn, y_conv, x2d, mod, w_out_bf16, ln_g, ln_b):
    tm = 256
    row = lambda i: (0, 0)
    return pl.pallas_call(
        _outproj_kernel,
        grid=(SEQ // tm,),
        in_specs=[
            pl.BlockSpec((tm, D_ATTN), lambda i: (i, 0)),
            pl.BlockSpec((tm, D_CONV), lambda i: (i, 0)),
            pl.BlockSpec((tm, D_MODEL), lambda i: (i, 0)),
            pl.BlockSpec((1, D_MODEL), lambda i: (0, 2)),
            pl.BlockSpec((D_ATTN, D_MODEL), lambda i: (0, 0)),
            pl.BlockSpec((D_CONV, D_MODEL), lambda i: (1, 0)),
            pl.BlockSpec((1, D_MODEL), row),
            pl.BlockSpec((1, D_MODEL), row),
        ],
        out_specs=pl.BlockSpec((tm, D_MODEL), lambda i: (i, 0)),
        out_shape=jax.ShapeDtypeStruct((SEQ, D_MODEL), F32),
        compiler_params=pltpu.CompilerParams(
            dimension_semantics=("arbitrary",), vmem_limit_bytes=VMEM_LIMIT),
        name="outproj",
    )(y_attn, y_conv, x2d, mod, w_out_bf16, w_out_bf16, ln_g, ln_b)


def kernel(x, c, w_ada, b_ada, w_in, rel_bias, sinks, conv_w, conv_b, conv_ln_g,
           conv_ln_b, w_pw, b_pw, w_out, ln_g, ln_b):
    assert x.shape == (1, SEQ, D_MODEL) and w_in.shape == (DEPTH, D_MODEL, D_IN)
    x2d = x.reshape(SEQ, D_MODEL)

    row = lambda v: v.reshape(1, -1)
    mod = _mod_call(c.reshape(D_MODEL, 1), w_ada.reshape(D_MODEL, 3 * D_MODEL), row(b_ada))
    bias = _bias_call(rel_bias, jnp.asarray(_t5_bucket_table()))
    proj = _inproj_call(x2d, mod, w_in.reshape(D_MODEL, D_IN).astype(BF16))
    y_attn = _attn_call(proj, bias, sinks.reshape(N_Q_HEADS))
    y_conv = _conv_call(proj, conv_w.reshape(CONV_WIDTH, D_CONV), row(conv_b),
                        row(conv_ln_g), row(conv_ln_b),
                        w_pw.reshape(D_CONV, D_CONV).astype(BF16), row(b_pw))
    out = _outproj_call(y_attn, y_conv, x2d, mod,
                        w_out.reshape(D_MODEL, D_MODEL).astype(BF16), row(ln_g), row(ln_b))
    return out.reshape(1, SEQ, D_MODEL)
```

```python
import functools
import math

import jax
import jax.numpy as jnp
import numpy as np
from jax.experimental import pallas as pl
from jax.experimental.pallas import tpu as pltpu

F32 = jnp.float32
BF16 = jnp.bfloat16

D_MODEL = 2048
SEQ = 8192
HEAD_DIM = 64
D_ATTN = 1024
D_CONV = 1024
N_Q_HEADS = 16
N_KV_HEADS = 4
GQA_GROUP = 4
D_KV = 256
WINDOW = 128
BLOCK = 128
CONV_WIDTH = 31
N_BUCKETS = 32
MAX_DISTANCE = 128
LN_EPS = 1e-5
DEPTH = 1
ALPHA = (2.0 * DEPTH) ** 0.25
D_IN = 2 * D_ATTN + 2 * D_KV + 3 * D_CONV

LANES = 128
SUBLANES = 8

COL_Q = 0
COL_K = COL_Q + D_ATTN
COL_V = COL_K + D_KV
COL_G_ATTN = COL_V + D_KV
COL_GLU_A = COL_G_ATTN + D_ATTN
COL_GLU_B = COL_GLU_A + D_CONV
COL_G_CONV = COL_GLU_B + D_CONV
HALF = 512

VMEM_LIMIT = 56 * 1024 * 1024


def _silu(v):
    return v * jax.nn.sigmoid(v)


def _t5_bucket_table():
    qi = np.arange(BLOCK)[:, None]
    kj = np.arange(2 * BLOCK)[None, :]
    dist = np.maximum(qi + BLOCK - kj, 0)
    max_exact = N_BUCKETS // 2
    d = np.maximum(dist, 1).astype(np.float64)
    large = max_exact + (np.log(d / max_exact) / math.log(MAX_DISTANCE / max_exact)
                         * (N_BUCKETS - max_exact)).astype(np.int32)
    large = np.minimum(large, N_BUCKETS - 1)
    return np.where(dist < max_exact, dist, large).astype(np.int32)


def _mod_kernel(c_ref, w_ref, b_ref, o_ref):
    c = c_ref[...]
    ca = _silu(c)
    o_ref[...] = jnp.sum(ca * w_ref[...], axis=0, keepdims=True) + b_ref[...]


def _mod_call(c_col, w_ada, b_ada):
    tn = 1024
    n = w_ada.shape[1]
    return pl.pallas_call(
        _mod_kernel,
        grid=(n // tn,),
        in_specs=[
            pl.BlockSpec((D_MODEL, 1), lambda j: (0, 0)),
            pl.BlockSpec((D_MODEL, tn), lambda j: (0, j)),
            pl.BlockSpec((1, tn), lambda j: (0, j)),
        ],
        out_specs=pl.BlockSpec((1, tn), lambda j: (0, j)),
        out_shape=jax.ShapeDtypeStruct((1, n), F32),
        compiler_params=pltpu.CompilerParams(
            dimension_semantics=("arbitrary",), vmem_limit_bytes=VMEM_LIMIT),
        name="mod",
    )(c_col, w_ada, b_ada)


def _bias_kernel(rb_ref, bucket_ref, o_ref):
    bucket = bucket_ref[...]
    for h in range(N_Q_HEADS):
        acc = jnp.zeros((BLOCK, 2 * BLOCK), F32)
        for b in range(N_BUCKETS):
            acc = jnp.where(bucket == b, rb_ref[b, h], acc)
        o_ref[h] = acc


def _bias_call(rel_bias, bucket):
    return pl.pallas_call(
        _bias_kernel,
        in_specs=[
            pl.BlockSpec(memory_space=pltpu.SMEM),
            pl.BlockSpec((BLOCK, 2 * BLOCK), lambda: (0, 0)),
        ],
        out_specs=pl.BlockSpec((N_Q_HEADS, BLOCK, 2 * BLOCK), lambda: (0, 0, 0)),
        out_shape=jax.ShapeDtypeStruct((N_Q_HEADS, BLOCK, 2 * BLOCK), F32),
        name="bias",
    )(rel_bias, bucket)


def _inproj_kernel(x_ref, shift_ref, scale_ref, w_ref, o_ref, h_ref):
    @pl.when(pl.program_id(1) == 0)
    def _():
        h = x_ref[...] * (1.0 + scale_ref[...]) + shift_ref[...]
        h_ref[...] = h.astype(BF16)

    o_ref[...] = jnp.dot(h_ref[...], w_ref[...],
                         preferred_element_type=F32).astype(BF16)


def _inproj_call(x2d, mod, w_in_bf16):
    tm, tn = 1024, 512
    return pl.pallas_call(
        _inproj_kernel,
        grid=(SEQ // tm, D_IN // tn),
        in_specs=[
            pl.BlockSpec((tm, D_MODEL), lambda i, j: (i, 0)),
            pl.BlockSpec((1, D_MODEL), lambda i, j: (0, 0)),
            pl.BlockSpec((1, D_MODEL), lambda i, j: (0, 1)),
            pl.BlockSpec((D_MODEL, tn), lambda i, j: (0, j)),
        ],
        out_specs=pl.BlockSpec((tm, tn), lambda i, j: (i, j)),
        out_shape=jax.ShapeDtypeStruct((SEQ, D_IN), BF16),
        scratch_shapes=[pltpu.VMEM((tm, D_MODEL), BF16)],
        compiler_params=pltpu.CompilerParams(
            dimension_semantics=("arbitrary", "arbitrary"),
            vmem_limit_bytes=VMEM_LIMIT),
        name="inproj",
    )(x2d, mod, mod, w_in_bf16)


def _swap_halves(t):
    return jnp.concatenate([t[:, HEAD_DIM:], t[:, :HEAD_DIM]], axis=1)


def _attn_unit(q_ref, g0_ref, g1_ref, k_tiles, v_tiles, bias_ref, sinks_ref, valid,
               r0, tile, ya_ref):
    lane = jax.lax.broadcasted_iota(jnp.int32, (BLOCK, LANES), 1)
    low_half = lane < HEAD_DIM
    neg = jnp.finfo(F32).min
    rows = slice(r0, r0 + BLOCK)
    sl = slice(tile * LANES, (tile + 1) * LANES)
    q_tile = q_ref[rows, sl] * (HEAD_DIM ** -0.5)
    scores = []
    for p in range(2):
        kv = (2 * tile + p) // GQA_GROUP
        kt = k_tiles[kv // 2][0 if (kv % 2) == p else 1]
        in_half = low_half if p == 0 else jnp.logical_not(low_half)
        qm = jnp.where(in_half, q_tile, jnp.zeros_like(q_tile))
        scores.append(jax.lax.dot_general(qm, kt, (((1,), (1,)), ((), ())),
                                          preferred_element_type=F32))
    outs = []
    for p in range(2):
        head = 2 * tile + p
        kv = head // GQA_GROUP
        vt = v_tiles[kv // 2][0 if (kv % 2) == p else 1]
        s = jnp.where(valid, scores[p] + bias_ref[head], neg)
        sink = sinks_ref[head]
        m = jnp.maximum(jnp.max(s, axis=1, keepdims=True), sink)
        e = jnp.exp(s - m)
        denom = jnp.sum(e, axis=1, keepdims=True) + jnp.exp(sink - m)
        o = jnp.dot(e.astype(BF16), vt, preferred_element_type=F32)
        outs.append(o / denom)
    y = jnp.where(low_half, outs[0], outs[1])
    g_ref = g0_ref if tile < HALF // LANES else g1_ref
    gl = (tile * LANES) % HALF
    gate = _silu(g_ref[rows, gl:gl + LANES].astype(F32))
    ya_ref[rows, sl] = (y * gate).astype(BF16)


ATT_TM = 2 * BLOCK
OUT_CHUNK = 256
PACKED_ROWS = 16


def _ordered_after(ref, token):
    zero = (pltpu.bitcast(token, jnp.uint32) >> 16) >> 16
    first = pltpu.bitcast(ref[0:PACKED_ROWS, 0:LANES], jnp.uint32) | zero
    top = jnp.concatenate([pltpu.bitcast(first, ref.dtype), ref[0:PACKED_ROWS, LANES:]],
                          axis=1)
    return jnp.concatenate([top, ref[PACKED_ROWS:, :]], axis=0)


def _attn_out_kernel(sinks_ref, q_ref, g0_ref, g1_ref, k_ref, v_ref, kp_ref, vp_ref,
                     bias_ref, yc_ref, x_ref, gate_ref, wa_ref, wc_ref, lng_ref, lnb_ref,
                     o_ref, ya_ref):
    i = pl.program_id(0)

    @pl.when(i == 0)
    def _():
        ya_ref[...] = jnp.zeros(ya_ref.shape, ya_ref.dtype)

    y = jnp.dot(ya_ref[...], wa_ref[...], preferred_element_type=F32)
    y = y + jnp.dot(yc_ref[...], wc_ref[...], preferred_element_type=F32)
    z = ALPHA * x_ref[...] + gate_ref[...] * y
    mu = jnp.mean(z, axis=1, keepdims=True)
    d = z - mu
    var = jnp.mean(d * d, axis=1, keepdims=True)
    o_ref[...] = d * jax.lax.rsqrt(var + LN_EPS) * lng_ref[...] + lnb_ref[...]

    _attn_tile(i, sinks_ref, q_ref, g0_ref, g1_ref, k_ref, v_ref, kp_ref, vp_ref,
               bias_ref, ya_ref)


def _attn_tile(i, sinks_ref, q_ref, g0_ref, g1_ref, k_ref, v_ref, kp_ref, vp_ref,
               bias_ref, ya_ref):
    row = jax.lax.broadcasted_iota(jnp.int32, (BLOCK, 2 * BLOCK), 0)
    col = jax.lax.broadcasted_iota(jnp.int32, (BLOCK, 2 * BLOCK), 1)
    in_window = (col > row) & (col <= row + WINDOW)
    first_key = jnp.where(i > 0, 0, BLOCK)
    n_lt = D_KV // LANES
    lt = lambda t: slice(t * LANES, (t + 1) * LANES)

    def block_operands(b):
        r0 = b * BLOCK
        if b == 0:
            k_prev = [kp_ref[:, lt(t)] for t in range(n_lt)]
            v_prev = [vp_ref[:, lt(t)] for t in range(n_lt)]
            valid = in_window & (col >= first_key)
        else:
            k_prev = [k_ref[r0 - BLOCK:r0, lt(t)] for t in range(n_lt)]
            v_prev = [v_ref[r0 - BLOCK:r0, lt(t)] for t in range(n_lt)]
            valid = in_window
        k_rows = [jnp.concatenate([k_prev[t], k_ref[r0:r0 + BLOCK, lt(t)]], axis=0)
                  for t in range(n_lt)]
        v_rows = [jnp.concatenate([v_prev[t], v_ref[r0:r0 + BLOCK, lt(t)]], axis=0)
                  for t in range(n_lt)]
        k_tiles = [(kt, _swap_halves(kt)) for kt in k_rows]
        v_tiles = [(vt, _swap_halves(vt)) for vt in v_rows]
        return k_tiles, v_tiles, valid

    for b in range(ATT_TM // BLOCK):
        k_tiles, v_tiles, valid = block_operands(b)
        for tile in range(D_ATTN // LANES):
            _attn_unit(q_ref, g0_ref, g1_ref, k_tiles, v_tiles, bias_ref, sinks_ref,
                       valid, b * BLOCK, tile, ya_ref)


def _attn_out_call(proj, bias, sinks, y_conv, x2d, mod, w_out_bf16, ln_g, ln_b):
    tm = ATT_TM
    n = SEQ // tm
    clamp = lambda t: jnp.clip(t, 0, n - 1)
    cur = lambda i: clamp(i)
    lag = lambda i: clamp(i - 1)
    kv_prev = lambda i: jnp.maximum(cur(i) * (tm // BLOCK) - 1, 0)
    row = lambda i: (0, 0)
    return pl.pallas_call(
        _attn_out_kernel,
        grid=(n + 1,),
        in_specs=[
            pl.BlockSpec(memory_space=pltpu.SMEM),
            pl.BlockSpec((tm, D_ATTN), lambda i: (cur(i), COL_Q // D_ATTN)),
            pl.BlockSpec((tm, HALF), lambda i: (cur(i), COL_G_ATTN // HALF)),
            pl.BlockSpec((tm, HALF), lambda i: (cur(i), COL_G_ATTN // HALF + 1)),
            pl.BlockSpec((tm, D_KV), lambda i: (cur(i), COL_K // D_KV)),
            pl.BlockSpec((tm, D_KV), lambda i: (cur(i), COL_V // D_KV)),
            pl.BlockSpec((BLOCK, D_KV), lambda i: (kv_prev(i), COL_K // D_KV)),
            pl.BlockSpec((BLOCK, D_KV), lambda i: (kv_prev(i), COL_V // D_KV)),
            pl.BlockSpec((N_Q_HEADS, BLOCK, 2 * BLOCK), lambda i: (0, 0, 0)),
            pl.BlockSpec((tm, D_CONV), lambda i: (lag(i), 0)),
            pl.BlockSpec((tm, D_MODEL), lambda i: (lag(i), 0)),
            pl.BlockSpec((1, D_MODEL), lambda i: (0, 2)),
            pl.BlockSpec((D_ATTN, D_MODEL), lambda i: (0, 0)),
            pl.BlockSpec((D_CONV, D_MODEL), lambda i: (1, 0)),
            pl.BlockSpec((1, D_MODEL), row),
            pl.BlockSpec((1, D_MODEL), row),
        ],
        out_specs=pl.BlockSpec((tm, D_MODEL), lambda i: (lag(i), 0)),
        out_shape=jax.ShapeDtypeStruct((SEQ, D_MODEL), F32),
        scratch_shapes=[pltpu.VMEM((tm, D_ATTN), BF16)],
        compiler_params=pltpu.CompilerParams(
            dimension_semantics=("arbitrary",), vmem_limit_bytes=VMEM_LIMIT),
        name="attn_out",
    )(sinks, proj, proj, proj, proj, proj, proj, proj, bias, y_conv, x2d, mod,
      w_out_bf16, w_out_bf16, ln_g, ln_b)


CONV_TM = 256
CONV_HALO = 32
CONV_RC = 64


def _conv_taps(u_ref, cw_ref, cb_ref, acc_ref, tm):
    off = CONV_HALO - (CONV_WIDTH - 1)
    for c in range(D_CONV // LANES):
        cs = slice(c * LANES, (c + 1) * LANES)
        w_rows = [cw_ref[k:k + 1, cs] for k in range(CONV_WIDTH)]
        for r0 in range(0, tm, CONV_RC):
            acc = jnp.broadcast_to(cb_ref[:, cs], (CONV_RC, LANES))
            for k in range(CONV_WIDTH):
                acc = acc + w_rows[k] * u_ref[c, r0 + off + k:r0 + off + k + CONV_RC, :]
            acc_ref[r0:r0 + CONV_RC, cs] = acc


def _conv_kernel(a0_ref, a1_ref, b0_ref, b1_ref, g0_ref, g1_ref, cw_ref, cb_ref,
                 lng_ref, lnb_ref, wpw_ref, bpw_ref, o_ref, u_ref, acc_ref):
    i = pl.program_id(0)
    tm = CONV_TM

    @pl.when(i == 0)
    def _():
        u_ref[:, 0:CONV_HALO, :] = jnp.zeros((D_CONV // LANES, CONV_HALO, LANES), F32)

    @pl.when(i > 0)
    def _():
        u_ref[:, 0:CONV_HALO, :] = u_ref[:, tm:tm + CONV_HALO, :]

    for a_ref, b_ref, c0 in ((a0_ref, b0_ref, 0), (a1_ref, b1_ref, HALF)):
        a = a_ref[...].astype(F32)
        b = b_ref[...].astype(F32)
        u = a * jax.nn.sigmoid(b)
        for c in range(HALF // LANES):
            u_ref[(c0 // LANES) + c, CONV_HALO:CONV_HALO + tm, :] = u[:, c * LANES:(c + 1) * LANES]

    _conv_taps(u_ref, cw_ref, cb_ref, acc_ref, tm)

    v = acc_ref[...]
    mu = jnp.mean(v, axis=1, keepdims=True)
    d = v - mu
    var = jnp.mean(d * d, axis=1, keepdims=True)
    y = d * jax.lax.rsqrt(var + LN_EPS) * lng_ref[...] + lnb_ref[...]
    s = _silu(y).astype(BF16)
    z = jnp.dot(s, wpw_ref[...], preferred_element_type=F32) + bpw_ref[...]
    for g_ref, c0 in ((g0_ref, 0), (g1_ref, HALF)):
        gate = _silu(g_ref[...].astype(F32))
        o_ref[:, c0:c0 + HALF] = (z[:, c0:c0 + HALF] * gate).astype(BF16)


def _conv_call(proj, conv_w, conv_b, ln_g, ln_b, w_pw_bf16, b_pw):
    tm = CONV_TM
    row = lambda i: (0, 0)
    half = lambda col, h: pl.BlockSpec((tm, HALF), lambda i: (i, col // HALF + h))
    return pl.pallas_call(
        _conv_kernel,
        grid=(SEQ // tm,),
        in_specs=[
            half(COL_GLU_A, 0), half(COL_GLU_A, 1),
            half(COL_GLU_B, 0), half(COL_GLU_B, 1),
            half(COL_G_CONV, 0), half(COL_G_CONV, 1),
            pl.BlockSpec((CONV_WIDTH, D_CONV), row),
            pl.BlockSpec((1, D_CONV), row),
            pl.BlockSpec((1, D_CONV), row),
            pl.BlockSpec((1, D_CONV), row),
            pl.BlockSpec((D_CONV, D_CONV), row),
            pl.BlockSpec((1, D_CONV), row),
        ],
        out_specs=pl.BlockSpec((tm, D_CONV), lambda i: (i, 0)),
        out_shape=jax.ShapeDtypeStruct((SEQ, D_CONV), BF16),
        scratch_shapes=[
            pltpu.VMEM((D_CONV // LANES, CONV_HALO + tm, LANES), F32),
            pltpu.VMEM((tm, D_CONV), F32),
        ],
        compiler_params=pltpu.CompilerParams(
            dimension_semantics=("arbitrary",), vmem_limit_bytes=VMEM_LIMIT),
        name="conv",
    )(proj, proj, proj, proj, proj, proj, conv_w, conv_b, ln_g, ln_b, w_pw_bf16, b_pw)


def kernel(x, c, w_ada, b_ada, w_in, rel_bias, sinks, conv_w, conv_b, conv_ln_g,
           conv_ln_b, w_pw, b_pw, w_out, ln_g, ln_b):
    assert x.shape == (1, SEQ, D_MODEL) and w_in.shape == (DEPTH, D_MODEL, D_IN)
    x2d = x.reshape(SEQ, D_MODEL)

    row = lambda v: v.reshape(1, -1)
    mod = _mod_call(c.reshape(D_MODEL, 1), w_ada.reshape(D_MODEL, 3 * D_MODEL), row(b_ada))
    bias = _bias_call(rel_bias, jnp.asarray(_t5_bucket_table()))
    proj = _inproj_call(x2d, mod, w_in.reshape(D_MODEL, D_IN).astype(BF16))
    y_conv = _conv_call(proj, conv_w.reshape(CONV_WIDTH, D_CONV), row(conv_b),
                        row(conv_ln_g), row(conv_ln_b),
                        w_pw.reshape(D_CONV, D_CONV).astype(BF16), row(b_pw))
    out = _attn_out_call(proj, bias, sinks.reshape(N_Q_HEADS), y_conv, x2d, mod,
                         w_out.reshape(D_MODEL, D_MODEL).astype(BF16), row(ln_g), row(ln_b))
    return out.reshape(1, SEQ, D_MODEL)
```

```python
import functools
import math

import jax
import jax.numpy as jnp
import numpy as np
from jax.experimental import pallas as pl
from jax.experimental.pallas import tpu as pltpu

F32 = jnp.float32
BF16 = jnp.bfloat16

D_MODEL = 2048
SEQ = 8192
HEAD_DIM = 64
D_ATTN = 1024
D_CONV = 1024
N_Q_HEADS = 16
N_KV_HEADS = 4
GQA_GROUP = 4
D_KV = 256
WINDOW = 128
BLOCK = 128
CONV_WIDTH = 31
N_BUCKETS = 32
MAX_DISTANCE = 128
LN_EPS = 1e-5
DEPTH = 1
ALPHA = (2.0 * DEPTH) ** 0.25
D_IN = 2 * D_ATTN + 2 * D_KV + 3 * D_CONV

LANES = 128
SUBLANES = 8

COL_Q = 0
COL_K = COL_Q + D_ATTN
COL_V = COL_K + D_KV
COL_G_ATTN = COL_V + D_KV
COL_GLU_A = COL_G_ATTN + D_ATTN
COL_GLU_B = COL_GLU_A + D_CONV
COL_G_CONV = COL_GLU_B + D_CONV
HALF = 512

VMEM_LIMIT = 56 * 1024 * 1024


def _silu(v):
    return v * jax.nn.sigmoid(v)


def _t5_bucket_table():
    qi = np.arange(BLOCK)[:, None]
    kj = np.arange(2 * BLOCK)[None, :]
    dist = np.maximum(qi + BLOCK - kj, 0)
    max_exact = N_BUCKETS // 2
    d = np.maximum(dist, 1).astype(np.float64)
    large = max_exact + (np.log(d / max_exact) / math.log(MAX_DISTANCE / max_exact)
                         * (N_BUCKETS - max_exact)).astype(np.int32)
    large = np.minimum(large, N_BUCKETS - 1)
    return np.where(dist < max_exact, dist, large).astype(np.int32)


def _mod_kernel(c_ref, w_ref, b_ref, o_ref):
    c = c_ref[...]
    ca = _silu(c)
    o_ref[...] = jnp.sum(ca * w_ref[...], axis=0, keepdims=True) + b_ref[...]


def _mod_call(c_col, w_ada, b_ada):
    tn = 1024
    n = w_ada.shape[1]
    return pl.pallas_call(
        _mod_kernel,
        grid=(n // tn,),
        in_specs=[
            pl.BlockSpec((D_MODEL, 1), lambda j: (0, 0)),
            pl.BlockSpec((D_MODEL, tn), lambda j: (0, j)),
            pl.BlockSpec((1, tn), lambda j: (0, j)),
        ],
        out_specs=pl.BlockSpec((1, tn), lambda j: (0, j)),
        out_shape=jax.ShapeDtypeStruct((1, n), F32),
        compiler_params=pltpu.CompilerParams(
            dimension_semantics=("arbitrary",), vmem_limit_bytes=VMEM_LIMIT),
        name="mod",
    )(c_col, w_ada, b_ada)


def _bias_kernel(rb_ref, bucket_ref, o_ref):
    bucket = bucket_ref[...]
    for h in range(N_Q_HEADS):
        acc = jnp.zeros((BLOCK, 2 * BLOCK), F32)
        for b in range(N_BUCKETS):
            acc = jnp.where(bucket == b, rb_ref[b, h], acc)
        o_ref[h] = acc


def _bias_call(rel_bias, bucket):
    return pl.pallas_call(
        _bias_kernel,
        in_specs=[
            pl.BlockSpec(memory_space=pltpu.SMEM),
            pl.BlockSpec((BLOCK, 2 * BLOCK), lambda: (0, 0)),
        ],
        out_specs=pl.BlockSpec((N_Q_HEADS, BLOCK, 2 * BLOCK), lambda: (0, 0, 0)),
        out_shape=jax.ShapeDtypeStruct((N_Q_HEADS, BLOCK, 2 * BLOCK), F32),
        name="bias",
    )(rel_bias, bucket)


def _inproj_kernel(x_ref, shift_ref, scale_ref, w_ref, o_ref, h_ref):
    @pl.when(pl.program_id(1) == 0)
    def _():
        h = x_ref[...] * (1.0 + scale_ref[...]) + shift_ref[...]
        h_ref[...] = h.astype(BF16)

    o_ref[...] = jnp.dot(h_ref[...], w_ref[...],
                         preferred_element_type=F32).astype(BF16)


def _inproj_call(x2d, mod, w_in_bf16):
    tm, tn = 1024, 512
    return pl.pallas_call(
        _inproj_kernel,
        grid=(SEQ // tm, D_IN // tn),
        in_specs=[
            pl.BlockSpec((tm, D_MODEL), lambda i, j: (i, 0)),
            pl.BlockSpec((1, D_MODEL), lambda i, j: (0, 0)),
            pl.BlockSpec((1, D_MODEL), lambda i, j: (0, 1)),
            pl.BlockSpec((D_MODEL, tn), lambda i, j: (0, j)),
        ],
        out_specs=pl.BlockSpec((tm, tn), lambda i, j: (i, j)),
        out_shape=jax.ShapeDtypeStruct((SEQ, D_IN), BF16),
        scratch_shapes=[pltpu.VMEM((tm, D_MODEL), BF16)],
        compiler_params=pltpu.CompilerParams(
            dimension_semantics=("arbitrary", "arbitrary"),
            vmem_limit_bytes=VMEM_LIMIT),
        name="inproj",
    )(x2d, mod, mod, w_in_bf16)


def _swap_halves(t):
    return jnp.concatenate([t[:, HEAD_DIM:], t[:, :HEAD_DIM]], axis=1)


def _attn_unit(q_ref, g0_ref, g1_ref, k_tiles, v_tiles, bias_ref, sinks_ref, valid,
               r0, tile, ya_ref):
    lane = jax.lax.broadcasted_iota(jnp.int32, (BLOCK, LANES), 1)
    low_half = lane < HEAD_DIM
    neg = jnp.finfo(F32).min
    rows = slice(r0, r0 + BLOCK)
    sl = slice(tile * LANES, (tile + 1) * LANES)
    q_tile = q_ref[rows, sl] * (HEAD_DIM ** -0.5)
    scores = []
    for p in range(2):
        kv = (2 * tile + p) // GQA_GROUP
        kt = k_tiles[kv // 2][0 if (kv % 2) == p else 1]
        in_half = low_half if p == 0 else jnp.logical_not(low_half)
        qm = jnp.where(in_half, q_tile, jnp.zeros_like(q_tile))
        scores.append(jax.lax.dot_general(qm, kt, (((1,), (1,)), ((), ())),
                                          preferred_element_type=F32))
    outs = []
    for p in range(2):
        head = 2 * tile + p
        kv = head // GQA_GROUP
        vt = v_tiles[kv // 2][0 if (kv % 2) == p else 1]
        s = jnp.where(valid, scores[p] + bias_ref[head], neg)
        sink = sinks_ref[head]
        m = jnp.maximum(jnp.max(s, axis=1, keepdims=True), sink)
        e = jnp.exp(s - m)
        denom = jnp.sum(e, axis=1, keepdims=True) + jnp.exp(sink - m)
        o = jnp.dot(e.astype(BF16), vt, preferred_element_type=F32)
        outs.append(o / denom)
    y = jnp.where(low_half, outs[0], outs[1])
    g_ref = g0_ref if tile < HALF // LANES else g1_ref
    gl = (tile * LANES) % HALF
    gate = g_ref[rows, gl:gl + LANES].astype(F32)
    ya_ref[rows, sl] = (y * gate).astype(BF16)


ATT_TM = 2 * BLOCK
OUT_CHUNK = 256
PACKED_ROWS = 16


def _ordered_after(ref, token):
    zero = (pltpu.bitcast(token, jnp.uint32) >> 16) >> 16
    first = pltpu.bitcast(ref[0:PACKED_ROWS, 0:LANES], jnp.uint32) | zero
    top = jnp.concatenate([pltpu.bitcast(first, ref.dtype), ref[0:PACKED_ROWS, LANES:]],
                          axis=1)
    return jnp.concatenate([top, ref[PACKED_ROWS:, :]], axis=0)


def _attn_out_kernel(sinks_ref, q_ref, g0_ref, g1_ref, k_ref, v_ref, kp_ref, vp_ref,
                     bias_ref, yc_ref, x_ref, gate_ref, wa_ref, wc_ref, lng_ref, lnb_ref,
                     o_ref, ya_ref):
    i = pl.program_id(0)

    @pl.when(i == 0)
    def _():
        ya_ref[...] = jnp.zeros(ya_ref.shape, ya_ref.dtype)

    y = jnp.dot(ya_ref[...], wa_ref[...], preferred_element_type=F32)
    y = y + jnp.dot(yc_ref[...], wc_ref[...], preferred_element_type=F32)
    z = ALPHA * x_ref[...] + gate_ref[...] * y
    mu = jnp.mean(z, axis=1, keepdims=True)
    d = z - mu
    var = jnp.mean(d * d, axis=1, keepdims=True)
    o_ref[...] = d * jax.lax.rsqrt(var + LN_EPS) * lng_ref[...] + lnb_ref[...]

    _attn_tile(i, sinks_ref, q_ref, g0_ref, g1_ref, k_ref, v_ref, kp_ref, vp_ref,
               bias_ref, ya_ref)


def _attn_tile(i, sinks_ref, q_ref, g0_ref, g1_ref, k_ref, v_ref, kp_ref, vp_ref,
               bias_ref, ya_ref):
    row = jax.lax.broadcasted_iota(jnp.int32, (BLOCK, 2 * BLOCK), 0)
    col = jax.lax.broadcasted_iota(jnp.int32, (BLOCK, 2 * BLOCK), 1)
    in_window = (col > row) & (col <= row + WINDOW)
    first_key = jnp.where(i > 0, 0, BLOCK)
    n_lt = D_KV // LANES
    lt = lambda t: slice(t * LANES, (t + 1) * LANES)

    def block_operands(b):
        r0 = b * BLOCK
        if b == 0:
            k_prev = [kp_ref[:, lt(t)] for t in range(n_lt)]
            v_prev = [vp_ref[:, lt(t)] for t in range(n_lt)]
            valid = in_window & (col >= first_key)
        else:
            k_prev = [k_ref[r0 - BLOCK:r0, lt(t)] for t in range(n_lt)]
            v_prev = [v_ref[r0 - BLOCK:r0, lt(t)] for t in range(n_lt)]
            valid = in_window
        k_rows = [jnp.concatenate([k_prev[t], k_ref[r0:r0 + BLOCK, lt(t)]], axis=0)
                  for t in range(n_lt)]
        v_rows = [jnp.concatenate([v_prev[t], v_ref[r0:r0 + BLOCK, lt(t)]], axis=0)
                  for t in range(n_lt)]
        k_tiles = [(kt, _swap_halves(kt)) for kt in k_rows]
        v_tiles = [(vt, _swap_halves(vt)) for vt in v_rows]
        return k_tiles, v_tiles, valid

    for b in range(ATT_TM // BLOCK):
        k_tiles, v_tiles, valid = block_operands(b)
        for tile in range(D_ATTN // LANES):
            _attn_unit(q_ref, g0_ref, g1_ref, k_tiles, v_tiles, bias_ref, sinks_ref,
                       valid, b * BLOCK, tile, ya_ref)


def _attn_out_call(proj, bias, sinks, y_conv, x2d, mod, w_out_bf16, ln_g, ln_b):
    tm = ATT_TM
    n = SEQ // tm
    clamp = lambda t: jnp.clip(t, 0, n - 1)
    cur = lambda i: clamp(i)
    lag = lambda i: clamp(i - 1)
    kv_prev = lambda i: jnp.maximum(cur(i) * (tm // BLOCK) - 1, 0)
    row = lambda i: (0, 0)
    return pl.pallas_call(
        _attn_out_kernel,
        grid=(n + 1,),
        in_specs=[
            pl.BlockSpec(memory_space=pltpu.SMEM),
            pl.BlockSpec((tm, D_ATTN), lambda i: (cur(i), COL_Q // D_ATTN)),
            pl.BlockSpec((tm, HALF), lambda i: (cur(i), COL_G_ATTN // HALF)),
            pl.BlockSpec((tm, HALF), lambda i: (cur(i), COL_G_ATTN // HALF + 1)),
            pl.BlockSpec((tm, D_KV), lambda i: (cur(i), COL_K // D_KV)),
            pl.BlockSpec((tm, D_KV), lambda i: (cur(i), COL_V // D_KV)),
            pl.BlockSpec((BLOCK, D_KV), lambda i: (kv_prev(i), COL_K // D_KV)),
            pl.BlockSpec((BLOCK, D_KV), lambda i: (kv_prev(i), COL_V // D_KV)),
            pl.BlockSpec((N_Q_HEADS, BLOCK, 2 * BLOCK), lambda i: (0, 0, 0)),
            pl.BlockSpec((tm, D_CONV), lambda i: (lag(i), 0)),
            pl.BlockSpec((tm, D_MODEL), lambda i: (lag(i), 0)),
            pl.BlockSpec((1, D_MODEL), lambda i: (0, 2)),
            pl.BlockSpec((D_ATTN, D_MODEL), lambda i: (0, 0)),
            pl.BlockSpec((D_CONV, D_MODEL), lambda i: (1, 0)),
            pl.BlockSpec((1, D_MODEL), row),
            pl.BlockSpec((1, D_MODEL), row),
        ],
        out_specs=pl.BlockSpec((tm, D_MODEL), lambda i: (lag(i), 0)),
        out_shape=jax.ShapeDtypeStruct((SEQ, D_MODEL), F32),
        scratch_shapes=[pltpu.VMEM((tm, D_ATTN), BF16)],
        compiler_params=pltpu.CompilerParams(
            dimension_semantics=("arbitrary",), vmem_limit_bytes=VMEM_LIMIT),
        name="attn_out",
    )(sinks, proj, proj, proj, proj, proj, proj, proj, bias, y_conv, x2d, mod,
      w_out_bf16, w_out_bf16, ln_g, ln_b)


CONV_TM = 256
CONV_HALO = 32
CONV_RC = 64


def _conv_taps(u_ref, cw_ref, cb_ref, acc_ref, tm):
    off = CONV_HALO - (CONV_WIDTH - 1)
    for c in range(D_CONV // LANES):
        cs = slice(c * LANES, (c + 1) * LANES)
        w_rows = [cw_ref[k:k + 1, cs] for k in range(CONV_WIDTH)]
        for r0 in range(0, tm, CONV_RC):
            acc = jnp.broadcast_to(cb_ref[:, cs], (CONV_RC, LANES))
            for k in range(CONV_WIDTH):
                acc = acc + w_rows[k] * u_ref[c, r0 + off + k:r0 + off + k + CONV_RC, :]
            acc_ref[r0:r0 + CONV_RC, cs] = acc


def _conv_kernel(a0_ref, a1_ref, b0_ref, b1_ref, g0_ref, g1_ref, cw_ref, cb_ref,
                 lng_ref, lnb_ref, wpw_ref, bpw_ref, o_ref, u_ref, acc_ref):
    i = pl.program_id(0)
    tm = CONV_TM

    @pl.when(i == 0)
    def _():
        u_ref[:, 0:CONV_HALO, :] = jnp.zeros((D_CONV // LANES, CONV_HALO, LANES), F32)

    @pl.when(i > 0)
    def _():
        u_ref[:, 0:CONV_HALO, :] = u_ref[:, tm:tm + CONV_HALO, :]

    for a_ref, b_ref, c0 in ((a0_ref, b0_ref, 0), (a1_ref, b1_ref, HALF)):
        a = a_ref[...].astype(F32)
        b = b_ref[...].astype(F32)
        u = a * jax.nn.sigmoid(b)
        for c in range(HALF // LANES):
            u_ref[(c0 // LANES) + c, CONV_HALO:CONV_HALO + tm, :] = u[:, c * LANES:(c + 1) * LANES]

    _conv_taps(u_ref, cw_ref, cb_ref, acc_ref, tm)

    v = acc_ref[...]
    mu = jnp.mean(v, axis=1, keepdims=True)
    d = v - mu
    var = jnp.mean(d * d, axis=1, keepdims=True)
    y = d * jax.lax.rsqrt(var + LN_EPS) * lng_ref[...] + lnb_ref[...]
    s = _silu(y).astype(BF16)
    z = jnp.dot(s, wpw_ref[...], preferred_element_type=F32) + bpw_ref[...]
    for g_ref, c0 in ((g0_ref, 0), (g1_ref, HALF)):
        gate = _silu(g_ref[...].astype(F32))
        o_ref[:, c0:c0 + HALF] = (z[:, c0:c0 + HALF] * gate).astype(BF16)


def _conv_call(proj, conv_w, conv_b, ln_g, ln_b, w_pw_bf16, b_pw):
    tm = CONV_TM
    row = lambda i: (0, 0)
    half = lambda col, h: pl.BlockSpec((tm, HALF), lambda i: (i, col // HALF + h))
    return pl.pallas_call(
        _conv_kernel,
        grid=(SEQ // tm,),
        in_specs=[
            half(COL_GLU_A, 0), half(COL_GLU_A, 1),
            half(COL_GLU_B, 0), half(COL_GLU_B, 1),
            half(COL_G_CONV, 0), half(COL_G_CONV, 1),
            pl.BlockSpec((CONV_WIDTH, D_CONV), row),
            pl.BlockSpec((1, D_CONV), row),
            pl.BlockSpec((1, D_CONV), row),
            pl.BlockSpec((1, D_CONV), row),
            pl.BlockSpec((D_CONV, D_CONV), row),
            pl.BlockSpec((1, D_CONV), row),
        ],
        out_specs=pl.BlockSpec((tm, D_CONV), lambda i: (i, 0)),
        out_shape=jax.ShapeDtypeStruct((SEQ, D_CONV), BF16),
        scratch_shapes=[
            pltpu.VMEM((D_CONV // LANES, CONV_HALO + tm, LANES), F32),
            pltpu.VMEM((tm, D_CONV), F32),
        ],
        compiler_params=pltpu.CompilerParams(
            dimension_semantics=("arbitrary",), vmem_limit_bytes=VMEM_LIMIT),
        name="conv",
    )(proj, proj, proj, proj, proj, proj, conv_w, conv_b, ln_g, ln_b, w_pw_bf16, b_pw)


IN_TM = 256
IN_CHUNK = 256
D_QKVG = COL_GLU_A
TAP_ROWS = 64
SYNC_EVERY = 2


def _token(v):
    t = v[0:SUBLANES, 0:LANES]
    for r in range(SUBLANES, v.shape[0], SUBLANES):
        t = t + v[r:r + SUBLANES, 0:LANES]
    return t


def _zero_of(token):
    return (pltpu.bitcast(token, jnp.uint32) >> 16) >> 16


def _copy_after(src_ref, dst_ref, tokens):
    zero = _zero_of(tokens[0])
    for t in tokens[1:]:
        zero = zero | _zero_of(t)
    bits = pltpu.bitcast(src_ref[...], jnp.uint32)
    reps = (bits.shape[0] // zero.shape[0], bits.shape[1] // zero.shape[1])
    dst_ref[...] = pltpu.bitcast(bits | jnp.tile(zero, reps), dst_ref.dtype)


def _tap_piece(u_ref, cw_ref, cb_ref, acc_ref, c, r0):
    off = CONV_HALO - (CONV_WIDTH - 1)
    cs = slice(c * LANES, (c + 1) * LANES)
    acc = jnp.broadcast_to(cb_ref[:, cs], (TAP_ROWS, LANES))
    for k in range(CONV_WIDTH):
        acc = acc + cw_ref[k:k + 1, cs] * u_ref[c, r0 + off + k:r0 + off + k + TAP_ROWS, :]
    acc_ref[r0:r0 + TAP_ROWS, cs] = acc
    return _token(acc)


def _inproj_conv_kernel(x_ref, shift_ref, scale_ref, w_ref, cw_ref, cb_ref, lng_ref,
                        lnb_ref, wpw_ref, bpw_ref, qkvg_ref, yc_ref,
                        h0_ref, h1_ref, u_ref, gc_ref, gcp_ref, acc_ref):
    i = pl.program_id(0)
    tm = IN_TM

    @pl.when(i == 0)
    def _():
        u_ref[...] = jnp.zeros(u_ref.shape, u_ref.dtype)
        gc_ref[...] = jnp.zeros(gc_ref.shape, gc_ref.dtype)

    gcp_ref[...] = gc_ref[...]
    h = x_ref[...] * (1.0 + scale_ref[...]) + shift_ref[...]
    h0_ref[...] = h.astype(BF16)
    h_bufs = [h0_ref, h1_ref]
    state = {"cur": 0}

    def sync(tokens):
        src, dst = h_bufs[state["cur"]], h_bufs[1 - state["cur"]]
        _copy_after(src, dst, tokens)
        state["cur"] = 1 - state["cur"]

    def project(col):
        return jnp.dot(h_bufs[state["cur"]][...], w_ref[:, col:col + IN_CHUNK],
                       preferred_element_type=F32)

    pieces = [(c, r0) for c in range(D_CONV // LANES) for r0 in range(0, tm, TAP_ROWS)]
    n_early = D_QKVG // IN_CHUNK
    n_sync = n_early // SYNC_EVERY - 1
    per_sync = -(-len(pieces) // n_sync)
    for j in range(n_early):
        if j % SYNC_EVERY == 0 and 0 < j // SYNC_EVERY <= n_sync:
            s_idx = j // SYNC_EVERY - 1
            sync([_tap_piece(u_ref, cw_ref, cb_ref, acc_ref, c, r0)
                  for c, r0 in pieces[s_idx * per_sync:(s_idx + 1) * per_sync]])
        col = j * IN_CHUNK
        res = project(col)
        if col >= COL_G_ATTN:
            res = _silu(res)
        qkvg_ref[:, col:col + IN_CHUNK] = res.astype(BF16)

    v = acc_ref[...]
    mu = jnp.mean(v, axis=1, keepdims=True)
    d = v - mu
    var = jnp.mean(d * d, axis=1, keepdims=True)
    y = d * jax.lax.rsqrt(var + LN_EPS) * lng_ref[...] + lnb_ref[...]
    s = _silu(y).astype(BF16)
    z = jnp.dot(s, wpw_ref[...], preferred_element_type=F32) + bpw_ref[...]
    yc_ref[...] = (z * gcp_ref[...]).astype(BF16)
    tail_token = _token(z)

    n_gc = D_CONV // IN_CHUNK
    for c in range(n_gc):
        res = project(COL_G_CONV + c * IN_CHUNK)
        gc_ref[:, c * IN_CHUNK:(c + 1) * IN_CHUNK] = _silu(res)

    u_ref[:, 0:CONV_HALO, :] = u_ref[:, tm:tm + CONV_HALO, :]
    for c in range(D_CONV // IN_CHUNK):
        if c == 1:
            sync([tail_token])
        a = project(COL_GLU_A + c * IN_CHUNK)
        b = project(COL_GLU_B + c * IN_CHUNK)
        u = a * jax.nn.sigmoid(b)
        for t in range(IN_CHUNK // LANES):
            u_ref[c * (IN_CHUNK // LANES) + t, CONV_HALO:CONV_HALO + tm, :] = (
                u[:, t * LANES:(t + 1) * LANES])


def _inproj_conv_call(x2d, mod, w_in_bf16, conv_w, conv_b, ln_g, ln_b, w_pw_bf16, b_pw):
    tm = IN_TM
    n = SEQ // tm
    cur = lambda i: jnp.minimum(i, n - 1)
    lag = lambda i: jnp.maximum(i - 1, 0)
    row = lambda i: (0, 0)
    resident = dict(pipeline_mode=pl.Buffered(1))
    return pl.pallas_call(
        _inproj_conv_kernel,
        grid=(n + 1,),
        in_specs=[
            pl.BlockSpec((tm, D_MODEL), lambda i: (cur(i), 0)),
            pl.BlockSpec((1, D_MODEL), lambda i: (0, 0)),
            pl.BlockSpec((1, D_MODEL), lambda i: (0, 1)),
            pl.BlockSpec((D_MODEL, D_IN), row, **resident),
            pl.BlockSpec((CONV_WIDTH, D_CONV), row),
            pl.BlockSpec((1, D_CONV), row),
            pl.BlockSpec((1, D_CONV), row),
            pl.BlockSpec((1, D_CONV), row),
            pl.BlockSpec((D_CONV, D_CONV), row, **resident),
            pl.BlockSpec((1, D_CONV), row),
        ],
        out_specs=[
            pl.BlockSpec((tm, D_QKVG), lambda i: (cur(i), 0)),
            pl.BlockSpec((tm, D_CONV), lambda i: (lag(i), 0)),
        ],
        out_shape=[
            jax.ShapeDtypeStruct((SEQ, D_QKVG), BF16),
            jax.ShapeDtypeStruct((SEQ, D_CONV), BF16),
        ],
        scratch_shapes=[
            pltpu.VMEM((tm, D_MODEL), BF16),
            pltpu.VMEM((tm, D_MODEL), BF16),
            pltpu.VMEM((D_CONV // LANES, CONV_HALO + tm, LANES), F32),
            pltpu.VMEM((tm, D_CONV), F32),
            pltpu.VMEM((tm, D_CONV), F32),
            pltpu.VMEM((tm, D_CONV), F32),
        ],
        compiler_params=pltpu.CompilerParams(
            dimension_semantics=("arbitrary",), vmem_limit_bytes=VMEM_LIMIT),
        name="inproj_conv",
    )(x2d, mod, mod, w_in_bf16, conv_w, conv_b, ln_g, ln_b, w_pw_bf16, b_pw)


def kernel(x, c, w_ada, b_ada, w_in, rel_bias, sinks, conv_w, conv_b, conv_ln_g,
           conv_ln_b, w_pw, b_pw, w_out, ln_g, ln_b):
    assert x.shape == (1, SEQ, D_MODEL) and w_in.shape == (DEPTH, D_MODEL, D_IN)
    x2d = x.reshape(SEQ, D_MODEL)

    row = lambda v: v.reshape(1, -1)
    mod = _mod_call(c.reshape(D_MODEL, 1), w_ada.reshape(D_MODEL, 3 * D_MODEL), row(b_ada))
    bias = _bias_call(rel_bias, jnp.asarray(_t5_bucket_table()))
    qkvg, y_conv = _inproj_conv_call(
        x2d, mod, w_in.reshape(D_MODEL, D_IN).astype(BF16),
        conv_w.reshape(CONV_WIDTH, D_CONV), row(conv_b), row(conv_ln_g), row(conv_ln_b),
        w_pw.reshape(D_CONV, D_CONV).astype(BF16), row(b_pw))
    out = _attn_out_call(qkvg, bias, sinks.reshape(N_Q_HEADS), y_conv, x2d, mod,
                         w_out.reshape(D_MODEL, D_MODEL).astype(BF16), row(ln_g), row(ln_b))
    return out.reshape(1, SEQ, D_MODEL)
```

```python
import functools
import math

import jax
import jax.numpy as jnp
import numpy as np
from jax.experimental import pallas as pl
from jax.experimental.pallas import tpu as pltpu

F32 = jnp.float32
BF16 = jnp.bfloat16

D_MODEL = 2048
SEQ = 8192
HEAD_DIM = 64
D_ATTN = 1024
D_CONV = 1024
N_Q_HEADS = 16
N_KV_HEADS = 4
GQA_GROUP = 4
D_KV = 256
WINDOW = 128
BLOCK = 128
CONV_WIDTH = 31
N_BUCKETS = 32
MAX_DISTANCE = 128
LN_EPS = 1e-5
DEPTH = 1
ALPHA = (2.0 * DEPTH) ** 0.25
D_IN = 2 * D_ATTN + 2 * D_KV + 3 * D_CONV

LANES = 128
SUBLANES = 8

COL_Q = 0
COL_K = COL_Q + D_ATTN
COL_V = COL_K + D_KV
COL_G_ATTN = COL_V + D_KV
COL_GLU_A = COL_G_ATTN + D_ATTN
COL_GLU_B = COL_GLU_A + D_CONV
COL_G_CONV = COL_GLU_B + D_CONV
HALF = 512

VMEM_LIMIT = 56 * 1024 * 1024


def _silu(v):
    return v * jax.nn.sigmoid(v)


def _t5_bucket_table():
    qi = np.arange(BLOCK)[:, None]
    kj = np.arange(2 * BLOCK)[None, :]
    dist = np.maximum(qi + BLOCK - kj, 0)
    max_exact = N_BUCKETS // 2
    d = np.maximum(dist, 1).astype(np.float64)
    large = max_exact + (np.log(d / max_exact) / math.log(MAX_DISTANCE / max_exact)
                         * (N_BUCKETS - max_exact)).astype(np.int32)
    large = np.minimum(large, N_BUCKETS - 1)
    return np.where(dist < max_exact, dist, large).astype(np.int32)


def _mod_kernel(c_ref, w_ref, b_ref, o_ref):
    c = c_ref[...]
    ca = _silu(c)
    o_ref[...] = jnp.sum(ca * w_ref[...], axis=0, keepdims=True) + b_ref[...]


def _mod_call(c_col, w_ada, b_ada):
    tn = 1024
    n = w_ada.shape[1]
    return pl.pallas_call(
        _mod_kernel,
        grid=(n // tn,),
        in_specs=[
            pl.BlockSpec((D_MODEL, 1), lambda j: (0, 0)),
            pl.BlockSpec((D_MODEL, tn), lambda j: (0, j)),
            pl.BlockSpec((1, tn), lambda j: (0, j)),
        ],
        out_specs=pl.BlockSpec((1, tn), lambda j: (0, j)),
        out_shape=jax.ShapeDtypeStruct((1, n), F32),
        compiler_params=pltpu.CompilerParams(
            dimension_semantics=("arbitrary",), vmem_limit_bytes=VMEM_LIMIT),
        name="mod",
    )(c_col, w_ada, b_ada)


LOG2E = math.log2(math.e)
MASKED = -1e30


def _lane_buckets():
    m = np.arange(2 * BLOCK)
    dist = BLOCK - m
    ok = (dist >= 0) & (dist < WINDOW)
    max_exact = N_BUCKETS // 2
    d = np.maximum(dist, 1).astype(np.float64)
    large = max_exact + (np.log(d / max_exact) / math.log(MAX_DISTANCE / max_exact)
                         * (N_BUCKETS - max_exact)).astype(np.int32)
    large = np.minimum(large, N_BUCKETS - 1)
    bucket = np.where(dist < max_exact, dist, large)
    return np.where(ok, bucket, -1).astype(np.int32).reshape(1, -1)


def _bias_kernel(rb_ref, bm_ref, o_ref):
    bm = bm_ref[...]
    col = jax.lax.broadcasted_iota(jnp.int32, (BLOCK, 2 * BLOCK), 1)
    for h in range(N_Q_HEADS):
        t = jnp.full(bm.shape, MASKED, F32)
        for b in range(N_BUCKETS):
            t = jnp.where(bm == b, rb_ref[b, h] * LOG2E, t)
        x = pltpu.roll(jnp.broadcast_to(t, (BLOCK, 2 * BLOCK)), 0, 1, stride=1, stride_axis=0)
        o_ref[1, h] = x
        o_ref[0, h] = jnp.where(col >= BLOCK, x, MASKED)


def _bias_call(rel_bias):
    shape = (2, N_Q_HEADS, BLOCK, 2 * BLOCK)
    return pl.pallas_call(
        _bias_kernel,
        in_specs=[
            pl.BlockSpec(memory_space=pltpu.SMEM),
            pl.BlockSpec((1, 2 * BLOCK), lambda: (0, 0)),
        ],
        out_specs=pl.BlockSpec(shape, lambda: (0, 0, 0, 0)),
        out_shape=jax.ShapeDtypeStruct(shape, F32),
        name="bias",
    )(rel_bias, jnp.asarray(_lane_buckets()))


def _inproj_kernel(x_ref, shift_ref, scale_ref, w_ref, o_ref, h_ref):
    @pl.when(pl.program_id(1) == 0)
    def _():
        h = x_ref[...] * (1.0 + scale_ref[...]) + shift_ref[...]
        h_ref[...] = h.astype(BF16)

    o_ref[...] = jnp.dot(h_ref[...], w_ref[...],
                         preferred_element_type=F32).astype(BF16)


def _inproj_call(x2d, mod, w_in_bf16):
    tm, tn = 1024, 512
    return pl.pallas_call(
        _inproj_kernel,
        grid=(SEQ // tm, D_IN // tn),
        in_specs=[
            pl.BlockSpec((tm, D_MODEL), lambda i, j: (i, 0)),
            pl.BlockSpec((1, D_MODEL), lambda i, j: (0, 0)),
            pl.BlockSpec((1, D_MODEL), lambda i, j: (0, 1)),
            pl.BlockSpec((D_MODEL, tn), lambda i, j: (0, j)),
        ],
        out_specs=pl.BlockSpec((tm, tn), lambda i, j: (i, j)),
        out_shape=jax.ShapeDtypeStruct((SEQ, D_IN), BF16),
        scratch_shapes=[pltpu.VMEM((tm, D_MODEL), BF16)],
        compiler_params=pltpu.CompilerParams(
            dimension_semantics=("arbitrary", "arbitrary"),
            vmem_limit_bytes=VMEM_LIMIT),
        name="inproj",
    )(x2d, mod, mod, w_in_bf16)


def _swap_halves(t):
    return jnp.concatenate([t[:, HEAD_DIM:], t[:, :HEAD_DIM]], axis=1)


def _attn_unit(q_ref, g0_ref, g1_ref, k_tiles, v_tiles, bias_ref, sinks_ref,
               r0, tile, ya_ref):
    lane = jax.lax.broadcasted_iota(jnp.int32, (BLOCK, LANES), 1)
    low_half = lane < HEAD_DIM
    rows = slice(r0, r0 + BLOCK)
    sl = slice(tile * LANES, (tile + 1) * LANES)
    q_tile = q_ref[rows, sl] * (LOG2E * HEAD_DIM ** -0.5)
    scores = []
    for p in range(2):
        kv = (2 * tile + p) // GQA_GROUP
        kt = k_tiles[kv // 2][0 if (kv % 2) == p else 1]
        in_half = low_half if p == 0 else jnp.logical_not(low_half)
        qm = jnp.where(in_half, q_tile, jnp.zeros_like(q_tile))
        scores.append(jax.lax.dot_general(qm, kt, (((1,), (1,)), ((), ())),
                                          preferred_element_type=F32))
    outs = []
    for p in range(2):
        head = 2 * tile + p
        kv = head // GQA_GROUP
        vt = v_tiles[kv // 2][0 if (kv % 2) == p else 1]
        s = scores[p] + bias_ref[0, head]
        sink = sinks_ref[head] * LOG2E
        m = jnp.max(s, axis=1, keepdims=True)
        e = jnp.exp2(s - m)
        denom = jnp.sum(e, axis=1, keepdims=True) + jnp.exp2(sink - m)
        o = jnp.dot(e.astype(BF16), vt, preferred_element_type=F32)
        outs.append(o / denom)
    y = jnp.where(low_half, outs[0], outs[1])
    g_ref = g0_ref if tile < HALF // LANES else g1_ref
    gl = (tile * LANES) % HALF
    gate = g_ref[rows, gl:gl + LANES].astype(F32)
    ya_ref[rows, sl] = (y * gate).astype(BF16)


ATT_TM = 2 * BLOCK
OUT_CHUNK = 256
PACKED_ROWS = 16


def _attn_out_kernel(sinks_ref, q_ref, g0_ref, g1_ref, k_ref, v_ref, kp_ref, vp_ref,
                     bias0_ref, bias1_ref, yc_ref, x_ref, gate_ref, wa_ref, wc_ref,
                     lng_ref, lnb_ref, o_ref, ya_ref):
    i = pl.program_id(0)

    @pl.when(i == 0)
    def _():
        ya_ref[...] = jnp.zeros(ya_ref.shape, ya_ref.dtype)

    y = jnp.dot(ya_ref[...], wa_ref[...], preferred_element_type=F32)
    y = y + jnp.dot(yc_ref[...], wc_ref[...], preferred_element_type=F32)
    z = ALPHA * x_ref[...] + gate_ref[...] * y
    mu = jnp.mean(z, axis=1, keepdims=True)
    d = z - mu
    var = jnp.mean(d * d, axis=1, keepdims=True)
    o_ref[...] = d * jax.lax.rsqrt(var + LN_EPS) * lng_ref[...] + lnb_ref[...]

    _attn_tile(sinks_ref, q_ref, g0_ref, g1_ref, k_ref, v_ref, kp_ref, vp_ref,
               [bias0_ref, bias1_ref], ya_ref)


def _attn_tile(sinks_ref, q_ref, g0_ref, g1_ref, k_ref, v_ref, kp_ref, vp_ref,
               bias_refs, ya_ref):
    n_lt = D_KV // LANES
    lt = lambda t: slice(t * LANES, (t + 1) * LANES)

    def block_operands(b):
        r0 = b * BLOCK
        if b == 0:
            k_prev = [kp_ref[:, lt(t)] for t in range(n_lt)]
            v_prev = [vp_ref[:, lt(t)] for t in range(n_lt)]
        else:
            k_prev = [k_ref[r0 - BLOCK:r0, lt(t)] for t in range(n_lt)]
            v_prev = [v_ref[r0 - BLOCK:r0, lt(t)] for t in range(n_lt)]
        k_rows = [jnp.concatenate([k_prev[t], k_ref[r0:r0 + BLOCK, lt(t)]], axis=0)
                  for t in range(n_lt)]
        v_rows = [jnp.concatenate([v_prev[t], v_ref[r0:r0 + BLOCK, lt(t)]], axis=0)
                  for t in range(n_lt)]
        k_tiles = [(kt, _swap_halves(kt)) for kt in k_rows]
        v_tiles = [(vt, _swap_halves(vt)) for vt in v_rows]
        return k_tiles, v_tiles

    for b in range(ATT_TM // BLOCK):
        k_tiles, v_tiles = block_operands(b)
        for tile in range(D_ATTN // LANES):
            _attn_unit(q_ref, g0_ref, g1_ref, k_tiles, v_tiles, bias_refs[b], sinks_ref,
                       b * BLOCK, tile, ya_ref)


def _attn_out_call(proj, bias, sinks, y_conv, x2d, mod, w_out_bf16, ln_g, ln_b):
    tm = ATT_TM
    n = SEQ // tm
    clamp = lambda t: jnp.clip(t, 0, n - 1)
    cur = lambda i: clamp(i)
    lag = lambda i: clamp(i - 1)
    kv_prev = lambda i: jnp.maximum(cur(i) * (tm // BLOCK) - 1, 0)
    row = lambda i: (0, 0)
    return pl.pallas_call(
        _attn_out_kernel,
        grid=(n + 1,),
        in_specs=[
            pl.BlockSpec(memory_space=pltpu.SMEM),
            pl.BlockSpec((tm, D_ATTN), lambda i: (cur(i), COL_Q // D_ATTN)),
            pl.BlockSpec((tm, HALF), lambda i: (cur(i), COL_G_ATTN // HALF)),
            pl.BlockSpec((tm, HALF), lambda i: (cur(i), COL_G_ATTN // HALF + 1)),
            pl.BlockSpec((tm, D_KV), lambda i: (cur(i), COL_K // D_KV)),
            pl.BlockSpec((tm, D_KV), lambda i: (cur(i), COL_V // D_KV)),
            pl.BlockSpec((BLOCK, D_KV), lambda i: (kv_prev(i), COL_K // D_KV)),
            pl.BlockSpec((BLOCK, D_KV), lambda i: (kv_prev(i), COL_V // D_KV)),
            pl.BlockSpec((1, N_Q_HEADS, BLOCK, 2 * BLOCK),
                         lambda i: (jnp.minimum(cur(i), 1), 0, 0, 0)),
            pl.BlockSpec((1, N_Q_HEADS, BLOCK, 2 * BLOCK), lambda i: (1, 0, 0, 0)),
            pl.BlockSpec((tm, D_CONV), lambda i: (lag(i), 0)),
            pl.BlockSpec((tm, D_MODEL), lambda i: (lag(i), 0)),
            pl.BlockSpec((1, D_MODEL), lambda i: (0, 2)),
            pl.BlockSpec((D_ATTN, D_MODEL), lambda i: (0, 0)),
            pl.BlockSpec((D_CONV, D_MODEL), lambda i: (1, 0)),
            pl.BlockSpec((1, D_MODEL), row),
            pl.BlockSpec((1, D_MODEL), row),
        ],
        out_specs=pl.BlockSpec((tm, D_MODEL), lambda i: (lag(i), 0)),
        out_shape=jax.ShapeDtypeStruct((SEQ, D_MODEL), F32),
        scratch_shapes=[pltpu.VMEM((tm, D_ATTN), BF16)],
        compiler_params=pltpu.CompilerParams(
            dimension_semantics=("arbitrary",), vmem_limit_bytes=VMEM_LIMIT),
        name="attn_out",
    )(sinks, proj, proj, proj, proj, proj, proj, proj, bias, bias, y_conv, x2d, mod,
      w_out_bf16, w_out_bf16, ln_g, ln_b)


CONV_TM = 256
CONV_HALO = 32
CONV_RC = 64


def _conv_taps(u_ref, cw_ref, cb_ref, acc_ref, tm):
    off = CONV_HALO - (CONV_WIDTH - 1)
    for c in range(D_CONV // LANES):
        cs = slice(c * LANES, (c + 1) * LANES)
        w_rows = [cw_ref[k:k + 1, cs] for k in range(CONV_WIDTH)]
        for r0 in range(0, tm, CONV_RC):
            acc = jnp.broadcast_to(cb_ref[:, cs], (CONV_RC, LANES))
            for k in range(CONV_WIDTH):
                acc = acc + w_rows[k] * u_ref[c, r0 + off + k:r0 + off + k + CONV_RC, :]
            acc_ref[r0:r0 + CONV_RC, cs] = acc


def _conv_kernel(a0_ref, a1_ref, b0_ref, b1_ref, g0_ref, g1_ref, cw_ref, cb_ref,
                 lng_ref, lnb_ref, wpw_ref, bpw_ref, o_ref, u_ref, acc_ref):
    i = pl.program_id(0)
    tm = CONV_TM

    @pl.when(i == 0)
    def _():
        u_ref[:, 0:CONV_HALO, :] = jnp.zeros((D_CONV // LANES, CONV_HALO, LANES), F32)

    @pl.when(i > 0)
    def _():
        u_ref[:, 0:CONV_HALO, :] = u_ref[:, tm:tm + CONV_HALO, :]

    for a_ref, b_ref, c0 in ((a0_ref, b0_ref, 0), (a1_ref, b1_ref, HALF)):
        a = a_ref[...].astype(F32)
        b = b_ref[...].astype(F32)
        u = a * jax.nn.sigmoid(b)
        for c in range(HALF // LANES):
            u_ref[(c0 // LANES) + c, CONV_HALO:CONV_HALO + tm, :] = u[:, c * LANES:(c + 1) * LANES]

    _conv_taps(u_ref, cw_ref, cb_ref, acc_ref, tm)

    v = acc_ref[...]
    mu = jnp.mean(v, axis=1, keepdims=True)
    d = v - mu
    var = jnp.mean(d * d, axis=1, keepdims=True)
    y = d * jax.lax.rsqrt(var + LN_EPS) * lng_ref[...] + lnb_ref[...]
    s = _silu(y).astype(BF16)
    z = jnp.dot(s, wpw_ref[...], preferred_element_type=F32) + bpw_ref[...]
    for g_ref, c0 in ((g0_ref, 0), (g1_ref, HALF)):
        gate = _silu(g_ref[...].astype(F32))
        o_ref[:, c0:c0 + HALF] = (z[:, c0:c0 + HALF] * gate).astype(BF16)


def _conv_call(proj, conv_w, conv_b, ln_g, ln_b, w_pw_bf16, b_pw):
    tm = CONV_TM
    row = lambda i: (0, 0)
    half = lambda col, h: pl.BlockSpec((tm, HALF), lambda i: (i, col // HALF + h))
    return pl.pallas_call(
        _conv_kernel,
        grid=(SEQ // tm,),
        in_specs=[
            half(COL_GLU_A, 0), half(COL_GLU_A, 1),
            half(COL_GLU_B, 0), half(COL_GLU_B, 1),
            half(COL_G_CONV, 0), half(COL_G_CONV, 1),
            pl.BlockSpec((CONV_WIDTH, D_CONV), row),
            pl.BlockSpec((1, D_CONV), row),
            pl.BlockSpec((1, D_CONV), row),
            pl.BlockSpec((1, D_CONV), row),
            pl.BlockSpec((D_CONV, D_CONV), row),
            pl.BlockSpec((1, D_CONV), row),
        ],
        out_specs=pl.BlockSpec((tm, D_CONV), lambda i: (i, 0)),
        out_shape=jax.ShapeDtypeStruct((SEQ, D_CONV), BF16),
        scratch_shapes=[
            pltpu.VMEM((D_CONV // LANES, CONV_HALO + tm, LANES), F32),
            pltpu.VMEM((tm, D_CONV), F32),
        ],
        compiler_params=pltpu.CompilerParams(
            dimension_semantics=("arbitrary",), vmem_limit_bytes=VMEM_LIMIT),
        name="conv",
    )(proj, proj, proj, proj, proj, proj, conv_w, conv_b, ln_g, ln_b, w_pw_bf16, b_pw)


IN_TM = 256
IN_CHUNK = 256
D_QKVG = COL_GLU_A
TAP_ROWS = 64
SYNC_EVERY = 2
TAIL_SYNC_CHUNK = 2


def _token(v):
    t = v[0:SUBLANES, 0:LANES]
    for r in range(SUBLANES, v.shape[0], SUBLANES):
        t = t + v[r:r + SUBLANES, 0:LANES]
    return t


def _zero_of(token):
    return (pltpu.bitcast(token, jnp.uint32) >> 16) >> 16


def _copy_after(src_ref, dst_ref, tokens):
    zero = _zero_of(tokens[0])
    for t in tokens[1:]:
        zero = zero | _zero_of(t)
    bits = pltpu.bitcast(src_ref[...], jnp.uint32)
    reps = (bits.shape[0] // zero.shape[0], bits.shape[1] // zero.shape[1])
    dst_ref[...] = pltpu.bitcast(bits | jnp.tile(zero, reps), dst_ref.dtype)


def _tap_piece(u_ref, cw_ref, cb_ref, acc_ref, c, r0):
    off = CONV_HALO - (CONV_WIDTH - 1)
    cs = slice(c * LANES, (c + 1) * LANES)
    acc = jnp.broadcast_to(cb_ref[:, cs], (TAP_ROWS, LANES))
    for k in range(CONV_WIDTH):
        acc = acc + cw_ref[k:k + 1, cs] * u_ref[c, r0 + off + k:r0 + off + k + TAP_ROWS, :]
    acc_ref[r0:r0 + TAP_ROWS, cs] = acc
    return _token(acc)


def _inproj_conv_kernel(x_ref, shift_ref, scale_ref, w_ref, cw_ref, cb_ref, lng_ref,
                        lnb_ref, wpw_ref, bpw_ref, qkvg_ref, yc_ref,
                        h0_ref, h1_ref, u_ref, gc_ref, gcp_ref, acc_ref):
    i = pl.program_id(0)
    tm = IN_TM

    @pl.when(i == 0)
    def _():
        u_ref[...] = jnp.zeros(u_ref.shape, u_ref.dtype)
        gc_ref[...] = jnp.zeros(gc_ref.shape, gc_ref.dtype)

    gcp_ref[...] = gc_ref[...]
    h = x_ref[...] * (1.0 + scale_ref[...]) + shift_ref[...]
    h0_ref[...] = h.astype(BF16)
    h_bufs = [h0_ref, h1_ref]
    state = {"cur": 0}

    def sync(tokens):
        src, dst = h_bufs[state["cur"]], h_bufs[1 - state["cur"]]
        _copy_after(src, dst, tokens)
        state["cur"] = 1 - state["cur"]

    def project(col):
        return jnp.dot(h_bufs[state["cur"]][...], w_ref[:, col:col + IN_CHUNK],
                       preferred_element_type=F32)

    pieces = [(c, r0) for c in range(D_CONV // LANES) for r0 in range(0, tm, TAP_ROWS)]
    cols = list(range(0, D_QKVG, IN_CHUNK)) + list(range(COL_G_CONV, D_IN, IN_CHUNK))
    n_sync = len(cols) // SYNC_EVERY - 1
    per_sync = -(-len(pieces) // n_sync)
    for j, col in enumerate(cols):
        if j % SYNC_EVERY == 0 and 0 < j // SYNC_EVERY <= n_sync:
            s_idx = j // SYNC_EVERY - 1
            sync([_tap_piece(u_ref, cw_ref, cb_ref, acc_ref, c, r0)
                  for c, r0 in pieces[s_idx * per_sync:(s_idx + 1) * per_sync]])
        res = project(col)
        if col < COL_G_ATTN:
            qkvg_ref[:, col:col + IN_CHUNK] = res.astype(BF16)
        elif col < D_QKVG:
            qkvg_ref[:, col:col + IN_CHUNK] = _silu(res).astype(BF16)
        else:
            gc_ref[:, col - COL_G_CONV:col - COL_G_CONV + IN_CHUNK] = _silu(res)

    v = acc_ref[...]
    mu = jnp.mean(v, axis=1, keepdims=True)
    d = v - mu
    var = jnp.mean(d * d, axis=1, keepdims=True)
    y = d * jax.lax.rsqrt(var + LN_EPS) * lng_ref[...] + lnb_ref[...]
    s = _silu(y).astype(BF16)
    z = jnp.dot(s, wpw_ref[...], preferred_element_type=F32) + bpw_ref[...]
    yc_ref[...] = (z * gcp_ref[...]).astype(BF16)
    tail_token = _token(z)

    u_ref[:, 0:CONV_HALO, :] = u_ref[:, tm:tm + CONV_HALO, :]
    for c in range(D_CONV // IN_CHUNK):
        if c == TAIL_SYNC_CHUNK:
            sync([tail_token])
        a = project(COL_GLU_A + c * IN_CHUNK)
        b = project(COL_GLU_B + c * IN_CHUNK)
        u = a * jax.nn.sigmoid(b)
        for t in range(IN_CHUNK // LANES):
            u_ref[c * (IN_CHUNK // LANES) + t, CONV_HALO:CONV_HALO + tm, :] = (
                u[:, t * LANES:(t + 1) * LANES])


def _inproj_conv_call(x2d, mod, w_in_bf16, conv_w, conv_b, ln_g, ln_b, w_pw_bf16, b_pw):
    tm = IN_TM
    n = SEQ // tm
    cur = lambda i: jnp.minimum(i, n - 1)
    lag = lambda i: jnp.maximum(i - 1, 0)
    row = lambda i: (0, 0)
    resident = dict(pipeline_mode=pl.Buffered(1))
    return pl.pallas_call(
        _inproj_conv_kernel,
        grid=(n + 1,),
        in_specs=[
            pl.BlockSpec((tm, D_MODEL), lambda i: (cur(i), 0)),
            pl.BlockSpec((1, D_MODEL), lambda i: (0, 0)),
            pl.BlockSpec((1, D_MODEL), lambda i: (0, 1)),
            pl.BlockSpec((D_MODEL, D_IN), row, **resident),
            pl.BlockSpec((CONV_WIDTH, D_CONV), row),
            pl.BlockSpec((1, D_CONV), row),
            pl.BlockSpec((1, D_CONV), row),
            pl.BlockSpec((1, D_CONV), row),
            pl.BlockSpec((D_CONV, D_CONV), row, **resident),
            pl.BlockSpec((1, D_CONV), row),
        ],
        out_specs=[
            pl.BlockSpec((tm, D_QKVG), lambda i: (cur(i), 0)),
            pl.BlockSpec((tm, D_CONV), lambda i: (lag(i), 0)),
        ],
        out_shape=[
            jax.ShapeDtypeStruct((SEQ, D_QKVG), BF16),
            jax.ShapeDtypeStruct((SEQ, D_CONV), BF16),
        ],
        scratch_shapes=[
            pltpu.VMEM((tm, D_MODEL), BF16),
            pltpu.VMEM((tm, D_MODEL), BF16),
            pltpu.VMEM((D_CONV // LANES, CONV_HALO + tm, LANES), F32),
            pltpu.VMEM((tm, D_CONV), F32),
            pltpu.VMEM((tm, D_CONV), F32),
            pltpu.VMEM((tm, D_CONV), F32),
        ],
        compiler_params=pltpu.CompilerParams(
            dimension_semantics=("arbitrary",), vmem_limit_bytes=VMEM_LIMIT),
        name="inproj_conv",
    )(x2d, mod, mod, w_in_bf16, conv_w, conv_b, ln_g, ln_b, w_pw_bf16, b_pw)


def kernel(x, c, w_ada, b_ada, w_in, rel_bias, sinks, conv_w, conv_b, conv_ln_g,
           conv_ln_b, w_pw, b_pw, w_out, ln_g, ln_b):
    assert x.shape == (1, SEQ, D_MODEL) and w_in.shape == (DEPTH, D_MODEL, D_IN)
    x2d = x.reshape(SEQ, D_MODEL)

    row = lambda v: v.reshape(1, -1)
    mod = _mod_call(c.reshape(D_MODEL, 1), w_ada.reshape(D_MODEL, 3 * D_MODEL), row(b_ada))
    bias = _bias_call(rel_bias)
    qkvg, y_conv = _inproj_conv_call(
        x2d, mod, w_in.reshape(D_MODEL, D_IN).astype(BF16),
        conv_w.reshape(CONV_WIDTH, D_CONV), row(conv_b), row(conv_ln_g), row(conv_ln_b),
        w_pw.reshape(D_CONV, D_CONV).astype(BF16), row(b_pw))
    out = _attn_out_call(qkvg, bias, sinks.reshape(N_Q_HEADS), y_conv, x2d, mod,
                         w_out.reshape(D_MODEL, D_MODEL).astype(BF16), row(ln_g), row(ln_b))
    return out.reshape(1, SEQ, D_MODEL)
```

```python
import functools
import math

import jax
import jax.numpy as jnp
import numpy as np
from jax.experimental import pallas as pl
from jax.experimental.pallas import tpu as pltpu

F32 = jnp.float32
BF16 = jnp.bfloat16

D_MODEL = 2048
SEQ = 8192
HEAD_DIM = 64
D_ATTN = 1024
D_CONV = 1024
N_Q_HEADS = 16
N_KV_HEADS = 4
GQA_GROUP = 4
D_KV = 256
WINDOW = 128
BLOCK = 128
CONV_WIDTH = 31
N_BUCKETS = 32
MAX_DISTANCE = 128
LN_EPS = 1e-5
DEPTH = 1
ALPHA = (2.0 * DEPTH) ** 0.25
D_IN = 2 * D_ATTN + 2 * D_KV + 3 * D_CONV

LANES = 128
SUBLANES = 8

COL_Q = 0
COL_K = COL_Q + D_ATTN
COL_V = COL_K + D_KV
COL_G_ATTN = COL_V + D_KV
COL_GLU_A = COL_G_ATTN + D_ATTN
COL_GLU_B = COL_GLU_A + D_CONV
COL_G_CONV = COL_GLU_B + D_CONV
HALF = 512

VMEM_LIMIT = 56 * 1024 * 1024


def _silu(v):
    return v * jax.nn.sigmoid(v)


def _t5_bucket_table():
    qi = np.arange(BLOCK)[:, None]
    kj = np.arange(2 * BLOCK)[None, :]
    dist = np.maximum(qi + BLOCK - kj, 0)
    max_exact = N_BUCKETS // 2
    d = np.maximum(dist, 1).astype(np.float64)
    large = max_exact + (np.log(d / max_exact) / math.log(MAX_DISTANCE / max_exact)
                         * (N_BUCKETS - max_exact)).astype(np.int32)
    large = np.minimum(large, N_BUCKETS - 1)
    return np.where(dist < max_exact, dist, large).astype(np.int32)


def _mod_kernel(c_ref, w_ref, b_ref, o_ref):
    c = c_ref[...]
    ca = _silu(c)
    o_ref[...] = jnp.sum(ca * w_ref[...], axis=0, keepdims=True) + b_ref[...]


def _mod_call(c_col, w_ada, b_ada):
    tn = 1024
    n = w_ada.shape[1]
    return pl.pallas_call(
        _mod_kernel,
        grid=(n // tn,),
        in_specs=[
            pl.BlockSpec((D_MODEL, 1), lambda j: (0, 0)),
            pl.BlockSpec((D_MODEL, tn), lambda j: (0, j)),
            pl.BlockSpec((1, tn), lambda j: (0, j)),
        ],
        out_specs=pl.BlockSpec((1, tn), lambda j: (0, j)),
        out_shape=jax.ShapeDtypeStruct((1, n), F32),
        compiler_params=pltpu.CompilerParams(
            dimension_semantics=("arbitrary",), vmem_limit_bytes=VMEM_LIMIT),
        name="mod",
    )(c_col, w_ada, b_ada)


LOG2E = math.log2(math.e)
MASKED = -1e30


def _lane_buckets():
    m = np.arange(2 * BLOCK)
    dist = BLOCK - m
    ok = (dist >= 0) & (dist < WINDOW)
    max_exact = N_BUCKETS // 2
    d = np.maximum(dist, 1).astype(np.float64)
    large = max_exact + (np.log(d / max_exact) / math.log(MAX_DISTANCE / max_exact)
                         * (N_BUCKETS - max_exact)).astype(np.int32)
    large = np.minimum(large, N_BUCKETS - 1)
    bucket = np.where(dist < max_exact, dist, large)
    return np.where(ok, bucket, -1).astype(np.int32).reshape(1, -1)


def _bias_kernel(rb_ref, bm_ref, o_ref):
    bm = bm_ref[...]
    col = jax.lax.broadcasted_iota(jnp.int32, (BLOCK, 2 * BLOCK), 1)
    for h in range(N_Q_HEADS):
        t = jnp.full(bm.shape, MASKED, F32)
        for b in range(N_BUCKETS):
            t = jnp.where(bm == b, rb_ref[b, h] * LOG2E, t)
        x = pltpu.roll(jnp.broadcast_to(t, (BLOCK, 2 * BLOCK)), 0, 1, stride=1, stride_axis=0)
        o_ref[1, h] = x
        o_ref[0, h] = jnp.where(col >= BLOCK, x, MASKED)


def _bias_call(rel_bias):
    shape = (2, N_Q_HEADS, BLOCK, 2 * BLOCK)
    return pl.pallas_call(
        _bias_kernel,
        in_specs=[
            pl.BlockSpec(memory_space=pltpu.SMEM),
            pl.BlockSpec((1, 2 * BLOCK), lambda: (0, 0)),
        ],
        out_specs=pl.BlockSpec(shape, lambda: (0, 0, 0, 0)),
        out_shape=jax.ShapeDtypeStruct(shape, F32),
        name="bias",
    )(rel_bias, jnp.asarray(_lane_buckets()))


def _inproj_kernel(x_ref, shift_ref, scale_ref, w_ref, o_ref, h_ref):
    @pl.when(pl.program_id(1) == 0)
    def _():
        h = x_ref[...] * (1.0 + scale_ref[...]) + shift_ref[...]
        h_ref[...] = h.astype(BF16)

    o_ref[...] = jnp.dot(h_ref[...], w_ref[...],
                         preferred_element_type=F32).astype(BF16)


def _inproj_call(x2d, mod, w_in_bf16):
    tm, tn = 1024, 512
    return pl.pallas_call(
        _inproj_kernel,
        grid=(SEQ // tm, D_IN // tn),
        in_specs=[
            pl.BlockSpec((tm, D_MODEL), lambda i, j: (i, 0)),
            pl.BlockSpec((1, D_MODEL), lambda i, j: (0, 0)),
            pl.BlockSpec((1, D_MODEL), lambda i, j: (0, 1)),
            pl.BlockSpec((D_MODEL, tn), lambda i, j: (0, j)),
        ],
        out_specs=pl.BlockSpec((tm, tn), lambda i, j: (i, j)),
        out_shape=jax.ShapeDtypeStruct((SEQ, D_IN), BF16),
        scratch_shapes=[pltpu.VMEM((tm, D_MODEL), BF16)],
        compiler_params=pltpu.CompilerParams(
            dimension_semantics=("arbitrary", "arbitrary"),
            vmem_limit_bytes=VMEM_LIMIT),
        name="inproj",
    )(x2d, mod, mod, w_in_bf16)


def _swap_halves(t):
    return jnp.concatenate([t[:, HEAD_DIM:], t[:, :HEAD_DIM]], axis=1)


def _attn_unit(q_ref, g0_ref, g1_ref, k_tiles, v_tiles, bias_ref, sinks_ref,
               r0, tile, ya_ref):
    lane = jax.lax.broadcasted_iota(jnp.int32, (BLOCK, LANES), 1)
    low_half = lane < HEAD_DIM
    rows = slice(r0, r0 + BLOCK)
    sl = slice(tile * LANES, (tile + 1) * LANES)
    q_tile = q_ref[rows, sl] * (LOG2E * HEAD_DIM ** -0.5)
    scores = []
    for p in range(2):
        kv = (2 * tile + p) // GQA_GROUP
        kt = k_tiles[kv // 2][0 if (kv % 2) == p else 1]
        in_half = low_half if p == 0 else jnp.logical_not(low_half)
        qm = jnp.where(in_half, q_tile, jnp.zeros_like(q_tile))
        scores.append(jax.lax.dot_general(qm, kt, (((1,), (1,)), ((), ())),
                                          preferred_element_type=F32))
    outs = []
    for p in range(2):
        head = 2 * tile + p
        kv = head // GQA_GROUP
        vt = v_tiles[kv // 2][0 if (kv % 2) == p else 1]
        s = scores[p] + bias_ref[0, head]
        sink = sinks_ref[head] * LOG2E
        m = jnp.max(s, axis=1, keepdims=True)
        e = jnp.exp2(s - m)
        denom = jnp.sum(e, axis=1, keepdims=True) + jnp.exp2(sink - m)
        o = jnp.dot(e.astype(BF16), vt, preferred_element_type=F32)
        outs.append(o / denom)
    y = jnp.where(low_half, outs[0], outs[1])
    g_ref = g0_ref if tile < HALF // LANES else g1_ref
    gl = (tile * LANES) % HALF
    gate = g_ref[rows, gl:gl + LANES].astype(F32)
    ya_ref[rows, sl] = (y * gate).astype(BF16)


ATT_TM = 2 * BLOCK
OUT_CHUNK = 256
PACKED_ROWS = 16


def _attn_out_kernel(sinks_ref, q_ref, g0_ref, g1_ref, k_ref, v_ref, kp_ref, vp_ref,
                     bias0_ref, bias1_ref, yc_ref, x_ref, gate_ref, wa_ref, wc_ref,
                     lng_ref, lnb_ref, o_ref, ya_ref):
    i = pl.program_id(0)

    @pl.when(i == 0)
    def _():
        ya_ref[...] = jnp.zeros(ya_ref.shape, ya_ref.dtype)

    y = jnp.dot(ya_ref[...], wa_ref[...], preferred_element_type=F32)
    y = y + jnp.dot(yc_ref[...], wc_ref[...], preferred_element_type=F32)
    z = ALPHA * x_ref[...] + gate_ref[...] * y
    mu = jnp.mean(z, axis=1, keepdims=True)
    d = z - mu
    var = jnp.mean(d * d, axis=1, keepdims=True)
    o_ref[...] = d * jax.lax.rsqrt(var + LN_EPS) * lng_ref[...] + lnb_ref[...]

    _attn_tile(sinks_ref, q_ref, g0_ref, g1_ref, k_ref, v_ref, kp_ref, vp_ref,
               [bias0_ref, bias1_ref], ya_ref)


def _attn_tile(sinks_ref, q_ref, g0_ref, g1_ref, k_ref, v_ref, kp_ref, vp_ref,
               bias_refs, ya_ref):
    n_lt = D_KV // LANES
    lt = lambda t: slice(t * LANES, (t + 1) * LANES)

    def block_operands(b):
        r0 = b * BLOCK
        if b == 0:
            k_prev = [kp_ref[:, lt(t)] for t in range(n_lt)]
            v_prev = [vp_ref[:, lt(t)] for t in range(n_lt)]
        else:
            k_prev = [k_ref[r0 - BLOCK:r0, lt(t)] for t in range(n_lt)]
            v_prev = [v_ref[r0 - BLOCK:r0, lt(t)] for t in range(n_lt)]
        k_rows = [jnp.concatenate([k_prev[t], k_ref[r0:r0 + BLOCK, lt(t)]], axis=0)
                  for t in range(n_lt)]
        v_rows = [jnp.concatenate([v_prev[t], v_ref[r0:r0 + BLOCK, lt(t)]], axis=0)
                  for t in range(n_lt)]
        k_tiles = [(kt, _swap_halves(kt)) for kt in k_rows]
        v_tiles = [(vt, _swap_halves(vt)) for vt in v_rows]
        return k_tiles, v_tiles

    for b in range(ATT_TM // BLOCK):
        k_tiles, v_tiles = block_operands(b)
        for tile in range(D_ATTN // LANES):
            _attn_unit(q_ref, g0_ref, g1_ref, k_tiles, v_tiles, bias_refs[b], sinks_ref,
                       b * BLOCK, tile, ya_ref)


def _attn_out_call(proj, bias, sinks, y_conv, x2d, mod, w_out_bf16, ln_g, ln_b):
    tm = ATT_TM
    n = SEQ // tm
    clamp = lambda t: jnp.clip(t, 0, n - 1)
    cur = lambda i: clamp(i)
    lag = lambda i: clamp(i - 1)
    kv_prev = lambda i: jnp.maximum(cur(i) * (tm // BLOCK) - 1, 0)
    row = lambda i: (0, 0)
    return pl.pallas_call(
        _attn_out_kernel,
        grid=(n + 1,),
        in_specs=[
            pl.BlockSpec(memory_space=pltpu.SMEM),
            pl.BlockSpec((tm, D_ATTN), lambda i: (cur(i), COL_Q // D_ATTN)),
            pl.BlockSpec((tm, HALF), lambda i: (cur(i), COL_G_ATTN // HALF)),
            pl.BlockSpec((tm, HALF), lambda i: (cur(i), COL_G_ATTN // HALF + 1)),
            pl.BlockSpec((tm, D_KV), lambda i: (cur(i), COL_K // D_KV)),
            pl.BlockSpec((tm, D_KV), lambda i: (cur(i), COL_V // D_KV)),
            pl.BlockSpec((BLOCK, D_KV), lambda i: (kv_prev(i), COL_K // D_KV)),
            pl.BlockSpec((BLOCK, D_KV), lambda i: (kv_prev(i), COL_V // D_KV)),
            pl.BlockSpec((1, N_Q_HEADS, BLOCK, 2 * BLOCK),
                         lambda i: (jnp.minimum(cur(i), 1), 0, 0, 0)),
            pl.BlockSpec((1, N_Q_HEADS, BLOCK, 2 * BLOCK), lambda i: (1, 0, 0, 0)),
            pl.BlockSpec((tm, D_CONV), lambda i: (lag(i), 0)),
            pl.BlockSpec((tm, D_MODEL), lambda i: (lag(i), 0)),
            pl.BlockSpec((1, D_MODEL), lambda i: (0, 2)),
            pl.BlockSpec((D_ATTN, D_MODEL), lambda i: (0, 0)),
            pl.BlockSpec((D_CONV, D_MODEL), lambda i: (1, 0)),
            pl.BlockSpec((1, D_MODEL), row),
            pl.BlockSpec((1, D_MODEL), row),
        ],
        out_specs=pl.BlockSpec((tm, D_MODEL), lambda i: (lag(i), 0)),
        out_shape=jax.ShapeDtypeStruct((SEQ, D_MODEL), F32),
        scratch_shapes=[pltpu.VMEM((tm, D_ATTN), BF16)],
        compiler_params=pltpu.CompilerParams(
            dimension_semantics=("arbitrary",), vmem_limit_bytes=VMEM_LIMIT),
        name="attn_out",
    )(sinks, proj, proj, proj, proj, proj, proj, proj, bias, bias, y_conv, x2d, mod,
      w_out_bf16, w_out_bf16, ln_g, ln_b)


CONV_TM = 256
CONV_HALO = 32
CONV_RC = 64


def _conv_taps(u_ref, cw_ref, cb_ref, acc_ref, tm):
    off = CONV_HALO - (CONV_WIDTH - 1)
    for c in range(D_CONV // LANES):
        cs = slice(c * LANES, (c + 1) * LANES)
        w_rows = [cw_ref[k:k + 1, cs] for k in range(CONV_WIDTH)]
        for r0 in range(0, tm, CONV_RC):
            acc = jnp.broadcast_to(cb_ref[:, cs], (CONV_RC, LANES))
            for k in range(CONV_WIDTH):
                acc = acc + w_rows[k] * u_ref[c, r0 + off + k:r0 + off + k + CONV_RC, :]
            acc_ref[r0:r0 + CONV_RC, cs] = acc


def _conv_kernel(a0_ref, a1_ref, b0_ref, b1_ref, g0_ref, g1_ref, cw_ref, cb_ref,
                 lng_ref, lnb_ref, wpw_ref, bpw_ref, o_ref, u_ref, acc_ref):
    i = pl.program_id(0)
    tm = CONV_TM

    @pl.when(i == 0)
    def _():
        u_ref[:, 0:CONV_HALO, :] = jnp.zeros((D_CONV // LANES, CONV_HALO, LANES), F32)

    @pl.when(i > 0)
    def _():
        u_ref[:, 0:CONV_HALO, :] = u_ref[:, tm:tm + CONV_HALO, :]

    for a_ref, b_ref, c0 in ((a0_ref, b0_ref, 0), (a1_ref, b1_ref, HALF)):
        a = a_ref[...].astype(F32)
        b = b_ref[...].astype(F32)
        u = a * jax.nn.sigmoid(b)
        for c in range(HALF // LANES):
            u_ref[(c0 // LANES) + c, CONV_HALO:CONV_HALO + tm, :] = u[:, c * LANES:(c + 1) * LANES]

    _conv_taps(u_ref, cw_ref, cb_ref, acc_ref, tm)

    v = acc_ref[...]
    mu = jnp.mean(v, axis=1, keepdims=True)
    d = v - mu
    var = jnp.mean(d * d, axis=1, keepdims=True)
    y = d * jax.lax.rsqrt(var + LN_EPS) * lng_ref[...] + lnb_ref[...]
    s = _silu(y).astype(BF16)
    z = jnp.dot(s, wpw_ref[...], preferred_element_type=F32) + bpw_ref[...]
    for g_ref, c0 in ((g0_ref, 0), (g1_ref, HALF)):
        gate = _silu(g_ref[...].astype(F32))
        o_ref[:, c0:c0 + HALF] = (z[:, c0:c0 + HALF] * gate).astype(BF16)


def _conv_call(proj, conv_w, conv_b, ln_g, ln_b, w_pw_bf16, b_pw):
    tm = CONV_TM
    row = lambda i: (0, 0)
    half = lambda col, h: pl.BlockSpec((tm, HALF), lambda i: (i, col // HALF + h))
    return pl.pallas_call(
        _conv_kernel,
        grid=(SEQ // tm,),
        in_specs=[
            half(COL_GLU_A, 0), half(COL_GLU_A, 1),
            half(COL_GLU_B, 0), half(COL_GLU_B, 1),
            half(COL_G_CONV, 0), half(COL_G_CONV, 1),
            pl.BlockSpec((CONV_WIDTH, D_CONV), row),
            pl.BlockSpec((1, D_CONV), row),
            pl.BlockSpec((1, D_CONV), row),
            pl.BlockSpec((1, D_CONV), row),
            pl.BlockSpec((D_CONV, D_CONV), row),
            pl.BlockSpec((1, D_CONV), row),
        ],
        out_specs=pl.BlockSpec((tm, D_CONV), lambda i: (i, 0)),
        out_shape=jax.ShapeDtypeStruct((SEQ, D_CONV), BF16),
        scratch_shapes=[
            pltpu.VMEM((D_CONV // LANES, CONV_HALO + tm, LANES), F32),
            pltpu.VMEM((tm, D_CONV), F32),
        ],
        compiler_params=pltpu.CompilerParams(
            dimension_semantics=("arbitrary",), vmem_limit_bytes=VMEM_LIMIT),
        name="conv",
    )(proj, proj, proj, proj, proj, proj, conv_w, conv_b, ln_g, ln_b, w_pw_bf16, b_pw)


IN_TM = 256
IN_CHUNK = 256
D_QKVG = COL_GLU_A
TAP_ROWS = 64
SYNC_EVERY = 2
LAST_TAP_SYNC_CHUNK = 18


def _token(v):
    t = v[0:SUBLANES, 0:LANES]
    for r in range(SUBLANES, v.shape[0], SUBLANES):
        t = t + v[r:r + SUBLANES, 0:LANES]
    return t


def _zero_of(token):
    return (pltpu.bitcast(token, jnp.uint32) >> 16) >> 16


def _copy_after(src_ref, dst_ref, tokens):
    zero = _zero_of(tokens[0])
    for t in tokens[1:]:
        zero = zero | _zero_of(t)
    bits = pltpu.bitcast(src_ref[...], jnp.uint32)
    reps = (bits.shape[0] // zero.shape[0], bits.shape[1] // zero.shape[1])
    dst_ref[...] = pltpu.bitcast(bits | jnp.tile(zero, reps), dst_ref.dtype)


def _tap_piece(u_ref, cw_ref, cb_ref, acc_ref, c, r0, start_token=None):
    off = CONV_HALO - (CONV_WIDTH - 1)
    cs = slice(c * LANES, (c + 1) * LANES)
    init = jnp.broadcast_to(cb_ref[:, cs], (SUBLANES, LANES))
    if start_token is not None:
        init = pltpu.bitcast(pltpu.bitcast(init, jnp.uint32) | _zero_of(start_token), F32)
    acc = jnp.tile(init, (TAP_ROWS // SUBLANES, 1))
    for k in range(CONV_WIDTH):
        acc = acc + cw_ref[k:k + 1, cs] * u_ref[c, r0 + off + k:r0 + off + k + TAP_ROWS, :]
    acc_ref[r0:r0 + TAP_ROWS, cs] = acc
    return _token(acc)


def _conv_tail(acc_ref, lng_ref, lnb_ref, wpw_ref, bpw_ref, gate_ref, yc_ref):
    v = acc_ref[...]
    mu = jnp.mean(v, axis=1, keepdims=True)
    d = v - mu
    var = jnp.mean(d * d, axis=1, keepdims=True)
    y = d * jax.lax.rsqrt(var + LN_EPS) * lng_ref[...] + lnb_ref[...]
    s = _silu(y).astype(BF16)
    z = jnp.dot(s, wpw_ref[...], preferred_element_type=F32) + bpw_ref[...]
    yc_ref[...] = (z * gate_ref[...]).astype(BF16)
    return _token(z)


def _inproj_conv_kernel(x_ref, shift_ref, scale_ref, w_ref, cw_ref, cb_ref, lng_ref,
                        lnb_ref, wpw_ref, bpw_ref, qkvg_ref, yc_ref,
                        h0_ref, h1_ref, u_ref, gc_ref, gcp_ref, acc_ref):
    i = pl.program_id(0)
    n_tiles = pl.num_programs(0) - 1
    tm = IN_TM
    pieces = [(c, r0) for c in range(D_CONV // LANES) for r0 in range(0, tm, TAP_ROWS)]

    @pl.when(i == 0)
    def _():
        u_ref[...] = jnp.zeros(u_ref.shape, u_ref.dtype)
        gc_ref[...] = jnp.zeros(gc_ref.shape, gc_ref.dtype)

    if True:
        gcp_ref[...] = gc_ref[...]
        h = x_ref[...] * (1.0 + scale_ref[...]) + shift_ref[...]
        h0_ref[...] = h.astype(BF16)
        h_bufs = [h0_ref, h1_ref]
        state = {"cur": 0}

        def sync(tokens):
            src, dst = h_bufs[state["cur"]], h_bufs[1 - state["cur"]]
            _copy_after(src, dst, tokens)
            state["cur"] = 1 - state["cur"]

        def project(col):
            return jnp.dot(h_bufs[state["cur"]][...], w_ref[:, col:col + IN_CHUNK],
                           preferred_element_type=F32)

        cols = list(range(0, D_QKVG, IN_CHUNK)) + list(range(COL_G_CONV, D_IN, IN_CHUNK))
        for c in range(D_CONV // IN_CHUNK):
            cols += [COL_GLU_A + c * IN_CHUNK, COL_GLU_B + c * IN_CHUNK]
        sync_chunks = list(range(SYNC_EVERY, LAST_TAP_SYNC_CHUNK + 1, SYNC_EVERY))
        bounds = [round(s * len(pieces) / len(sync_chunks)) for s in range(len(sync_chunks) + 1)]
        glu_a = None
        pieces_done = 0
        for j, col in enumerate(cols):
            if j in sync_chunks:
                s_idx = sync_chunks.index(j)
                group = pieces[bounds[s_idx]:bounds[s_idx + 1]]
                sync([_tap_piece(u_ref, cw_ref, cb_ref, acc_ref, c, r0) for c, r0 in group])
                pieces_done = bounds[s_idx + 1]
                if pieces_done == len(pieces):
                    _conv_tail(acc_ref, lng_ref, lnb_ref, wpw_ref, bpw_ref, gcp_ref, yc_ref)
            res = project(col)
            if col < COL_G_ATTN:
                qkvg_ref[:, col:col + IN_CHUNK] = res.astype(BF16)
            elif col < D_QKVG:
                qkvg_ref[:, col:col + IN_CHUNK] = _silu(res).astype(BF16)
            elif col < COL_GLU_B:
                glu_a = res
            elif col < COL_G_CONV:
                u = glu_a * jax.nn.sigmoid(res)
                t0 = (col - COL_GLU_B) // LANES
                for t in range(t0, t0 + IN_CHUNK // LANES):
                    assert pieces_done >= (t + 1) * (tm // TAP_ROWS)
                    u_ref[t, 0:CONV_HALO, :] = u_ref[t, tm:tm + CONV_HALO, :]
                    u_ref[t, CONV_HALO:CONV_HALO + tm, :] = (
                        u[:, (t - t0) * LANES:(t - t0 + 1) * LANES])
            else:
                gc_ref[:, col - COL_G_CONV:col - COL_G_CONV + IN_CHUNK] = _silu(res)


def _inproj_conv_call(x2d, mod, w_in_bf16, conv_w, conv_b, ln_g, ln_b, w_pw_bf16, b_pw):
    tm = IN_TM
    n = SEQ // tm
    cur = lambda i: jnp.minimum(i, n - 1)
    lag = lambda i: jnp.maximum(i - 1, 0)
    row = lambda i: (0, 0)
    resident = dict(pipeline_mode=pl.Buffered(1))
    return pl.pallas_call(
        _inproj_conv_kernel,
        grid=(n + 1,),
        in_specs=[
            pl.BlockSpec((tm, D_MODEL), lambda i: (cur(i), 0)),
            pl.BlockSpec((1, D_MODEL), lambda i: (0, 0)),
            pl.BlockSpec((1, D_MODEL), lambda i: (0, 1)),
            pl.BlockSpec((D_MODEL, D_IN), row, **resident),
            pl.BlockSpec((CONV_WIDTH, D_CONV), row),
            pl.BlockSpec((1, D_CONV), row),
            pl.BlockSpec((1, D_CONV), row),
            pl.BlockSpec((1, D_CONV), row),
            pl.BlockSpec((D_CONV, D_CONV), row, **resident),
            pl.BlockSpec((1, D_CONV), row),
        ],
        out_specs=[
            pl.BlockSpec((tm, D_QKVG), lambda i: (cur(i), 0)),
            pl.BlockSpec((tm, D_CONV), lambda i: (lag(i), 0)),
        ],
        out_shape=[
            jax.ShapeDtypeStruct((SEQ, D_QKVG), BF16),
            jax.ShapeDtypeStruct((SEQ, D_CONV), BF16),
        ],
        scratch_shapes=[
            pltpu.VMEM((tm, D_MODEL), BF16),
            pltpu.VMEM((tm, D_MODEL), BF16),
            pltpu.VMEM((D_CONV // LANES, CONV_HALO + tm, LANES), F32),
            pltpu.VMEM((tm, D_CONV), F32),
            pltpu.VMEM((tm, D_CONV), F32),
            pltpu.VMEM((tm, D_CONV), F32),
        ],
        compiler_params=pltpu.CompilerParams(
            dimension_semantics=("arbitrary",), vmem_limit_bytes=VMEM_LIMIT),
        name="inproj_conv",
    )(x2d, mod, mod, w_in_bf16, conv_w, conv_b, ln_g, ln_b, w_pw_bf16, b_pw)


def kernel(x, c, w_ada, b_ada, w_in, rel_bias, sinks, conv_w, conv_b, conv_ln_g,
           conv_ln_b, w_pw, b_pw, w_out, ln_g, ln_b):
    assert x.shape == (1, SEQ, D_MODEL) and w_in.shape == (DEPTH, D_MODEL, D_IN)
    x2d = x.reshape(SEQ, D_MODEL)

    row = lambda v: v.reshape(1, -1)
    mod = _mod_call(c.reshape(D_MODEL, 1), w_ada.reshape(D_MODEL, 3 * D_MODEL), row(b_ada))
    bias = _bias_call(rel_bias)
    qkvg, y_conv = _inproj_conv_call(
        x2d, mod, w_in.reshape(D_MODEL, D_IN).astype(BF16),
        conv_w.reshape(CONV_WIDTH, D_CONV), row(conv_b), row(conv_ln_g), row(conv_ln_b),
        w_pw.reshape(D_CONV, D_CONV).astype(BF16), row(b_pw))
    out = _attn_out_call(qkvg, bias, sinks.reshape(N_Q_HEADS), y_conv, x2d, mod,
                         w_out.reshape(D_MODEL, D_MODEL).astype(BF16), row(ln_g), row(ln_b))
    return out.reshape(1, SEQ, D_MODEL)
```

```python
import functools
import math

import jax
import jax.numpy as jnp
import numpy as np
from jax.experimental import pallas as pl
from jax.experimental.pallas import tpu as pltpu

F32 = jnp.float32
BF16 = jnp.bfloat16

D_MODEL = 2048
SEQ = 8192
HEAD_DIM = 64
D_ATTN = 1024
D_CONV = 1024
N_Q_HEADS = 16
N_KV_HEADS = 4
GQA_GROUP = 4
D_KV = 256
WINDOW = 128
BLOCK = 128
CONV_WIDTH = 31
N_BUCKETS = 32
MAX_DISTANCE = 128
LN_EPS = 1e-5
DEPTH = 1
ALPHA = (2.0 * DEPTH) ** 0.25
D_IN = 2 * D_ATTN + 2 * D_KV + 3 * D_CONV

LANES = 128
SUBLANES = 8

COL_Q = 0
COL_K = COL_Q + D_ATTN
COL_V = COL_K + D_KV
COL_G_ATTN = COL_V + D_KV
COL_GLU_A = COL_G_ATTN + D_ATTN
COL_GLU_B = COL_GLU_A + D_CONV
COL_G_CONV = COL_GLU_B + D_CONV
HALF = 512

VMEM_LIMIT = 56 * 1024 * 1024


def _silu(v):
    return v * jax.nn.sigmoid(v)


def _t5_bucket_table():
    qi = np.arange(BLOCK)[:, None]
    kj = np.arange(2 * BLOCK)[None, :]
    dist = np.maximum(qi + BLOCK - kj, 0)
    max_exact = N_BUCKETS // 2
    d = np.maximum(dist, 1).astype(np.float64)
    large = max_exact + (np.log(d / max_exact) / math.log(MAX_DISTANCE / max_exact)
                         * (N_BUCKETS - max_exact)).astype(np.int32)
    large = np.minimum(large, N_BUCKETS - 1)
    return np.where(dist < max_exact, dist, large).astype(np.int32)


def _mod_kernel(c_ref, w_ref, b_ref, o_ref):
    c = c_ref[...]
    ca = _silu(c)
    o_ref[...] = jnp.sum(ca * w_ref[...], axis=0, keepdims=True) + b_ref[...]


def _mod_call(c_col, w_ada, b_ada):
    tn = 1024
    n = w_ada.shape[1]
    return pl.pallas_call(
        _mod_kernel,
        grid=(n // tn,),
        in_specs=[
            pl.BlockSpec((D_MODEL, 1), lambda j: (0, 0)),
            pl.BlockSpec((D_MODEL, tn), lambda j: (0, j)),
            pl.BlockSpec((1, tn), lambda j: (0, j)),
        ],
        out_specs=pl.BlockSpec((1, tn), lambda j: (0, j)),
        out_shape=jax.ShapeDtypeStruct((1, n), F32),
        compiler_params=pltpu.CompilerParams(
            dimension_semantics=("arbitrary",), vmem_limit_bytes=VMEM_LIMIT),
        name="mod",
    )(c_col, w_ada, b_ada)


LOG2E = math.log2(math.e)
Q_SCALE = LOG2E * HEAD_DIM ** -0.5
MASKED = -1e30


def _lane_buckets():
    m = np.arange(2 * BLOCK)
    dist = BLOCK - m
    ok = (dist >= 0) & (dist < WINDOW)
    max_exact = N_BUCKETS // 2
    d = np.maximum(dist, 1).astype(np.float64)
    large = max_exact + (np.log(d / max_exact) / math.log(MAX_DISTANCE / max_exact)
                         * (N_BUCKETS - max_exact)).astype(np.int32)
    large = np.minimum(large, N_BUCKETS - 1)
    bucket = np.where(dist < max_exact, dist, large)
    return np.where(ok, bucket, -1).astype(np.int32).reshape(1, -1)


def _bias_kernel(rb_ref, bm_ref, o_ref):
    bm = bm_ref[...]
    col = jax.lax.broadcasted_iota(jnp.int32, (BLOCK, 2 * BLOCK), 1)
    for h in range(N_Q_HEADS):
        t = jnp.full(bm.shape, MASKED, F32)
        for b in range(N_BUCKETS):
            t = jnp.where(bm == b, rb_ref[b, h] * LOG2E, t)
        x = pltpu.roll(jnp.broadcast_to(t, (BLOCK, 2 * BLOCK)), 0, 1, stride=1, stride_axis=0)
        o_ref[1, h] = x
        o_ref[0, h] = jnp.where(col >= BLOCK, x, MASKED)


def _bias_call(rel_bias):
    shape = (2, N_Q_HEADS, BLOCK, 2 * BLOCK)
    return pl.pallas_call(
        _bias_kernel,
        in_specs=[
            pl.BlockSpec(memory_space=pltpu.SMEM),
            pl.BlockSpec((1, 2 * BLOCK), lambda: (0, 0)),
        ],
        out_specs=pl.BlockSpec(shape, lambda: (0, 0, 0, 0)),
        out_shape=jax.ShapeDtypeStruct(shape, F32),
        name="bias",
    )(rel_bias, jnp.asarray(_lane_buckets()))


def _inproj_kernel(x_ref, shift_ref, scale_ref, w_ref, o_ref, h_ref):
    @pl.when(pl.program_id(1) == 0)
    def _():
        h = x_ref[...] * (1.0 + scale_ref[...]) + shift_ref[...]
        h_ref[...] = h.astype(BF16)

    o_ref[...] = jnp.dot(h_ref[...], w_ref[...],
                         preferred_element_type=F32).astype(BF16)


def _inproj_call(x2d, mod, w_in_bf16):
    tm, tn = 1024, 512
    return pl.pallas_call(
        _inproj_kernel,
        grid=(SEQ // tm, D_IN // tn),
        in_specs=[
            pl.BlockSpec((tm, D_MODEL), lambda i, j: (i, 0)),
            pl.BlockSpec((1, D_MODEL), lambda i, j: (0, 0)),
            pl.BlockSpec((1, D_MODEL), lambda i, j: (0, 1)),
            pl.BlockSpec((D_MODEL, tn), lambda i, j: (0, j)),
        ],
        out_specs=pl.BlockSpec((tm, tn), lambda i, j: (i, j)),
        out_shape=jax.ShapeDtypeStruct((SEQ, D_IN), BF16),
        scratch_shapes=[pltpu.VMEM((tm, D_MODEL), BF16)],
        compiler_params=pltpu.CompilerParams(
            dimension_semantics=("arbitrary", "arbitrary"),
            vmem_limit_bytes=VMEM_LIMIT),
        name="inproj",
    )(x2d, mod, mod, w_in_bf16)


def _dup_halves(t):
    swapped = jnp.concatenate([t[:, HEAD_DIM:], t[:, :HEAD_DIM]], axis=1)
    lane = jax.lax.broadcasted_iota(jnp.int32, t.shape, 1)
    low_half = lane < HEAD_DIM
    return jnp.where(low_half, t, swapped), jnp.where(low_half, swapped, t)


def _attn_group(q_ref, g0_ref, g1_ref, k_dup, v_dup, bias_ref, sinks_ref, r0, kv, ya_ref):
    lane = jax.lax.broadcasted_iota(jnp.int32, (BLOCK, LANES), 1)
    low_half = lane < HEAD_DIM
    rows = slice(r0, r0 + BLOCK)
    tiles = [GQA_GROUP * kv // 2 + t for t in range(GQA_GROUP // 2)]
    stacked = []
    for tile in tiles:
        q_tile = q_ref[rows, tile * LANES:(tile + 1) * LANES]
        zero = jnp.zeros_like(q_tile)
        stacked += [jnp.where(low_half, q_tile, zero), jnp.where(low_half, zero, q_tile)]
    s_all = jax.lax.dot_general(jnp.concatenate(stacked, axis=0), k_dup,
                                (((1,), (1,)), ((), ())),
                                preferred_element_type=F32)
    probs, denoms = [], []
    for r in range(GQA_GROUP):
        head = GQA_GROUP * kv + r
        s = s_all[r * BLOCK:(r + 1) * BLOCK] + bias_ref[0, head]
        sink = sinks_ref[head] * LOG2E
        m = jnp.max(s, axis=1, keepdims=True)
        e = jnp.exp2(s - m)
        denoms.append(jnp.sum(e, axis=1, keepdims=True) + jnp.exp2(sink - m))
        probs.append(e.astype(BF16))
    o_all = jnp.dot(jnp.concatenate(probs, axis=0), v_dup,
                    preferred_element_type=F32)
    outs = [o_all[r * BLOCK:(r + 1) * BLOCK] / denoms[r] for r in range(GQA_GROUP)]
    for t, tile in enumerate(tiles):
        y = jnp.where(low_half, outs[2 * t], outs[2 * t + 1])
        g_ref = g0_ref if tile < HALF // LANES else g1_ref
        gl = (tile * LANES) % HALF
        gate = g_ref[rows, gl:gl + LANES].astype(F32)
        ya_ref[rows, tile * LANES:(tile + 1) * LANES] = (y * gate).astype(BF16)


ATT_TM = 2 * BLOCK
OUT_CHUNK = 256
PACKED_ROWS = 16


def _attn_out_kernel(sinks_ref, q_ref, g0_ref, g1_ref, k_ref, v_ref, kp_ref, vp_ref,
                     bias0_ref, bias1_ref, yc_ref, x_ref, gate_ref, wa_ref, wc_ref,
                     lng_ref, lnb_ref, o_ref, ya_ref):
    i = pl.program_id(0)

    @pl.when(i == 0)
    def _():
        ya_ref[...] = jnp.zeros(ya_ref.shape, ya_ref.dtype)

    y = jnp.dot(ya_ref[...], wa_ref[...], preferred_element_type=F32)
    y = y + jnp.dot(yc_ref[...], wc_ref[...], preferred_element_type=F32)
    z = ALPHA * x_ref[...] + gate_ref[...] * y
    mu = jnp.mean(z, axis=1, keepdims=True)
    d = z - mu
    var = jnp.mean(d * d, axis=1, keepdims=True)
    o_ref[...] = d * jax.lax.rsqrt(var + LN_EPS) * lng_ref[...] + lnb_ref[...]

    _attn_tile(sinks_ref, q_ref, g0_ref, g1_ref, k_ref, v_ref, kp_ref, vp_ref,
               [bias0_ref, bias1_ref], ya_ref)


def _attn_tile(sinks_ref, q_ref, g0_ref, g1_ref, k_ref, v_ref, kp_ref, vp_ref,
               bias_refs, ya_ref):
    n_lt = D_KV // LANES
    lt = lambda t: slice(t * LANES, (t + 1) * LANES)

    def block_operands(b):
        r0 = b * BLOCK
        if b == 0:
            k_prev = [kp_ref[:, lt(t)] for t in range(n_lt)]
            v_prev = [vp_ref[:, lt(t)] for t in range(n_lt)]
        else:
            k_prev = [k_ref[r0 - BLOCK:r0, lt(t)] for t in range(n_lt)]
            v_prev = [v_ref[r0 - BLOCK:r0, lt(t)] for t in range(n_lt)]
        k_rows = [jnp.concatenate([k_prev[t], k_ref[r0:r0 + BLOCK, lt(t)]], axis=0)
                  for t in range(n_lt)]
        v_rows = [jnp.concatenate([v_prev[t], v_ref[r0:r0 + BLOCK, lt(t)]], axis=0)
                  for t in range(n_lt)]
        return ([_dup_halves(kt) for kt in k_rows], [_dup_halves(vt) for vt in v_rows])

    for b in range(ATT_TM // BLOCK):
        k_dups, v_dups = block_operands(b)
        for kv in range(N_KV_HEADS):
            _attn_group(q_ref, g0_ref, g1_ref, k_dups[kv // 2][kv % 2], v_dups[kv // 2][kv % 2],
                        bias_refs[b], sinks_ref, b * BLOCK, kv, ya_ref)


def _attn_out_call(proj, bias, sinks, y_conv, x2d, mod, w_out_bf16, ln_g, ln_b):
    tm = ATT_TM
    n = SEQ // tm
    clamp = lambda t: jnp.clip(t, 0, n - 1)
    cur = lambda i: clamp(i)
    lag = lambda i: clamp(i - 1)
    kv_prev = lambda i: jnp.maximum(cur(i) * (tm // BLOCK) - 1, 0)
    row = lambda i: (0, 0)
    return pl.pallas_call(
        _attn_out_kernel,
        grid=(n + 1,),
        in_specs=[
            pl.BlockSpec(memory_space=pltpu.SMEM),
            pl.BlockSpec((tm, D_ATTN), lambda i: (cur(i), COL_Q // D_ATTN)),
            pl.BlockSpec((tm, HALF), lambda i: (cur(i), COL_G_ATTN // HALF)),
            pl.BlockSpec((tm, HALF), lambda i: (cur(i), COL_G_ATTN // HALF + 1)),
            pl.BlockSpec((tm, D_KV), lambda i: (cur(i), COL_K // D_KV)),
            pl.BlockSpec((tm, D_KV), lambda i: (cur(i), COL_V // D_KV)),
            pl.BlockSpec((BLOCK, D_KV), lambda i: (kv_prev(i), COL_K // D_KV)),
            pl.BlockSpec((BLOCK, D_KV), lambda i: (kv_prev(i), COL_V // D_KV)),
            pl.BlockSpec((1, N_Q_HEADS, BLOCK, 2 * BLOCK),
                         lambda i: (jnp.minimum(cur(i), 1), 0, 0, 0)),
            pl.BlockSpec((1, N_Q_HEADS, BLOCK, 2 * BLOCK), lambda i: (1, 0, 0, 0)),
            pl.BlockSpec((tm, D_CONV), lambda i: (lag(i), 0)),
            pl.BlockSpec((tm, D_MODEL), lambda i: (lag(i), 0)),
            pl.BlockSpec((1, D_MODEL), lambda i: (0, 2)),
            pl.BlockSpec((D_ATTN, D_MODEL), lambda i: (0, 0)),
            pl.BlockSpec((D_CONV, D_MODEL), lambda i: (1, 0)),
            pl.BlockSpec((1, D_MODEL), row),
            pl.BlockSpec((1, D_MODEL), row),
        ],
        out_specs=pl.BlockSpec((tm, D_MODEL), lambda i: (lag(i), 0)),
        out_shape=jax.ShapeDtypeStruct((SEQ, D_MODEL), F32),
        scratch_shapes=[pltpu.VMEM((tm, D_ATTN), BF16)],
        compiler_params=pltpu.CompilerParams(
            dimension_semantics=("arbitrary",), vmem_limit_bytes=VMEM_LIMIT),
        name="attn_out",
    )(sinks, proj, proj, proj, proj, proj, proj, proj, bias, bias, y_conv, x2d, mod,
      w_out_bf16, w_out_bf16, ln_g, ln_b)


CONV_TM = 256
CONV_HALO = 32
CONV_RC = 64


def _conv_taps(u_ref, cw_ref, cb_ref, acc_ref, tm):
    off = CONV_HALO - (CONV_WIDTH - 1)
    for c in range(D_CONV // LANES):
        cs = slice(c * LANES, (c + 1) * LANES)
        w_rows = [cw_ref[k:k + 1, cs] for k in range(CONV_WIDTH)]
        for r0 in range(0, tm, CONV_RC):
            acc = jnp.broadcast_to(cb_ref[:, cs], (CONV_RC, LANES))
            for k in range(CONV_WIDTH):
                acc = acc + w_rows[k] * u_ref[c, r0 + off + k:r0 + off + k + CONV_RC, :]
            acc_ref[r0:r0 + CONV_RC, cs] = acc


def _conv_kernel(a0_ref, a1_ref, b0_ref, b1_ref, g0_ref, g1_ref, cw_ref, cb_ref,
                 lng_ref, lnb_ref, wpw_ref, bpw_ref, o_ref, u_ref, acc_ref):
    i = pl.program_id(0)
    tm = CONV_TM

    @pl.when(i == 0)
    def _():
        u_ref[:, 0:CONV_HALO, :] = jnp.zeros((D_CONV // LANES, CONV_HALO, LANES), F32)

    @pl.when(i > 0)
    def _():
        u_ref[:, 0:CONV_HALO, :] = u_ref[:, tm:tm + CONV_HALO, :]

    for a_ref, b_ref, c0 in ((a0_ref, b0_ref, 0), (a1_ref, b1_ref, HALF)):
        a = a_ref[...].astype(F32)
        b = b_ref[...].astype(F32)
        u = a * jax.nn.sigmoid(b)
        for c in range(HALF // LANES):
            u_ref[(c0 // LANES) + c, CONV_HALO:CONV_HALO + tm, :] = u[:, c * LANES:(c + 1) * LANES]

    _conv_taps(u_ref, cw_ref, cb_ref, acc_ref, tm)

    v = acc_ref[...]
    mu = jnp.mean(v, axis=1, keepdims=True)
    d = v - mu
    var = jnp.mean(d * d, axis=1, keepdims=True)
    y = d * jax.lax.rsqrt(var + LN_EPS) * lng_ref[...] + lnb_ref[...]
    s = _silu(y).astype(BF16)
    z = jnp.dot(s, wpw_ref[...], preferred_element_type=F32) + bpw_ref[...]
    for g_ref, c0 in ((g0_ref, 0), (g1_ref, HALF)):
        gate = _silu(g_ref[...].astype(F32))
        o_ref[:, c0:c0 + HALF] = (z[:, c0:c0 + HALF] * gate).astype(BF16)


def _conv_call(proj, conv_w, conv_b, ln_g, ln_b, w_pw_bf16, b_pw):
    tm = CONV_TM
    row = lambda i: (0, 0)
    half = lambda col, h: pl.BlockSpec((tm, HALF), lambda i: (i, col // HALF + h))
    return pl.pallas_call(
        _conv_kernel,
        grid=(SEQ // tm,),
        in_specs=[
            half(COL_GLU_A, 0), half(COL_GLU_A, 1),
            half(COL_GLU_B, 0), half(COL_GLU_B, 1),
            half(COL_G_CONV, 0), half(COL_G_CONV, 1),
            pl.BlockSpec((CONV_WIDTH, D_CONV), row),
            pl.BlockSpec((1, D_CONV), row),
            pl.BlockSpec((1, D_CONV), row),
            pl.BlockSpec((1, D_CONV), row),
            pl.BlockSpec((D_CONV, D_CONV), row),
            pl.BlockSpec((1, D_CONV), row),
        ],
        out_specs=pl.BlockSpec((tm, D_CONV), lambda i: (i, 0)),
        out_shape=jax.ShapeDtypeStruct((SEQ, D_CONV), BF16),
        scratch_shapes=[
            pltpu.VMEM((D_CONV // LANES, CONV_HALO + tm, LANES), F32),
            pltpu.VMEM((tm, D_CONV), F32),
        ],
        compiler_params=pltpu.CompilerParams(
            dimension_semantics=("arbitrary",), vmem_limit_bytes=VMEM_LIMIT),
        name="conv",
    )(proj, proj, proj, proj, proj, proj, conv_w, conv_b, ln_g, ln_b, w_pw_bf16, b_pw)


IN_TM = 256
IN_CHUNK = 256
D_QKVG = COL_GLU_A
TAP_ROWS = 64
SYNC_EVERY = 2
LAST_TAP_SYNC_CHUNK = 18


def _token(v):
    t = v[0:SUBLANES, 0:LANES]
    for r in range(SUBLANES, v.shape[0], SUBLANES):
        t = t + v[r:r + SUBLANES, 0:LANES]
    return t


def _zero_of(token):
    return (pltpu.bitcast(token, jnp.uint32) >> 16) >> 16


def _copy_after(src_ref, dst_ref, tokens):
    zero = _zero_of(tokens[0])
    for t in tokens[1:]:
        zero = zero | _zero_of(t)
    bits = pltpu.bitcast(src_ref[...], jnp.uint32)
    reps = (bits.shape[0] // zero.shape[0], bits.shape[1] // zero.shape[1])
    dst_ref[...] = pltpu.bitcast(bits | jnp.tile(zero, reps), dst_ref.dtype)


def _tap_piece(u_ref, cw_ref, cb_ref, acc_ref, c, r0, start_token=None):
    off = CONV_HALO - (CONV_WIDTH - 1)
    cs = slice(c * LANES, (c + 1) * LANES)
    init = jnp.broadcast_to(cb_ref[:, cs], (SUBLANES, LANES))
    if start_token is not None:
        init = pltpu.bitcast(pltpu.bitcast(init, jnp.uint32) | _zero_of(start_token), F32)
    acc = jnp.tile(init, (TAP_ROWS // SUBLANES, 1))
    for k in range(CONV_WIDTH):
        acc = acc + cw_ref[k:k + 1, cs] * u_ref[c, r0 + off + k:r0 + off + k + TAP_ROWS, :]
    acc_ref[r0:r0 + TAP_ROWS, cs] = acc
    return _token(acc)


def _conv_tail(acc_ref, lng_ref, lnb_ref, wpw_ref, bpw_ref, gate_ref, yc_ref):
    v = acc_ref[...]
    mu = jnp.mean(v, axis=1, keepdims=True)
    d = v - mu
    var = jnp.mean(d * d, axis=1, keepdims=True)
    y = d * jax.lax.rsqrt(var + LN_EPS) * lng_ref[...] + lnb_ref[...]
    s = _silu(y).astype(BF16)
    z = jnp.dot(s, wpw_ref[...], preferred_element_type=F32) + bpw_ref[...]
    yc_ref[...] = (z * gate_ref[...]).astype(BF16)
    return _token(z)


def _inproj_conv_kernel(x_ref, shift_ref, scale_ref, w_ref, cw_ref, cb_ref, lng_ref,
                        lnb_ref, wpw_ref, bpw_ref, qkvg_ref, yc_ref,
                        h0_ref, h1_ref, u_ref, gc_ref, gcp_ref, acc_ref):
    i = pl.program_id(0)
    n_tiles = pl.num_programs(0) - 1
    tm = IN_TM
    pieces = [(c, r0) for c in range(D_CONV // LANES) for r0 in range(0, tm, TAP_ROWS)]

    @pl.when(i == 0)
    def _():
        u_ref[...] = jnp.zeros(u_ref.shape, u_ref.dtype)
        gc_ref[...] = jnp.zeros(gc_ref.shape, gc_ref.dtype)

    if True:
        gcp_ref[...] = gc_ref[...]
        h = x_ref[...] * (1.0 + scale_ref[...]) + shift_ref[...]
        h0_ref[...] = h.astype(BF16)
        h_bufs = [h0_ref, h1_ref]
        state = {"cur": 0}

        def sync(tokens):
            src, dst = h_bufs[state["cur"]], h_bufs[1 - state["cur"]]
            _copy_after(src, dst, tokens)
            state["cur"] = 1 - state["cur"]

        def project(col):
            return jnp.dot(h_bufs[state["cur"]][...], w_ref[:, col:col + IN_CHUNK],
                           preferred_element_type=F32)

        cols = list(range(0, D_QKVG, IN_CHUNK)) + list(range(COL_G_CONV, D_IN, IN_CHUNK))
        for c in range(D_CONV // IN_CHUNK):
            cols += [COL_GLU_A + c * IN_CHUNK, COL_GLU_B + c * IN_CHUNK]
        sync_chunks = list(range(SYNC_EVERY, LAST_TAP_SYNC_CHUNK + 1, SYNC_EVERY))
        bounds = [round(s * len(pieces) / len(sync_chunks)) for s in range(len(sync_chunks) + 1)]
        glu_a = None
        pieces_done = 0
        for j, col in enumerate(cols):
            if j in sync_chunks:
                s_idx = sync_chunks.index(j)
                group = pieces[bounds[s_idx]:bounds[s_idx + 1]]
                sync([_tap_piece(u_ref, cw_ref, cb_ref, acc_ref, c, r0) for c, r0 in group])
                pieces_done = bounds[s_idx + 1]
                if pieces_done == len(pieces):
                    _conv_tail(acc_ref, lng_ref, lnb_ref, wpw_ref, bpw_ref, gcp_ref, yc_ref)
            res = project(col)
            if col < COL_K:
                qkvg_ref[:, col:col + IN_CHUNK] = (res * Q_SCALE).astype(BF16)
            elif col < COL_G_ATTN:
                qkvg_ref[:, col:col + IN_CHUNK] = res.astype(BF16)
            elif col < D_QKVG:
                qkvg_ref[:, col:col + IN_CHUNK] = _silu(res).astype(BF16)
            elif col < COL_GLU_B:
                glu_a = res
            elif col < COL_G_CONV:
                u = glu_a * jax.nn.sigmoid(res)
                t0 = (col - COL_GLU_B) // LANES
                for t in range(t0, t0 + IN_CHUNK // LANES):
                    assert pieces_done >= (t + 1) * (tm // TAP_ROWS)
                    u_ref[t, 0:CONV_HALO, :] = u_ref[t, tm:tm + CONV_HALO, :]
                    u_ref[t, CONV_HALO:CONV_HALO + tm, :] = (
                        u[:, (t - t0) * LANES:(t - t0 + 1) * LANES])
            else:
                gc_ref[:, col - COL_G_CONV:col - COL_G_CONV + IN_CHUNK] = _silu(res)


def _inproj_conv_call(x2d, mod, w_in_bf16, conv_w, conv_b, ln_g, ln_b, w_pw_bf16, b_pw):
    tm = IN_TM
    n = SEQ // tm
    cur = lambda i: jnp.minimum(i, n - 1)
    lag = lambda i: jnp.maximum(i - 1, 0)
    row = lambda i: (0, 0)
    resident = dict(pipeline_mode=pl.Buffered(1))
    return pl.pallas_call(
        _inproj_conv_kernel,
        grid=(n + 1,),
        in_specs=[
            pl.BlockSpec((tm, D_MODEL), lambda i: (cur(i), 0)),
            pl.BlockSpec((1, D_MODEL), lambda i: (0, 0)),
            pl.BlockSpec((1, D_MODEL), lambda i: (0, 1)),
            pl.BlockSpec((D_MODEL, D_IN), row, **resident),
            pl.BlockSpec((CONV_WIDTH, D_CONV), row),
            pl.BlockSpec((1, D_CONV), row),
            pl.BlockSpec((1, D_CONV), row),
            pl.BlockSpec((1, D_CONV), row),
            pl.BlockSpec((D_CONV, D_CONV), row, **resident),
            pl.BlockSpec((1, D_CONV), row),
        ],
        out_specs=[
            pl.BlockSpec((tm, D_QKVG), lambda i: (cur(i), 0)),
            pl.BlockSpec((tm, D_CONV), lambda i: (lag(i), 0)),
        ],
        out_shape=[
            jax.ShapeDtypeStruct((SEQ, D_QKVG), BF16),
            jax.ShapeDtypeStruct((SEQ, D_CONV), BF16),
        ],
        scratch_shapes=[
            pltpu.VMEM((tm, D_MODEL), BF16),
            pltpu.VMEM((tm, D_MODEL), BF16),
            pltpu.VMEM((D_CONV // LANES, CONV_HALO + tm, LANES), F32),
            pltpu.VMEM((tm, D_CONV), F32),
            pltpu.VMEM((tm, D_CONV), F32),
            pltpu.VMEM((tm, D_CONV), F32),
        ],
        compiler_params=pltpu.CompilerParams(
            dimension_semantics=("arbitrary",), vmem_limit_bytes=VMEM_LIMIT),
        name="inproj_conv",
    )(x2d, mod, mod, w_in_bf16, conv_w, conv_b, ln_g, ln_b, w_pw_bf16, b_pw)


def kernel(x, c, w_ada, b_ada, w_in, rel_bias, sinks, conv_w, conv_b, conv_ln_g,
           conv_ln_b, w_pw, b_pw, w_out, ln_g, ln_b):
    assert x.shape == (1, SEQ, D_MODEL) and w_in.shape == (DEPTH, D_MODEL, D_IN)
    x2d = x.reshape(SEQ, D_MODEL)

    row = lambda v: v.reshape(1, -1)
    mod = _mod_call(c.reshape(D_MODEL, 1), w_ada.reshape(D_MODEL, 3 * D_MODEL), row(b_ada))
    bias = _bias_call(rel_bias)
    qkvg, y_conv = _inproj_conv_call(
        x2d, mod, w_in.reshape(D_MODEL, D_IN).astype(BF16),
        conv_w.reshape(CONV_WIDTH, D_CONV), row(conv_b), row(conv_ln_g), row(conv_ln_b),
        w_pw.reshape(D_CONV, D_CONV).astype(BF16), row(b_pw))
    out = _attn_out_call(qkvg, bias, sinks.reshape(N_Q_HEADS), y_conv, x2d, mod,
                         w_out.reshape(D_MODEL, D_MODEL).astype(BF16), row(ln_g), row(ln_b))
    return out.reshape(1, SEQ, D_MODEL)
```

```python
import functools
import math

import jax
import jax.numpy as jnp
import numpy as np
from jax.experimental import pallas as pl
from jax.experimental.pallas import tpu as pltpu

F32 = jnp.float32
BF16 = jnp.bfloat16

D_MODEL = 2048
SEQ = 8192
HEAD_DIM = 64
D_ATTN = 1024
D_CONV = 1024
N_Q_HEADS = 16
N_KV_HEADS = 4
GQA_GROUP = 4
D_KV = 256
WINDOW = 128
BLOCK = 128
CONV_WIDTH = 31
N_BUCKETS = 32
MAX_DISTANCE = 128
LN_EPS = 1e-5
DEPTH = 1
ALPHA = (2.0 * DEPTH) ** 0.25
D_IN = 2 * D_ATTN + 2 * D_KV + 3 * D_CONV

LANES = 128
SUBLANES = 8

COL_Q = 0
COL_K = COL_Q + D_ATTN
COL_V = COL_K + D_KV
COL_G_ATTN = COL_V + D_KV
COL_GLU_A = COL_G_ATTN + D_ATTN
COL_GLU_B = COL_GLU_A + D_CONV
COL_G_CONV = COL_GLU_B + D_CONV
HALF = 512

VMEM_LIMIT = 56 * 1024 * 1024


def _silu(v):
    return v * jax.nn.sigmoid(v)


def _t5_bucket_table():
    qi = np.arange(BLOCK)[:, None]
    kj = np.arange(2 * BLOCK)[None, :]
    dist = np.maximum(qi + BLOCK - kj, 0)
    max_exact = N_BUCKETS // 2
    d = np.maximum(dist, 1).astype(np.float64)
    large = max_exact + (np.log(d / max_exact) / math.log(MAX_DISTANCE / max_exact)
                         * (N_BUCKETS - max_exact)).astype(np.int32)
    large = np.minimum(large, N_BUCKETS - 1)
    return np.where(dist < max_exact, dist, large).astype(np.int32)


def _mod_kernel(c_ref, w_ref, b_ref, o_ref):
    c = c_ref[...]
    ca = _silu(c)
    o_ref[...] = jnp.sum(ca * w_ref[...], axis=0, keepdims=True) + b_ref[...]


def _mod_call(c_col, w_ada, b_ada):
    tn = 1024
    n = w_ada.shape[1]
    return pl.pallas_call(
        _mod_kernel,
        grid=(n // tn,),
        in_specs=[
            pl.BlockSpec((D_MODEL, 1), lambda j: (0, 0)),
            pl.BlockSpec((D_MODEL, tn), lambda j: (0, j)),
            pl.BlockSpec((1, tn), lambda j: (0, j)),
        ],
        out_specs=pl.BlockSpec((1, tn), lambda j: (0, j)),
        out_shape=jax.ShapeDtypeStruct((1, n), F32),
        compiler_params=pltpu.CompilerParams(
            dimension_semantics=("arbitrary",), vmem_limit_bytes=VMEM_LIMIT),
        name="mod",
    )(c_col, w_ada, b_ada)


LOG2E = math.log2(math.e)
Q_SCALE = LOG2E * HEAD_DIM ** -0.5
MASKED = -1e30


def _lane_buckets():
    m = np.arange(2 * BLOCK)
    dist = BLOCK - m
    ok = (dist >= 0) & (dist < WINDOW)
    max_exact = N_BUCKETS // 2
    d = np.maximum(dist, 1).astype(np.float64)
    large = max_exact + (np.log(d / max_exact) / math.log(MAX_DISTANCE / max_exact)
                         * (N_BUCKETS - max_exact)).astype(np.int32)
    large = np.minimum(large, N_BUCKETS - 1)
    bucket = np.where(dist < max_exact, dist, large)
    return np.where(ok, bucket, -1).astype(np.int32).reshape(1, -1)


def _bias_kernel(rb_ref, bm_ref, o_ref):
    bm = bm_ref[...]
    col = jax.lax.broadcasted_iota(jnp.int32, (BLOCK, 2 * BLOCK), 1)
    for h in range(N_Q_HEADS):
        t = jnp.full(bm.shape, MASKED, F32)
        for b in range(N_BUCKETS):
            t = jnp.where(bm == b, rb_ref[b, h] * LOG2E, t)
        x = pltpu.roll(jnp.broadcast_to(t, (BLOCK, 2 * BLOCK)), 0, 1, stride=1, stride_axis=0)
        o_ref[1, h] = x
        o_ref[0, h] = jnp.where(col >= BLOCK, x, MASKED)


def _bias_call(rel_bias):
    shape = (2, N_Q_HEADS, BLOCK, 2 * BLOCK)
    return pl.pallas_call(
        _bias_kernel,
        in_specs=[
            pl.BlockSpec(memory_space=pltpu.SMEM),
            pl.BlockSpec((1, 2 * BLOCK), lambda: (0, 0)),
        ],
        out_specs=pl.BlockSpec(shape, lambda: (0, 0, 0, 0)),
        out_shape=jax.ShapeDtypeStruct(shape, F32),
        name="bias",
    )(rel_bias, jnp.asarray(_lane_buckets()))


def _inproj_kernel(x_ref, shift_ref, scale_ref, w_ref, o_ref, h_ref):
    @pl.when(pl.program_id(1) == 0)
    def _():
        h = x_ref[...] * (1.0 + scale_ref[...]) + shift_ref[...]
        h_ref[...] = h.astype(BF16)

    o_ref[...] = jnp.dot(h_ref[...], w_ref[...],
                         preferred_element_type=F32).astype(BF16)


def _inproj_call(x2d, mod, w_in_bf16):
    tm, tn = 1024, 512
    return pl.pallas_call(
        _inproj_kernel,
        grid=(SEQ // tm, D_IN // tn),
        in_specs=[
            pl.BlockSpec((tm, D_MODEL), lambda i, j: (i, 0)),
            pl.BlockSpec((1, D_MODEL), lambda i, j: (0, 0)),
            pl.BlockSpec((1, D_MODEL), lambda i, j: (0, 1)),
            pl.BlockSpec((D_MODEL, tn), lambda i, j: (0, j)),
        ],
        out_specs=pl.BlockSpec((tm, tn), lambda i, j: (i, j)),
        out_shape=jax.ShapeDtypeStruct((SEQ, D_IN), BF16),
        scratch_shapes=[pltpu.VMEM((tm, D_MODEL), BF16)],
        compiler_params=pltpu.CompilerParams(
            dimension_semantics=("arbitrary", "arbitrary"),
            vmem_limit_bytes=VMEM_LIMIT),
        name="inproj",
    )(x2d, mod, mod, w_in_bf16)


def _dup_halves(t):
    swapped = jnp.concatenate([t[:, HEAD_DIM:], t[:, :HEAD_DIM]], axis=1)
    lane = jax.lax.broadcasted_iota(jnp.int32, t.shape, 1)
    low_half = lane < HEAD_DIM
    return jnp.where(low_half, t, swapped), jnp.where(low_half, swapped, t)


def _attn_group(q_ref, g0_ref, g1_ref, k_dup, v_dup, bias_ref, sinks_ref, r0, kv, ya_ref):
    lane = jax.lax.broadcasted_iota(jnp.int32, (BLOCK, LANES), 1)
    low_half = lane < HEAD_DIM
    rows = slice(r0, r0 + BLOCK)
    tiles = [GQA_GROUP * kv // 2 + t for t in range(GQA_GROUP // 2)]
    stacked = []
    for tile in tiles:
        q_tile = q_ref[rows, tile * LANES:(tile + 1) * LANES]
        zero = jnp.zeros_like(q_tile)
        stacked += [jnp.where(low_half, q_tile, zero), jnp.where(low_half, zero, q_tile)]
    s_all = jax.lax.dot_general(jnp.concatenate(stacked, axis=0), k_dup,
                                (((1,), (1,)), ((), ())),
                                preferred_element_type=F32)
    probs, denoms = [], []
    for r in range(GQA_GROUP):
        head = GQA_GROUP * kv + r
        s = s_all[r * BLOCK:(r + 1) * BLOCK] + bias_ref[0, head]
        sink = sinks_ref[head] * LOG2E
        m = jnp.max(s, axis=1, keepdims=True)
        e = jnp.exp2(s - m)
        denoms.append(jnp.sum(e, axis=1, keepdims=True) + jnp.exp2(sink - m))
        probs.append(e.astype(BF16))
    o_all = jnp.dot(jnp.concatenate(probs, axis=0), v_dup,
                    preferred_element_type=F32)
    outs = [o_all[r * BLOCK:(r + 1) * BLOCK] / denoms[r] for r in range(GQA_GROUP)]
    for t, tile in enumerate(tiles):
        y = jnp.where(low_half, outs[2 * t], outs[2 * t + 1])
        g_ref = g0_ref if tile < HALF // LANES else g1_ref
        gl = (tile * LANES) % HALF
        gate = g_ref[rows, gl:gl + LANES].astype(F32)
        ya_ref[rows, tile * LANES:(tile + 1) * LANES] = (y * gate).astype(BF16)


ATT_TM = 2 * BLOCK
OUT_CHUNK = 256
PACKED_ROWS = 16


def _attn_out_kernel(sinks_ref, q_ref, g0_ref, g1_ref, k_ref, v_ref, kp_ref, vp_ref,
                     bias0_ref, bias1_ref, yc_ref, x_ref, gate_ref, wout_hbm,
                     lng_ref, lnb_ref, o_ref, ya_ref, wout_ref, stage_ref, sem_ref):
    i = pl.program_id(0)

    @pl.when(i == 0)
    def _():
        ya_ref[...] = jnp.zeros(ya_ref.shape, ya_ref.dtype)
        _load_weight_bf16(wout_hbm, wout_ref, stage_ref, sem_ref)

    y = jnp.dot(ya_ref[...], wout_ref[0:D_ATTN, :], preferred_element_type=F32)
    y = y + jnp.dot(yc_ref[...], wout_ref[D_ATTN:D_MODEL, :], preferred_element_type=F32)
    z = ALPHA * x_ref[...] + gate_ref[...] * y
    mu = jnp.mean(z, axis=1, keepdims=True)
    d = z - mu
    var = jnp.mean(d * d, axis=1, keepdims=True)
    o_ref[...] = d * jax.lax.rsqrt(var + LN_EPS) * lng_ref[...] + lnb_ref[...]

    _attn_tile(sinks_ref, q_ref, g0_ref, g1_ref, k_ref, v_ref, kp_ref, vp_ref,
               [bias0_ref, bias1_ref], ya_ref)


def _attn_tile(sinks_ref, q_ref, g0_ref, g1_ref, k_ref, v_ref, kp_ref, vp_ref,
               bias_refs, ya_ref):
    n_lt = D_KV // LANES
    lt = lambda t: slice(t * LANES, (t + 1) * LANES)

    def block_operands(b):
        r0 = b * BLOCK
        if b == 0:
            k_prev = [kp_ref[:, lt(t)] for t in range(n_lt)]
            v_prev = [vp_ref[:, lt(t)] for t in range(n_lt)]
        else:
            k_prev = [k_ref[r0 - BLOCK:r0, lt(t)] for t in range(n_lt)]
            v_prev = [v_ref[r0 - BLOCK:r0, lt(t)] for t in range(n_lt)]
        k_rows = [jnp.concatenate([k_prev[t], k_ref[r0:r0 + BLOCK, lt(t)]], axis=0)
                  for t in range(n_lt)]
        v_rows = [jnp.concatenate([v_prev[t], v_ref[r0:r0 + BLOCK, lt(t)]], axis=0)
                  for t in range(n_lt)]
        return ([_dup_halves(kt) for kt in k_rows], [_dup_halves(vt) for vt in v_rows])

    for b in range(ATT_TM // BLOCK):
        k_dups, v_dups = block_operands(b)
        for kv in range(N_KV_HEADS):
            _attn_group(q_ref, g0_ref, g1_ref, k_dups[kv // 2][kv % 2], v_dups[kv // 2][kv % 2],
                        bias_refs[b], sinks_ref, b * BLOCK, kv, ya_ref)


def _attn_out_call(proj, bias, sinks, y_conv, x2d, mod, w_out, ln_g, ln_b):
    tm = ATT_TM
    n = SEQ // tm
    clamp = lambda t: jnp.clip(t, 0, n - 1)
    cur = lambda i: clamp(i)
    lag = lambda i: clamp(i - 1)
    kv_prev = lambda i: jnp.maximum(cur(i) * (tm // BLOCK) - 1, 0)
    row = lambda i: (0, 0)
    return pl.pallas_call(
        _attn_out_kernel,
        grid=(n + 1,),
        in_specs=[
            pl.BlockSpec(memory_space=pltpu.SMEM),
            pl.BlockSpec((tm, D_ATTN), lambda i: (cur(i), COL_Q // D_ATTN)),
            pl.BlockSpec((tm, HALF), lambda i: (cur(i), COL_G_ATTN // HALF)),
            pl.BlockSpec((tm, HALF), lambda i: (cur(i), COL_G_ATTN // HALF + 1)),
            pl.BlockSpec((tm, D_KV), lambda i: (cur(i), COL_K // D_KV)),
            pl.BlockSpec((tm, D_KV), lambda i: (cur(i), COL_V // D_KV)),
            pl.BlockSpec((BLOCK, D_KV), lambda i: (kv_prev(i), COL_K // D_KV)),
            pl.BlockSpec((BLOCK, D_KV), lambda i: (kv_prev(i), COL_V // D_KV)),
            pl.BlockSpec((1, N_Q_HEADS, BLOCK, 2 * BLOCK),
                         lambda i: (jnp.minimum(cur(i), 1), 0, 0, 0)),
            pl.BlockSpec((1, N_Q_HEADS, BLOCK, 2 * BLOCK), lambda i: (1, 0, 0, 0)),
            pl.BlockSpec((tm, D_CONV), lambda i: (lag(i), 0)),
            pl.BlockSpec((tm, D_MODEL), lambda i: (lag(i), 0)),
            pl.BlockSpec((1, D_MODEL), lambda i: (0, 2)),
            pl.BlockSpec(memory_space=pl.ANY),
            pl.BlockSpec((1, D_MODEL), row),
            pl.BlockSpec((1, D_MODEL), row),
        ],
        out_specs=pl.BlockSpec((tm, D_MODEL), lambda i: (lag(i), 0)),
        out_shape=jax.ShapeDtypeStruct((SEQ, D_MODEL), F32),
        scratch_shapes=[
            pltpu.VMEM((tm, D_ATTN), BF16),
            pltpu.VMEM((D_MODEL, D_MODEL), BF16),
            pltpu.VMEM((2, D_MODEL, W_STAGE_COLS), F32),
            pltpu.SemaphoreType.DMA((2,)),
        ],
        compiler_params=pltpu.CompilerParams(
            dimension_semantics=("arbitrary",), vmem_limit_bytes=VMEM_LIMIT),
        name="attn_out",
    )(sinks, proj, proj, proj, proj, proj, proj, proj, bias, bias, y_conv, x2d, mod,
      w_out, ln_g, ln_b)


CONV_TM = 256
CONV_HALO = 32
CONV_RC = 64


def _conv_taps(u_ref, cw_ref, cb_ref, acc_ref, tm):
    off = CONV_HALO - (CONV_WIDTH - 1)
    for c in range(D_CONV // LANES):
        cs = slice(c * LANES, (c + 1) * LANES)
        w_rows = [cw_ref[k:k + 1, cs] for k in range(CONV_WIDTH)]
        for r0 in range(0, tm, CONV_RC):
            acc = jnp.broadcast_to(cb_ref[:, cs], (CONV_RC, LANES))
            for k in range(CONV_WIDTH):
                acc = acc + w_rows[k] * u_ref[c, r0 + off + k:r0 + off + k + CONV_RC, :]
            acc_ref[r0:r0 + CONV_RC, cs] = acc


def _conv_kernel(a0_ref, a1_ref, b0_ref, b1_ref, g0_ref, g1_ref, cw_ref, cb_ref,
                 lng_ref, lnb_ref, wpw_ref, bpw_ref, o_ref, u_ref, acc_ref):
    i = pl.program_id(0)
    tm = CONV_TM

    @pl.when(i == 0)
    def _():
        u_ref[:, 0:CONV_HALO, :] = jnp.zeros((D_CONV // LANES, CONV_HALO, LANES), F32)

    @pl.when(i > 0)
    def _():
        u_ref[:, 0:CONV_HALO, :] = u_ref[:, tm:tm + CONV_HALO, :]

    for a_ref, b_ref, c0 in ((a0_ref, b0_ref, 0), (a1_ref, b1_ref, HALF)):
        a = a_ref[...].astype(F32)
        b = b_ref[...].astype(F32)
        u = a * jax.nn.sigmoid(b)
        for c in range(HALF // LANES):
            u_ref[(c0 // LANES) + c, CONV_HALO:CONV_HALO + tm, :] = u[:, c * LANES:(c + 1) * LANES]

    _conv_taps(u_ref, cw_ref, cb_ref, acc_ref, tm)

    v = acc_ref[...]
    mu = jnp.mean(v, axis=1, keepdims=True)
    d = v - mu
    var = jnp.mean(d * d, axis=1, keepdims=True)
    y = d * jax.lax.rsqrt(var + LN_EPS) * lng_ref[...] + lnb_ref[...]
    s = _silu(y).astype(BF16)
    z = jnp.dot(s, wpw_ref[...], preferred_element_type=F32) + bpw_ref[...]
    for g_ref, c0 in ((g0_ref, 0), (g1_ref, HALF)):
        gate = _silu(g_ref[...].astype(F32))
        o_ref[:, c0:c0 + HALF] = (z[:, c0:c0 + HALF] * gate).astype(BF16)


def _conv_call(proj, conv_w, conv_b, ln_g, ln_b, w_pw_bf16, b_pw):
    tm = CONV_TM
    row = lambda i: (0, 0)
    half = lambda col, h: pl.BlockSpec((tm, HALF), lambda i: (i, col // HALF + h))
    return pl.pallas_call(
        _conv_kernel,
        grid=(SEQ // tm,),
        in_specs=[
            half(COL_GLU_A, 0), half(COL_GLU_A, 1),
            half(COL_GLU_B, 0), half(COL_GLU_B, 1),
            half(COL_G_CONV, 0), half(COL_G_CONV, 1),
            pl.BlockSpec((CONV_WIDTH, D_CONV), row),
            pl.BlockSpec((1, D_CONV), row),
            pl.BlockSpec((1, D_CONV), row),
            pl.BlockSpec((1, D_CONV), row),
            pl.BlockSpec((D_CONV, D_CONV), row),
            pl.BlockSpec((1, D_CONV), row),
        ],
        out_specs=pl.BlockSpec((tm, D_CONV), lambda i: (i, 0)),
        out_shape=jax.ShapeDtypeStruct((SEQ, D_CONV), BF16),
        scratch_shapes=[
            pltpu.VMEM((D_CONV // LANES, CONV_HALO + tm, LANES), F32),
            pltpu.VMEM((tm, D_CONV), F32),
        ],
        compiler_params=pltpu.CompilerParams(
            dimension_semantics=("arbitrary",), vmem_limit_bytes=VMEM_LIMIT),
        name="conv",
    )(proj, proj, proj, proj, proj, proj, conv_w, conv_b, ln_g, ln_b, w_pw_bf16, b_pw)


IN_TM = 256
IN_CHUNK = 256
D_QKVG = COL_GLU_A
TAP_ROWS = 64
SYNC_EVERY = 2
LAST_TAP_SYNC_CHUNK = 18


def _token(v):
    t = v[0:SUBLANES, 0:LANES]
    for r in range(SUBLANES, v.shape[0], SUBLANES):
        t = t + v[r:r + SUBLANES, 0:LANES]
    return t


def _zero_of(token):
    return (pltpu.bitcast(token, jnp.uint32) >> 16) >> 16


def _copy_after(src_ref, dst_ref, tokens):
    zero = _zero_of(tokens[0])
    for t in tokens[1:]:
        zero = zero | _zero_of(t)
    bits = pltpu.bitcast(src_ref[...], jnp.uint32)
    reps = (bits.shape[0] // zero.shape[0], bits.shape[1] // zero.shape[1])
    dst_ref[...] = pltpu.bitcast(bits | jnp.tile(zero, reps), dst_ref.dtype)


def _tap_piece(u_ref, cw_ref, cb_ref, acc_ref, c, r0, start_token=None):
    off = CONV_HALO - (CONV_WIDTH - 1)
    cs = slice(c * LANES, (c + 1) * LANES)
    init = jnp.broadcast_to(cb_ref[:, cs], (SUBLANES, LANES))
    if start_token is not None:
        init = pltpu.bitcast(pltpu.bitcast(init, jnp.uint32) | _zero_of(start_token), F32)
    acc = jnp.tile(init, (TAP_ROWS // SUBLANES, 1))
    for k in range(CONV_WIDTH):
        acc = acc + cw_ref[k:k + 1, cs] * u_ref[c, r0 + off + k:r0 + off + k + TAP_ROWS, :]
    acc_ref[r0:r0 + TAP_ROWS, cs] = acc
    return _token(acc)


def _conv_tail(acc_ref, lng_ref, lnb_ref, wpw_ref, bpw_ref, gate_ref, yc_ref):
    v = acc_ref[...]
    mu = jnp.mean(v, axis=1, keepdims=True)
    d = v - mu
    var = jnp.mean(d * d, axis=1, keepdims=True)
    y = d * jax.lax.rsqrt(var + LN_EPS) * lng_ref[...] + lnb_ref[...]
    s = _silu(y).astype(BF16)
    z = jnp.dot(s, wpw_ref[...], preferred_element_type=F32) + bpw_ref[...]
    yc_ref[...] = (z * gate_ref[...]).astype(BF16)
    return _token(z)


W_STAGE_COLS = 256


def _load_weight_bf16(w_hbm, w_ref, stage_ref, sem_ref):
    rows, cols = w_ref.shape
    n = cols // W_STAGE_COLS

    def chunk_copy(j):
        slot = j % 2
        return pltpu.make_async_copy(
            w_hbm.at[:, pl.ds(j * W_STAGE_COLS, W_STAGE_COLS)],
            stage_ref.at[slot, pl.ds(0, rows), :], sem_ref.at[slot])

    chunk_copy(0).start()
    for j in range(n):
        if j + 1 < n:
            chunk_copy(j + 1).start()
        chunk_copy(j).wait()
        w_ref[:, j * W_STAGE_COLS:(j + 1) * W_STAGE_COLS] = (
            stage_ref[j % 2, 0:rows, :].astype(BF16))


def _inproj_conv_kernel(x_ref, shift_ref, scale_ref, w_hbm, cw_ref, cb_ref, lng_ref,
                        lnb_ref, wpw_hbm, bpw_ref, qkvg_ref, yc_ref,
                        h0_ref, h1_ref, u_ref, gc_ref, gcp_ref, acc_ref,
                        w_ref, wpw_ref, stage_ref, sem_ref):
    i = pl.program_id(0)
    n_tiles = pl.num_programs(0) - 1
    tm = IN_TM
    pieces = [(c, r0) for c in range(D_CONV // LANES) for r0 in range(0, tm, TAP_ROWS)]

    @pl.when(i == 0)
    def _():
        u_ref[...] = jnp.zeros(u_ref.shape, u_ref.dtype)
        gc_ref[...] = jnp.zeros(gc_ref.shape, gc_ref.dtype)
        _load_weight_bf16(w_hbm, w_ref, stage_ref, sem_ref)
        _load_weight_bf16(wpw_hbm, wpw_ref, stage_ref, sem_ref)

    if True:
        gcp_ref[...] = gc_ref[...]
        h = x_ref[...] * (1.0 + scale_ref[...]) + shift_ref[...]
        h0_ref[...] = h.astype(BF16)
        h_bufs = [h0_ref, h1_ref]
        state = {"cur": 0}

        def sync(tokens):
            src, dst = h_bufs[state["cur"]], h_bufs[1 - state["cur"]]
            _copy_after(src, dst, tokens)
            state["cur"] = 1 - state["cur"]

        def project(col):
            return jnp.dot(h_bufs[state["cur"]][...], w_ref[:, col:col + IN_CHUNK],
                           preferred_element_type=F32)

        cols = list(range(0, D_QKVG, IN_CHUNK)) + list(range(COL_G_CONV, D_IN, IN_CHUNK))
        for c in range(D_CONV // IN_CHUNK):
            cols += [COL_GLU_A + c * IN_CHUNK, COL_GLU_B + c * IN_CHUNK]
        sync_chunks = list(range(SYNC_EVERY, LAST_TAP_SYNC_CHUNK + 1, SYNC_EVERY))
        bounds = [round(s * len(pieces) / len(sync_chunks)) for s in range(len(sync_chunks) + 1)]
        glu_a = None
        pieces_done = 0
        for j, col in enumerate(cols):
            if j in sync_chunks:
                s_idx = sync_chunks.index(j)
                group = pieces[bounds[s_idx]:bounds[s_idx + 1]]
                sync([_tap_piece(u_ref, cw_ref, cb_ref, acc_ref, c, r0) for c, r0 in group])
                pieces_done = bounds[s_idx + 1]
                if pieces_done == len(pieces):
                    _conv_tail(acc_ref, lng_ref, lnb_ref, wpw_ref, bpw_ref, gcp_ref, yc_ref)
            res = project(col)
            if col < COL_K:
                qkvg_ref[:, col:col + IN_CHUNK] = (res * Q_SCALE).astype(BF16)
            elif col < COL_G_ATTN:
                qkvg_ref[:, col:col + IN_CHUNK] = res.astype(BF16)
            elif col < D_QKVG:
                qkvg_ref[:, col:col + IN_CHUNK] = _silu(res).astype(BF16)
            elif col < COL_GLU_B:
                glu_a = res
            elif col < COL_G_CONV:
                u = glu_a * jax.nn.sigmoid(res)
                t0 = (col - COL_GLU_B) // LANES
                for t in range(t0, t0 + IN_CHUNK // LANES):
                    assert pieces_done >= (t + 1) * (tm // TAP_ROWS)
                    u_ref[t, 0:CONV_HALO, :] = u_ref[t, tm:tm + CONV_HALO, :]
                    u_ref[t, CONV_HALO:CONV_HALO + tm, :] = (
                        u[:, (t - t0) * LANES:(t - t0 + 1) * LANES])
            else:
                gc_ref[:, col - COL_G_CONV:col - COL_G_CONV + IN_CHUNK] = _silu(res)


def _inproj_conv_call(x2d, mod, w_in, conv_w, conv_b, ln_g, ln_b, w_pw, b_pw):
    tm = IN_TM
    n = SEQ // tm
    cur = lambda i: jnp.minimum(i, n - 1)
    lag = lambda i: jnp.maximum(i - 1, 0)
    row = lambda i: (0, 0)
    return pl.pallas_call(
        _inproj_conv_kernel,
        grid=(n + 1,),
        in_specs=[
            pl.BlockSpec((tm, D_MODEL), lambda i: (cur(i), 0)),
            pl.BlockSpec((1, D_MODEL), lambda i: (0, 0)),
            pl.BlockSpec((1, D_MODEL), lambda i: (0, 1)),
            pl.BlockSpec(memory_space=pl.ANY),
            pl.BlockSpec((CONV_WIDTH, D_CONV), row),
            pl.BlockSpec((1, D_CONV), row),
            pl.BlockSpec((1, D_CONV), row),
            pl.BlockSpec((1, D_CONV), row),
            pl.BlockSpec(memory_space=pl.ANY),
            pl.BlockSpec((1, D_CONV), row),
        ],
        out_specs=[
            pl.BlockSpec((tm, D_QKVG), lambda i: (cur(i), 0)),
            pl.BlockSpec((tm, D_CONV), lambda i: (lag(i), 0)),
        ],
        out_shape=[
            jax.ShapeDtypeStruct((SEQ, D_QKVG), BF16),
            jax.ShapeDtypeStruct((SEQ, D_CONV), BF16),
        ],
        scratch_shapes=[
            pltpu.VMEM((tm, D_MODEL), BF16),
            pltpu.VMEM((tm, D_MODEL), BF16),
            pltpu.VMEM((D_CONV // LANES, CONV_HALO + tm, LANES), F32),
            pltpu.VMEM((tm, D_CONV), F32),
            pltpu.VMEM((tm, D_CONV), F32),
            pltpu.VMEM((tm, D_CONV), F32),
            pltpu.VMEM((D_MODEL, D_IN), BF16),
            pltpu.VMEM((D_CONV, D_CONV), BF16),
            pltpu.VMEM((2, D_MODEL, W_STAGE_COLS), F32),
            pltpu.SemaphoreType.DMA((2,)),
        ],
        compiler_params=pltpu.CompilerParams(
            dimension_semantics=("arbitrary",), vmem_limit_bytes=VMEM_LIMIT),
        name="inproj_conv",
    )(x2d, mod, mod, w_in, conv_w, conv_b, ln_g, ln_b, w_pw, b_pw)


def kernel(x, c, w_ada, b_ada, w_in, rel_bias, sinks, conv_w, conv_b, conv_ln_g,
           conv_ln_b, w_pw, b_pw, w_out, ln_g, ln_b):
    assert x.shape == (1, SEQ, D_MODEL) and w_in.shape == (DEPTH, D_MODEL, D_IN)
    x2d = x.reshape(SEQ, D_MODEL)

    row = lambda v: v.reshape(1, -1)
    mod = _mod_call(c.reshape(D_MODEL, 1), w_ada.reshape(D_MODEL, 3 * D_MODEL), row(b_ada))
    bias = _bias_call(rel_bias)
    qkvg, y_conv = _inproj_conv_call(
        x2d, mod, w_in.reshape(D_MODEL, D_IN),
        conv_w.reshape(CONV_WIDTH, D_CONV), row(conv_b), row(conv_ln_g), row(conv_ln_b),
        w_pw.reshape(D_CONV, D_CONV), row(b_pw))
    out = _attn_out_call(qkvg, bias, sinks.reshape(N_Q_HEADS), y_conv, x2d, mod,
                         w_out.reshape(D_MODEL, D_MODEL), row(ln_g), row(ln_b))
    return out.reshape(1, SEQ, D_MODEL)
```

```python
import functools
import math

import jax
import jax.numpy as jnp
import numpy as np
from jax.experimental import pallas as pl
from jax.experimental.pallas import tpu as pltpu

F32 = jnp.float32
BF16 = jnp.bfloat16

D_MODEL = 2048
SEQ = 8192
HEAD_DIM = 64
D_ATTN = 1024
D_CONV = 1024
N_Q_HEADS = 16
N_KV_HEADS = 4
GQA_GROUP = 4
D_KV = 256
WINDOW = 128
BLOCK = 128
CONV_WIDTH = 31
N_BUCKETS = 32
MAX_DISTANCE = 128
LN_EPS = 1e-5
DEPTH = 1
ALPHA = (2.0 * DEPTH) ** 0.25
D_IN = 2 * D_ATTN + 2 * D_KV + 3 * D_CONV

LANES = 128
SUBLANES = 8

COL_Q = 0
COL_K = COL_Q + D_ATTN
COL_V = COL_K + D_KV
COL_G_ATTN = COL_V + D_KV
COL_GLU_A = COL_G_ATTN + D_ATTN
COL_GLU_B = COL_GLU_A + D_CONV
COL_G_CONV = COL_GLU_B + D_CONV
HALF = 512

VMEM_LIMIT = 56 * 1024 * 1024


def _silu(v):
    return v * jax.nn.sigmoid(v)


def _t5_bucket_table():
    qi = np.arange(BLOCK)[:, None]
    kj = np.arange(2 * BLOCK)[None, :]
    dist = np.maximum(qi + BLOCK - kj, 0)
    max_exact = N_BUCKETS // 2
    d = np.maximum(dist, 1).astype(np.float64)
    large = max_exact + (np.log(d / max_exact) / math.log(MAX_DISTANCE / max_exact)
                         * (N_BUCKETS - max_exact)).astype(np.int32)
    large = np.minimum(large, N_BUCKETS - 1)
    return np.where(dist < max_exact, dist, large).astype(np.int32)


def _mod_kernel(c_ref, w_ref, b_ref, o_ref):
    c = c_ref[...]
    ca = _silu(c)
    o_ref[...] = jnp.sum(ca * w_ref[...], axis=0, keepdims=True) + b_ref[...]


def _mod_call(c_col, w_ada, b_ada):
    tn = 1024
    n = w_ada.shape[1]
    return pl.pallas_call(
        _mod_kernel,
        grid=(n // tn,),
        in_specs=[
            pl.BlockSpec((D_MODEL, 1), lambda j: (0, 0)),
            pl.BlockSpec((D_MODEL, tn), lambda j: (0, j)),
            pl.BlockSpec((1, tn), lambda j: (0, j)),
        ],
        out_specs=pl.BlockSpec((1, tn), lambda j: (0, j)),
        out_shape=jax.ShapeDtypeStruct((1, n), F32),
        compiler_params=pltpu.CompilerParams(
            dimension_semantics=("arbitrary",), vmem_limit_bytes=VMEM_LIMIT),
        name="mod",
    )(c_col, w_ada, b_ada)


LOG2E = math.log2(math.e)
Q_SCALE = LOG2E * HEAD_DIM ** -0.5
MASKED = -1e30


def _lane_buckets():
    m = np.arange(2 * BLOCK)
    dist = BLOCK - m
    ok = (dist >= 0) & (dist < WINDOW)
    max_exact = N_BUCKETS // 2
    d = np.maximum(dist, 1).astype(np.float64)
    large = max_exact + (np.log(d / max_exact) / math.log(MAX_DISTANCE / max_exact)
                         * (N_BUCKETS - max_exact)).astype(np.int32)
    large = np.minimum(large, N_BUCKETS - 1)
    bucket = np.where(dist < max_exact, dist, large)
    return np.where(ok, bucket, -1).astype(np.int32).reshape(1, -1)


def _bias_kernel(rb_ref, bm_ref, o_ref):
    bm = bm_ref[...]
    col = jax.lax.broadcasted_iota(jnp.int32, (BLOCK, 2 * BLOCK), 1)
    for h in range(N_Q_HEADS):
        t = jnp.full(bm.shape, MASKED, F32)
        for b in range(N_BUCKETS):
            t = jnp.where(bm == b, rb_ref[b, h] * LOG2E, t)
        x = pltpu.roll(jnp.broadcast_to(t, (BLOCK, 2 * BLOCK)), 0, 1, stride=1, stride_axis=0)
        o_ref[1, h] = x
        o_ref[0, h] = jnp.where(col >= BLOCK, x, MASKED)


def _bias_call(rel_bias):
    shape = (2, N_Q_HEADS, BLOCK, 2 * BLOCK)
    return pl.pallas_call(
        _bias_kernel,
        in_specs=[
            pl.BlockSpec(memory_space=pltpu.SMEM),
            pl.BlockSpec((1, 2 * BLOCK), lambda: (0, 0)),
        ],
        out_specs=pl.BlockSpec(shape, lambda: (0, 0, 0, 0)),
        out_shape=jax.ShapeDtypeStruct(shape, F32),
        name="bias",
    )(rel_bias, jnp.asarray(_lane_buckets()))


def _inproj_kernel(x_ref, shift_ref, scale_ref, w_ref, o_ref, h_ref):
    @pl.when(pl.program_id(1) == 0)
    def _():
        h = x_ref[...] * (1.0 + scale_ref[...]) + shift_ref[...]
        h_ref[...] = h.astype(BF16)

    o_ref[...] = jnp.dot(h_ref[...], w_ref[...],
                         preferred_element_type=F32).astype(BF16)


def _inproj_call(x2d, mod, w_in_bf16):
    tm, tn = 1024, 512
    return pl.pallas_call(
        _inproj_kernel,
        grid=(SEQ // tm, D_IN // tn),
        in_specs=[
            pl.BlockSpec((tm, D_MODEL), lambda i, j: (i, 0)),
            pl.BlockSpec((1, D_MODEL), lambda i, j: (0, 0)),
            pl.BlockSpec((1, D_MODEL), lambda i, j: (0, 1)),
            pl.BlockSpec((D_MODEL, tn), lambda i, j: (0, j)),
        ],
        out_specs=pl.BlockSpec((tm, tn), lambda i, j: (i, j)),
        out_shape=jax.ShapeDtypeStruct((SEQ, D_IN), BF16),
        scratch_shapes=[pltpu.VMEM((tm, D_MODEL), BF16)],
        compiler_params=pltpu.CompilerParams(
            dimension_semantics=("arbitrary", "arbitrary"),
            vmem_limit_bytes=VMEM_LIMIT),
        name="inproj",
    )(x2d, mod, mod, w_in_bf16)


def _dup_halves(t):
    swapped = jnp.concatenate([t[:, HEAD_DIM:], t[:, :HEAD_DIM]], axis=1)
    lane = jax.lax.broadcasted_iota(jnp.int32, t.shape, 1)
    low_half = lane < HEAD_DIM
    return jnp.where(low_half, t, swapped), jnp.where(low_half, swapped, t)


def _attn_group(q_ref, g0_ref, g1_ref, k_dup, v_dup, bias_ref, sinks_ref, r0, kv, ya_ref):
    lane = jax.lax.broadcasted_iota(jnp.int32, (BLOCK, LANES), 1)
    low_half = lane < HEAD_DIM
    rows = slice(r0, r0 + BLOCK)
    tiles = [GQA_GROUP * kv // 2 + t for t in range(GQA_GROUP // 2)]
    stacked = []
    for tile in tiles:
        q_tile = q_ref[rows, tile * LANES:(tile + 1) * LANES]
        zero = jnp.zeros_like(q_tile)
        stacked += [jnp.where(low_half, q_tile, zero), jnp.where(low_half, zero, q_tile)]
    s_all = jax.lax.dot_general(jnp.concatenate(stacked, axis=0), k_dup,
                                (((1,), (1,)), ((), ())),
                                preferred_element_type=F32)
    probs, denoms = [], []
    for r in range(GQA_GROUP):
        head = GQA_GROUP * kv + r
        s = s_all[r * BLOCK:(r + 1) * BLOCK] + bias_ref[0, head]
        sink = sinks_ref[head] * LOG2E
        m = jnp.max(s, axis=1, keepdims=True)
        e = jnp.exp2(s - m)
        denoms.append(jnp.sum(e, axis=1, keepdims=True) + jnp.exp2(sink - m))
        probs.append(e.astype(BF16))
    o_all = jnp.dot(jnp.concatenate(probs, axis=0), v_dup,
                    preferred_element_type=F32)
    outs = [o_all[r * BLOCK:(r + 1) * BLOCK] / denoms[r] for r in range(GQA_GROUP)]
    for t, tile in enumerate(tiles):
        y = jnp.where(low_half, outs[2 * t], outs[2 * t + 1])
        g_ref = g0_ref if tile < HALF // LANES else g1_ref
        gl = (tile * LANES) % HALF
        gate = g_ref[rows, gl:gl + LANES].astype(F32)
        ya_ref[rows, tile * LANES:(tile + 1) * LANES] = (y * gate).astype(BF16)


ATT_TM = 2 * BLOCK
OUT_CHUNK = 256
PACKED_ROWS = 16


def _attn_out_kernel(sinks_ref, q_ref, g0_ref, g1_ref, k_ref, v_ref, kp_ref, vp_ref,
                     bias0_ref, bias1_ref, yc_ref, x_ref, gate_ref, wout_hbm,
                     lng_ref, lnb_ref, o_ref, ya_ref, wout_ref, stage_ref, sem_ref):
    i = pl.program_id(0)

    @pl.when(i == 0)
    def _():
        ya_ref[...] = jnp.zeros(ya_ref.shape, ya_ref.dtype)
        _load_weight_bf16(wout_hbm, wout_ref, stage_ref, sem_ref)

    y = jnp.dot(ya_ref[...], wout_ref[0:D_ATTN, :], preferred_element_type=F32)
    y = y + jnp.dot(yc_ref[...], wout_ref[D_ATTN:D_MODEL, :], preferred_element_type=F32)
    z = ALPHA * x_ref[...] + gate_ref[...] * y
    mu = jnp.mean(z, axis=1, keepdims=True)
    d = z - mu
    var = jnp.mean(d * d, axis=1, keepdims=True)
    o_ref[...] = d * jax.lax.rsqrt(var + LN_EPS) * lng_ref[...] + lnb_ref[...]

    _attn_tile(sinks_ref, q_ref, g0_ref, g1_ref, k_ref, v_ref, kp_ref, vp_ref,
               [bias0_ref, bias1_ref], ya_ref)


def _attn_tile(sinks_ref, q_ref, g0_ref, g1_ref, k_ref, v_ref, kp_ref, vp_ref,
               bias_refs, ya_ref):
    n_lt = D_KV // LANES
    lt = lambda t: slice(t * LANES, (t + 1) * LANES)

    def block_operands(b):
        r0 = b * BLOCK
        if b == 0:
            k_prev = [kp_ref[:, lt(t)] for t in range(n_lt)]
            v_prev = [vp_ref[:, lt(t)] for t in range(n_lt)]
        else:
            k_prev = [k_ref[r0 - BLOCK:r0, lt(t)] for t in range(n_lt)]
            v_prev = [v_ref[r0 - BLOCK:r0, lt(t)] for t in range(n_lt)]
        k_rows = [jnp.concatenate([k_prev[t], k_ref[r0:r0 + BLOCK, lt(t)]], axis=0)
                  for t in range(n_lt)]
        v_rows = [jnp.concatenate([v_prev[t], v_ref[r0:r0 + BLOCK, lt(t)]], axis=0)
                  for t in range(n_lt)]
        return ([_dup_halves(kt) for kt in k_rows], [_dup_halves(vt) for vt in v_rows])

    for b in range(ATT_TM // BLOCK):
        k_dups, v_dups = block_operands(b)
        for kv in range(N_KV_HEADS):
            _attn_group(q_ref, g0_ref, g1_ref, k_dups[kv // 2][kv % 2], v_dups[kv // 2][kv % 2],
                        bias_refs[b], sinks_ref, b * BLOCK, kv, ya_ref)


def _attn_out_call(proj, bias, sinks, y_conv, x2d, mod, w_out, ln_g, ln_b):
    tm = ATT_TM
    n = SEQ // tm
    clamp = lambda t: jnp.clip(t, 0, n - 1)
    cur = lambda i: clamp(i)
    lag = lambda i: clamp(i - 1)
    kv_prev = lambda i: jnp.maximum(cur(i) * (tm // BLOCK) - 1, 0)
    row = lambda i: (0, 0)
    return pl.pallas_call(
        _attn_out_kernel,
        grid=(n + 1,),
        in_specs=[
            pl.BlockSpec(memory_space=pltpu.SMEM),
            pl.BlockSpec((tm, D_ATTN), lambda i: (cur(i), COL_Q // D_ATTN)),
            pl.BlockSpec((tm, HALF), lambda i: (cur(i), COL_G_ATTN // HALF)),
            pl.BlockSpec((tm, HALF), lambda i: (cur(i), COL_G_ATTN // HALF + 1)),
            pl.BlockSpec((tm, D_KV), lambda i: (cur(i), COL_K // D_KV)),
            pl.BlockSpec((tm, D_KV), lambda i: (cur(i), COL_V // D_KV)),
            pl.BlockSpec((BLOCK, D_KV), lambda i: (kv_prev(i), COL_K // D_KV)),
            pl.BlockSpec((BLOCK, D_KV), lambda i: (kv_prev(i), COL_V // D_KV)),
            pl.BlockSpec((1, N_Q_HEADS, BLOCK, 2 * BLOCK),
                         lambda i: (jnp.minimum(cur(i), 1), 0, 0, 0)),
            pl.BlockSpec((1, N_Q_HEADS, BLOCK, 2 * BLOCK), lambda i: (1, 0, 0, 0)),
            pl.BlockSpec((tm, D_CONV), lambda i: (lag(i), 0)),
            pl.BlockSpec((tm, D_MODEL), lambda i: (lag(i), 0)),
            pl.BlockSpec((1, D_MODEL), lambda i: (0, 2)),
            pl.BlockSpec(memory_space=pl.ANY),
            pl.BlockSpec((1, D_MODEL), row),
            pl.BlockSpec((1, D_MODEL), row),
        ],
        out_specs=pl.BlockSpec((tm, D_MODEL), lambda i: (lag(i), 0)),
        out_shape=jax.ShapeDtypeStruct((SEQ, D_MODEL), F32),
        scratch_shapes=[
            pltpu.VMEM((tm, D_ATTN), BF16),
            pltpu.VMEM((D_MODEL, D_MODEL), BF16),
            pltpu.VMEM((2, D_MODEL, W_STAGE_COLS), F32),
            pltpu.SemaphoreType.DMA((2,)),
        ],
        compiler_params=pltpu.CompilerParams(
            dimension_semantics=("arbitrary",), vmem_limit_bytes=VMEM_LIMIT),
        name="attn_out",
    )(sinks, proj, proj, proj, proj, proj, proj, proj, bias, bias, y_conv, x2d, mod,
      w_out, ln_g, ln_b)


CONV_TM = 256
CONV_HALO = 32
CONV_RC = 64


def _conv_taps(u_ref, cw_ref, cb_ref, acc_ref, tm):
    off = CONV_HALO - (CONV_WIDTH - 1)
    for c in range(D_CONV // LANES):
        cs = slice(c * LANES, (c + 1) * LANES)
        w_rows = [cw_ref[k:k + 1, cs] for k in range(CONV_WIDTH)]
        for r0 in range(0, tm, CONV_RC):
            acc = jnp.broadcast_to(cb_ref[:, cs], (CONV_RC, LANES))
            for k in range(CONV_WIDTH):
                acc = acc + w_rows[k] * u_ref[c, r0 + off + k:r0 + off + k + CONV_RC, :]
            acc_ref[r0:r0 + CONV_RC, cs] = acc


def _conv_kernel(a0_ref, a1_ref, b0_ref, b1_ref, g0_ref, g1_ref, cw_ref, cb_ref,
                 lng_ref, lnb_ref, wpw_ref, bpw_ref, o_ref, u_ref, acc_ref):
    i = pl.program_id(0)
    tm = CONV_TM

    @pl.when(i == 0)
    def _():
        u_ref[:, 0:CONV_HALO, :] = jnp.zeros((D_CONV // LANES, CONV_HALO, LANES), F32)

    @pl.when(i > 0)
    def _():
        u_ref[:, 0:CONV_HALO, :] = u_ref[:, tm:tm + CONV_HALO, :]

    for a_ref, b_ref, c0 in ((a0_ref, b0_ref, 0), (a1_ref, b1_ref, HALF)):
        a = a_ref[...].astype(F32)
        b = b_ref[...].astype(F32)
        u = a * jax.nn.sigmoid(b)
        for c in range(HALF // LANES):
            u_ref[(c0 // LANES) + c, CONV_HALO:CONV_HALO + tm, :] = u[:, c * LANES:(c + 1) * LANES]

    _conv_taps(u_ref, cw_ref, cb_ref, acc_ref, tm)

    v = acc_ref[...]
    mu = jnp.mean(v, axis=1, keepdims=True)
    d = v - mu
    var = jnp.mean(d * d, axis=1, keepdims=True)
    y = d * jax.lax.rsqrt(var + LN_EPS) * lng_ref[...] + lnb_ref[...]
    s = _silu(y).astype(BF16)
    z = jnp.dot(s, wpw_ref[...], preferred_element_type=F32) + bpw_ref[...]
    for g_ref, c0 in ((g0_ref, 0), (g1_ref, HALF)):
        gate = _silu(g_ref[...].astype(F32))
        o_ref[:, c0:c0 + HALF] = (z[:, c0:c0 + HALF] * gate).astype(BF16)


def _conv_call(proj, conv_w, conv_b, ln_g, ln_b, w_pw_bf16, b_pw):
    tm = CONV_TM
    row = lambda i: (0, 0)
    half = lambda col, h: pl.BlockSpec((tm, HALF), lambda i: (i, col // HALF + h))
    return pl.pallas_call(
        _conv_kernel,
        grid=(SEQ // tm,),
        in_specs=[
            half(COL_GLU_A, 0), half(COL_GLU_A, 1),
            half(COL_GLU_B, 0), half(COL_GLU_B, 1),
            half(COL_G_CONV, 0), half(COL_G_CONV, 1),
            pl.BlockSpec((CONV_WIDTH, D_CONV), row),
            pl.BlockSpec((1, D_CONV), row),
            pl.BlockSpec((1, D_CONV), row),
            pl.BlockSpec((1, D_CONV), row),
            pl.BlockSpec((D_CONV, D_CONV), row),
            pl.BlockSpec((1, D_CONV), row),
        ],
        out_specs=pl.BlockSpec((tm, D_CONV), lambda i: (i, 0)),
        out_shape=jax.ShapeDtypeStruct((SEQ, D_CONV), BF16),
        scratch_shapes=[
            pltpu.VMEM((D_CONV // LANES, CONV_HALO + tm, LANES), F32),
            pltpu.VMEM((tm, D_CONV), F32),
        ],
        compiler_params=pltpu.CompilerParams(
            dimension_semantics=("arbitrary",), vmem_limit_bytes=VMEM_LIMIT),
        name="conv",
    )(proj, proj, proj, proj, proj, proj, conv_w, conv_b, ln_g, ln_b, w_pw_bf16, b_pw)


IN_TM = 256
IN_CHUNK = 256
D_QKVG = COL_GLU_A
TAP_ROWS = 64
SYNC_EVERY = 2
LAST_TAP_SYNC_CHUNK = 18


def _token(v):
    t = v[0:SUBLANES, 0:LANES]
    for r in range(SUBLANES, v.shape[0], SUBLANES):
        t = t + v[r:r + SUBLANES, 0:LANES]
    return t


def _zero_of(token):
    return (pltpu.bitcast(token, jnp.uint32) >> 16) >> 16


def _copy_after(src_ref, dst_ref, tokens):
    zero = _zero_of(tokens[0])
    for t in tokens[1:]:
        zero = zero | _zero_of(t)
    bits = pltpu.bitcast(src_ref[...], jnp.uint32)
    reps = (bits.shape[0] // zero.shape[0], bits.shape[1] // zero.shape[1])
    dst_ref[...] = pltpu.bitcast(bits | jnp.tile(zero, reps), dst_ref.dtype)


def _tap_piece(u_ref, cw_ref, cb_ref, acc_ref, c, r0, start_token=None):
    off = CONV_HALO - (CONV_WIDTH - 1)
    cs = slice(c * LANES, (c + 1) * LANES)
    init = jnp.broadcast_to(cb_ref[:, cs], (SUBLANES, LANES))
    if start_token is not None:
        init = pltpu.bitcast(pltpu.bitcast(init, jnp.uint32) | _zero_of(start_token), F32)
    acc = jnp.tile(init, (TAP_ROWS // SUBLANES, 1))
    for k in range(CONV_WIDTH):
        acc = acc + cw_ref[k:k + 1, cs] * u_ref[c, r0 + off + k:r0 + off + k + TAP_ROWS, :]
    acc_ref[r0:r0 + TAP_ROWS, cs] = acc
    return _token(acc)


def _conv_tail(acc_ref, lng_ref, lnb_ref, wpw_ref, bpw_ref, gate_ref, yc_ref):
    v = acc_ref[...]
    mu = jnp.mean(v, axis=1, keepdims=True)
    d = v - mu
    var = jnp.mean(d * d, axis=1, keepdims=True)
    y = d * jax.lax.rsqrt(var + LN_EPS) * lng_ref[...] + lnb_ref[...]
    s = _silu(y).astype(BF16)
    z = jnp.dot(s, wpw_ref[...], preferred_element_type=F32) + bpw_ref[...]
    yc_ref[...] = (z * gate_ref[...]).astype(BF16)
    return _token(z)


W_STAGE_COLS = 256


def _load_weight_bf16(w_hbm, w_ref, stage_ref, sem_ref):
    rows, cols = w_ref.shape
    n = cols // W_STAGE_COLS

    def chunk_copy(j):
        slot = j % 2
        return pltpu.make_async_copy(
            w_hbm.at[:, pl.ds(j * W_STAGE_COLS, W_STAGE_COLS)],
            stage_ref.at[slot, pl.ds(0, rows), :], sem_ref.at[slot])

    chunk_copy(0).start()
    for j in range(n):
        if j + 1 < n:
            chunk_copy(j + 1).start()
        chunk_copy(j).wait()
        w_ref[:, j * W_STAGE_COLS:(j + 1) * W_STAGE_COLS] = (
            stage_ref[j % 2, 0:rows, :].astype(BF16))


def _inproj_conv_kernel(x0_ref, xn_ref, shift_ref, scale_ref, w_hbm, cw_ref, cb_ref,
                        lng_ref, lnb_ref, wpw_hbm, bpw_ref, qkvg_ref, yc_ref,
                        hn_ref, h0_ref, h1_ref, u_ref, gc_ref, gcp_ref, acc_ref,
                        w_ref, wpw_ref, stage_ref, sem_ref):
    i = pl.program_id(0)
    tm = IN_TM
    pieces = [(c, r0) for c in range(D_CONV // LANES) for r0 in range(0, tm, TAP_ROWS)]

    def modulate(x_ref):
        h = x_ref[...] * (1.0 + scale_ref[...]) + shift_ref[...]
        hn_ref[...] = h.astype(BF16)

    @pl.when(i == 0)
    def _():
        u_ref[...] = jnp.zeros(u_ref.shape, u_ref.dtype)
        gc_ref[...] = jnp.zeros(gc_ref.shape, gc_ref.dtype)
        _load_weight_bf16(w_hbm, w_ref, stage_ref, sem_ref)
        _load_weight_bf16(wpw_hbm, wpw_ref, stage_ref, sem_ref)
        modulate(x0_ref)

    if True:
        gcp_ref[...] = gc_ref[...]
        h_bufs = [hn_ref, h0_ref, h1_ref]
        state = {"cur": 0, "syncs": 0}

        def sync(tokens):
            src = h_bufs[state["cur"]]
            nxt = 1 + state["syncs"] % 2
            _copy_after(src, h_bufs[nxt], tokens)
            state["cur"] = nxt
            state["syncs"] += 1
            if state["syncs"] == 1:
                modulate(xn_ref)

        def project(col):
            return jnp.dot(h_bufs[state["cur"]][...], w_ref[:, col:col + IN_CHUNK],
                           preferred_element_type=F32)

        cols = list(range(0, D_QKVG, IN_CHUNK)) + list(range(COL_G_CONV, D_IN, IN_CHUNK))
        for c in range(D_CONV // IN_CHUNK):
            cols += [COL_GLU_A + c * IN_CHUNK, COL_GLU_B + c * IN_CHUNK]
        sync_chunks = list(range(SYNC_EVERY, LAST_TAP_SYNC_CHUNK + 1, SYNC_EVERY))
        bounds = [round(s * len(pieces) / len(sync_chunks)) for s in range(len(sync_chunks) + 1)]
        glu_a = None
        pieces_done = 0
        for j, col in enumerate(cols):
            if j in sync_chunks:
                s_idx = sync_chunks.index(j)
                group = pieces[bounds[s_idx]:bounds[s_idx + 1]]
                sync([_tap_piece(u_ref, cw_ref, cb_ref, acc_ref, c, r0) for c, r0 in group])
                pieces_done = bounds[s_idx + 1]
                if pieces_done == len(pieces):
                    _conv_tail(acc_ref, lng_ref, lnb_ref, wpw_ref, bpw_ref, gcp_ref, yc_ref)
            res = project(col)
            if col < COL_K:
                qkvg_ref[:, col:col + IN_CHUNK] = (res * Q_SCALE).astype(BF16)
            elif col < COL_G_ATTN:
                qkvg_ref[:, col:col + IN_CHUNK] = res.astype(BF16)
            elif col < D_QKVG:
                qkvg_ref[:, col:col + IN_CHUNK] = _silu(res).astype(BF16)
            elif col < COL_GLU_B:
                glu_a = res
            elif col < COL_G_CONV:
                u = glu_a * jax.nn.sigmoid(res)
                t0 = (col - COL_GLU_B) // LANES
                for t in range(t0, t0 + IN_CHUNK // LANES):
                    assert pieces_done >= (t + 1) * (tm // TAP_ROWS)
                    u_ref[t, 0:CONV_HALO, :] = u_ref[t, tm:tm + CONV_HALO, :]
                    u_ref[t, CONV_HALO:CONV_HALO + tm, :] = (
                        u[:, (t - t0) * LANES:(t - t0 + 1) * LANES])
            else:
                gc_ref[:, col - COL_G_CONV:col - COL_G_CONV + IN_CHUNK] = _silu(res)


def _inproj_conv_call(x2d, mod, w_in, conv_w, conv_b, ln_g, ln_b, w_pw, b_pw):
    tm = IN_TM
    n = SEQ // tm
    cur = lambda i: jnp.minimum(i, n - 1)
    lag = lambda i: jnp.maximum(i - 1, 0)
    row = lambda i: (0, 0)
    return pl.pallas_call(
        _inproj_conv_kernel,
        grid=(n + 1,),
        in_specs=[
            pl.BlockSpec((tm, D_MODEL), lambda i: (0, 0)),
            pl.BlockSpec((tm, D_MODEL), lambda i: (cur(i + 1), 0)),
            pl.BlockSpec((1, D_MODEL), lambda i: (0, 0)),
            pl.BlockSpec((1, D_MODEL), lambda i: (0, 1)),
            pl.BlockSpec(memory_space=pl.ANY),
            pl.BlockSpec((CONV_WIDTH, D_CONV), row),
            pl.BlockSpec((1, D_CONV), row),
            pl.BlockSpec((1, D_CONV), row),
            pl.BlockSpec((1, D_CONV), row),
            pl.BlockSpec(memory_space=pl.ANY),
            pl.BlockSpec((1, D_CONV), row),
        ],
        out_specs=[
            pl.BlockSpec((tm, D_QKVG), lambda i: (cur(i), 0)),
            pl.BlockSpec((tm, D_CONV), lambda i: (lag(i), 0)),
        ],
        out_shape=[
            jax.ShapeDtypeStruct((SEQ, D_QKVG), BF16),
            jax.ShapeDtypeStruct((SEQ, D_CONV), BF16),
        ],
        scratch_shapes=[
            pltpu.VMEM((tm, D_MODEL), BF16),
            pltpu.VMEM((tm, D_MODEL), BF16),
            pltpu.VMEM((tm, D_MODEL), BF16),
            pltpu.VMEM((D_CONV // LANES, CONV_HALO + tm, LANES), F32),
            pltpu.VMEM((tm, D_CONV), F32),
            pltpu.VMEM((tm, D_CONV), F32),
            pltpu.VMEM((tm, D_CONV), F32),
            pltpu.VMEM((D_MODEL, D_IN), BF16),
            pltpu.VMEM((D_CONV, D_CONV), BF16),
            pltpu.VMEM((2, D_MODEL, W_STAGE_COLS), F32),
            pltpu.SemaphoreType.DMA((2,)),
        ],
        compiler_params=pltpu.CompilerParams(
            dimension_semantics=("arbitrary",), vmem_limit_bytes=VMEM_LIMIT),
        name="inproj_conv",
    )(x2d, x2d, mod, mod, w_in, conv_w, conv_b, ln_g, ln_b, w_pw, b_pw)


def kernel(x, c, w_ada, b_ada, w_in, rel_bias, sinks, conv_w, conv_b, conv_ln_g,
           conv_ln_b, w_pw, b_pw, w_out, ln_g, ln_b):
    assert x.shape == (1, SEQ, D_MODEL) and w_in.shape == (DEPTH, D_MODEL, D_IN)
    x2d = x.reshape(SEQ, D_MODEL)

    row = lambda v: v.reshape(1, -1)
    mod = _mod_call(c.reshape(D_MODEL, 1), w_ada.reshape(D_MODEL, 3 * D_MODEL), row(b_ada))
    bias = _bias_call(rel_bias)
    qkvg, y_conv = _inproj_conv_call(
        x2d, mod, w_in.reshape(D_MODEL, D_IN),
        conv_w.reshape(CONV_WIDTH, D_CONV), row(conv_b), row(conv_ln_g), row(conv_ln_b),
        w_pw.reshape(D_CONV, D_CONV), row(b_pw))
    out = _attn_out_call(qkvg, bias, sinks.reshape(N_Q_HEADS), y_conv, x2d, mod,
                         w_out.reshape(D_MODEL, D_MODEL), row(ln_g), row(ln_b))
    return out.reshape(1, SEQ, D_MODEL)
```

```python
import functools
import math

import jax
import jax.numpy as jnp
import numpy as np
from jax.experimental import pallas as pl
from jax.experimental.pallas import tpu as pltpu

F32 = jnp.float32
BF16 = jnp.bfloat16

D_MODEL = 2048
SEQ = 8192
HEAD_DIM = 64
D_ATTN = 1024
D_CONV = 1024
N_Q_HEADS = 16
N_KV_HEADS = 4
GQA_GROUP = 4
D_KV = 256
WINDOW = 128
BLOCK = 128
CONV_WIDTH = 31
N_BUCKETS = 32
MAX_DISTANCE = 128
LN_EPS = 1e-5
DEPTH = 1
ALPHA = (2.0 * DEPTH) ** 0.25
D_IN = 2 * D_ATTN + 2 * D_KV + 3 * D_CONV

LANES = 128
SUBLANES = 8

COL_Q = 0
COL_K = COL_Q + D_ATTN
COL_V = COL_K + D_KV
COL_G_ATTN = COL_V + D_KV
COL_GLU_A = COL_G_ATTN + D_ATTN
COL_GLU_B = COL_GLU_A + D_CONV
COL_G_CONV = COL_GLU_B + D_CONV
HALF = 512

VMEM_LIMIT = 56 * 1024 * 1024


def _silu(v):
    return v * jax.nn.sigmoid(v)


def _t5_bucket_table():
    qi = np.arange(BLOCK)[:, None]
    kj = np.arange(2 * BLOCK)[None, :]
    dist = np.maximum(qi + BLOCK - kj, 0)
    max_exact = N_BUCKETS // 2
    d = np.maximum(dist, 1).astype(np.float64)
    large = max_exact + (np.log(d / max_exact) / math.log(MAX_DISTANCE / max_exact)
                         * (N_BUCKETS - max_exact)).astype(np.int32)
    large = np.minimum(large, N_BUCKETS - 1)
    return np.where(dist < max_exact, dist, large).astype(np.int32)


def _mod_kernel(c_ref, w_ref, b_ref, o_ref):
    c = c_ref[...]
    ca = _silu(c)
    o_ref[...] = jnp.sum(ca * w_ref[...], axis=0, keepdims=True) + b_ref[...]


def _mod_call(c_col, w_ada, b_ada):
    tn = 1024
    n = w_ada.shape[1]
    return pl.pallas_call(
        _mod_kernel,
        grid=(n // tn,),
        in_specs=[
            pl.BlockSpec((D_MODEL, 1), lambda j: (0, 0)),
            pl.BlockSpec((D_MODEL, tn), lambda j: (0, j)),
            pl.BlockSpec((1, tn), lambda j: (0, j)),
        ],
        out_specs=pl.BlockSpec((1, tn), lambda j: (0, j)),
        out_shape=jax.ShapeDtypeStruct((1, n), F32),
        compiler_params=pltpu.CompilerParams(
            dimension_semantics=("arbitrary",), vmem_limit_bytes=VMEM_LIMIT),
        name="mod",
    )(c_col, w_ada, b_ada)


LOG2E = math.log2(math.e)
Q_SCALE = LOG2E * HEAD_DIM ** -0.5
MASKED = -1e30


def _lane_buckets():
    m = np.arange(2 * BLOCK)
    dist = BLOCK - m
    ok = (dist >= 0) & (dist < WINDOW)
    max_exact = N_BUCKETS // 2
    d = np.maximum(dist, 1).astype(np.float64)
    large = max_exact + (np.log(d / max_exact) / math.log(MAX_DISTANCE / max_exact)
                         * (N_BUCKETS - max_exact)).astype(np.int32)
    large = np.minimum(large, N_BUCKETS - 1)
    bucket = np.where(dist < max_exact, dist, large)
    return np.where(ok, bucket, -1).astype(np.int32).reshape(1, -1)


def _bias_kernel(rb_ref, bm_ref, o_ref):
    bm = bm_ref[...]
    col = jax.lax.broadcasted_iota(jnp.int32, (BLOCK, 2 * BLOCK), 1)
    for h in range(N_Q_HEADS):
        t = jnp.full(bm.shape, MASKED, F32)
        for b in range(N_BUCKETS):
            t = jnp.where(bm == b, rb_ref[b, h] * LOG2E, t)
        x = pltpu.roll(jnp.broadcast_to(t, (BLOCK, 2 * BLOCK)), 0, 1, stride=1, stride_axis=0)
        o_ref[1, h] = x
        o_ref[0, h] = jnp.where(col >= BLOCK, x, MASKED)


def _bias_call(rel_bias):
    shape = (2, N_Q_HEADS, BLOCK, 2 * BLOCK)
    return pl.pallas_call(
        _bias_kernel,
        in_specs=[
            pl.BlockSpec(memory_space=pltpu.SMEM),
            pl.BlockSpec((1, 2 * BLOCK), lambda: (0, 0)),
        ],
        out_specs=pl.BlockSpec(shape, lambda: (0, 0, 0, 0)),
        out_shape=jax.ShapeDtypeStruct(shape, F32),
        name="bias",
    )(rel_bias, jnp.asarray(_lane_buckets()))


def _inproj_kernel(x_ref, shift_ref, scale_ref, w_ref, o_ref, h_ref):
    @pl.when(pl.program_id(1) == 0)
    def _():
        h = x_ref[...] * (1.0 + scale_ref[...]) + shift_ref[...]
        h_ref[...] = h.astype(BF16)

    o_ref[...] = jnp.dot(h_ref[...], w_ref[...],
                         preferred_element_type=F32).astype(BF16)


def _inproj_call(x2d, mod, w_in_bf16):
    tm, tn = 1024, 512
    return pl.pallas_call(
        _inproj_kernel,
        grid=(SEQ // tm, D_IN // tn),
        in_specs=[
            pl.BlockSpec((tm, D_MODEL), lambda i, j: (i, 0)),
            pl.BlockSpec((1, D_MODEL), lambda i, j: (0, 0)),
            pl.BlockSpec((1, D_MODEL), lambda i, j: (0, 1)),
            pl.BlockSpec((D_MODEL, tn), lambda i, j: (0, j)),
        ],
        out_specs=pl.BlockSpec((tm, tn), lambda i, j: (i, j)),
        out_shape=jax.ShapeDtypeStruct((SEQ, D_IN), BF16),
        scratch_shapes=[pltpu.VMEM((tm, D_MODEL), BF16)],
        compiler_params=pltpu.CompilerParams(
            dimension_semantics=("arbitrary", "arbitrary"),
            vmem_limit_bytes=VMEM_LIMIT),
        name="inproj",
    )(x2d, mod, mod, w_in_bf16)


def _dup_halves(t):
    swapped = jnp.concatenate([t[:, HEAD_DIM:], t[:, :HEAD_DIM]], axis=1)
    lane = jax.lax.broadcasted_iota(jnp.int32, t.shape, 1)
    low_half = lane < HEAD_DIM
    return jnp.where(low_half, t, swapped), jnp.where(low_half, swapped, t)


def _attn_group(q_ref, g0_ref, g1_ref, k_dup, v_dup, bias_ref, sinks_ref, r0, kv, ya_ref):
    lane = jax.lax.broadcasted_iota(jnp.int32, (BLOCK, LANES), 1)
    low_half = lane < HEAD_DIM
    rows = slice(r0, r0 + BLOCK)
    all_tiles = [GQA_GROUP * kv // 2 + t for t in range(GQA_GROUP // 2)]
    for t0 in range(0, len(all_tiles), ATT_TILES_PER_DOT):
        tiles = all_tiles[t0:t0 + ATT_TILES_PER_DOT]
        stacked = []
        for tile in tiles:
            q_tile = q_ref[rows, tile * LANES:(tile + 1) * LANES]
            zero = jnp.zeros_like(q_tile)
            stacked += [jnp.where(low_half, q_tile, zero), jnp.where(low_half, zero, q_tile)]
        s_all = jax.lax.dot_general(jnp.concatenate(stacked, axis=0), k_dup,
                                    (((1,), (1,)), ((), ())),
                                    preferred_element_type=F32)
        probs, denoms = [], []
        for r in range(2 * len(tiles)):
            head = 2 * tiles[0] + r
            s = s_all[r * BLOCK:(r + 1) * BLOCK] + bias_ref[0, head]
            sink = sinks_ref[head] * LOG2E
            m = jnp.max(s, axis=1, keepdims=True)
            e = jnp.exp2(s - m)
            denoms.append(jnp.sum(e, axis=1, keepdims=True) + jnp.exp2(sink - m))
            probs.append(e.astype(BF16))
        o_all = jnp.dot(jnp.concatenate(probs, axis=0), v_dup,
                        preferred_element_type=F32)
        outs = [o_all[r * BLOCK:(r + 1) * BLOCK] / denoms[r] for r in range(2 * len(tiles))]
        for t, tile in enumerate(tiles):
            y = jnp.where(low_half, outs[2 * t], outs[2 * t + 1])
            g_ref = g0_ref if tile < HALF // LANES else g1_ref
            gl = (tile * LANES) % HALF
            gate = g_ref[rows, gl:gl + LANES].astype(F32)
            ya_ref[rows, tile * LANES:(tile + 1) * LANES] = (y * gate).astype(BF16)


ATT_TM = 2 * BLOCK
ATT_TILES_PER_DOT = 2
OUT_CHUNK = 256
PACKED_ROWS = 16


def _attn_out_kernel(sinks_ref, q_ref, g0_ref, g1_ref, k_ref, v_ref, kp_ref, vp_ref,
                     bias0_ref, bias1_ref, yc_ref, x_ref, gate_ref, wout_hbm,
                     lng_ref, lnb_ref, o_ref, ya_ref, wout_ref, stage_ref, sem_ref):
    i = pl.program_id(0)

    @pl.when(i == 0)
    def _():
        ya_ref[...] = jnp.zeros(ya_ref.shape, ya_ref.dtype)
        _load_weight_bf16(wout_hbm, wout_ref, stage_ref, sem_ref,
                          col_scale=gate_ref[...] * (1.0 / ALPHA))

    y = jnp.dot(ya_ref[...], wout_ref[0:D_ATTN, :], preferred_element_type=F32)
    y = y + jnp.dot(yc_ref[...], wout_ref[D_ATTN:D_MODEL, :], preferred_element_type=F32)
    z = x_ref[...] + y
    mu = jnp.mean(z, axis=1, keepdims=True)
    d = z - mu
    var = jnp.mean(d * d, axis=1, keepdims=True)
    o_ref[...] = d * jax.lax.rsqrt(var + LN_EPS / ALPHA ** 2) * lng_ref[...] + lnb_ref[...]

    _attn_tile(sinks_ref, q_ref, g0_ref, g1_ref, k_ref, v_ref, kp_ref, vp_ref,
               [bias0_ref, bias1_ref], ya_ref)


def _attn_tile(sinks_ref, q_ref, g0_ref, g1_ref, k_ref, v_ref, kp_ref, vp_ref,
               bias_refs, ya_ref):
    n_lt = D_KV // LANES
    lt = lambda t: slice(t * LANES, (t + 1) * LANES)

    def block_operands(b):
        r0 = b * BLOCK
        if b == 0:
            k_prev = [kp_ref[:, lt(t)] for t in range(n_lt)]
            v_prev = [vp_ref[:, lt(t)] for t in range(n_lt)]
        else:
            k_prev = [k_ref[r0 - BLOCK:r0, lt(t)] for t in range(n_lt)]
            v_prev = [v_ref[r0 - BLOCK:r0, lt(t)] for t in range(n_lt)]
        k_rows = [jnp.concatenate([k_prev[t], k_ref[r0:r0 + BLOCK, lt(t)]], axis=0)
                  for t in range(n_lt)]
        v_rows = [jnp.concatenate([v_prev[t], v_ref[r0:r0 + BLOCK, lt(t)]], axis=0)
                  for t in range(n_lt)]
        return ([_dup_halves(kt) for kt in k_rows], [_dup_halves(vt) for vt in v_rows])

    for b in range(ATT_TM // BLOCK):
        k_dups, v_dups = block_operands(b)
        for kv in range(N_KV_HEADS):
            _attn_group(q_ref, g0_ref, g1_ref, k_dups[kv // 2][kv % 2], v_dups[kv // 2][kv % 2],
                        bias_refs[b], sinks_ref, b * BLOCK, kv, ya_ref)


def _attn_out_call(proj, bias, sinks, y_conv, x2d, mod, w_out, ln_g, ln_b):
    tm = ATT_TM
    n = SEQ // tm
    clamp = lambda t: jnp.clip(t, 0, n - 1)
    cur = lambda i: clamp(i)
    lag = lambda i: clamp(i - 1)
    kv_prev = lambda i: jnp.maximum(cur(i) * (tm // BLOCK) - 1, 0)
    row = lambda i: (0, 0)
    return pl.pallas_call(
        _attn_out_kernel,
        grid=(n + 1,),
        in_specs=[
            pl.BlockSpec(memory_space=pltpu.SMEM),
            pl.BlockSpec((tm, D_ATTN), lambda i: (cur(i), COL_Q // D_ATTN)),
            pl.BlockSpec((tm, HALF), lambda i: (cur(i), COL_G_ATTN // HALF)),
            pl.BlockSpec((tm, HALF), lambda i: (cur(i), COL_G_ATTN // HALF + 1)),
            pl.BlockSpec((tm, D_KV), lambda i: (cur(i), COL_K // D_KV)),
            pl.BlockSpec((tm, D_KV), lambda i: (cur(i), COL_V // D_KV)),
            pl.BlockSpec((BLOCK, D_KV), lambda i: (kv_prev(i), COL_K // D_KV)),
            pl.BlockSpec((BLOCK, D_KV), lambda i: (kv_prev(i), COL_V // D_KV)),
            pl.BlockSpec((1, N_Q_HEADS, BLOCK, 2 * BLOCK),
                         lambda i: (jnp.minimum(cur(i), 1), 0, 0, 0)),
            pl.BlockSpec((1, N_Q_HEADS, BLOCK, 2 * BLOCK), lambda i: (1, 0, 0, 0)),
            pl.BlockSpec((tm, D_CONV), lambda i: (lag(i), 0)),
            pl.BlockSpec((tm, D_MODEL), lambda i: (lag(i), 0)),
            pl.BlockSpec((1, D_MODEL), lambda i: (0, 2)),
            pl.BlockSpec(memory_space=pl.ANY),
            pl.BlockSpec((1, D_MODEL), row),
            pl.BlockSpec((1, D_MODEL), row),
        ],
        out_specs=pl.BlockSpec((tm, D_MODEL), lambda i: (lag(i), 0)),
        out_shape=jax.ShapeDtypeStruct((SEQ, D_MODEL), F32),
        scratch_shapes=[
            pltpu.VMEM((tm, D_ATTN), BF16),
            pltpu.VMEM((D_MODEL, D_MODEL), BF16),
            pltpu.VMEM((2, D_MODEL, W_STAGE_COLS), F32),
            pltpu.SemaphoreType.DMA((2,)),
        ],
        compiler_params=pltpu.CompilerParams(
            dimension_semantics=("arbitrary",), vmem_limit_bytes=VMEM_LIMIT),
        name="attn_out",
    )(sinks, proj, proj, proj, proj, proj, proj, proj, bias, bias, y_conv, x2d, mod,
      w_out, ln_g, ln_b)


CONV_TM = 256
CONV_HALO = 32
CONV_RC = 64


def _conv_taps(u_ref, cw_ref, cb_ref, acc_ref, tm):
    off = CONV_HALO - (CONV_WIDTH - 1)
    for c in range(D_CONV // LANES):
        cs = slice(c * LANES, (c + 1) * LANES)
        w_rows = [cw_ref[k:k + 1, cs] for k in range(CONV_WIDTH)]
        for r0 in range(0, tm, CONV_RC):
            acc = jnp.broadcast_to(cb_ref[:, cs], (CONV_RC, LANES))
            for k in range(CONV_WIDTH):
                acc = acc + w_rows[k] * u_ref[c, r0 + off + k:r0 + off + k + CONV_RC, :]
            acc_ref[r0:r0 + CONV_RC, cs] = acc


def _conv_kernel(a0_ref, a1_ref, b0_ref, b1_ref, g0_ref, g1_ref, cw_ref, cb_ref,
                 lng_ref, lnb_ref, wpw_ref, bpw_ref, o_ref, u_ref, acc_ref):
    i = pl.program_id(0)
    tm = CONV_TM

    @pl.when(i == 0)
    def _():
        u_ref[:, 0:CONV_HALO, :] = jnp.zeros((D_CONV // LANES, CONV_HALO, LANES), F32)

    @pl.when(i > 0)
    def _():
        u_ref[:, 0:CONV_HALO, :] = u_ref[:, tm:tm + CONV_HALO, :]

    for a_ref, b_ref, c0 in ((a0_ref, b0_ref, 0), (a1_ref, b1_ref, HALF)):
        a = a_ref[...].astype(F32)
        b = b_ref[...].astype(F32)
        u = a * jax.nn.sigmoid(b)
        for c in range(HALF // LANES):
            u_ref[(c0 // LANES) + c, CONV_HALO:CONV_HALO + tm, :] = u[:, c * LANES:(c + 1) * LANES]

    _conv_taps(u_ref, cw_ref, cb_ref, acc_ref, tm)

    v = acc_ref[...]
    mu = jnp.mean(v, axis=1, keepdims=True)
    d = v - mu
    var = jnp.mean(d * d, axis=1, keepdims=True)
    y = d * jax.lax.rsqrt(var + LN_EPS) * lng_ref[...] + lnb_ref[...]
    s = _silu(y).astype(BF16)
    z = jnp.dot(s, wpw_ref[...], preferred_element_type=F32) + bpw_ref[...]
    for g_ref, c0 in ((g0_ref, 0), (g1_ref, HALF)):
        gate = _silu(g_ref[...].astype(F32))
        o_ref[:, c0:c0 + HALF] = (z[:, c0:c0 + HALF] * gate).astype(BF16)


def _conv_call(proj, conv_w, conv_b, ln_g, ln_b, w_pw_bf16, b_pw):
    tm = CONV_TM
    row = lambda i: (0, 0)
    half = lambda col, h: pl.BlockSpec((tm, HALF), lambda i: (i, col // HALF + h))
    return pl.pallas_call(
        _conv_kernel,
        grid=(SEQ // tm,),
        in_specs=[
            half(COL_GLU_A, 0), half(COL_GLU_A, 1),
            half(COL_GLU_B, 0), half(COL_GLU_B, 1),
            half(COL_G_CONV, 0), half(COL_G_CONV, 1),
            pl.BlockSpec((CONV_WIDTH, D_CONV), row),
            pl.BlockSpec((1, D_CONV), row),
            pl.BlockSpec((1, D_CONV), row),
            pl.BlockSpec((1, D_CONV), row),
            pl.BlockSpec((D_CONV, D_CONV), row),
            pl.BlockSpec((1, D_CONV), row),
        ],
        out_specs=pl.BlockSpec((tm, D_CONV), lambda i: (i, 0)),
        out_shape=jax.ShapeDtypeStruct((SEQ, D_CONV), BF16),
        scratch_shapes=[
            pltpu.VMEM((D_CONV // LANES, CONV_HALO + tm, LANES), F32),
            pltpu.VMEM((tm, D_CONV), F32),
        ],
        compiler_params=pltpu.CompilerParams(
            dimension_semantics=("arbitrary",), vmem_limit_bytes=VMEM_LIMIT),
        name="conv",
    )(proj, proj, proj, proj, proj, proj, conv_w, conv_b, ln_g, ln_b, w_pw_bf16, b_pw)


IN_TM = 256
IN_CHUNK = 256
D_QKVG = COL_GLU_A
TAP_ROWS = 64
SYNC_EVERY = 2
LAST_TAP_SYNC_CHUNK = 18


def _token(v):
    t = v[0:SUBLANES, 0:LANES]
    for r in range(SUBLANES, v.shape[0], SUBLANES):
        t = t + v[r:r + SUBLANES, 0:LANES]
    return t


def _zero_of(token):
    return (pltpu.bitcast(token, jnp.uint32) >> 16) >> 16


def _copy_after(src_ref, dst_ref, tokens):
    zero = _zero_of(tokens[0])
    for t in tokens[1:]:
        zero = zero | _zero_of(t)
    bits = pltpu.bitcast(src_ref[...], jnp.uint32)
    reps = (bits.shape[0] // zero.shape[0], bits.shape[1] // zero.shape[1])
    dst_ref[...] = pltpu.bitcast(bits | jnp.tile(zero, reps), dst_ref.dtype)


def _tap_piece(u_ref, cw_ref, cb_ref, acc_ref, c, r0, start_token=None):
    off = CONV_HALO - (CONV_WIDTH - 1)
    cs = slice(c * LANES, (c + 1) * LANES)
    init = jnp.broadcast_to(cb_ref[:, cs], (SUBLANES, LANES))
    if start_token is not None:
        init = pltpu.bitcast(pltpu.bitcast(init, jnp.uint32) | _zero_of(start_token), F32)
    acc = jnp.tile(init, (TAP_ROWS // SUBLANES, 1))
    for k in range(CONV_WIDTH):
        acc = acc + cw_ref[k:k + 1, cs] * u_ref[c, r0 + off + k:r0 + off + k + TAP_ROWS, :]
    acc_ref[r0:r0 + TAP_ROWS, cs] = acc
    return _token(acc)


def _conv_tail(acc_ref, lng_ref, lnb_ref, wpw_ref, bpw_ref, gate_ref, yc_ref):
    v = acc_ref[...]
    mu = jnp.mean(v, axis=1, keepdims=True)
    d = v - mu
    var = jnp.mean(d * d, axis=1, keepdims=True)
    y = d * jax.lax.rsqrt(var + LN_EPS) * lng_ref[...] + lnb_ref[...]
    s = _silu(y).astype(BF16)
    z = jnp.dot(s, wpw_ref[...], preferred_element_type=F32) + bpw_ref[...]
    yc_ref[...] = (z * gate_ref[...]).astype(BF16)
    return _token(z)


W_STAGE_COLS = 256


def _load_weight_bf16(w_hbm, w_ref, stage_ref, sem_ref, col_scale=None):
    rows, cols = w_ref.shape
    n = cols // W_STAGE_COLS

    def chunk_copy(j):
        slot = j % 2
        return pltpu.make_async_copy(
            w_hbm.at[:, pl.ds(j * W_STAGE_COLS, W_STAGE_COLS)],
            stage_ref.at[slot, pl.ds(0, rows), :], sem_ref.at[slot])

    chunk_copy(0).start()
    for j in range(n):
        if j + 1 < n:
            chunk_copy(j + 1).start()
        chunk_copy(j).wait()
        cs = slice(j * W_STAGE_COLS, (j + 1) * W_STAGE_COLS)
        chunk = stage_ref[j % 2, 0:rows, :]
        if col_scale is not None:
            chunk = chunk * col_scale[:, cs]
        w_ref[:, cs] = chunk.astype(BF16)


def _inproj_conv_kernel(x_ref, shift_ref, scale_ref, w_hbm, cw_ref, cb_ref, lng_ref,
                        lnb_ref, wpw_hbm, bpw_ref, qkvg_ref, yc_ref,
                        h0_ref, h1_ref, u_ref, gc_ref, gcp_ref, acc_ref,
                        w_ref, wpw_ref, stage_ref, sem_ref):
    i = pl.program_id(0)
    tm = IN_TM
    pieces = [(c, r0) for c in range(D_CONV // LANES) for r0 in range(0, tm, TAP_ROWS)]

    @pl.when(i == 0)
    def _():
        u_ref[...] = jnp.zeros(u_ref.shape, u_ref.dtype)
        gc_ref[...] = jnp.zeros(gc_ref.shape, gc_ref.dtype)
        _load_weight_bf16(w_hbm, w_ref, stage_ref, sem_ref)
        _load_weight_bf16(wpw_hbm, wpw_ref, stage_ref, sem_ref)

    if True:
        gcp_ref[...] = gc_ref[...]
        h = x_ref[...] * (1.0 + scale_ref[...]) + shift_ref[...]
        h0_ref[...] = h.astype(BF16)
        h_bufs = [h0_ref, h1_ref]
        state = {"cur": 0}

        def sync(tokens):
            src, dst = h_bufs[state["cur"]], h_bufs[1 - state["cur"]]
            _copy_after(src, dst, tokens)
            state["cur"] = 1 - state["cur"]

        def project(col):
            return jnp.dot(h_bufs[state["cur"]][...], w_ref[:, col:col + IN_CHUNK],
                           preferred_element_type=F32)

        cols = list(range(0, D_QKVG, IN_CHUNK)) + list(range(COL_G_CONV, D_IN, IN_CHUNK))
        for c in range(D_CONV // IN_CHUNK):
            cols += [COL_GLU_A + c * IN_CHUNK, COL_GLU_B + c * IN_CHUNK]
        sync_chunks = list(range(SYNC_EVERY, LAST_TAP_SYNC_CHUNK + 1, SYNC_EVERY))
        bounds = [round(s * len(pieces) / len(sync_chunks)) for s in range(len(sync_chunks) + 1)]
        glu_a = None
        pieces_done = 0
        for j, col in enumerate(cols):
            if j in sync_chunks:
                s_idx = sync_chunks.index(j)
                group = pieces[bounds[s_idx]:bounds[s_idx + 1]]
                sync([_tap_piece(u_ref, cw_ref, cb_ref, acc_ref, c, r0) for c, r0 in group])
                pieces_done = bounds[s_idx + 1]
                if pieces_done == len(pieces):
                    _conv_tail(acc_ref, lng_ref, lnb_ref, wpw_ref, bpw_ref, gcp_ref, yc_ref)
            res = project(col)
            if col < COL_K:
                qkvg_ref[:, col:col + IN_CHUNK] = (res * Q_SCALE).astype(BF16)
            elif col < COL_G_ATTN:
                qkvg_ref[:, col:col + IN_CHUNK] = res.astype(BF16)
            elif col < D_QKVG:
                qkvg_ref[:, col:col + IN_CHUNK] = _silu(res).astype(BF16)
            elif col < COL_GLU_B:
                glu_a = res
            elif col < COL_G_CONV:
                u = glu_a * jax.nn.sigmoid(res)
                t0 = (col - COL_GLU_B) // LANES
                for t in range(t0, t0 + IN_CHUNK // LANES):
                    assert pieces_done >= (t + 1) * (tm // TAP_ROWS)
                    u_ref[t, 0:CONV_HALO, :] = u_ref[t, tm:tm + CONV_HALO, :]
                    u_ref[t, CONV_HALO:CONV_HALO + tm, :] = (
                        u[:, (t - t0) * LANES:(t - t0 + 1) * LANES])
            else:
                gc_ref[:, col - COL_G_CONV:col - COL_G_CONV + IN_CHUNK] = _silu(res)


def _inproj_conv_call(x2d, mod, w_in, conv_w, conv_b, ln_g, ln_b, w_pw, b_pw):
    tm = IN_TM
    n = SEQ // tm
    cur = lambda i: jnp.minimum(i, n - 1)
    lag = lambda i: jnp.maximum(i - 1, 0)
    row = lambda i: (0, 0)
    return pl.pallas_call(
        _inproj_conv_kernel,
        grid=(n + 1,),
        in_specs=[
            pl.BlockSpec((tm, D_MODEL), lambda i: (cur(i), 0)),
            pl.BlockSpec((1, D_MODEL), lambda i: (0, 0)),
            pl.BlockSpec((1, D_MODEL), lambda i: (0, 1)),
            pl.BlockSpec(memory_space=pl.ANY),
            pl.BlockSpec((CONV_WIDTH, D_CONV), row),
            pl.BlockSpec((1, D_CONV), row),
            pl.BlockSpec((1, D_CONV), row),
            pl.BlockSpec((1, D_CONV), row),
            pl.BlockSpec(memory_space=pl.ANY),
            pl.BlockSpec((1, D_CONV), row),
        ],
        out_specs=[
            pl.BlockSpec((tm, D_QKVG), lambda i: (cur(i), 0)),
            pl.BlockSpec((tm, D_CONV), lambda i: (lag(i), 0)),
        ],
        out_shape=[
            jax.ShapeDtypeStruct((SEQ, D_QKVG), BF16),
            jax.ShapeDtypeStruct((SEQ, D_CONV), BF16),
        ],
        scratch_shapes=[
            pltpu.VMEM((tm, D_MODEL), BF16),
            pltpu.VMEM((tm, D_MODEL), BF16),
            pltpu.VMEM((D_CONV // LANES, CONV_HALO + tm, LANES), F32),
            pltpu.VMEM((tm, D_CONV), F32),
            pltpu.VMEM((tm, D_CONV), F32),
            pltpu.VMEM((tm, D_CONV), F32),
            pltpu.VMEM((D_MODEL, D_IN), BF16),
            pltpu.VMEM((D_CONV, D_CONV), BF16),
            pltpu.VMEM((2, D_MODEL, W_STAGE_COLS), F32),
            pltpu.SemaphoreType.DMA((2,)),
        ],
        compiler_params=pltpu.CompilerParams(
            dimension_semantics=("arbitrary",), vmem_limit_bytes=VMEM_LIMIT),
        name="inproj_conv",
    )(x2d, mod, mod, w_in, conv_w, conv_b, ln_g, ln_b, w_pw, b_pw)


def kernel(x, c, w_ada, b_ada, w_in, rel_bias, sinks, conv_w, conv_b, conv_ln_g,
           conv_ln_b, w_pw, b_pw, w_out, ln_g, ln_b):
    assert x.shape == (1, SEQ, D_MODEL) and w_in.shape == (DEPTH, D_MODEL, D_IN)
    x2d = x.reshape(SEQ, D_MODEL)

    row = lambda v: v.reshape(1, -1)
    mod = _mod_call(c.reshape(D_MODEL, 1), w_ada.reshape(D_MODEL, 3 * D_MODEL), row(b_ada))
    bias = _bias_call(rel_bias)
    qkvg, y_conv = _inproj_conv_call(
        x2d, mod, w_in.reshape(D_MODEL, D_IN),
        conv_w.reshape(CONV_WIDTH, D_CONV), row(conv_b), row(conv_ln_g), row(conv_ln_b),
        w_pw.reshape(D_CONV, D_CONV), row(b_pw))
    out = _attn_out_call(qkvg, bias, sinks.reshape(N_Q_HEADS), y_conv, x2d, mod,
                         w_out.reshape(D_MODEL, D_MODEL), row(ln_g), row(ln_b))
    return out.reshape(1, SEQ, D_MODEL)
```

```python
import math

import jax
import jax.numpy as jnp
import numpy as np
from jax.experimental import pallas as pl
from jax.experimental.pallas import tpu as pltpu

F32 = jnp.float32
BF16 = jnp.bfloat16

D_MODEL = 2048
SEQ = 8192
HEAD_DIM = 64
D_ATTN = 1024
D_CONV = 1024
N_Q_HEADS = 16
N_KV_HEADS = 4
GQA_GROUP = 4
D_KV = 256
WINDOW = 128
BLOCK = 128
CONV_WIDTH = 31
N_BUCKETS = 32
MAX_DISTANCE = 128
LN_EPS = 1e-5
DEPTH = 1
ALPHA = (2.0 * DEPTH) ** 0.25
D_IN = 2 * D_ATTN + 2 * D_KV + 3 * D_CONV

LANES = 128
SUBLANES = 8

COL_Q = 0
COL_K = COL_Q + D_ATTN
COL_V = COL_K + D_KV
COL_G_ATTN = COL_V + D_KV
COL_GLU_A = COL_G_ATTN + D_ATTN
COL_GLU_B = COL_GLU_A + D_CONV
COL_G_CONV = COL_GLU_B + D_CONV
HALF = 512

VMEM_LIMIT = 56 * 1024 * 1024


def _silu(v):
    return v * jax.nn.sigmoid(v)


def _mod_kernel(c_ref, w_ref, b_ref, o_ref):
    ca = _silu(c_ref[...])
    ca_cols = jnp.transpose(jnp.broadcast_to(ca, (LANES, D_MODEL)))
    for t in range(w_ref.shape[1] // LANES):
        ts = slice(t * LANES, (t + 1) * LANES)
        o_ref[:, ts] = jnp.sum(ca_cols * w_ref[:, ts], axis=0, keepdims=True) + b_ref[:, ts]


def _mod_call(c_row, w_ada, b_ada):
    tn = 1024
    n = w_ada.shape[1]
    return pl.pallas_call(
        _mod_kernel,
        grid=(n // tn,),
        in_specs=[
            pl.BlockSpec((1, D_MODEL), lambda j: (0, 0)),
            pl.BlockSpec((D_MODEL, tn), lambda j: (0, j)),
            pl.BlockSpec((1, tn), lambda j: (0, j)),
        ],
        out_specs=pl.BlockSpec((1, tn), lambda j: (0, j)),
        out_shape=jax.ShapeDtypeStruct((1, n), F32),
        compiler_params=pltpu.CompilerParams(
            dimension_semantics=("arbitrary",), vmem_limit_bytes=VMEM_LIMIT),
        name="mod",
    )(c_row, w_ada, b_ada)


LOG2E = math.log2(math.e)
Q_SCALE = LOG2E * HEAD_DIM ** -0.5
MASKED = -1e30


def _lane_buckets():
    m = np.arange(2 * BLOCK)
    dist = BLOCK - m
    ok = (dist >= 0) & (dist < WINDOW)
    max_exact = N_BUCKETS // 2
    d = np.maximum(dist, 1).astype(np.float64)
    large = max_exact + (np.log(d / max_exact) / math.log(MAX_DISTANCE / max_exact)
                         * (N_BUCKETS - max_exact)).astype(np.int32)
    large = np.minimum(large, N_BUCKETS - 1)
    bucket = np.where(dist < max_exact, dist, large)
    return np.where(ok, bucket, -1).astype(np.int32).reshape(1, -1)


def _bias_kernel(rb_ref, bm_ref, o_ref):
    bm = bm_ref[...]
    col = jax.lax.broadcasted_iota(jnp.int32, (BLOCK, 2 * BLOCK), 1)
    for h in range(N_Q_HEADS):
        t = jnp.full(bm.shape, MASKED, F32)
        for b in range(N_BUCKETS):
            t = jnp.where(bm == b, rb_ref[b, h] * LOG2E, t)
        x = pltpu.roll(jnp.broadcast_to(t, (BLOCK, 2 * BLOCK)), 0, 1, stride=1, stride_axis=0)
        o_ref[1, h] = x
        o_ref[0, h] = jnp.where(col >= BLOCK, x, MASKED)


def _bias_call(rel_bias):
    shape = (2, N_Q_HEADS, BLOCK, 2 * BLOCK)
    return pl.pallas_call(
        _bias_kernel,
        in_specs=[
            pl.BlockSpec(memory_space=pltpu.SMEM),
            pl.BlockSpec((1, 2 * BLOCK), lambda: (0, 0)),
        ],
        out_specs=pl.BlockSpec(shape, lambda: (0, 0, 0, 0)),
        out_shape=jax.ShapeDtypeStruct(shape, F32),
        name="bias",
    )(rel_bias, jnp.asarray(_lane_buckets()))


W_STAGE_COLS = 256


def _load_weight_bf16(w_hbm, w_ref, stage_ref, sem_ref, col_scale=None):
    rows, cols = w_ref.shape
    n = cols // W_STAGE_COLS

    def chunk_copy(j):
        slot = j % 2
        return pltpu.make_async_copy(
            w_hbm.at[:, pl.ds(j * W_STAGE_COLS, W_STAGE_COLS)],
            stage_ref.at[slot, pl.ds(0, rows), :], sem_ref.at[slot])

    chunk_copy(0).start()
    for j in range(n):
        if j + 1 < n:
            chunk_copy(j + 1).start()
        chunk_copy(j).wait()
        cs = slice(j * W_STAGE_COLS, (j + 1) * W_STAGE_COLS)
        chunk = stage_ref[j % 2, 0:rows, :]
        if col_scale is not None:
            chunk = chunk * col_scale[:, cs]
        w_ref[:, cs] = chunk.astype(BF16)


def _token(v):
    t = v[0:SUBLANES, 0:LANES]
    for r in range(SUBLANES, v.shape[0], SUBLANES):
        t = t + v[r:r + SUBLANES, 0:LANES]
    return t


def _zero_of(token):
    return (pltpu.bitcast(token, jnp.uint32) >> 16) >> 16


def _copy_after(src_ref, dst_ref, tokens):
    zero = _zero_of(tokens[0])
    for t in tokens[1:]:
        zero = zero | _zero_of(t)
    bits = pltpu.bitcast(src_ref[...], jnp.uint32)
    reps = (bits.shape[0] // zero.shape[0], bits.shape[1] // zero.shape[1])
    dst_ref[...] = pltpu.bitcast(bits | jnp.tile(zero, reps), dst_ref.dtype)


ATT_TM = 2 * BLOCK


def _dup_halves(t):
    swapped = jnp.concatenate([t[:, HEAD_DIM:], t[:, :HEAD_DIM]], axis=1)
    lane = jax.lax.broadcasted_iota(jnp.int32, t.shape, 1)
    low_half = lane < HEAD_DIM
    return jnp.where(low_half, t, swapped), jnp.where(low_half, swapped, t)


def _attn_group(q_ref, g0_ref, g1_ref, k_dup, v_dup, bias_ref, sinks_ref, r0, kv, ya_ref):
    lane = jax.lax.broadcasted_iota(jnp.int32, (BLOCK, LANES), 1)
    low_half = lane < HEAD_DIM
    rows = slice(r0, r0 + BLOCK)
    tiles = [GQA_GROUP * kv // 2 + t for t in range(GQA_GROUP // 2)]
    stacked = []
    for tile in tiles:
        q_tile = q_ref[rows, tile * LANES:(tile + 1) * LANES]
        zero = jnp.zeros_like(q_tile)
        stacked += [jnp.where(low_half, q_tile, zero), jnp.where(low_half, zero, q_tile)]
    s_all = jax.lax.dot_general(jnp.concatenate(stacked, axis=0), k_dup,
                                (((1,), (1,)), ((), ())),
                                preferred_element_type=F32)
    probs, denoms = [], []
    for r in range(GQA_GROUP):
        head = GQA_GROUP * kv + r
        s = s_all[r * BLOCK:(r + 1) * BLOCK] + bias_ref[0, head]
        sink = sinks_ref[head] * LOG2E
        m = jnp.max(s, axis=1, keepdims=True)
        e = jnp.exp2(s - m)
        denoms.append(jnp.sum(e, axis=1, keepdims=True) + jnp.exp2(sink - m))
        probs.append(e.astype(BF16))
    o_all = jnp.dot(jnp.concatenate(probs, axis=0), v_dup,
                    preferred_element_type=F32)
    outs = [o_all[r * BLOCK:(r + 1) * BLOCK] / denoms[r] for r in range(GQA_GROUP)]
    for t, tile in enumerate(tiles):
        y = jnp.where(low_half, outs[2 * t], outs[2 * t + 1])
        g_ref = g0_ref if tile < HALF // LANES else g1_ref
        gl = (tile * LANES) % HALF
        gate = g_ref[rows, gl:gl + LANES].astype(F32)
        ya_ref[rows, tile * LANES:(tile + 1) * LANES] = (y * gate).astype(BF16)


def _attn_tile(sinks_ref, q_ref, g0_ref, g1_ref, k_ref, v_ref, kp_ref, vp_ref,
               bias_refs, ya_ref):
    n_lt = D_KV // LANES
    lt = lambda t: slice(t * LANES, (t + 1) * LANES)

    def block_operands(b):
        r0 = b * BLOCK
        if b == 0:
            k_prev = [kp_ref[:, lt(t)] for t in range(n_lt)]
            v_prev = [vp_ref[:, lt(t)] for t in range(n_lt)]
        else:
            k_prev = [k_ref[r0 - BLOCK:r0, lt(t)] for t in range(n_lt)]
            v_prev = [v_ref[r0 - BLOCK:r0, lt(t)] for t in range(n_lt)]
        k_rows = [jnp.concatenate([k_prev[t], k_ref[r0:r0 + BLOCK, lt(t)]], axis=0)
                  for t in range(n_lt)]
        v_rows = [jnp.concatenate([v_prev[t], v_ref[r0:r0 + BLOCK, lt(t)]], axis=0)
                  for t in range(n_lt)]
        return ([_dup_halves(kt) for kt in k_rows], [_dup_halves(vt) for vt in v_rows])

    for b in range(ATT_TM // BLOCK):
        k_dups, v_dups = block_operands(b)
        for kv in range(N_KV_HEADS):
            _attn_group(q_ref, g0_ref, g1_ref, k_dups[kv // 2][kv % 2], v_dups[kv // 2][kv % 2],
                        bias_refs[b], sinks_ref, b * BLOCK, kv, ya_ref)


def _attn_out_kernel(sinks_ref, q_ref, g0_ref, g1_ref, k_ref, v_ref, kp_ref, vp_ref,
                     bias0_ref, bias1_ref, yc_ref, x_ref, gate_ref, wout_hbm,
                     lng_ref, lnb_ref, o_ref, ya_ref, wout_ref, stage_ref, sem_ref):
    i = pl.program_id(0)

    @pl.when(i == 0)
    def _():
        ya_ref[...] = jnp.zeros(ya_ref.shape, ya_ref.dtype)
        _load_weight_bf16(wout_hbm, wout_ref, stage_ref, sem_ref,
                          col_scale=gate_ref[...] * (1.0 / ALPHA))

    y = jnp.dot(ya_ref[...], wout_ref[0:D_ATTN, :], preferred_element_type=F32)
    y = y + jnp.dot(yc_ref[...], wout_ref[D_ATTN:D_MODEL, :], preferred_element_type=F32)
    z = x_ref[...] + y
    mu = jnp.mean(z, axis=1, keepdims=True)
    d = z - mu
    var = jnp.mean(d * d, axis=1, keepdims=True)
    o_ref[...] = d * jax.lax.rsqrt(var + LN_EPS / ALPHA ** 2) * lng_ref[...] + lnb_ref[...]

    _attn_tile(sinks_ref, q_ref, g0_ref, g1_ref, k_ref, v_ref, kp_ref, vp_ref,
               [bias0_ref, bias1_ref], ya_ref)


def _attn_out_call(qkvg, bias, sinks, y_conv, x2d, mod, w_out, ln_g, ln_b):
    tm = ATT_TM
    n = SEQ // tm
    clamp = lambda t: jnp.clip(t, 0, n - 1)
    cur = lambda i: clamp(i)
    lag = lambda i: clamp(i - 1)
    kv_prev = lambda i: jnp.maximum(cur(i) * (tm // BLOCK) - 1, 0)
    row = lambda i: (0, 0)
    return pl.pallas_call(
        _attn_out_kernel,
        grid=(n + 1,),
        in_specs=[
            pl.BlockSpec(memory_space=pltpu.SMEM),
            pl.BlockSpec((tm, D_ATTN), lambda i: (cur(i), COL_Q // D_ATTN)),
            pl.BlockSpec((tm, HALF), lambda i: (cur(i), COL_G_ATTN // HALF)),
            pl.BlockSpec((tm, HALF), lambda i: (cur(i), COL_G_ATTN // HALF + 1)),
            pl.BlockSpec((tm, D_KV), lambda i: (cur(i), COL_K // D_KV)),
            pl.BlockSpec((tm, D_KV), lambda i: (cur(i), COL_V // D_KV)),
            pl.BlockSpec((BLOCK, D_KV), lambda i: (kv_prev(i), COL_K // D_KV)),
            pl.BlockSpec((BLOCK, D_KV), lambda i: (kv_prev(i), COL_V // D_KV)),
            pl.BlockSpec((1, N_Q_HEADS, BLOCK, 2 * BLOCK),
                         lambda i: (jnp.minimum(cur(i), 1), 0, 0, 0)),
            pl.BlockSpec((1, N_Q_HEADS, BLOCK, 2 * BLOCK), lambda i: (1, 0, 0, 0)),
            pl.BlockSpec((tm, D_CONV), lambda i: (lag(i), 0)),
            pl.BlockSpec((tm, D_MODEL), lambda i: (lag(i), 0)),
            pl.BlockSpec((1, D_MODEL), lambda i: (0, 2)),
            pl.BlockSpec(memory_space=pl.ANY),
            pl.BlockSpec((1, D_MODEL), row),
            pl.BlockSpec((1, D_MODEL), row),
        ],
        out_specs=pl.BlockSpec((tm, D_MODEL), lambda i: (lag(i), 0)),
        out_shape=jax.ShapeDtypeStruct((SEQ, D_MODEL), F32),
        scratch_shapes=[
            pltpu.VMEM((tm, D_ATTN), BF16),
            pltpu.VMEM((D_MODEL, D_MODEL), BF16),
            pltpu.VMEM((2, D_MODEL, W_STAGE_COLS), F32),
            pltpu.SemaphoreType.DMA((2,)),
        ],
        compiler_params=pltpu.CompilerParams(
            dimension_semantics=("arbitrary",), vmem_limit_bytes=VMEM_LIMIT),
        name="attn_out",
    )(sinks, qkvg, qkvg, qkvg, qkvg, qkvg, qkvg, qkvg, bias, bias, y_conv, x2d, mod,
      w_out, ln_g, ln_b)


IN_TM = 256
IN_CHUNK = 256
D_QKVG = COL_GLU_A
CONV_HALO = 32
TAP_ROWS = 64
SYNC_EVERY = 2
LAST_TAP_SYNC_CHUNK = 18


def _tap_piece(u_ref, cw_ref, cb_ref, acc_ref, c, r0):
    off = CONV_HALO - (CONV_WIDTH - 1)
    cs = slice(c * LANES, (c + 1) * LANES)
    acc = jnp.broadcast_to(cb_ref[:, cs], (TAP_ROWS, LANES))
    for k in range(CONV_WIDTH):
        acc = acc + cw_ref[0, k:k + 1, cs] * u_ref[c, r0 + off + k:r0 + off + k + TAP_ROWS, :]
    acc_ref[r0:r0 + TAP_ROWS, cs] = acc
    return _token(acc)


def _conv_tail(acc_ref, lng_ref, lnb_ref, wpw_ref, bpw_ref, gate_ref, yc_ref):
    v = acc_ref[...]
    mu = jnp.mean(v, axis=1, keepdims=True)
    d = v - mu
    var = jnp.mean(d * d, axis=1, keepdims=True)
    y = d * jax.lax.rsqrt(var + LN_EPS) * lng_ref[...] + lnb_ref[...]
    s = _silu(y).astype(BF16)
    z = jnp.dot(s, wpw_ref[...], preferred_element_type=F32) + bpw_ref[...]
    yc_ref[...] = (z * gate_ref[...]).astype(BF16)


def _inproj_conv_kernel(x_ref, shift_ref, scale_ref, w_hbm, cw_ref, cb_ref, lng_ref,
                        lnb_ref, wpw_hbm, bpw_ref, qkvg_ref, yc_ref,
                        h0_ref, h1_ref, u_ref, gc_ref, gcp_ref, acc_ref,
                        w_ref, wpw_ref, stage_ref, sem_ref):
    i = pl.program_id(0)
    tm = IN_TM
    pieces = [(c, r0) for c in range(D_CONV // LANES) for r0 in range(0, tm, TAP_ROWS)]

    @pl.when(i == 0)
    def _():
        u_ref[...] = jnp.zeros(u_ref.shape, u_ref.dtype)
        gc_ref[...] = jnp.zeros(gc_ref.shape, gc_ref.dtype)
        _load_weight_bf16(w_hbm, w_ref, stage_ref, sem_ref)
        _load_weight_bf16(wpw_hbm, wpw_ref, stage_ref, sem_ref)

    gcp_ref[...] = gc_ref[...]
    h = x_ref[...] * (1.0 + scale_ref[...]) + shift_ref[...]
    h0_ref[...] = h.astype(BF16)
    h_bufs = [h0_ref, h1_ref]
    state = {"cur": 0}

    def sync(tokens):
        src, dst = h_bufs[state["cur"]], h_bufs[1 - state["cur"]]
        _copy_after(src, dst, tokens)
        state["cur"] = 1 - state["cur"]

    def project(col):
        return jnp.dot(h_bufs[state["cur"]][...], w_ref[:, col:col + IN_CHUNK],
                       preferred_element_type=F32)

    cols = list(range(0, D_QKVG, IN_CHUNK)) + list(range(COL_G_CONV, D_IN, IN_CHUNK))
    for c in range(D_CONV // IN_CHUNK):
        cols += [COL_GLU_A + c * IN_CHUNK, COL_GLU_B + c * IN_CHUNK]
    sync_chunks = list(range(SYNC_EVERY, LAST_TAP_SYNC_CHUNK + 1, SYNC_EVERY))
    bounds = [round(s * len(pieces) / len(sync_chunks)) for s in range(len(sync_chunks) + 1)]
    glu_a = None
    pieces_done = 0
    for j, col in enumerate(cols):
        if j in sync_chunks:
            s_idx = sync_chunks.index(j)
            group = pieces[bounds[s_idx]:bounds[s_idx + 1]]
            sync([_tap_piece(u_ref, cw_ref, cb_ref, acc_ref, c, r0) for c, r0 in group])
            pieces_done = bounds[s_idx + 1]
            if pieces_done == len(pieces):
                _conv_tail(acc_ref, lng_ref, lnb_ref, wpw_ref, bpw_ref, gcp_ref, yc_ref)
        res = project(col)
        if col < COL_K:
            qkvg_ref[:, col:col + IN_CHUNK] = (res * Q_SCALE).astype(BF16)
        elif col < COL_G_ATTN:
            qkvg_ref[:, col:col + IN_CHUNK] = res.astype(BF16)
        elif col < D_QKVG:
            qkvg_ref[:, col:col + IN_CHUNK] = _silu(res).astype(BF16)
        elif col < COL_GLU_B:
            glu_a = res
        elif col < COL_G_CONV:
            u = glu_a * jax.nn.sigmoid(res)
            t0 = (col - COL_GLU_B) // LANES
            for t in range(t0, t0 + IN_CHUNK // LANES):
                assert pieces_done >= (t + 1) * (tm // TAP_ROWS)
                u_ref[t, 0:CONV_HALO, :] = u_ref[t, tm:tm + CONV_HALO, :]
                u_ref[t, CONV_HALO:CONV_HALO + tm, :] = u[:, (t - t0) * LANES:(t - t0 + 1) * LANES]
        else:
            gc_ref[:, col - COL_G_CONV:col - COL_G_CONV + IN_CHUNK] = _silu(res)


def _inproj_conv_call(x2d, mod, w_in, conv_w, conv_b, ln_g, ln_b, w_pw, b_pw):
    tm = IN_TM
    n = SEQ // tm
    cur = lambda i: jnp.minimum(i, n - 1)
    lag = lambda i: jnp.maximum(i - 1, 0)
    row = lambda i: (0, 0)
    return pl.pallas_call(
        _inproj_conv_kernel,
        grid=(n + 1,),
        in_specs=[
            pl.BlockSpec((tm, D_MODEL), lambda i: (cur(i), 0)),
            pl.BlockSpec((1, D_MODEL), lambda i: (0, 0)),
            pl.BlockSpec((1, D_MODEL), lambda i: (0, 1)),
            pl.BlockSpec(memory_space=pl.ANY),
            pl.BlockSpec((DEPTH, CONV_WIDTH, D_CONV), lambda i: (0, 0, 0)),
            pl.BlockSpec((1, D_CONV), row),
            pl.BlockSpec((1, D_CONV), row),
            pl.BlockSpec((1, D_CONV), row),
            pl.BlockSpec(memory_space=pl.ANY),
            pl.BlockSpec((1, D_CONV), row),
        ],
        out_specs=[
            pl.BlockSpec((tm, D_QKVG), lambda i: (cur(i), 0)),
            pl.BlockSpec((tm, D_CONV), lambda i: (lag(i), 0)),
        ],
        out_shape=[
            jax.ShapeDtypeStruct((SEQ, D_QKVG), BF16),
            jax.ShapeDtypeStruct((SEQ, D_CONV), BF16),
        ],
        scratch_shapes=[
            pltpu.VMEM((tm, D_MODEL), BF16),
            pltpu.VMEM((tm, D_MODEL), BF16),
            pltpu.VMEM((D_CONV // LANES, CONV_HALO + tm, LANES), F32),
            pltpu.VMEM((tm, D_CONV), F32),
            pltpu.VMEM((tm, D_CONV), F32),
            pltpu.VMEM((tm, D_CONV), F32),
            pltpu.VMEM((D_MODEL, D_IN), BF16),
            pltpu.VMEM((D_CONV, D_CONV), BF16),
            pltpu.VMEM((2, D_MODEL, W_STAGE_COLS), F32),
            pltpu.SemaphoreType.DMA((2,)),
        ],
        compiler_params=pltpu.CompilerParams(
            dimension_semantics=("arbitrary",), vmem_limit_bytes=VMEM_LIMIT),
        name="inproj_conv",
    )(x2d, mod, mod, w_in, conv_w, conv_b, ln_g, ln_b, w_pw, b_pw)


def kernel(x, c, w_ada, b_ada, w_in, rel_bias, sinks, conv_w, conv_b, conv_ln_g,
           conv_ln_b, w_pw, b_pw, w_out, ln_g, ln_b):
    assert x.shape == (1, SEQ, D_MODEL) and w_in.shape == (DEPTH, D_MODEL, D_IN)
    x2d = x.reshape(SEQ, D_MODEL)

    row = lambda v: v.reshape(1, -1)
    mod = _mod_call(row(c), w_ada.reshape(D_MODEL, 3 * D_MODEL), row(b_ada))
    bias = _bias_call(rel_bias)
    qkvg, y_conv = _inproj_conv_call(
        x2d, mod, w_in.reshape(D_MODEL, D_IN),
        conv_w, row(conv_b), row(conv_ln_g), row(conv_ln_b),
        w_pw.reshape(D_CONV, D_CONV), row(b_pw))
    out = _attn_out_call(qkvg, bias, sinks.reshape(N_Q_HEADS), y_conv, x2d, mod,
                         w_out.reshape(D_MODEL, D_MODEL), row(ln_g), row(ln_b))
    return out.reshape(1, SEQ, D_MODEL)
```

```python
import math

import jax
import jax.numpy as jnp
import numpy as np
from jax.experimental import pallas as pl
from jax.experimental.pallas import tpu as pltpu

F32 = jnp.float32
BF16 = jnp.bfloat16

D_MODEL = 2048
SEQ = 8192
HEAD_DIM = 64
D_ATTN = 1024
D_CONV = 1024
N_Q_HEADS = 16
N_KV_HEADS = 4
GQA_GROUP = 4
D_KV = 256
WINDOW = 128
BLOCK = 128
CONV_WIDTH = 31
N_BUCKETS = 32
MAX_DISTANCE = 128
LN_EPS = 1e-5
DEPTH = 1
ALPHA = (2.0 * DEPTH) ** 0.25
D_IN = 2 * D_ATTN + 2 * D_KV + 3 * D_CONV

LANES = 128
SUBLANES = 8

COL_Q = 0
COL_K = COL_Q + D_ATTN
COL_V = COL_K + D_KV
COL_G_ATTN = COL_V + D_KV
COL_GLU_A = COL_G_ATTN + D_ATTN
COL_GLU_B = COL_GLU_A + D_CONV
COL_G_CONV = COL_GLU_B + D_CONV
HALF = 512

VMEM_LIMIT = 56 * 1024 * 1024


def _silu(v):
    return v * jax.nn.sigmoid(v)


def _mod_kernel(c_ref, w_ref, b_ref, o_ref):
    ca = _silu(c_ref[...])
    ca_cols = jnp.transpose(jnp.broadcast_to(ca, (LANES, D_MODEL)))
    for t in range(w_ref.shape[1] // LANES):
        ts = slice(t * LANES, (t + 1) * LANES)
        o_ref[:, ts] = jnp.sum(ca_cols * w_ref[:, ts], axis=0, keepdims=True) + b_ref[:, ts]


def _mod_call(c_row, w_ada, b_ada):
    tn = 512
    n = w_ada.shape[1]
    return pl.pallas_call(
        _mod_kernel,
        grid=(n // tn,),
        in_specs=[
            pl.BlockSpec((1, D_MODEL), lambda j: (0, 0)),
            pl.BlockSpec((D_MODEL, tn), lambda j: (0, j)),
            pl.BlockSpec((1, tn), lambda j: (0, j)),
        ],
        out_specs=pl.BlockSpec((1, tn), lambda j: (0, j)),
        out_shape=jax.ShapeDtypeStruct((1, n), F32),
        compiler_params=pltpu.CompilerParams(
            dimension_semantics=("arbitrary",), vmem_limit_bytes=VMEM_LIMIT),
        name="mod",
    )(c_row, w_ada, b_ada)


LOG2E = math.log2(math.e)
Q_SCALE = LOG2E * HEAD_DIM ** -0.5
MASKED = -1e30


def _lane_buckets():
    m = np.arange(2 * BLOCK)
    dist = BLOCK - m
    ok = (dist >= 0) & (dist < WINDOW)
    max_exact = N_BUCKETS // 2
    d = np.maximum(dist, 1).astype(np.float64)
    large = max_exact + (np.log(d / max_exact) / math.log(MAX_DISTANCE / max_exact)
                         * (N_BUCKETS - max_exact)).astype(np.int32)
    large = np.minimum(large, N_BUCKETS - 1)
    bucket = np.where(dist < max_exact, dist, large)
    return np.where(ok, bucket, -1).astype(np.int32).reshape(1, -1)


def _bias_kernel(rb_ref, bm_ref, o_ref):
    bm = bm_ref[...]
    col = jax.lax.broadcasted_iota(jnp.int32, (BLOCK, 2 * BLOCK), 1)
    for h in range(N_Q_HEADS):
        t = jnp.full(bm.shape, MASKED, F32)
        for b in range(N_BUCKETS):
            t = jnp.where(bm == b, rb_ref[b, h] * LOG2E, t)
        x = pltpu.roll(jnp.broadcast_to(t, (BLOCK, 2 * BLOCK)), 0, 1, stride=1, stride_axis=0)
        o_ref[1, h] = x
        o_ref[0, h] = jnp.where(col >= BLOCK, x, MASKED)


def _bias_call(rel_bias):
    shape = (2, N_Q_HEADS, BLOCK, 2 * BLOCK)
    return pl.pallas_call(
        _bias_kernel,
        in_specs=[
            pl.BlockSpec(memory_space=pltpu.SMEM),
            pl.BlockSpec((1, 2 * BLOCK), lambda: (0, 0)),
        ],
        out_specs=pl.BlockSpec(shape, lambda: (0, 0, 0, 0)),
        out_shape=jax.ShapeDtypeStruct(shape, F32),
        name="bias",
    )(rel_bias, jnp.asarray(_lane_buckets()))


W_STAGE_COLS = 256


def _load_weight_bf16(w_hbm, w_ref, stage_ref, sem_ref, col_scale=None):
    rows, cols = w_ref.shape
    n = cols // W_STAGE_COLS

    def chunk_copy(j):
        slot = j % 2
        return pltpu.make_async_copy(
            w_hbm.at[:, pl.ds(j * W_STAGE_COLS, W_STAGE_COLS)],
            stage_ref.at[slot, pl.ds(0, rows), :], sem_ref.at[slot])

    chunk_copy(0).start()
    for j in range(n):
        if j + 1 < n:
            chunk_copy(j + 1).start()
        chunk_copy(j).wait()
        cs = slice(j * W_STAGE_COLS, (j + 1) * W_STAGE_COLS)
        chunk = stage_ref[j % 2, 0:rows, :]
        if col_scale is not None:
            chunk = chunk * col_scale[:, cs]
        w_ref[:, cs] = chunk.astype(BF16)


def _token(v):
    t = v[0:SUBLANES, 0:LANES]
    for r in range(SUBLANES, v.shape[0], SUBLANES):
        t = t + v[r:r + SUBLANES, 0:LANES]
    return t


def _zero_of(token):
    return (pltpu.bitcast(token, jnp.uint32) >> 16) >> 16


def _copy_after(src_ref, dst_ref, tokens):
    zero = _zero_of(tokens[0])
    for t in tokens[1:]:
        zero = zero | _zero_of(t)
    bits = pltpu.bitcast(src_ref[...], jnp.uint32)
    reps = (bits.shape[0] // zero.shape[0], bits.shape[1] // zero.shape[1])
    dst_ref[...] = pltpu.bitcast(bits | jnp.tile(zero, reps), dst_ref.dtype)


ATT_TM = 2 * BLOCK


def _dup_halves(t):
    swapped = jnp.concatenate([t[:, HEAD_DIM:], t[:, :HEAD_DIM]], axis=1)
    lane = jax.lax.broadcasted_iota(jnp.int32, t.shape, 1)
    low_half = lane < HEAD_DIM
    return jnp.where(low_half, t, swapped), jnp.where(low_half, swapped, t)


def _attn_group(q_ref, g0_ref, g1_ref, k_dup, v_dup, bias_ref, sinks_ref, r0, kv, ya_ref):
    lane = jax.lax.broadcasted_iota(jnp.int32, (BLOCK, LANES), 1)
    low_half = lane < HEAD_DIM
    rows = slice(r0, r0 + BLOCK)
    tiles = [GQA_GROUP * kv // 2 + t for t in range(GQA_GROUP // 2)]
    stacked = []
    for tile in tiles:
        q_tile = q_ref[rows, tile * LANES:(tile + 1) * LANES]
        zero = jnp.zeros_like(q_tile)
        stacked += [jnp.where(low_half, q_tile, zero), jnp.where(low_half, zero, q_tile)]
    s_all = jax.lax.dot_general(jnp.concatenate(stacked, axis=0), k_dup,
                                (((1,), (1,)), ((), ())),
                                preferred_element_type=F32)
    probs, denoms = [], []
    for r in range(GQA_GROUP):
        head = GQA_GROUP * kv + r
        s = s_all[r * BLOCK:(r + 1) * BLOCK] + bias_ref[0, head]
        sink = sinks_ref[head] * LOG2E
        m = jnp.max(s, axis=1, keepdims=True)
        e = jnp.exp2(s - m)
        denoms.append(jnp.sum(e, axis=1, keepdims=True) + jnp.exp2(sink - m))
        probs.append(e.astype(BF16))
    o_all = jnp.dot(jnp.concatenate(probs, axis=0), v_dup,
                    preferred_element_type=F32)
    outs = [o_all[r * BLOCK:(r + 1) * BLOCK] / denoms[r] for r in range(GQA_GROUP)]
    for t, tile in enumerate(tiles):
        y = jnp.where(low_half, outs[2 * t], outs[2 * t + 1])
        g_ref = g0_ref if tile < HALF // LANES else g1_ref
        gl = (tile * LANES) % HALF
        gate = g_ref[rows, gl:gl + LANES].astype(F32)
        ya_ref[rows, tile * LANES:(tile + 1) * LANES] = (y * gate).astype(BF16)


def _attn_tile(sinks_ref, q_ref, g0_ref, g1_ref, k_ref, v_ref, kp_ref, vp_ref,
               bias_refs, ya_ref):
    n_lt = D_KV // LANES
    lt = lambda t: slice(t * LANES, (t + 1) * LANES)

    def block_operands(b):
        r0 = b * BLOCK
        if b == 0:
            k_prev = [kp_ref[:, lt(t)] for t in range(n_lt)]
            v_prev = [vp_ref[:, lt(t)] for t in range(n_lt)]
        else:
            k_prev = [k_ref[r0 - BLOCK:r0, lt(t)] for t in range(n_lt)]
            v_prev = [v_ref[r0 - BLOCK:r0, lt(t)] for t in range(n_lt)]
        k_rows = [jnp.concatenate([k_prev[t], k_ref[r0:r0 + BLOCK, lt(t)]], axis=0)
                  for t in range(n_lt)]
        v_rows = [jnp.concatenate([v_prev[t], v_ref[r0:r0 + BLOCK, lt(t)]], axis=0)
                  for t in range(n_lt)]
        return ([_dup_halves(kt) for kt in k_rows], [_dup_halves(vt) for vt in v_rows])

    for b in range(ATT_TM // BLOCK):
        k_dups, v_dups = block_operands(b)
        for kv in range(N_KV_HEADS):
            _attn_group(q_ref, g0_ref, g1_ref, k_dups[kv // 2][kv % 2], v_dups[kv // 2][kv % 2],
                        bias_refs[b], sinks_ref, b * BLOCK, kv, ya_ref)


def _attn_out_kernel(sinks_ref, q_ref, g0_ref, g1_ref, k_ref, v_ref, kp_ref, vp_ref,
                     bias0_ref, bias1_ref, yc_ref, x_ref, gate_ref, wout_hbm,
                     lng_ref, lnb_ref, o_ref, ya_ref, wout_ref, stage_ref, sem_ref):
    i = pl.program_id(0)

    @pl.when(i == 0)
    def _():
        ya_ref[...] = jnp.zeros(ya_ref.shape, ya_ref.dtype)
        _load_weight_bf16(wout_hbm, wout_ref, stage_ref, sem_ref,
                          col_scale=gate_ref[...] * (1.0 / ALPHA))

    y = jnp.dot(ya_ref[...], wout_ref[0:D_ATTN, :], preferred_element_type=F32)
    y = y + jnp.dot(yc_ref[...], wout_ref[D_ATTN:D_MODEL, :], preferred_element_type=F32)
    z = x_ref[...] + y
    mu = jnp.mean(z, axis=1, keepdims=True)
    d = z - mu
    var = jnp.mean(d * d, axis=1, keepdims=True)
    o_ref[...] = d * jax.lax.rsqrt(var + LN_EPS / ALPHA ** 2) * lng_ref[...] + lnb_ref[...]

    _attn_tile(sinks_ref, q_ref, g0_ref, g1_ref, k_ref, v_ref, kp_ref, vp_ref,
               [bias0_ref, bias1_ref], ya_ref)


def _attn_out_call(qkvg, bias, sinks, y_conv, x2d, mod, w_out, ln_g, ln_b):
    tm = ATT_TM
    n = SEQ // tm
    clamp = lambda t: jnp.clip(t, 0, n - 1)
    cur = lambda i: clamp(i)
    lag = lambda i: clamp(i - 1)
    kv_prev = lambda i: jnp.maximum(cur(i) * (tm // BLOCK) - 1, 0)
    row = lambda i: (0, 0)
    return pl.pallas_call(
        _attn_out_kernel,
        grid=(n + 1,),
        in_specs=[
            pl.BlockSpec(memory_space=pltpu.SMEM),
            pl.BlockSpec((tm, D_ATTN), lambda i: (cur(i), COL_Q // D_ATTN)),
            pl.BlockSpec((tm, HALF), lambda i: (cur(i), COL_G_ATTN // HALF)),
            pl.BlockSpec((tm, HALF), lambda i: (cur(i), COL_G_ATTN // HALF + 1)),
            pl.BlockSpec((tm, D_KV), lambda i: (cur(i), COL_K // D_KV)),
            pl.BlockSpec((tm, D_KV), lambda i: (cur(i), COL_V // D_KV)),
            pl.BlockSpec((BLOCK, D_KV), lambda i: (kv_prev(i), COL_K // D_KV)),
            pl.BlockSpec((BLOCK, D_KV), lambda i: (kv_prev(i), COL_V // D_KV)),
            pl.BlockSpec((1, N_Q_HEADS, BLOCK, 2 * BLOCK),
                         lambda i: (jnp.minimum(cur(i), 1), 0, 0, 0)),
            pl.BlockSpec((1, N_Q_HEADS, BLOCK, 2 * BLOCK), lambda i: (1, 0, 0, 0)),
            pl.BlockSpec((tm, D_CONV), lambda i: (lag(i), 0)),
            pl.BlockSpec((tm, D_MODEL), lambda i: (lag(i), 0)),
            pl.BlockSpec((1, D_MODEL), lambda i: (0, 2)),
            pl.BlockSpec(memory_space=pl.ANY),
            pl.BlockSpec((1, D_MODEL), row),
            pl.BlockSpec((1, D_MODEL), row),
        ],
        out_specs=pl.BlockSpec((tm, D_MODEL), lambda i: (lag(i), 0)),
        out_shape=jax.ShapeDtypeStruct((SEQ, D_MODEL), F32),
        scratch_shapes=[
            pltpu.VMEM((tm, D_ATTN), BF16),
            pltpu.VMEM((D_MODEL, D_MODEL), BF16),
            pltpu.VMEM((2, D_MODEL, W_STAGE_COLS), F32),
            pltpu.SemaphoreType.DMA((2,)),
        ],
        compiler_params=pltpu.CompilerParams(
            dimension_semantics=("arbitrary",), vmem_limit_bytes=VMEM_LIMIT),
        name="attn_out",
    )(sinks, qkvg, qkvg, qkvg, qkvg, qkvg, qkvg, qkvg, bias, bias, y_conv, x2d, mod,
      w_out, ln_g, ln_b)


IN_TM = 256
IN_CHUNK = 256
D_QKVG = COL_GLU_A
CONV_HALO = 32
TAP_ROWS = 64
SYNC_EVERY = 2
LAST_TAP_SYNC_CHUNK = 18


def _tap_piece(u_ref, cw_ref, cb_ref, acc_ref, c, r0):
    off = CONV_HALO - (CONV_WIDTH - 1)
    cs = slice(c * LANES, (c + 1) * LANES)
    acc = jnp.broadcast_to(cb_ref[:, cs], (TAP_ROWS, LANES))
    for k in range(CONV_WIDTH):
        acc = acc + cw_ref[0, k:k + 1, cs] * u_ref[c, r0 + off + k:r0 + off + k + TAP_ROWS, :]
    acc_ref[r0:r0 + TAP_ROWS, cs] = acc
    return _token(acc)


def _conv_tail(acc_ref, lng_ref, lnb_ref, wpw_ref, bpw_ref, gate_ref, yc_ref):
    v = acc_ref[...]
    mu = jnp.mean(v, axis=1, keepdims=True)
    d = v - mu
    var = jnp.mean(d * d, axis=1, keepdims=True)
    y = d * jax.lax.rsqrt(var + LN_EPS) * lng_ref[...] + lnb_ref[...]
    s = _silu(y).astype(BF16)
    z = jnp.dot(s, wpw_ref[...], preferred_element_type=F32) + bpw_ref[...]
    yc_ref[...] = (z * gate_ref[...]).astype(BF16)


def _inproj_conv_kernel(x_ref, shift_ref, scale_ref, w_hbm, cw_ref, cb_ref, lng_ref,
                        lnb_ref, wpw_hbm, bpw_ref, qkvg_ref, yc_ref,
                        h0_ref, h1_ref, u_ref, gc_ref, gcp_ref, acc_ref,
                        w_ref, wpw_ref, stage_ref, sem_ref):
    i = pl.program_id(0)
    tm = IN_TM
    pieces = [(c, r0) for c in range(D_CONV // LANES) for r0 in range(0, tm, TAP_ROWS)]

    @pl.when(i == 0)
    def _():
        u_ref[...] = jnp.zeros(u_ref.shape, u_ref.dtype)
        gc_ref[...] = jnp.zeros(gc_ref.shape, gc_ref.dtype)
        _load_weight_bf16(w_hbm, w_ref, stage_ref, sem_ref)
        _load_weight_bf16(wpw_hbm, wpw_ref, stage_ref, sem_ref)

    gcp_ref[...] = gc_ref[...]
    h = x_ref[...] * (1.0 + scale_ref[...]) + shift_ref[...]
    h0_ref[...] = h.astype(BF16)
    h_bufs = [h0_ref, h1_ref]
    state = {"cur": 0}

    def sync(tokens):
        src, dst = h_bufs[state["cur"]], h_bufs[1 - state["cur"]]
        _copy_after(src, dst, tokens)
        state["cur"] = 1 - state["cur"]

    def project(col):
        return jnp.dot(h_bufs[state["cur"]][...], w_ref[:, col:col + IN_CHUNK],
                       preferred_element_type=F32)

    cols = list(range(0, D_QKVG, IN_CHUNK)) + list(range(COL_G_CONV, D_IN, IN_CHUNK))
    for c in range(D_CONV // IN_CHUNK):
        cols += [COL_GLU_A + c * IN_CHUNK, COL_GLU_B + c * IN_CHUNK]
    sync_chunks = list(range(SYNC_EVERY, LAST_TAP_SYNC_CHUNK + 1, SYNC_EVERY))
    bounds = [round(s * len(pieces) / len(sync_chunks)) for s in range(len(sync_chunks) + 1)]
    glu_a = None
    pieces_done = 0
    for j, col in enumerate(cols):
        if j in sync_chunks:
            s_idx = sync_chunks.index(j)
            group = pieces[bounds[s_idx]:bounds[s_idx + 1]]
            sync([_tap_piece(u_ref, cw_ref, cb_ref, acc_ref, c, r0) for c, r0 in group])
            pieces_done = bounds[s_idx + 1]
            if pieces_done == len(pieces):
                _conv_tail(acc_ref, lng_ref, lnb_ref, wpw_ref, bpw_ref, gcp_ref, yc_ref)
        res = project(col)
        if col < COL_K:
            qkvg_ref[:, col:col + IN_CHUNK] = (res * Q_SCALE).astype(BF16)
        elif col < COL_G_ATTN:
            qkvg_ref[:, col:col + IN_CHUNK] = res.astype(BF16)
        elif col < D_QKVG:
            qkvg_ref[:, col:col + IN_CHUNK] = _silu(res).astype(BF16)
        elif col < COL_GLU_B:
            glu_a = res
        elif col < COL_G_CONV:
            u = glu_a * jax.nn.sigmoid(res)
            t0 = (col - COL_GLU_B) // LANES
            for t in range(t0, t0 + IN_CHUNK // LANES):
                assert pieces_done >= (t + 1) * (tm // TAP_ROWS)
                u_ref[t, 0:CONV_HALO, :] = u_ref[t, tm:tm + CONV_HALO, :]
                u_ref[t, CONV_HALO:CONV_HALO + tm, :] = u[:, (t - t0) * LANES:(t - t0 + 1) * LANES]
        else:
            gc_ref[:, col - COL_G_CONV:col - COL_G_CONV + IN_CHUNK] = _silu(res)


def _inproj_conv_call(x2d, mod, w_in, conv_w, conv_b, ln_g, ln_b, w_pw, b_pw):
    tm = IN_TM
    n = SEQ // tm
    cur = lambda i: jnp.minimum(i, n - 1)
    lag = lambda i: jnp.maximum(i - 1, 0)
    row = lambda i: (0, 0)
    return pl.pallas_call(
        _inproj_conv_kernel,
        grid=(n + 1,),
        in_specs=[
            pl.BlockSpec((tm, D_MODEL), lambda i: (cur(i), 0)),
            pl.BlockSpec((1, D_MODEL), lambda i: (0, 0)),
            pl.BlockSpec((1, D_MODEL), lambda i: (0, 1)),
            pl.BlockSpec(memory_space=pl.ANY),
            pl.BlockSpec((DEPTH, CONV_WIDTH, D_CONV), lambda i: (0, 0, 0)),
            pl.BlockSpec((1, D_CONV), row),
            pl.BlockSpec((1, D_CONV), row),
            pl.BlockSpec((1, D_CONV), row),
            pl.BlockSpec(memory_space=pl.ANY),
            pl.BlockSpec((1, D_CONV), row),
        ],
        out_specs=[
            pl.BlockSpec((tm, D_QKVG), lambda i: (cur(i), 0)),
            pl.BlockSpec((tm, D_CONV), lambda i: (lag(i), 0)),
        ],
        out_shape=[
            jax.ShapeDtypeStruct((SEQ, D_QKVG), BF16),
            jax.ShapeDtypeStruct((SEQ, D_CONV), BF16),
        ],
        scratch_shapes=[
            pltpu.VMEM((tm, D_MODEL), BF16),
            pltpu.VMEM((tm, D_MODEL), BF16),
            pltpu.VMEM((D_CONV // LANES, CONV_HALO + tm, LANES), F32),
            pltpu.VMEM((tm, D_CONV), F32),
            pltpu.VMEM((tm, D_CONV), F32),
            pltpu.VMEM((tm, D_CONV), F32),
            pltpu.VMEM((D_MODEL, D_IN), BF16),
            pltpu.VMEM((D_CONV, D_CONV), BF16),
            pltpu.VMEM((2, D_MODEL, W_STAGE_COLS), F32),
            pltpu.SemaphoreType.DMA((2,)),
        ],
        compiler_params=pltpu.CompilerParams(
            dimension_semantics=("arbitrary",), vmem_limit_bytes=VMEM_LIMIT),
        name="inproj_conv",
    )(x2d, mod, mod, w_in, conv_w, conv_b, ln_g, ln_b, w_pw, b_pw)


def kernel(x, c, w_ada, b_ada, w_in, rel_bias, sinks, conv_w, conv_b, conv_ln_g,
           conv_ln_b, w_pw, b_pw, w_out, ln_g, ln_b):
    assert x.shape == (1, SEQ, D_MODEL) and w_in.shape == (DEPTH, D_MODEL, D_IN)
    x2d = x.reshape(SEQ, D_MODEL)

    row = lambda v: v.reshape(1, -1)
    mod = _mod_call(row(c), w_ada.reshape(D_MODEL, 3 * D_MODEL), row(b_ada))
    bias = _bias_call(rel_bias)
    qkvg, y_conv = _inproj_conv_call(
        x2d, mod, w_in.reshape(D_MODEL, D_IN),
        conv_w, row(conv_b), row(conv_ln_g), row(conv_ln_b),
        w_pw.reshape(D_CONV, D_CONV), row(b_pw))
    out = _attn_out_call(qkvg, bias, sinks.reshape(N_Q_HEADS), y_conv, x2d, mod,
                         w_out.reshape(D_MODEL, D_MODEL), row(ln_g), row(ln_b))
    return out.reshape(1, SEQ, D_MODEL)
```

```python
import math

import jax
import jax.numpy as jnp
import numpy as np
from jax.experimental import pallas as pl
from jax.experimental.pallas import tpu as pltpu

F32 = jnp.float32
BF16 = jnp.bfloat16

D_MODEL = 2048
SEQ = 8192
HEAD_DIM = 64
D_ATTN = 1024
D_CONV = 1024
N_Q_HEADS = 16
N_KV_HEADS = 4
GQA_GROUP = 4
D_KV = 256
WINDOW = 128
BLOCK = 128
CONV_WIDTH = 31
N_BUCKETS = 32
MAX_DISTANCE = 128
LN_EPS = 1e-5
DEPTH = 1
ALPHA = (2.0 * DEPTH) ** 0.25
D_IN = 2 * D_ATTN + 2 * D_KV + 3 * D_CONV

LANES = 128
SUBLANES = 8

COL_Q = 0
COL_K = COL_Q + D_ATTN
COL_V = COL_K + D_KV
COL_G_ATTN = COL_V + D_KV
COL_GLU_A = COL_G_ATTN + D_ATTN
COL_GLU_B = COL_GLU_A + D_CONV
COL_G_CONV = COL_GLU_B + D_CONV
HALF = 512

VMEM_LIMIT = 56 * 1024 * 1024


def _silu(v):
    return v * jax.nn.sigmoid(v)


def _mod_kernel(c_ref, w_ref, b_ref, o_ref):
    ca = _silu(c_ref[...])
    ca_cols = jnp.transpose(jnp.broadcast_to(ca, (LANES, D_MODEL)))
    for t in range(w_ref.shape[1] // LANES):
        ts = slice(t * LANES, (t + 1) * LANES)
        o_ref[:, ts] = jnp.sum(ca_cols * w_ref[:, ts], axis=0, keepdims=True) + b_ref[:, ts]


def _mod_call(c_row, w_ada, b_ada):
    tn = 1024
    n = w_ada.shape[1]
    return pl.pallas_call(
        _mod_kernel,
        grid=(n // tn,),
        in_specs=[
            pl.BlockSpec((1, D_MODEL), lambda j: (0, 0)),
            pl.BlockSpec((D_MODEL, tn), lambda j: (0, j)),
            pl.BlockSpec((1, tn), lambda j: (0, j)),
        ],
        out_specs=pl.BlockSpec((1, tn), lambda j: (0, j)),
        out_shape=jax.ShapeDtypeStruct((1, n), F32),
        compiler_params=pltpu.CompilerParams(
            dimension_semantics=("arbitrary",), vmem_limit_bytes=VMEM_LIMIT),
        name="mod",
    )(c_row, w_ada, b_ada)


LOG2E = math.log2(math.e)
Q_SCALE = LOG2E * HEAD_DIM ** -0.5
MASKED = -1e30


def _lane_buckets():
    m = np.arange(2 * BLOCK)
    dist = BLOCK - m
    ok = (dist >= 0) & (dist < WINDOW)
    max_exact = N_BUCKETS // 2
    d = np.maximum(dist, 1).astype(np.float64)
    large = max_exact + (np.log(d / max_exact) / math.log(MAX_DISTANCE / max_exact)
                         * (N_BUCKETS - max_exact)).astype(np.int32)
    large = np.minimum(large, N_BUCKETS - 1)
    bucket = np.where(dist < max_exact, dist, large)
    return np.where(ok, bucket, -1).astype(np.int32).reshape(1, -1)


def _bias_kernel(rb_ref, bm_ref, o_ref):
    bm = bm_ref[...]
    col = jax.lax.broadcasted_iota(jnp.int32, (BLOCK, 2 * BLOCK), 1)
    for h in range(N_Q_HEADS):
        t = jnp.full(bm.shape, MASKED, F32)
        for b in range(N_BUCKETS):
            t = jnp.where(bm == b, rb_ref[b, h] * LOG2E, t)
        x = pltpu.roll(jnp.broadcast_to(t, (BLOCK, 2 * BLOCK)), 0, 1, stride=1, stride_axis=0)
        o_ref[1, h] = x
        o_ref[0, h] = jnp.where(col >= BLOCK, x, MASKED)


def _bias_call(rel_bias):
    shape = (2, N_Q_HEADS, BLOCK, 2 * BLOCK)
    return pl.pallas_call(
        _bias_kernel,
        in_specs=[
            pl.BlockSpec(memory_space=pltpu.SMEM),
            pl.BlockSpec((1, 2 * BLOCK), lambda: (0, 0)),
        ],
        out_specs=pl.BlockSpec(shape, lambda: (0, 0, 0, 0)),
        out_shape=jax.ShapeDtypeStruct(shape, F32),
        name="bias",
    )(rel_bias, jnp.asarray(_lane_buckets()))


W_STAGE_COLS = 256


def _load_weight_bf16(w_hbm, w_ref, stage_ref, sem_ref, col_scale=None):
    rows, cols = w_ref.shape
    n = cols // W_STAGE_COLS

    def chunk_copy(j):
        slot = j % 2
        return pltpu.make_async_copy(
            w_hbm.at[:, pl.ds(j * W_STAGE_COLS, W_STAGE_COLS)],
            stage_ref.at[slot, pl.ds(0, rows), :], sem_ref.at[slot])

    chunk_copy(0).start()
    for j in range(n):
        if j + 1 < n:
            chunk_copy(j + 1).start()
        chunk_copy(j).wait()
        cs = slice(j * W_STAGE_COLS, (j + 1) * W_STAGE_COLS)
        chunk = stage_ref[j % 2, 0:rows, :]
        if col_scale is not None:
            chunk = chunk * col_scale[:, cs]
        w_ref[:, cs] = chunk.astype(BF16)


def _token(v):
    t = v[0:SUBLANES, 0:LANES]
    for r in range(SUBLANES, v.shape[0], SUBLANES):
        t = t + v[r:r + SUBLANES, 0:LANES]
    return t


def _zero_of(token):
    return (pltpu.bitcast(token, jnp.uint32) >> 16) >> 16


def _copy_after(src_ref, dst_ref, tokens):
    zero = _zero_of(tokens[0])
    for t in tokens[1:]:
        zero = zero | _zero_of(t)
    bits = pltpu.bitcast(src_ref[...], jnp.uint32)
    reps = (bits.shape[0] // zero.shape[0], bits.shape[1] // zero.shape[1])
    dst_ref[...] = pltpu.bitcast(bits | jnp.tile(zero, reps), dst_ref.dtype)


ATT_TM = 2 * BLOCK
ATT_ROWS = BLOCK


def _dup_halves(t):
    swapped = jnp.concatenate([t[:, HEAD_DIM:], t[:, :HEAD_DIM]], axis=1)
    lane = jax.lax.broadcasted_iota(jnp.int32, t.shape, 1)
    low_half = lane < HEAD_DIM
    return jnp.where(low_half, t, swapped), jnp.where(low_half, swapped, t)


def _attn_group(q_ref, g0_ref, g1_ref, k_dup, v_dup, bias_ref, sinks_ref, r0, b0, kv, ya_ref):
    nr = ATT_ROWS
    lane = jax.lax.broadcasted_iota(jnp.int32, (nr, LANES), 1)
    low_half = lane < HEAD_DIM
    rows = slice(r0, r0 + nr)
    tiles = [GQA_GROUP * kv // 2 + t for t in range(GQA_GROUP // 2)]
    stacked = []
    for tile in tiles:
        q_tile = q_ref[rows, tile * LANES:(tile + 1) * LANES]
        zero = jnp.zeros_like(q_tile)
        stacked += [jnp.where(low_half, q_tile, zero), jnp.where(low_half, zero, q_tile)]
    s_all = jax.lax.dot_general(jnp.concatenate(stacked, axis=0), k_dup,
                                (((1,), (1,)), ((), ())),
                                preferred_element_type=F32)
    probs, denoms = [], []
    for r in range(GQA_GROUP):
        head = GQA_GROUP * kv + r
        s = s_all[r * nr:(r + 1) * nr] + bias_ref[0, head, b0:b0 + nr, :]
        sink = sinks_ref[head] * LOG2E
        m = jnp.max(s, axis=1, keepdims=True)
        e = jnp.exp2(s - m)
        denoms.append(jnp.sum(e, axis=1, keepdims=True) + jnp.exp2(sink - m))
        probs.append(e.astype(BF16))
    o_all = jnp.dot(jnp.concatenate(probs, axis=0), v_dup,
                    preferred_element_type=F32)
    outs = [o_all[r * nr:(r + 1) * nr] / denoms[r] for r in range(GQA_GROUP)]
    for t, tile in enumerate(tiles):
        y = jnp.where(low_half, outs[2 * t], outs[2 * t + 1])
        g_ref = g0_ref if tile < HALF // LANES else g1_ref
        gl = (tile * LANES) % HALF
        gate = g_ref[rows, gl:gl + LANES].astype(F32)
        ya_ref[rows, tile * LANES:(tile + 1) * LANES] = (y * gate).astype(BF16)


def _attn_tile(sinks_ref, q_ref, g0_ref, g1_ref, k_ref, v_ref, kp_ref, vp_ref,
               bias_refs, ya_ref):
    n_lt = D_KV // LANES
    lt = lambda t: slice(t * LANES, (t + 1) * LANES)

    def block_operands(b):
        r0 = b * BLOCK
        if b == 0:
            k_prev = [kp_ref[:, lt(t)] for t in range(n_lt)]
            v_prev = [vp_ref[:, lt(t)] for t in range(n_lt)]
        else:
            k_prev = [k_ref[r0 - BLOCK:r0, lt(t)] for t in range(n_lt)]
            v_prev = [v_ref[r0 - BLOCK:r0, lt(t)] for t in range(n_lt)]
        k_rows = [jnp.concatenate([k_prev[t], k_ref[r0:r0 + BLOCK, lt(t)]], axis=0)
                  for t in range(n_lt)]
        v_rows = [jnp.concatenate([v_prev[t], v_ref[r0:r0 + BLOCK, lt(t)]], axis=0)
                  for t in range(n_lt)]
        return ([_dup_halves(kt) for kt in k_rows], [_dup_halves(vt) for vt in v_rows])

    for b in range(ATT_TM // BLOCK):
        k_dups, v_dups = block_operands(b)
        for b0 in range(0, BLOCK, ATT_ROWS):
            for kv in range(N_KV_HEADS):
                _attn_group(q_ref, g0_ref, g1_ref, k_dups[kv // 2][kv % 2],
                            v_dups[kv // 2][kv % 2], bias_refs[b], sinks_ref,
                            b * BLOCK + b0, b0, kv, ya_ref)


def _attn_out_kernel(sinks_ref, q_ref, g0_ref, g1_ref, k_ref, v_ref, kp_ref, vp_ref,
                     bias0_ref, bias1_ref, yc_ref, x_ref, gate_ref, wout_hbm,
                     lng_ref, lnb_ref, o_ref, ya_ref, wout_ref, stage_ref, sem_ref):
    i = pl.program_id(0)

    @pl.when(i == 0)
    def _():
        ya_ref[...] = jnp.zeros(ya_ref.shape, ya_ref.dtype)
        _load_weight_bf16(wout_hbm, wout_ref, stage_ref, sem_ref,
                          col_scale=gate_ref[...] * (1.0 / ALPHA))

    y = jnp.dot(ya_ref[...], wout_ref[0:D_ATTN, :], preferred_element_type=F32)
    y = y + jnp.dot(yc_ref[...], wout_ref[D_ATTN:D_MODEL, :], preferred_element_type=F32)
    z = x_ref[...] + y
    mu = jnp.mean(z, axis=1, keepdims=True)
    d = z - mu
    var = jnp.mean(d * d, axis=1, keepdims=True)
    o_ref[...] = d * jax.lax.rsqrt(var + LN_EPS / ALPHA ** 2) * lng_ref[...] + lnb_ref[...]

    _attn_tile(sinks_ref, q_ref, g0_ref, g1_ref, k_ref, v_ref, kp_ref, vp_ref,
               [bias0_ref, bias1_ref], ya_ref)


def _attn_out_call(qkvg, bias, sinks, y_conv, x2d, mod, w_out, ln_g, ln_b):
    tm = ATT_TM
    n = SEQ // tm
    clamp = lambda t: jnp.clip(t, 0, n - 1)
    cur = lambda i: clamp(i)
    lag = lambda i: clamp(i - 1)
    kv_prev = lambda i: jnp.maximum(cur(i) * (tm // BLOCK) - 1, 0)
    row = lambda i: (0, 0)
    return pl.pallas_call(
        _attn_out_kernel,
        grid=(n + 1,),
        in_specs=[
            pl.BlockSpec(memory_space=pltpu.SMEM),
            pl.BlockSpec((tm, D_ATTN), lambda i: (cur(i), COL_Q // D_ATTN)),
            pl.BlockSpec((tm, HALF), lambda i: (cur(i), COL_G_ATTN // HALF)),
            pl.BlockSpec((tm, HALF), lambda i: (cur(i), COL_G_ATTN // HALF + 1)),
            pl.BlockSpec((tm, D_KV), lambda i: (cur(i), COL_K // D_KV)),
            pl.BlockSpec((tm, D_KV), lambda i: (cur(i), COL_V // D_KV)),
            pl.BlockSpec((BLOCK, D_KV), lambda i: (kv_prev(i), COL_K // D_KV)),
            pl.BlockSpec((BLOCK, D_KV), lambda i: (kv_prev(i), COL_V // D_KV)),
            pl.BlockSpec((1, N_Q_HEADS, BLOCK, 2 * BLOCK),
                         lambda i: (jnp.minimum(cur(i), 1), 0, 0, 0)),
            pl.BlockSpec((1, N_Q_HEADS, BLOCK, 2 * BLOCK), lambda i: (1, 0, 0, 0)),
            pl.BlockSpec((tm, D_CONV), lambda i: (lag(i), 0)),
            pl.BlockSpec((tm, D_MODEL), lambda i: (lag(i), 0)),
            pl.BlockSpec((1, D_MODEL), lambda i: (0, 2)),
            pl.BlockSpec(memory_space=pl.ANY),
            pl.BlockSpec((1, D_MODEL), row),
            pl.BlockSpec((1, D_MODEL), row),
        ],
        out_specs=pl.BlockSpec((tm, D_MODEL), lambda i: (lag(i), 0)),
        out_shape=jax.ShapeDtypeStruct((SEQ, D_MODEL), F32),
        scratch_shapes=[
            pltpu.VMEM((tm, D_ATTN), BF16),
            pltpu.VMEM((D_MODEL, D_MODEL), BF16),
            pltpu.VMEM((2, D_MODEL, W_STAGE_COLS), F32),
            pltpu.SemaphoreType.DMA((2,)),
        ],
        compiler_params=pltpu.CompilerParams(
            dimension_semantics=("arbitrary",), vmem_limit_bytes=VMEM_LIMIT),
        name="attn_out",
    )(sinks, qkvg, qkvg, qkvg, qkvg, qkvg, qkvg, qkvg, bias, bias, y_conv, x2d, mod,
      w_out, ln_g, ln_b)


IN_TM = 256
IN_CHUNK = 256
D_QKVG = COL_GLU_A
CONV_HALO = 32
TAP_ROWS = 32
SYNC_EVERY = 2
LAST_TAP_SYNC_CHUNK = 18


def _tap_piece(u_ref, cw_ref, cb_ref, acc_ref, c, r0):
    off = CONV_HALO - (CONV_WIDTH - 1)
    cs = slice(c * LANES, (c + 1) * LANES)
    acc = jnp.broadcast_to(cb_ref[:, cs], (TAP_ROWS, LANES))
    for k in range(CONV_WIDTH):
        acc = acc + cw_ref[0, k:k + 1, cs] * u_ref[c, r0 + off + k:r0 + off + k + TAP_ROWS, :]
    acc_ref[r0:r0 + TAP_ROWS, cs] = acc
    return _token(acc)


def _conv_tail(acc_ref, lng_ref, lnb_ref, wpw_ref, bpw_ref, gate_ref, yc_ref):
    v = acc_ref[...]
    mu = jnp.mean(v, axis=1, keepdims=True)
    d = v - mu
    var = jnp.mean(d * d, axis=1, keepdims=True)
    y = d * jax.lax.rsqrt(var + LN_EPS) * lng_ref[...] + lnb_ref[...]
    s = _silu(y).astype(BF16)
    z = jnp.dot(s, wpw_ref[...], preferred_element_type=F32) + bpw_ref[...]
    yc_ref[...] = (z * gate_ref[...]).astype(BF16)


def _inproj_conv_kernel(x_ref, shift_ref, scale_ref, w_hbm, cw_ref, cb_ref, lng_ref,
                        lnb_ref, wpw_hbm, bpw_ref, qkvg_ref, yc_ref,
                        h0_ref, h1_ref, u_ref, gc_ref, gcp_ref, acc_ref,
                        w_ref, wpw_ref, stage_ref, sem_ref):
    i = pl.program_id(0)
    tm = IN_TM
    pieces = [(c, r0) for c in range(D_CONV // LANES) for r0 in range(0, tm, TAP_ROWS)]

    @pl.when(i == 0)
    def _():
        u_ref[...] = jnp.zeros(u_ref.shape, u_ref.dtype)
        gc_ref[...] = jnp.zeros(gc_ref.shape, gc_ref.dtype)
        _load_weight_bf16(w_hbm, w_ref, stage_ref, sem_ref)
        _load_weight_bf16(wpw_hbm, wpw_ref, stage_ref, sem_ref)

    gcp_ref[...] = gc_ref[...]
    h = x_ref[...] * (1.0 + scale_ref[...]) + shift_ref[...]
    h0_ref[...] = h.astype(BF16)
    h_bufs = [h0_ref, h1_ref]
    state = {"cur": 0}

    def sync(tokens):
        src, dst = h_bufs[state["cur"]], h_bufs[1 - state["cur"]]
        _copy_after(src, dst, tokens)
        state["cur"] = 1 - state["cur"]

    def project(col):
        return jnp.dot(h_bufs[state["cur"]][...], w_ref[:, col:col + IN_CHUNK],
                       preferred_element_type=F32)

    cols = list(range(0, D_QKVG, IN_CHUNK)) + list(range(COL_G_CONV, D_IN, IN_CHUNK))
    for c in range(D_CONV // IN_CHUNK):
        cols += [COL_GLU_A + c * IN_CHUNK, COL_GLU_B + c * IN_CHUNK]
    sync_chunks = list(range(SYNC_EVERY, LAST_TAP_SYNC_CHUNK + 1, SYNC_EVERY))
    bounds = [round(s * len(pieces) / len(sync_chunks)) for s in range(len(sync_chunks) + 1)]
    glu_a = None
    pieces_done = 0
    for j, col in enumerate(cols):
        if j in sync_chunks:
            s_idx = sync_chunks.index(j)
            group = pieces[bounds[s_idx]:bounds[s_idx + 1]]
            sync([_tap_piece(u_ref, cw_ref, cb_ref, acc_ref, c, r0) for c, r0 in group])
            pieces_done = bounds[s_idx + 1]
            if pieces_done == len(pieces):
                _conv_tail(acc_ref, lng_ref, lnb_ref, wpw_ref, bpw_ref, gcp_ref, yc_ref)
        res = project(col)
        if col < COL_K:
            qkvg_ref[:, col:col + IN_CHUNK] = (res * Q_SCALE).astype(BF16)
        elif col < COL_G_ATTN:
            qkvg_ref[:, col:col + IN_CHUNK] = res.astype(BF16)
        elif col < D_QKVG:
            qkvg_ref[:, col:col + IN_CHUNK] = _silu(res).astype(BF16)
        elif col < COL_GLU_B:
            glu_a = res
        elif col < COL_G_CONV:
            u = glu_a * jax.nn.sigmoid(res)
            t0 = (col - COL_GLU_B) // LANES
            for t in range(t0, t0 + IN_CHUNK // LANES):
                assert pieces_done >= (t + 1) * (tm // TAP_ROWS)
                u_ref[t, 0:CONV_HALO, :] = u_ref[t, tm:tm + CONV_HALO, :]
                u_ref[t, CONV_HALO:CONV_HALO + tm, :] = u[:, (t - t0) * LANES:(t - t0 + 1) * LANES]
        else:
            gc_ref[:, col - COL_G_CONV:col - COL_G_CONV + IN_CHUNK] = _silu(res)


def _inproj_conv_call(x2d, mod, w_in, conv_w, conv_b, ln_g, ln_b, w_pw, b_pw):
    tm = IN_TM
    n = SEQ // tm
    cur = lambda i: jnp.minimum(i, n - 1)
    lag = lambda i: jnp.maximum(i - 1, 0)
    row = lambda i: (0, 0)
    return pl.pallas_call(
        _inproj_conv_kernel,
        grid=(n + 1,),
        in_specs=[
            pl.BlockSpec((tm, D_MODEL), lambda i: (cur(i), 0)),
            pl.BlockSpec((1, D_MODEL), lambda i: (0, 0)),
            pl.BlockSpec((1, D_MODEL), lambda i: (0, 1)),
            pl.BlockSpec(memory_space=pl.ANY),
            pl.BlockSpec((DEPTH, CONV_WIDTH, D_CONV), lambda i: (0, 0, 0)),
            pl.BlockSpec((1, D_CONV), row),
            pl.BlockSpec((1, D_CONV), row),
            pl.BlockSpec((1, D_CONV), row),
            pl.BlockSpec(memory_space=pl.ANY),
            pl.BlockSpec((1, D_CONV), row),
        ],
        out_specs=[
            pl.BlockSpec((tm, D_QKVG), lambda i: (cur(i), 0)),
            pl.BlockSpec((tm, D_CONV), lambda i: (lag(i), 0)),
        ],
        out_shape=[
            jax.ShapeDtypeStruct((SEQ, D_QKVG), BF16),
            jax.ShapeDtypeStruct((SEQ, D_CONV), BF16),
        ],
        scratch_shapes=[
            pltpu.VMEM((tm, D_MODEL), BF16),
            pltpu.VMEM((tm, D_MODEL), BF16),
            pltpu.VMEM((D_CONV // LANES, CONV_HALO + tm, LANES), F32),
            pltpu.VMEM((tm, D_CONV), F32),
            pltpu.VMEM((tm, D_CONV), F32),
            pltpu.VMEM((tm, D_CONV), F32),
            pltpu.VMEM((D_MODEL, D_IN), BF16),
            pltpu.VMEM((D_CONV, D_CONV), BF16),
            pltpu.VMEM((2, D_MODEL, W_STAGE_COLS), F32),
            pltpu.SemaphoreType.DMA((2,)),
        ],
        compiler_params=pltpu.CompilerParams(
            dimension_semantics=("arbitrary",), vmem_limit_bytes=VMEM_LIMIT),
        name="inproj_conv",
    )(x2d, mod, mod, w_in, conv_w, conv_b, ln_g, ln_b, w_pw, b_pw)


def kernel(x, c, w_ada, b_ada, w_in, rel_bias, sinks, conv_w, conv_b, conv_ln_g,
           conv_ln_b, w_pw, b_pw, w_out, ln_g, ln_b):
    assert x.shape == (1, SEQ, D_MODEL) and w_in.shape == (DEPTH, D_MODEL, D_IN)
    x2d = x.reshape(SEQ, D_MODEL)

    row = lambda v: v.reshape(1, -1)
    mod = _mod_call(row(c), w_ada.reshape(D_MODEL, 3 * D_MODEL), row(b_ada))
    bias = _bias_call(rel_bias)
    qkvg, y_conv = _inproj_conv_call(
        x2d, mod, w_in.reshape(D_MODEL, D_IN),
        conv_w, row(conv_b), row(conv_ln_g), row(conv_ln_b),
        w_pw.reshape(D_CONV, D_CONV), row(b_pw))
    out = _attn_out_call(qkvg, bias, sinks.reshape(N_Q_HEADS), y_conv, x2d, mod,
                         w_out.reshape(D_MODEL, D_MODEL), row(ln_g), row(ln_b))
    return out.reshape(1, SEQ, D_MODEL)
```

```python
import math

import jax
import jax.numpy as jnp
import numpy as np
from jax.experimental import pallas as pl
from jax.experimental.pallas import tpu as pltpu

F32 = jnp.float32
BF16 = jnp.bfloat16

D_MODEL = 2048
SEQ = 8192
HEAD_DIM = 64
D_ATTN = 1024
D_CONV = 1024
N_Q_HEADS = 16
N_KV_HEADS = 4
GQA_GROUP = 4
D_KV = 256
WINDOW = 128
BLOCK = 128
CONV_WIDTH = 31
N_BUCKETS = 32
MAX_DISTANCE = 128
LN_EPS = 1e-5
DEPTH = 1
ALPHA = (2.0 * DEPTH) ** 0.25
D_IN = 2 * D_ATTN + 2 * D_KV + 3 * D_CONV

LANES = 128
SUBLANES = 8

COL_Q = 0
COL_K = COL_Q + D_ATTN
COL_V = COL_K + D_KV
COL_G_ATTN = COL_V + D_KV
COL_GLU_A = COL_G_ATTN + D_ATTN
COL_GLU_B = COL_GLU_A + D_CONV
COL_G_CONV = COL_GLU_B + D_CONV
HALF = 512

VMEM_LIMIT = 56 * 1024 * 1024


def _silu(v):
    return v * jax.nn.sigmoid(v)


LOG2E = math.log2(math.e)
Q_SCALE = LOG2E * HEAD_DIM ** -0.5
MASKED = -1e30
MOD_TN = 1024


def _lane_buckets():
    m = np.arange(2 * BLOCK)
    dist = BLOCK - m
    ok = (dist >= 0) & (dist < WINDOW)
    max_exact = N_BUCKETS // 2
    d = np.maximum(dist, 1).astype(np.float64)
    large = max_exact + (np.log(d / max_exact) / math.log(MAX_DISTANCE / max_exact)
                         * (N_BUCKETS - max_exact)).astype(np.int32)
    large = np.minimum(large, N_BUCKETS - 1)
    bucket = np.where(dist < max_exact, dist, large)
    return np.where(ok, bucket, -1).astype(np.int32).reshape(1, -1)


def _bias_tables(rb_ref, bm_ref, o_ref):
    bm = bm_ref[...]
    col = jax.lax.broadcasted_iota(jnp.int32, (BLOCK, 2 * BLOCK), 1)
    for h in range(N_Q_HEADS):
        t = jnp.full(bm.shape, MASKED, F32)
        for b in range(N_BUCKETS):
            t = jnp.where(bm == b, rb_ref[b, h] * LOG2E, t)
        x = pltpu.roll(jnp.broadcast_to(t, (BLOCK, 2 * BLOCK)), 0, 1, stride=1, stride_axis=0)
        o_ref[1, h] = x
        o_ref[0, h] = jnp.where(col >= BLOCK, x, MASKED)


def _mod_bias_kernel(rb_ref, c_ref, w_ref, b_ref, bm_ref, mod_ref, bias_ref):
    ca = _silu(c_ref[...])
    ca_cols = jnp.transpose(jnp.broadcast_to(ca, (LANES, D_MODEL)))
    for t in range(w_ref.shape[1] // LANES):
        ts = slice(t * LANES, (t + 1) * LANES)
        mod_ref[:, ts] = jnp.sum(ca_cols * w_ref[:, ts], axis=0, keepdims=True) + b_ref[:, ts]

    @pl.when(pl.program_id(0) == 0)
    def _():
        _bias_tables(rb_ref, bm_ref, bias_ref)


def _mod_bias_call(c_row, w_ada, b_ada, rel_bias):
    n = w_ada.shape[1]
    bias_shape = (2, N_Q_HEADS, BLOCK, 2 * BLOCK)
    return pl.pallas_call(
        _mod_bias_kernel,
        grid=(n // MOD_TN,),
        in_specs=[
            pl.BlockSpec(memory_space=pltpu.SMEM),
            pl.BlockSpec((1, D_MODEL), lambda j: (0, 0)),
            pl.BlockSpec((D_MODEL, MOD_TN), lambda j: (0, j)),
            pl.BlockSpec((1, MOD_TN), lambda j: (0, j)),
            pl.BlockSpec((1, 2 * BLOCK), lambda j: (0, 0)),
        ],
        out_specs=[
            pl.BlockSpec((1, MOD_TN), lambda j: (0, j)),
            pl.BlockSpec(bias_shape, lambda j: (0, 0, 0, 0)),
        ],
        out_shape=[
            jax.ShapeDtypeStruct((1, n), F32),
            jax.ShapeDtypeStruct(bias_shape, F32),
        ],
        compiler_params=pltpu.CompilerParams(
            dimension_semantics=("arbitrary",), vmem_limit_bytes=VMEM_LIMIT),
        name="mod_bias",
    )(rel_bias, c_row, w_ada, b_ada, jnp.asarray(_lane_buckets()))


W_STAGE_COLS = 256


def _load_weight_bf16(w_hbm, w_ref, stage_ref, sem_ref, col_scale=None):
    rows, cols = w_ref.shape
    n = cols // W_STAGE_COLS

    def chunk_copy(j):
        slot = j % 2
        return pltpu.make_async_copy(
            w_hbm.at[:, pl.ds(j * W_STAGE_COLS, W_STAGE_COLS)],
            stage_ref.at[slot, pl.ds(0, rows), :], sem_ref.at[slot])

    chunk_copy(0).start()
    for j in range(n):
        if j + 1 < n:
            chunk_copy(j + 1).start()
        chunk_copy(j).wait()
        cs = slice(j * W_STAGE_COLS, (j + 1) * W_STAGE_COLS)
        chunk = stage_ref[j % 2, 0:rows, :]
        if col_scale is not None:
            chunk = chunk * col_scale[:, cs]
        w_ref[:, cs] = chunk.astype(BF16)


def _token(v):
    t = v[0:SUBLANES, 0:LANES]
    for r in range(SUBLANES, v.shape[0], SUBLANES):
        t = t + v[r:r + SUBLANES, 0:LANES]
    return t


def _zero_of(token):
    return (pltpu.bitcast(token, jnp.uint32) >> 16) >> 16


def _copy_after(src_ref, dst_ref, tokens):
    zero = _zero_of(tokens[0])
    for t in tokens[1:]:
        zero = zero | _zero_of(t)
    bits = pltpu.bitcast(src_ref[...], jnp.uint32)
    reps = (bits.shape[0] // zero.shape[0], bits.shape[1] // zero.shape[1])
    dst_ref[...] = pltpu.bitcast(bits | jnp.tile(zero, reps), dst_ref.dtype)


ATT_TM = 2 * BLOCK
ATT_ROWS = BLOCK


def _dup_halves(t):
    swapped = jnp.concatenate([t[:, HEAD_DIM:], t[:, :HEAD_DIM]], axis=1)
    lane = jax.lax.broadcasted_iota(jnp.int32, t.shape, 1)
    low_half = lane < HEAD_DIM
    return jnp.where(low_half, t, swapped), jnp.where(low_half, swapped, t)


def _attn_group(q_ref, g0_ref, g1_ref, k_dup, v_dup, bias_ref, sinks_ref, r0, b0, kv, ya_ref):
    nr = ATT_ROWS
    lane = jax.lax.broadcasted_iota(jnp.int32, (nr, LANES), 1)
    low_half = lane < HEAD_DIM
    rows = slice(r0, r0 + nr)
    tiles = [GQA_GROUP * kv // 2 + t for t in range(GQA_GROUP // 2)]
    stacked = []
    for tile in tiles:
        q_tile = q_ref[rows, tile * LANES:(tile + 1) * LANES]
        zero = jnp.zeros_like(q_tile)
        stacked += [jnp.where(low_half, q_tile, zero), jnp.where(low_half, zero, q_tile)]
    s_all = jax.lax.dot_general(jnp.concatenate(stacked, axis=0), k_dup,
                                (((1,), (1,)), ((), ())),
                                preferred_element_type=F32)
    probs, denoms = [], []
    for r in range(GQA_GROUP):
        head = GQA_GROUP * kv + r
        s = s_all[r * nr:(r + 1) * nr] + bias_ref[0, head, b0:b0 + nr, :]
        sink = sinks_ref[head] * LOG2E
        m = jnp.max(s, axis=1, keepdims=True)
        e = jnp.exp2(s - m)
        denoms.append(jnp.sum(e, axis=1, keepdims=True) + jnp.exp2(sink - m))
        probs.append(e.astype(BF16))
    o_all = jnp.dot(jnp.concatenate(probs, axis=0), v_dup,
                    preferred_element_type=F32)
    outs = [o_all[r * nr:(r + 1) * nr] / denoms[r] for r in range(GQA_GROUP)]
    for t, tile in enumerate(tiles):
        y = jnp.where(low_half, outs[2 * t], outs[2 * t + 1])
        g_ref = g0_ref if tile < HALF // LANES else g1_ref
        gl = (tile * LANES) % HALF
        gate = g_ref[rows, gl:gl + LANES].astype(F32)
        ya_ref[rows, tile * LANES:(tile + 1) * LANES] = (y * gate).astype(BF16)


def _attn_tile(sinks_ref, q_ref, g0_ref, g1_ref, k_ref, v_ref, kp_ref, vp_ref,
               bias_refs, ya_ref):
    n_lt = D_KV // LANES
    lt = lambda t: slice(t * LANES, (t + 1) * LANES)

    def block_operands(b):
        r0 = b * BLOCK
        if b == 0:
            k_prev = [kp_ref[:, lt(t)] for t in range(n_lt)]
            v_prev = [vp_ref[:, lt(t)] for t in range(n_lt)]
        else:
            k_prev = [k_ref[r0 - BLOCK:r0, lt(t)] for t in range(n_lt)]
            v_prev = [v_ref[r0 - BLOCK:r0, lt(t)] for t in range(n_lt)]
        k_rows = [jnp.concatenate([k_prev[t], k_ref[r0:r0 + BLOCK, lt(t)]], axis=0)
                  for t in range(n_lt)]
        v_rows = [jnp.concatenate([v_prev[t], v_ref[r0:r0 + BLOCK, lt(t)]], axis=0)
                  for t in range(n_lt)]
        return ([_dup_halves(kt) for kt in k_rows], [_dup_halves(vt) for vt in v_rows])

    for b in range(ATT_TM // BLOCK):
        k_dups, v_dups = block_operands(b)
        for b0 in range(0, BLOCK, ATT_ROWS):
            for kv in range(N_KV_HEADS):
                _attn_group(q_ref, g0_ref, g1_ref, k_dups[kv // 2][kv % 2],
                            v_dups[kv // 2][kv % 2], bias_refs[b], sinks_ref,
                            b * BLOCK + b0, b0, kv, ya_ref)


def _attn_out_kernel(sinks_ref, q_ref, g0_ref, g1_ref, k_ref, v_ref, kp_ref, vp_ref,
                     bias0_ref, bias1_ref, yc_ref, x_ref, gate_ref, wout_hbm,
                     lng_ref, lnb_ref, o_ref, ya_ref, wout_ref, stage_ref, sem_ref):
    i = pl.program_id(0)

    @pl.when(i == 0)
    def _():
        ya_ref[...] = jnp.zeros(ya_ref.shape, ya_ref.dtype)
        _load_weight_bf16(wout_hbm, wout_ref, stage_ref, sem_ref,
                          col_scale=gate_ref[...] * (1.0 / ALPHA))

    y = jnp.dot(ya_ref[...], wout_ref[0:D_ATTN, :], preferred_element_type=F32)
    y = y + jnp.dot(yc_ref[...], wout_ref[D_ATTN:D_MODEL, :], preferred_element_type=F32)
    z = x_ref[...] + y
    mu = jnp.mean(z, axis=1, keepdims=True)
    d = z - mu
    var = jnp.mean(d * d, axis=1, keepdims=True)
    o_ref[...] = d * jax.lax.rsqrt(var + LN_EPS / ALPHA ** 2) * lng_ref[...] + lnb_ref[...]

    _attn_tile(sinks_ref, q_ref, g0_ref, g1_ref, k_ref, v_ref, kp_ref, vp_ref,
               [bias0_ref] + [bias1_ref] * (ATT_TM // BLOCK - 1), ya_ref)


def _attn_out_call(qkvg, bias, sinks, y_conv, x2d, mod, w_out, ln_g, ln_b):
    tm = ATT_TM
    n = SEQ // tm
    clamp = lambda t: jnp.clip(t, 0, n - 1)
    cur = lambda i: clamp(i)
    lag = lambda i: clamp(i - 1)
    kv_prev = lambda i: jnp.maximum(cur(i) * (tm // BLOCK) - 1, 0)
    row = lambda i: (0, 0)
    return pl.pallas_call(
        _attn_out_kernel,
        grid=(n + 1,),
        in_specs=[
            pl.BlockSpec(memory_space=pltpu.SMEM),
            pl.BlockSpec((tm, D_ATTN), lambda i: (cur(i), COL_Q // D_ATTN)),
            pl.BlockSpec((tm, HALF), lambda i: (cur(i), COL_G_ATTN // HALF)),
            pl.BlockSpec((tm, HALF), lambda i: (cur(i), COL_G_ATTN // HALF + 1)),
            pl.BlockSpec((tm, D_KV), lambda i: (cur(i), COL_K // D_KV)),
            pl.BlockSpec((tm, D_KV), lambda i: (cur(i), COL_V // D_KV)),
            pl.BlockSpec((BLOCK, D_KV), lambda i: (kv_prev(i), COL_K // D_KV)),
            pl.BlockSpec((BLOCK, D_KV), lambda i: (kv_prev(i), COL_V // D_KV)),
            pl.BlockSpec((1, N_Q_HEADS, BLOCK, 2 * BLOCK),
                         lambda i: (jnp.minimum(cur(i), 1), 0, 0, 0)),
            pl.BlockSpec((1, N_Q_HEADS, BLOCK, 2 * BLOCK), lambda i: (1, 0, 0, 0)),
            pl.BlockSpec((tm, D_CONV), lambda i: (lag(i), 0)),
            pl.BlockSpec((tm, D_MODEL), lambda i: (lag(i), 0)),
            pl.BlockSpec((1, D_MODEL), lambda i: (0, 2)),
            pl.BlockSpec(memory_space=pl.ANY),
            pl.BlockSpec((1, D_MODEL), row),
            pl.BlockSpec((1, D_MODEL), row),
        ],
        out_specs=pl.BlockSpec((tm, D_MODEL), lambda i: (lag(i), 0)),
        out_shape=jax.ShapeDtypeStruct((SEQ, D_MODEL), F32),
        scratch_shapes=[
            pltpu.VMEM((tm, D_ATTN), BF16),
            pltpu.VMEM((D_MODEL, D_MODEL), BF16),
            pltpu.VMEM((2, D_MODEL, W_STAGE_COLS), F32),
            pltpu.SemaphoreType.DMA((2,)),
        ],
        compiler_params=pltpu.CompilerParams(
            dimension_semantics=("arbitrary",), vmem_limit_bytes=VMEM_LIMIT),
        name="attn_out",
    )(sinks, qkvg, qkvg, qkvg, qkvg, qkvg, qkvg, qkvg, bias, bias, y_conv, x2d, mod,
      w_out, ln_g, ln_b)


IN_TM = 256
IN_CHUNK = 256
D_QKVG = COL_GLU_A
CONV_HALO = 32
TAP_ROWS = 32
SYNC_EVERY = 2
LAST_TAP_SYNC_CHUNK = 18


def _tap_piece(u_ref, cw_ref, cb_ref, acc_ref, c, r0):
    off = CONV_HALO - (CONV_WIDTH - 1)
    cs = slice(c * LANES, (c + 1) * LANES)
    acc = jnp.broadcast_to(cb_ref[:, cs], (TAP_ROWS, LANES))
    for k in range(CONV_WIDTH):
        acc = acc + cw_ref[0, k:k + 1, cs] * u_ref[c, r0 + off + k:r0 + off + k + TAP_ROWS, :]
    acc_ref[r0:r0 + TAP_ROWS, cs] = acc
    return _token(acc)


def _conv_tail(acc_ref, lng_ref, lnb_ref, wpw_ref, bpw_ref, gate_ref, yc_ref):
    v = acc_ref[...]
    mu = jnp.mean(v, axis=1, keepdims=True)
    d = v - mu
    var = jnp.mean(d * d, axis=1, keepdims=True)
    y = d * jax.lax.rsqrt(var + LN_EPS) * lng_ref[...] + lnb_ref[...]
    s = _silu(y).astype(BF16)
    z = jnp.dot(s, wpw_ref[...], preferred_element_type=F32) + bpw_ref[...]
    yc_ref[...] = (z * gate_ref[...]).astype(BF16)


def _inproj_conv_kernel(x_ref, shift_ref, scale_ref, w_hbm, cw_ref, cb_ref, lng_ref,
                        lnb_ref, wpw_hbm, bpw_ref, qkvg_ref, yc_ref,
                        h0_ref, h1_ref, u_ref, gc_ref, gcp_ref, acc_ref,
                        w_ref, wpw_ref, stage_ref, sem_ref):
    i = pl.program_id(0)
    tm = IN_TM
    pieces = [(c, r0) for c in range(D_CONV // LANES) for r0 in range(0, tm, TAP_ROWS)]

    @pl.when(i == 0)
    def _():
        u_ref[...] = jnp.zeros(u_ref.shape, u_ref.dtype)
        gc_ref[...] = jnp.zeros(gc_ref.shape, gc_ref.dtype)
        _load_weight_bf16(w_hbm, w_ref, stage_ref, sem_ref)
        _load_weight_bf16(wpw_hbm, wpw_ref, stage_ref, sem_ref)

    gcp_ref[...] = gc_ref[...]
    h = x_ref[...] * (1.0 + scale_ref[...]) + shift_ref[...]
    h0_ref[...] = h.astype(BF16)
    h_bufs = [h0_ref, h1_ref]
    state = {"cur": 0}

    def sync(tokens):
        src, dst = h_bufs[state["cur"]], h_bufs[1 - state["cur"]]
        _copy_after(src, dst, tokens)
        state["cur"] = 1 - state["cur"]

    def project(col):
        return jnp.dot(h_bufs[state["cur"]][...], w_ref[:, col:col + IN_CHUNK],
                       preferred_element_type=F32)

    cols = list(range(0, D_QKVG, IN_CHUNK)) + list(range(COL_G_CONV, D_IN, IN_CHUNK))
    for c in range(D_CONV // IN_CHUNK):
        cols += [COL_GLU_A + c * IN_CHUNK, COL_GLU_B + c * IN_CHUNK]
    sync_chunks = list(range(SYNC_EVERY, LAST_TAP_SYNC_CHUNK + 1, SYNC_EVERY))
    bounds = [round(s * len(pieces) / len(sync_chunks)) for s in range(len(sync_chunks) + 1)]
    glu_a = None
    pieces_done = 0
    for j, col in enumerate(cols):
        if j in sync_chunks:
            s_idx = sync_chunks.index(j)
            group = pieces[bounds[s_idx]:bounds[s_idx + 1]]
            sync([_tap_piece(u_ref, cw_ref, cb_ref, acc_ref, c, r0) for c, r0 in group])
            pieces_done = bounds[s_idx + 1]
            if pieces_done == len(pieces):
                _conv_tail(acc_ref, lng_ref, lnb_ref, wpw_ref, bpw_ref, gcp_ref, yc_ref)
        res = project(col)
        if col < COL_K:
            qkvg_ref[:, col:col + IN_CHUNK] = (res * Q_SCALE).astype(BF16)
        elif col < COL_G_ATTN:
            qkvg_ref[:, col:col + IN_CHUNK] = res.astype(BF16)
        elif col < D_QKVG:
            qkvg_ref[:, col:col + IN_CHUNK] = _silu(res).astype(BF16)
        elif col < COL_GLU_B:
            glu_a = res
        elif col < COL_G_CONV:
            u = glu_a * jax.nn.sigmoid(res)
            t0 = (col - COL_GLU_B) // LANES
            for t in range(t0, t0 + IN_CHUNK // LANES):
                assert pieces_done >= (t + 1) * (tm // TAP_ROWS)
                u_ref[t, 0:CONV_HALO, :] = u_ref[t, tm:tm + CONV_HALO, :]
                u_ref[t, CONV_HALO:CONV_HALO + tm, :] = u[:, (t - t0) * LANES:(t - t0 + 1) * LANES]
        else:
            gc_ref[:, col - COL_G_CONV:col - COL_G_CONV + IN_CHUNK] = _silu(res)


def _inproj_conv_call(x2d, mod, w_in, conv_w, conv_b, ln_g, ln_b, w_pw, b_pw):
    tm = IN_TM
    n = SEQ // tm
    cur = lambda i: jnp.minimum(i, n - 1)
    lag = lambda i: jnp.maximum(i - 1, 0)
    row = lambda i: (0, 0)
    return pl.pallas_call(
        _inproj_conv_kernel,
        grid=(n + 1,),
        in_specs=[
            pl.BlockSpec((tm, D_MODEL), lambda i: (cur(i), 0)),
            pl.BlockSpec((1, D_MODEL), lambda i: (0, 0)),
            pl.BlockSpec((1, D_MODEL), lambda i: (0, 1)),
            pl.BlockSpec(memory_space=pl.ANY),
            pl.BlockSpec((DEPTH, CONV_WIDTH, D_CONV), lambda i: (0, 0, 0)),
            pl.BlockSpec((1, D_CONV), row),
            pl.BlockSpec((1, D_CONV), row),
            pl.BlockSpec((1, D_CONV), row),
            pl.BlockSpec(memory_space=pl.ANY),
            pl.BlockSpec((1, D_CONV), row),
        ],
        out_specs=[
            pl.BlockSpec((tm, D_QKVG), lambda i: (cur(i), 0)),
            pl.BlockSpec((tm, D_CONV), lambda i: (lag(i), 0)),
        ],
        out_shape=[
            jax.ShapeDtypeStruct((SEQ, D_QKVG), BF16),
            jax.ShapeDtypeStruct((SEQ, D_CONV), BF16),
        ],
        scratch_shapes=[
            pltpu.VMEM((tm, D_MODEL), BF16),
            pltpu.VMEM((tm, D_MODEL), BF16),
            pltpu.VMEM((D_CONV // LANES, CONV_HALO + tm, LANES), F32),
            pltpu.VMEM((tm, D_CONV), F32),
            pltpu.VMEM((tm, D_CONV), F32),
            pltpu.VMEM((tm, D_CONV), F32),
            pltpu.VMEM((D_MODEL, D_IN), BF16),
            pltpu.VMEM((D_CONV, D_CONV), BF16),
            pltpu.VMEM((2, D_MODEL, W_STAGE_COLS), F32),
            pltpu.SemaphoreType.DMA((2,)),
        ],
        compiler_params=pltpu.CompilerParams(
            dimension_semantics=("arbitrary",), vmem_limit_bytes=VMEM_LIMIT),
        name="inproj_conv",
    )(x2d, mod, mod, w_in, conv_w, conv_b, ln_g, ln_b, w_pw, b_pw)


def kernel(x, c, w_ada, b_ada, w_in, rel_bias, sinks, conv_w, conv_b, conv_ln_g,
           conv_ln_b, w_pw, b_pw, w_out, ln_g, ln_b):
    assert x.shape == (1, SEQ, D_MODEL) and w_in.shape == (DEPTH, D_MODEL, D_IN)
    x2d = x.reshape(SEQ, D_MODEL)

    row = lambda v: v.reshape(1, -1)
    mod, bias = _mod_bias_call(row(c), w_ada.reshape(D_MODEL, 3 * D_MODEL), row(b_ada), rel_bias)
    qkvg, y_conv = _inproj_conv_call(
        x2d, mod, w_in.reshape(D_MODEL, D_IN),
        conv_w, row(conv_b), row(conv_ln_g), row(conv_ln_b),
        w_pw.reshape(D_CONV, D_CONV), row(b_pw))
    out = _attn_out_call(qkvg, bias, sinks.reshape(N_Q_HEADS), y_conv, x2d, mod,
                         w_out.reshape(D_MODEL, D_MODEL), row(ln_g), row(ln_b))
    return out.reshape(1, SEQ, D_MODEL)
```

```python
import math

import jax
import jax.numpy as jnp
import numpy as np
from jax.experimental import pallas as pl
from jax.experimental.pallas import tpu as pltpu

F32 = jnp.float32
BF16 = jnp.bfloat16

D_MODEL = 2048
SEQ = 8192
HEAD_DIM = 64
D_ATTN = 1024
D_CONV = 1024
N_Q_HEADS = 16
N_KV_HEADS = 4
GQA_GROUP = 4
D_KV = 256
WINDOW = 128
BLOCK = 128
CONV_WIDTH = 31
N_BUCKETS = 32
MAX_DISTANCE = 128
LN_EPS = 1e-5
DEPTH = 1
ALPHA = (2.0 * DEPTH) ** 0.25
D_IN = 2 * D_ATTN + 2 * D_KV + 3 * D_CONV

LANES = 128
SUBLANES = 8

COL_Q = 0
COL_K = COL_Q + D_ATTN
COL_V = COL_K + D_KV
COL_G_ATTN = COL_V + D_KV
COL_GLU_A = COL_G_ATTN + D_ATTN
COL_GLU_B = COL_GLU_A + D_CONV
COL_G_CONV = COL_GLU_B + D_CONV
HALF = 512

VMEM_LIMIT = 56 * 1024 * 1024


def _silu(v):
    return v * jax.nn.sigmoid(v)


def _mod_kernel(c_ref, w_ref, b_ref, o_ref):
    ca = _silu(c_ref[...])
    ca_cols = jnp.transpose(jnp.broadcast_to(ca, (LANES, D_MODEL)))
    for t in range(w_ref.shape[1] // LANES):
        ts = slice(t * LANES, (t + 1) * LANES)
        o_ref[:, ts] = jnp.sum(ca_cols * w_ref[:, ts], axis=0, keepdims=True) + b_ref[:, ts]


def _mod_call(c_row, w_ada, b_ada):
    tn = 1024
    n = w_ada.shape[1]
    return pl.pallas_call(
        _mod_kernel,
        grid=(n // tn,),
        in_specs=[
            pl.BlockSpec((1, D_MODEL), lambda j: (0, 0)),
            pl.BlockSpec((D_MODEL, tn), lambda j: (0, j)),
            pl.BlockSpec((1, tn), lambda j: (0, j)),
        ],
        out_specs=pl.BlockSpec((1, tn), lambda j: (0, j)),
        out_shape=jax.ShapeDtypeStruct((1, n), F32),
        compiler_params=pltpu.CompilerParams(
            dimension_semantics=("arbitrary",), vmem_limit_bytes=VMEM_LIMIT),
        name="mod",
    )(c_row, w_ada, b_ada)


LOG2E = math.log2(math.e)
Q_SCALE = LOG2E * HEAD_DIM ** -0.5
MASKED = -1e30


def _lane_buckets():
    m = np.arange(2 * BLOCK)
    dist = BLOCK - m
    ok = (dist >= 0) & (dist < WINDOW)
    max_exact = N_BUCKETS // 2
    d = np.maximum(dist, 1).astype(np.float64)
    large = max_exact + (np.log(d / max_exact) / math.log(MAX_DISTANCE / max_exact)
                         * (N_BUCKETS - max_exact)).astype(np.int32)
    large = np.minimum(large, N_BUCKETS - 1)
    bucket = np.where(dist < max_exact, dist, large)
    return np.where(ok, bucket, -1).astype(np.int32).reshape(1, -1)


def _bias_kernel(rb_ref, bm_ref, o_ref):
    bm = bm_ref[...]
    col = jax.lax.broadcasted_iota(jnp.int32, (BLOCK, 2 * BLOCK), 1)
    for h in range(N_Q_HEADS):
        t = jnp.full(bm.shape, MASKED, F32)
        for b in range(N_BUCKETS):
            t = jnp.where(bm == b, rb_ref[b, h] * LOG2E, t)
        x = pltpu.roll(jnp.broadcast_to(t, (BLOCK, 2 * BLOCK)), 0, 1, stride=1, stride_axis=0)
        o_ref[1, h] = x
        o_ref[0, h] = jnp.where(col >= BLOCK, x, MASKED)


def _bias_call(rel_bias):
    shape = (2, N_Q_HEADS, BLOCK, 2 * BLOCK)
    return pl.pallas_call(
        _bias_kernel,
        in_specs=[
            pl.BlockSpec(memory_space=pltpu.SMEM),
            pl.BlockSpec((1, 2 * BLOCK), lambda: (0, 0)),
        ],
        out_specs=pl.BlockSpec(shape, lambda: (0, 0, 0, 0)),
        out_shape=jax.ShapeDtypeStruct(shape, F32),
        name="bias",
    )(rel_bias, jnp.asarray(_lane_buckets()))


W_STAGE_COLS = 256


def _load_weight_bf16(w_hbm, w_ref, stage_ref, sem_ref, col_scale=None):
    rows, cols = w_ref.shape
    n = cols // W_STAGE_COLS

    def chunk_copy(j):
        slot = j % 2
        return pltpu.make_async_copy(
            w_hbm.at[:, pl.ds(j * W_STAGE_COLS, W_STAGE_COLS)],
            stage_ref.at[slot, pl.ds(0, rows), :], sem_ref.at[slot])

    chunk_copy(0).start()
    for j in range(n):
        if j + 1 < n:
            chunk_copy(j + 1).start()
        chunk_copy(j).wait()
        cs = slice(j * W_STAGE_COLS, (j + 1) * W_STAGE_COLS)
        chunk = stage_ref[j % 2, 0:rows, :]
        if col_scale is not None:
            chunk = chunk * col_scale[:, cs]
        w_ref[:, cs] = chunk.astype(BF16)


def _token(v):
    t = v[0:SUBLANES, 0:LANES]
    for r in range(SUBLANES, v.shape[0], SUBLANES):
        t = t + v[r:r + SUBLANES, 0:LANES]
    return t


def _zero_of(token):
    return (pltpu.bitcast(token, jnp.uint32) >> 16) >> 16


def _copy_after(src_ref, dst_ref, tokens):
    zero = _zero_of(tokens[0])
    for t in tokens[1:]:
        zero = zero | _zero_of(t)
    bits = pltpu.bitcast(src_ref[...], jnp.uint32)
    reps = (bits.shape[0] // zero.shape[0], bits.shape[1] // zero.shape[1])
    dst_ref[...] = pltpu.bitcast(bits | jnp.tile(zero, reps), dst_ref.dtype)


ATT_TM = 4 * BLOCK
ATT_ROWS = BLOCK


def _dup_halves(t):
    swapped = jnp.concatenate([t[:, HEAD_DIM:], t[:, :HEAD_DIM]], axis=1)
    lane = jax.lax.broadcasted_iota(jnp.int32, t.shape, 1)
    low_half = lane < HEAD_DIM
    return jnp.where(low_half, t, swapped), jnp.where(low_half, swapped, t)


def _attn_group(q_ref, g0_ref, g1_ref, k_dup, v_dup, bias_ref, sinks_ref, r0, b0, kv, ya_ref):
    nr = ATT_ROWS
    lane = jax.lax.broadcasted_iota(jnp.int32, (nr, LANES), 1)
    low_half = lane < HEAD_DIM
    rows = slice(r0, r0 + nr)
    tiles = [GQA_GROUP * kv // 2 + t for t in range(GQA_GROUP // 2)]
    stacked = []
    for tile in tiles:
        q_tile = q_ref[rows, tile * LANES:(tile + 1) * LANES]
        zero = jnp.zeros_like(q_tile)
        stacked += [jnp.where(low_half, q_tile, zero), jnp.where(low_half, zero, q_tile)]
    s_all = jax.lax.dot_general(jnp.concatenate(stacked, axis=0), k_dup,
                                (((1,), (1,)), ((), ())),
                                preferred_element_type=F32)
    probs, denoms = [], []
    for r in range(GQA_GROUP):
        head = GQA_GROUP * kv + r
        s = s_all[r * nr:(r + 1) * nr] + bias_ref[0, head, b0:b0 + nr, :]
        sink = sinks_ref[head] * LOG2E
        m = jnp.max(s, axis=1, keepdims=True)
        e = jnp.exp2(s - m)
        denoms.append(jnp.sum(e, axis=1, keepdims=True) + jnp.exp2(sink - m))
        probs.append(e.astype(BF16))
    o_all = jnp.dot(jnp.concatenate(probs, axis=0), v_dup,
                    preferred_element_type=F32)
    outs = [o_all[r * nr:(r + 1) * nr] / denoms[r] for r in range(GQA_GROUP)]
    for t, tile in enumerate(tiles):
        y = jnp.where(low_half, outs[2 * t], outs[2 * t + 1])
        g_ref = g0_ref if tile < HALF // LANES else g1_ref
        gl = (tile * LANES) % HALF
        gate = g_ref[rows, gl:gl + LANES].astype(F32)
        ya_ref[rows, tile * LANES:(tile + 1) * LANES] = (y * gate).astype(BF16)


def _attn_tile(sinks_ref, q_ref, g0_ref, g1_ref, k_ref, v_ref, kp_ref, vp_ref,
               bias_refs, ya_ref):
    n_lt = D_KV // LANES
    lt = lambda t: slice(t * LANES, (t + 1) * LANES)

    def block_operands(b):
        r0 = b * BLOCK
        if b == 0:
            k_prev = [kp_ref[:, lt(t)] for t in range(n_lt)]
            v_prev = [vp_ref[:, lt(t)] for t in range(n_lt)]
        else:
            k_prev = [k_ref[r0 - BLOCK:r0, lt(t)] for t in range(n_lt)]
            v_prev = [v_ref[r0 - BLOCK:r0, lt(t)] for t in range(n_lt)]
        k_rows = [jnp.concatenate([k_prev[t], k_ref[r0:r0 + BLOCK, lt(t)]], axis=0)
                  for t in range(n_lt)]
        v_rows = [jnp.concatenate([v_prev[t], v_ref[r0:r0 + BLOCK, lt(t)]], axis=0)
                  for t in range(n_lt)]
        return ([_dup_halves(kt) for kt in k_rows], [_dup_halves(vt) for vt in v_rows])

    for b in range(ATT_TM // BLOCK):
        k_dups, v_dups = block_operands(b)
        for b0 in range(0, BLOCK, ATT_ROWS):
            for kv in range(N_KV_HEADS):
                _attn_group(q_ref, g0_ref, g1_ref, k_dups[kv // 2][kv % 2],
                            v_dups[kv // 2][kv % 2], bias_refs[b], sinks_ref,
                            b * BLOCK + b0, b0, kv, ya_ref)


def _attn_out_kernel(sinks_ref, q_ref, g0_ref, g1_ref, k_ref, v_ref, kp_ref, vp_ref,
                     bias0_ref, bias1_ref, yc_ref, x_ref, gate_ref, wout_hbm,
                     lng_ref, lnb_ref, o_ref, ya_ref, wout_ref, stage_ref, sem_ref):
    i = pl.program_id(0)

    @pl.when(i == 0)
    def _():
        ya_ref[...] = jnp.zeros(ya_ref.shape, ya_ref.dtype)
        _load_weight_bf16(wout_hbm, wout_ref, stage_ref, sem_ref,
                          col_scale=gate_ref[...] * (1.0 / ALPHA))

    y = jnp.dot(ya_ref[...], wout_ref[0:D_ATTN, :], preferred_element_type=F32)
    y = y + jnp.dot(yc_ref[...], wout_ref[D_ATTN:D_MODEL, :], preferred_element_type=F32)
    z = x_ref[...] + y
    mu = jnp.mean(z, axis=1, keepdims=True)
    d = z - mu
    var = jnp.mean(d * d, axis=1, keepdims=True)
    o_ref[...] = d * jax.lax.rsqrt(var + LN_EPS / ALPHA ** 2) * lng_ref[...] + lnb_ref[...]

    _attn_tile(sinks_ref, q_ref, g0_ref, g1_ref, k_ref, v_ref, kp_ref, vp_ref,
               [bias0_ref] + [bias1_ref] * (ATT_TM // BLOCK - 1), ya_ref)


def _attn_out_call(qkvg, bias, sinks, y_conv, x2d, mod, w_out, ln_g, ln_b):
    tm = ATT_TM
    n = SEQ // tm
    clamp = lambda t: jnp.clip(t, 0, n - 1)
    cur = lambda i: clamp(i)
    lag = lambda i: clamp(i - 1)
    kv_prev = lambda i: jnp.maximum(cur(i) * (tm // BLOCK) - 1, 0)
    row = lambda i: (0, 0)
    return pl.pallas_call(
        _attn_out_kernel,
        grid=(n + 1,),
        in_specs=[
            pl.BlockSpec(memory_space=pltpu.SMEM),
            pl.BlockSpec((tm, D_ATTN), lambda i: (cur(i), COL_Q // D_ATTN)),
            pl.BlockSpec((tm, HALF), lambda i: (cur(i), COL_G_ATTN // HALF)),
            pl.BlockSpec((tm, HALF), lambda i: (cur(i), COL_G_ATTN // HALF + 1)),
            pl.BlockSpec((tm, D_KV), lambda i: (cur(i), COL_K // D_KV)),
            pl.BlockSpec((tm, D_KV), lambda i: (cur(i), COL_V // D_KV)),
            pl.BlockSpec((BLOCK, D_KV), lambda i: (kv_prev(i), COL_K // D_KV)),
            pl.BlockSpec((BLOCK, D_KV), lambda i: (kv_prev(i), COL_V // D_KV)),
            pl.BlockSpec((1, N_Q_HEADS, BLOCK, 2 * BLOCK),
                         lambda i: (jnp.minimum(cur(i), 1), 0, 0, 0)),
            pl.BlockSpec((1, N_Q_HEADS, BLOCK, 2 * BLOCK), lambda i: (1, 0, 0, 0)),
            pl.BlockSpec((tm, D_CONV), lambda i: (lag(i), 0)),
            pl.BlockSpec((tm, D_MODEL), lambda i: (lag(i), 0)),
            pl.BlockSpec((1, D_MODEL), lambda i: (0, 2)),
            pl.BlockSpec(memory_space=pl.ANY),
            pl.BlockSpec((1, D_MODEL), row),
            pl.BlockSpec((1, D_MODEL), row),
        ],
        out_specs=pl.BlockSpec((tm, D_MODEL), lambda i: (lag(i), 0)),
        out_shape=jax.ShapeDtypeStruct((SEQ, D_MODEL), F32),
        scratch_shapes=[
            pltpu.VMEM((tm, D_ATTN), BF16),
            pltpu.VMEM((D_MODEL, D_MODEL), BF16),
            pltpu.VMEM((2, D_MODEL, W_STAGE_COLS), F32),
            pltpu.SemaphoreType.DMA((2,)),
        ],
        compiler_params=pltpu.CompilerParams(
            dimension_semantics=("arbitrary",), vmem_limit_bytes=VMEM_LIMIT),
        name="attn_out",
    )(sinks, qkvg, qkvg, qkvg, qkvg, qkvg, qkvg, qkvg, bias, bias, y_conv, x2d, mod,
      w_out, ln_g, ln_b)


IN_TM = 256
IN_CHUNK = 256
D_QKVG = COL_GLU_A
CONV_HALO = 32
TAP_ROWS = 32
SYNC_EVERY = 2
LAST_TAP_SYNC_CHUNK = 18


def _tap_piece(u_ref, cw_ref, cb_ref, acc_ref, c, r0):
    off = CONV_HALO - (CONV_WIDTH - 1)
    cs = slice(c * LANES, (c + 1) * LANES)
    acc = jnp.broadcast_to(cb_ref[:, cs], (TAP_ROWS, LANES))
    for k in range(CONV_WIDTH):
        acc = acc + cw_ref[0, k:k + 1, cs] * u_ref[c, r0 + off + k:r0 + off + k + TAP_ROWS, :]
    acc_ref[r0:r0 + TAP_ROWS, cs] = acc
    return _token(acc)


def _conv_tail(acc_ref, lng_ref, lnb_ref, wpw_ref, bpw_ref, gate_ref, yc_ref):
    v = acc_ref[...]
    mu = jnp.mean(v, axis=1, keepdims=True)
    d = v - mu
    var = jnp.mean(d * d, axis=1, keepdims=True)
    y = d * jax.lax.rsqrt(var + LN_EPS) * lng_ref[...] + lnb_ref[...]
    s = _silu(y).astype(BF16)
    z = jnp.dot(s, wpw_ref[...], preferred_element_type=F32) + bpw_ref[...]
    yc_ref[...] = (z * gate_ref[...]).astype(BF16)


def _inproj_conv_kernel(x_ref, shift_ref, scale_ref, w_hbm, cw_ref, cb_ref, lng_ref,
                        lnb_ref, wpw_hbm, bpw_ref, qkvg_ref, yc_ref,
                        h0_ref, h1_ref, u_ref, gc_ref, gcp_ref, acc_ref,
                        w_ref, wpw_ref, stage_ref, sem_ref):
    i = pl.program_id(0)
    tm = IN_TM
    pieces = [(c, r0) for c in range(D_CONV // LANES) for r0 in range(0, tm, TAP_ROWS)]

    @pl.when(i == 0)
    def _():
        u_ref[...] = jnp.zeros(u_ref.shape, u_ref.dtype)
        gc_ref[...] = jnp.zeros(gc_ref.shape, gc_ref.dtype)
        _load_weight_bf16(w_hbm, w_ref, stage_ref, sem_ref)
        _load_weight_bf16(wpw_hbm, wpw_ref, stage_ref, sem_ref)

    gcp_ref[...] = gc_ref[...]
    h = x_ref[...] * (1.0 + scale_ref[...]) + shift_ref[...]
    h0_ref[...] = h.astype(BF16)
    h_bufs = [h0_ref, h1_ref]
    state = {"cur": 0}

    def sync(tokens):
        src, dst = h_bufs[state["cur"]], h_bufs[1 - state["cur"]]
        _copy_after(src, dst, tokens)
        state["cur"] = 1 - state["cur"]

    def project(col):
        return jnp.dot(h_bufs[state["cur"]][...], w_ref[:, col:col + IN_CHUNK],
                       preferred_element_type=F32)

    cols = list(range(0, D_QKVG, IN_CHUNK)) + list(range(COL_G_CONV, D_IN, IN_CHUNK))
    for c in range(D_CONV // IN_CHUNK):
        cols += [COL_GLU_A + c * IN_CHUNK, COL_GLU_B + c * IN_CHUNK]
    sync_chunks = list(range(SYNC_EVERY, LAST_TAP_SYNC_CHUNK + 1, SYNC_EVERY))
    bounds = [round(s * len(pieces) / len(sync_chunks)) for s in range(len(sync_chunks) + 1)]
    glu_a = None
    pieces_done = 0
    for j, col in enumerate(cols):
        if j in sync_chunks:
            s_idx = sync_chunks.index(j)
            group = pieces[bounds[s_idx]:bounds[s_idx + 1]]
            sync([_tap_piece(u_ref, cw_ref, cb_ref, acc_ref, c, r0) for c, r0 in group])
            pieces_done = bounds[s_idx + 1]
            if pieces_done == len(pieces):
                _conv_tail(acc_ref, lng_ref, lnb_ref, wpw_ref, bpw_ref, gcp_ref, yc_ref)
        res = project(col)
        if col < COL_K:
            qkvg_ref[:, col:col + IN_CHUNK] = (res * Q_SCALE).astype(BF16)
        elif col < COL_G_ATTN:
            qkvg_ref[:, col:col + IN_CHUNK] = res.astype(BF16)
        elif col < D_QKVG:
            qkvg_ref[:, col:col + IN_CHUNK] = _silu(res).astype(BF16)
        elif col < COL_GLU_B:
            glu_a = res
        elif col < COL_G_CONV:
            u = glu_a * jax.nn.sigmoid(res)
            t0 = (col - COL_GLU_B) // LANES
            for t in range(t0, t0 + IN_CHUNK // LANES):
                assert pieces_done >= (t + 1) * (tm // TAP_ROWS)
                u_ref[t, 0:CONV_HALO, :] = u_ref[t, tm:tm + CONV_HALO, :]
                u_ref[t, CONV_HALO:CONV_HALO + tm, :] = u[:, (t - t0) * LANES:(t - t0 + 1) * LANES]
        else:
            gc_ref[:, col - COL_G_CONV:col - COL_G_CONV + IN_CHUNK] = _silu(res)


def _inproj_conv_call(x2d, mod, w_in, conv_w, conv_b, ln_g, ln_b, w_pw, b_pw):
    tm = IN_TM
    n = SEQ // tm
    cur = lambda i: jnp.minimum(i, n - 1)
    lag = lambda i: jnp.maximum(i - 1, 0)
    row = lambda i: (0, 0)
    return pl.pallas_call(
        _inproj_conv_kernel,
        grid=(n + 1,),
        in_specs=[
            pl.BlockSpec((tm, D_MODEL), lambda i: (cur(i), 0)),
            pl.BlockSpec((1, D_MODEL), lambda i: (0, 0)),
            pl.BlockSpec((1, D_MODEL), lambda i: (0, 1)),
            pl.BlockSpec(memory_space=pl.ANY),
            pl.BlockSpec((DEPTH, CONV_WIDTH, D_CONV), lambda i: (0, 0, 0)),
            pl.BlockSpec((1, D_CONV), row),
            pl.BlockSpec((1, D_CONV), row),
            pl.BlockSpec((1, D_CONV), row),
            pl.BlockSpec(memory_space=pl.ANY),
            pl.BlockSpec((1, D_CONV), row),
        ],
        out_specs=[
            pl.BlockSpec((tm, D_QKVG), lambda i: (cur(i), 0)),
            pl.BlockSpec((tm, D_CONV), lambda i: (lag(i), 0)),
        ],
        out_shape=[
            jax.ShapeDtypeStruct((SEQ, D_QKVG), BF16),
            jax.ShapeDtypeStruct((SEQ, D_CONV), BF16),
        ],
        scratch_shapes=[
            pltpu.VMEM((tm, D_MODEL), BF16),
            pltpu.VMEM((tm, D_MODEL), BF16),
            pltpu.VMEM((D_CONV // LANES, CONV_HALO + tm, LANES), F32),
            pltpu.VMEM((tm, D_CONV), F32),
            pltpu.VMEM((tm, D_CONV), F32),
            pltpu.VMEM((tm, D_CONV), F32),
            pltpu.VMEM((D_MODEL, D_IN), BF16),
            pltpu.VMEM((D_CONV, D_CONV), BF16),
            pltpu.VMEM((2, D_MODEL, W_STAGE_COLS), F32),
            pltpu.SemaphoreType.DMA((2,)),
        ],
        compiler_params=pltpu.CompilerParams(
            dimension_semantics=("arbitrary",), vmem_limit_bytes=VMEM_LIMIT),
        name="inproj_conv",
    )(x2d, mod, mod, w_in, conv_w, conv_b, ln_g, ln_b, w_pw, b_pw)


def kernel(x, c, w_ada, b_ada, w_in, rel_bias, sinks, conv_w, conv_b, conv_ln_g,
           conv_ln_b, w_pw, b_pw, w_out, ln_g, ln_b):
    assert x.shape == (1, SEQ, D_MODEL) and w_in.shape == (DEPTH, D_MODEL, D_IN)
    x2d = x.reshape(SEQ, D_MODEL)

    row = lambda v: v.reshape(1, -1)
    mod = _mod_call(row(c), w_ada.reshape(D_MODEL, 3 * D_MODEL), row(b_ada))
    bias = _bias_call(rel_bias)
    qkvg, y_conv = _inproj_conv_call(
        x2d, mod, w_in.reshape(D_MODEL, D_IN),
        conv_w, row(conv_b), row(conv_ln_g), row(conv_ln_b),
        w_pw.reshape(D_CONV, D_CONV), row(b_pw))
    out = _attn_out_call(qkvg, bias, sinks.reshape(N_Q_HEADS), y_conv, x2d, mod,
                         w_out.reshape(D_MODEL, D_MODEL), row(ln_g), row(ln_b))
    return out.reshape(1, SEQ, D_MODEL)
```

```python
import math

import jax
import jax.numpy as jnp
import numpy as np
from jax.experimental import pallas as pl
from jax.experimental.pallas import tpu as pltpu

F32 = jnp.float32
BF16 = jnp.bfloat16

D_MODEL = 2048
SEQ = 8192
HEAD_DIM = 64
D_ATTN = 1024
D_CONV = 1024
N_Q_HEADS = 16
N_KV_HEADS = 4
GQA_GROUP = 4
D_KV = 256
WINDOW = 128
BLOCK = 128
CONV_WIDTH = 31
N_BUCKETS = 32
MAX_DISTANCE = 128
LN_EPS = 1e-5
DEPTH = 1
ALPHA = (2.0 * DEPTH) ** 0.25
D_IN = 2 * D_ATTN + 2 * D_KV + 3 * D_CONV

LANES = 128
SUBLANES = 8

COL_Q = 0
COL_K = COL_Q + D_ATTN
COL_V = COL_K + D_KV
COL_G_ATTN = COL_V + D_KV
COL_GLU_A = COL_G_ATTN + D_ATTN
COL_GLU_B = COL_GLU_A + D_CONV
COL_G_CONV = COL_GLU_B + D_CONV
HALF = 512

VMEM_LIMIT = 56 * 1024 * 1024


def _silu(v):
    return v * jax.nn.sigmoid(v)


def _mod_kernel(c_ref, w_ref, b_ref, o_ref):
    ca = _silu(c_ref[...])
    ca_cols = jnp.transpose(jnp.broadcast_to(ca, (LANES, D_MODEL)))
    for t in range(w_ref.shape[1] // LANES):
        ts = slice(t * LANES, (t + 1) * LANES)
        o_ref[:, ts] = jnp.sum(ca_cols * w_ref[:, ts], axis=0, keepdims=True) + b_ref[:, ts]


def _mod_call(c_row, w_ada, b_ada):
    tn = 1024
    n = w_ada.shape[1]
    return pl.pallas_call(
        _mod_kernel,
        grid=(n // tn,),
        in_specs=[
            pl.BlockSpec((1, D_MODEL), lambda j: (0, 0)),
            pl.BlockSpec((D_MODEL, tn), lambda j: (0, j)),
            pl.BlockSpec((1, tn), lambda j: (0, j)),
        ],
        out_specs=pl.BlockSpec((1, tn), lambda j: (0, j)),
        out_shape=jax.ShapeDtypeStruct((1, n), F32),
        compiler_params=pltpu.CompilerParams(
            dimension_semantics=("arbitrary",), vmem_limit_bytes=VMEM_LIMIT),
        name="mod",
    )(c_row, w_ada, b_ada)


LOG2E = math.log2(math.e)
Q_SCALE = LOG2E * HEAD_DIM ** -0.5
MASKED = -1e30


def _lane_buckets():
    m = np.arange(2 * BLOCK)
    dist = BLOCK - m
    ok = (dist >= 0) & (dist < WINDOW)
    max_exact = N_BUCKETS // 2
    d = np.maximum(dist, 1).astype(np.float64)
    large = max_exact + (np.log(d / max_exact) / math.log(MAX_DISTANCE / max_exact)
                         * (N_BUCKETS - max_exact)).astype(np.int32)
    large = np.minimum(large, N_BUCKETS - 1)
    bucket = np.where(dist < max_exact, dist, large)
    return np.where(ok, bucket, -1).astype(np.int32).reshape(1, -1)


def _bias_kernel(rb_ref, bm_ref, o_ref):
    bm = bm_ref[...]
    col = jax.lax.broadcasted_iota(jnp.int32, (BLOCK, 2 * BLOCK), 1)
    for h in range(N_Q_HEADS):
        t = jnp.full(bm.shape, MASKED, F32)
        for b in range(N_BUCKETS):
            t = jnp.where(bm == b, rb_ref[b, h] * LOG2E, t)
        x = pltpu.roll(jnp.broadcast_to(t, (BLOCK, 2 * BLOCK)), 0, 1, stride=1, stride_axis=0)
        o_ref[1, h] = x
        o_ref[0, h] = jnp.where(col >= BLOCK, x, MASKED)


def _bias_call(rel_bias):
    shape = (2, N_Q_HEADS, BLOCK, 2 * BLOCK)
    return pl.pallas_call(
        _bias_kernel,
        in_specs=[
            pl.BlockSpec(memory_space=pltpu.SMEM),
            pl.BlockSpec((1, 2 * BLOCK), lambda: (0, 0)),
        ],
        out_specs=pl.BlockSpec(shape, lambda: (0, 0, 0, 0)),
        out_shape=jax.ShapeDtypeStruct(shape, F32),
        name="bias",
    )(rel_bias, jnp.asarray(_lane_buckets()))


W_STAGE_ROWS = 128


def _load_weight_bf16(w_hbm, w_ref, stage_ref, sem_ref, col_scale=None):
    rows, cols = w_ref.shape
    n = rows // W_STAGE_ROWS

    def chunk_copy(j):
        slot = j % 2
        return pltpu.make_async_copy(
            w_hbm.at[pl.ds(j * W_STAGE_ROWS, W_STAGE_ROWS), :],
            stage_ref.at[slot, :, pl.ds(0, cols)], sem_ref.at[slot])

    chunk_copy(0).start()
    for j in range(n):
        if j + 1 < n:
            chunk_copy(j + 1).start()
        chunk_copy(j).wait()
        chunk = stage_ref[j % 2, :, 0:cols]
        if col_scale is not None:
            chunk = chunk * col_scale
        w_ref[j * W_STAGE_ROWS:(j + 1) * W_STAGE_ROWS, :] = chunk.astype(BF16)


def _token(v):
    t = v[0:SUBLANES, 0:LANES]
    for r in range(SUBLANES, v.shape[0], SUBLANES):
        t = t + v[r:r + SUBLANES, 0:LANES]
    return t


def _zero_of(token):
    return (pltpu.bitcast(token, jnp.uint32) >> 16) >> 16


def _copy_after(src_ref, dst_ref, tokens):
    zero = _zero_of(tokens[0])
    for t in tokens[1:]:
        zero = zero | _zero_of(t)
    bits = pltpu.bitcast(src_ref[...], jnp.uint32)
    reps = (bits.shape[0] // zero.shape[0], bits.shape[1] // zero.shape[1])
    dst_ref[...] = pltpu.bitcast(bits | jnp.tile(zero, reps), dst_ref.dtype)


ATT_TM = 4 * BLOCK
ATT_ROWS = BLOCK


def _dup_halves(t):
    swapped = jnp.concatenate([t[:, HEAD_DIM:], t[:, :HEAD_DIM]], axis=1)
    lane = jax.lax.broadcasted_iota(jnp.int32, t.shape, 1)
    low_half = lane < HEAD_DIM
    return jnp.where(low_half, t, swapped), jnp.where(low_half, swapped, t)


def _attn_group(q_ref, g0_ref, g1_ref, k_dup, v_dup, bias_ref, sinks_ref, r0, b0, kv, ya_ref):
    nr = ATT_ROWS
    lane = jax.lax.broadcasted_iota(jnp.int32, (nr, LANES), 1)
    low_half = lane < HEAD_DIM
    rows = slice(r0, r0 + nr)
    tiles = [GQA_GROUP * kv // 2 + t for t in range(GQA_GROUP // 2)]
    stacked = []
    for tile in tiles:
        q_tile = q_ref[rows, tile * LANES:(tile + 1) * LANES]
        zero = jnp.zeros_like(q_tile)
        stacked += [jnp.where(low_half, q_tile, zero), jnp.where(low_half, zero, q_tile)]
    s_all = jax.lax.dot_general(jnp.concatenate(stacked, axis=0), k_dup,
                                (((1,), (1,)), ((), ())),
                                preferred_element_type=F32)
    probs, denoms = [], []
    for r in range(GQA_GROUP):
        head = GQA_GROUP * kv + r
        s = s_all[r * nr:(r + 1) * nr] + bias_ref[0, head, b0:b0 + nr, :]
        sink = sinks_ref[head] * LOG2E
        m = jnp.max(s, axis=1, keepdims=True)
        e = jnp.exp2(s - m)
        denoms.append(jnp.sum(e, axis=1, keepdims=True) + jnp.exp2(sink - m))
        probs.append(e.astype(BF16))
    o_all = jnp.dot(jnp.concatenate(probs, axis=0), v_dup,
                    preferred_element_type=F32)
    outs = [o_all[r * nr:(r + 1) * nr] / denoms[r] for r in range(GQA_GROUP)]
    for t, tile in enumerate(tiles):
        y = jnp.where(low_half, outs[2 * t], outs[2 * t + 1])
        g_ref = g0_ref if tile < HALF // LANES else g1_ref
        gl = (tile * LANES) % HALF
        gate = g_ref[rows, gl:gl + LANES].astype(F32)
        ya_ref[rows, tile * LANES:(tile + 1) * LANES] = (y * gate).astype(BF16)


def _attn_tile(sinks_ref, q_ref, g0_ref, g1_ref, k_ref, v_ref, kp_ref, vp_ref,
               bias_refs, ya_ref):
    n_lt = D_KV // LANES
    lt = lambda t: slice(t * LANES, (t + 1) * LANES)

    def block_operands(b):
        r0 = b * BLOCK
        if b == 0:
            k_prev = [kp_ref[:, lt(t)] for t in range(n_lt)]
            v_prev = [vp_ref[:, lt(t)] for t in range(n_lt)]
        else:
            k_prev = [k_ref[r0 - BLOCK:r0, lt(t)] for t in range(n_lt)]
            v_prev = [v_ref[r0 - BLOCK:r0, lt(t)] for t in range(n_lt)]
        k_rows = [jnp.concatenate([k_prev[t], k_ref[r0:r0 + BLOCK, lt(t)]], axis=0)
                  for t in range(n_lt)]
        v_rows = [jnp.concatenate([v_prev[t], v_ref[r0:r0 + BLOCK, lt(t)]], axis=0)
                  for t in range(n_lt)]
        return ([_dup_halves(kt) for kt in k_rows], [_dup_halves(vt) for vt in v_rows])

    for b in range(ATT_TM // BLOCK):
        k_dups, v_dups = block_operands(b)
        for b0 in range(0, BLOCK, ATT_ROWS):
            for kv in range(N_KV_HEADS):
                _attn_group(q_ref, g0_ref, g1_ref, k_dups[kv // 2][kv % 2],
                            v_dups[kv // 2][kv % 2], bias_refs[b], sinks_ref,
                            b * BLOCK + b0, b0, kv, ya_ref)


def _attn_out_kernel(sinks_ref, q_ref, g0_ref, g1_ref, k_ref, v_ref, kp_ref, vp_ref,
                     bias0_ref, bias1_ref, yc_ref, x_ref, gate_ref, wout_hbm,
                     lng_ref, lnb_ref, o_ref, ya_ref, wout_ref, stage_ref, sem_ref):
    i = pl.program_id(0)

    @pl.when(i == 0)
    def _():
        ya_ref[...] = jnp.zeros(ya_ref.shape, ya_ref.dtype)
        _load_weight_bf16(wout_hbm, wout_ref, stage_ref, sem_ref,
                          col_scale=gate_ref[...] * (1.0 / ALPHA))

    y = jnp.dot(ya_ref[...], wout_ref[0:D_ATTN, :], preferred_element_type=F32)
    y = y + jnp.dot(yc_ref[...], wout_ref[D_ATTN:D_MODEL, :], preferred_element_type=F32)
    z = x_ref[...] + y
    mu = jnp.mean(z, axis=1, keepdims=True)
    d = z - mu
    var = jnp.mean(d * d, axis=1, keepdims=True)
    o_ref[...] = d * jax.lax.rsqrt(var + LN_EPS / ALPHA ** 2) * lng_ref[...] + lnb_ref[...]

    _attn_tile(sinks_ref, q_ref, g0_ref, g1_ref, k_ref, v_ref, kp_ref, vp_ref,
               [bias0_ref] + [bias1_ref] * (ATT_TM // BLOCK - 1), ya_ref)


def _attn_out_call(qkvg, bias, sinks, y_conv, x2d, mod, w_out, ln_g, ln_b):
    tm = ATT_TM
    n = SEQ // tm
    clamp = lambda t: jnp.clip(t, 0, n - 1)
    cur = lambda i: clamp(i)
    lag = lambda i: clamp(i - 1)
    kv_prev = lambda i: jnp.maximum(cur(i) * (tm // BLOCK) - 1, 0)
    row = lambda i: (0, 0)
    return pl.pallas_call(
        _attn_out_kernel,
        grid=(n + 1,),
        in_specs=[
            pl.BlockSpec(memory_space=pltpu.SMEM),
            pl.BlockSpec((tm, D_ATTN), lambda i: (cur(i), COL_Q // D_ATTN)),
            pl.BlockSpec((tm, HALF), lambda i: (cur(i), COL_G_ATTN // HALF)),
            pl.BlockSpec((tm, HALF), lambda i: (cur(i), COL_G_ATTN // HALF + 1)),
            pl.BlockSpec((tm, D_KV), lambda i: (cur(i), COL_K // D_KV)),
            pl.BlockSpec((tm, D_KV), lambda i: (cur(i), COL_V // D_KV)),
            pl.BlockSpec((BLOCK, D_KV), lambda i: (kv_prev(i), COL_K // D_KV)),
            pl.BlockSpec((BLOCK, D_KV), lambda i: (kv_prev(i), COL_V // D_KV)),
            pl.BlockSpec((1, N_Q_HEADS, BLOCK, 2 * BLOCK),
                         lambda i: (jnp.minimum(cur(i), 1), 0, 0, 0)),
            pl.BlockSpec((1, N_Q_HEADS, BLOCK, 2 * BLOCK), lambda i: (1, 0, 0, 0)),
            pl.BlockSpec((tm, D_CONV), lambda i: (lag(i), 0)),
            pl.BlockSpec((tm, D_MODEL), lambda i: (lag(i), 0)),
            pl.BlockSpec((1, D_MODEL), lambda i: (0, 2)),
            pl.BlockSpec(memory_space=pl.ANY),
            pl.BlockSpec((1, D_MODEL), row),
            pl.BlockSpec((1, D_MODEL), row),
        ],
        out_specs=pl.BlockSpec((tm, D_MODEL), lambda i: (lag(i), 0)),
        out_shape=jax.ShapeDtypeStruct((SEQ, D_MODEL), F32),
        scratch_shapes=[
            pltpu.VMEM((tm, D_ATTN), BF16),
            pltpu.VMEM((D_MODEL, D_MODEL), BF16),
            pltpu.VMEM((2, W_STAGE_ROWS, D_MODEL), F32),
            pltpu.SemaphoreType.DMA((2,)),
        ],
        compiler_params=pltpu.CompilerParams(
            dimension_semantics=("arbitrary",), vmem_limit_bytes=VMEM_LIMIT),
        name="attn_out",
    )(sinks, qkvg, qkvg, qkvg, qkvg, qkvg, qkvg, qkvg, bias, bias, y_conv, x2d, mod,
      w_out, ln_g, ln_b)


IN_TM = 256
IN_CHUNK = 256
D_QKVG = COL_GLU_A
CONV_HALO = 32
TAP_ROWS = 32
SYNC_EVERY = 2
LAST_TAP_SYNC_CHUNK = 18


def _tap_piece(u_ref, cw_ref, cb_ref, acc_ref, c, r0):
    off = CONV_HALO - (CONV_WIDTH - 1)
    cs = slice(c * LANES, (c + 1) * LANES)
    acc = jnp.broadcast_to(cb_ref[:, cs], (TAP_ROWS, LANES))
    for k in range(CONV_WIDTH):
        acc = acc + cw_ref[0, k:k + 1, cs] * u_ref[c, r0 + off + k:r0 + off + k + TAP_ROWS, :]
    acc_ref[r0:r0 + TAP_ROWS, cs] = acc
    return _token(acc)


def _conv_tail(acc_ref, lng_ref, lnb_ref, wpw_ref, bpw_ref, gate_ref, yc_ref):
    v = acc_ref[...]
    mu = jnp.mean(v, axis=1, keepdims=True)
    d = v - mu
    var = jnp.mean(d * d, axis=1, keepdims=True)
    y = d * jax.lax.rsqrt(var + LN_EPS) * lng_ref[...] + lnb_ref[...]
    s = _silu(y).astype(BF16)
    z = jnp.dot(s, wpw_ref[...], preferred_element_type=F32) + bpw_ref[...]
    yc_ref[...] = (z * gate_ref[...]).astype(BF16)


def _inproj_conv_kernel(x_ref, shift_ref, scale_ref, w_hbm, cw_ref, cb_ref, lng_ref,
                        lnb_ref, wpw_hbm, bpw_ref, qkvg_ref, yc_ref,
                        h0_ref, h1_ref, u_ref, gc_ref, gcp_ref, acc_ref,
                        w_ref, wpw_ref, stage_ref, sem_ref):
    i = pl.program_id(0)
    tm = IN_TM
    pieces = [(c, r0) for c in range(D_CONV // LANES) for r0 in range(0, tm, TAP_ROWS)]

    @pl.when(i == 0)
    def _():
        u_ref[...] = jnp.zeros(u_ref.shape, u_ref.dtype)
        gc_ref[...] = jnp.zeros(gc_ref.shape, gc_ref.dtype)
        _load_weight_bf16(w_hbm, w_ref, stage_ref, sem_ref)
        _load_weight_bf16(wpw_hbm, wpw_ref, stage_ref, sem_ref)

    gcp_ref[...] = gc_ref[...]
    h = x_ref[...] * (1.0 + scale_ref[...]) + shift_ref[...]
    h0_ref[...] = h.astype(BF16)
    h_bufs = [h0_ref, h1_ref]
    state = {"cur": 0}

    def sync(tokens):
        src, dst = h_bufs[state["cur"]], h_bufs[1 - state["cur"]]
        _copy_after(src, dst, tokens)
        state["cur"] = 1 - state["cur"]

    def project(col):
        return jnp.dot(h_bufs[state["cur"]][...], w_ref[:, col:col + IN_CHUNK],
                       preferred_element_type=F32)

    cols = list(range(0, D_QKVG, IN_CHUNK)) + list(range(COL_G_CONV, D_IN, IN_CHUNK))
    for c in range(D_CONV // IN_CHUNK):
        cols += [COL_GLU_A + c * IN_CHUNK, COL_GLU_B + c * IN_CHUNK]
    sync_chunks = list(range(SYNC_EVERY, LAST_TAP_SYNC_CHUNK + 1, SYNC_EVERY))
    bounds = [round(s * len(pieces) / len(sync_chunks)) for s in range(len(sync_chunks) + 1)]
    glu_a = None
    pieces_done = 0
    for j, col in enumerate(cols):
        if j in sync_chunks:
            s_idx = sync_chunks.index(j)
            group = pieces[bounds[s_idx]:bounds[s_idx + 1]]
            sync([_tap_piece(u_ref, cw_ref, cb_ref, acc_ref, c, r0) for c, r0 in group])
            pieces_done = bounds[s_idx + 1]
            if pieces_done == len(pieces):
                _conv_tail(acc_ref, lng_ref, lnb_ref, wpw_ref, bpw_ref, gcp_ref, yc_ref)
        res = project(col)
        if col < COL_K:
            qkvg_ref[:, col:col + IN_CHUNK] = (res * Q_SCALE).astype(BF16)
        elif col < COL_G_ATTN:
            qkvg_ref[:, col:col + IN_CHUNK] = res.astype(BF16)
        elif col < D_QKVG:
            qkvg_ref[:, col:col + IN_CHUNK] = _silu(res).astype(BF16)
        elif col < COL_GLU_B:
            glu_a = res
        elif col < COL_G_CONV:
            u = glu_a * jax.nn.sigmoid(res)
            t0 = (col - COL_GLU_B) // LANES
            for t in range(t0, t0 + IN_CHUNK // LANES):
                assert pieces_done >= (t + 1) * (tm // TAP_ROWS)
                u_ref[t, 0:CONV_HALO, :] = u_ref[t, tm:tm + CONV_HALO, :]
                u_ref[t, CONV_HALO:CONV_HALO + tm, :] = u[:, (t - t0) * LANES:(t - t0 + 1) * LANES]
        else:
            gc_ref[:, col - COL_G_CONV:col - COL_G_CONV + IN_CHUNK] = _silu(res)


def _inproj_conv_call(x2d, mod, w_in, conv_w, conv_b, ln_g, ln_b, w_pw, b_pw):
    tm = IN_TM
    n = SEQ // tm
    cur = lambda i: jnp.minimum(i, n - 1)
    lag = lambda i: jnp.maximum(i - 1, 0)
    row = lambda i: (0, 0)
    return pl.pallas_call(
        _inproj_conv_kernel,
        grid=(n + 1,),
        in_specs=[
            pl.BlockSpec((tm, D_MODEL), lambda i: (cur(i), 0)),
            pl.BlockSpec((1, D_MODEL), lambda i: (0, 0)),
            pl.BlockSpec((1, D_MODEL), lambda i: (0, 1)),
            pl.BlockSpec(memory_space=pl.ANY),
            pl.BlockSpec((DEPTH, CONV_WIDTH, D_CONV), lambda i: (0, 0, 0)),
            pl.BlockSpec((1, D_CONV), row),
            pl.BlockSpec((1, D_CONV), row),
            pl.BlockSpec((1, D_CONV), row),
            pl.BlockSpec(memory_space=pl.ANY),
            pl.BlockSpec((1, D_CONV), row),
        ],
        out_specs=[
            pl.BlockSpec((tm, D_QKVG), lambda i: (cur(i), 0)),
            pl.BlockSpec((tm, D_CONV), lambda i: (lag(i), 0)),
        ],
        out_shape=[
            jax.ShapeDtypeStruct((SEQ, D_QKVG), BF16),
            jax.ShapeDtypeStruct((SEQ, D_CONV), BF16),
        ],
        scratch_shapes=[
            pltpu.VMEM((tm, D_MODEL), BF16),
            pltpu.VMEM((tm, D_MODEL), BF16),
            pltpu.VMEM((D_CONV // LANES, CONV_HALO + tm, LANES), F32),
            pltpu.VMEM((tm, D_CONV), F32),
            pltpu.VMEM((tm, D_CONV), F32),
            pltpu.VMEM((tm, D_CONV), F32),
            pltpu.VMEM((D_MODEL, D_IN), BF16),
            pltpu.VMEM((D_CONV, D_CONV), BF16),
            pltpu.VMEM((2, W_STAGE_ROWS, D_IN), F32),
            pltpu.SemaphoreType.DMA((2,)),
        ],
        compiler_params=pltpu.CompilerParams(
            dimension_semantics=("arbitrary",), vmem_limit_bytes=VMEM_LIMIT),
        name="inproj_conv",
    )(x2d, mod, mod, w_in, conv_w, conv_b, ln_g, ln_b, w_pw, b_pw)


def kernel(x, c, w_ada, b_ada, w_in, rel_bias, sinks, conv_w, conv_b, conv_ln_g,
           conv_ln_b, w_pw, b_pw, w_out, ln_g, ln_b):
    assert x.shape == (1, SEQ, D_MODEL) and w_in.shape == (DEPTH, D_MODEL, D_IN)
    x2d = x.reshape(SEQ, D_MODEL)

    row = lambda v: v.reshape(1, -1)
    mod = _mod_call(row(c), w_ada.reshape(D_MODEL, 3 * D_MODEL), row(b_ada))
    bias = _bias_call(rel_bias)
    qkvg, y_conv = _inproj_conv_call(
        x2d, mod, w_in.reshape(D_MODEL, D_IN),
        conv_w, row(conv_b), row(conv_ln_g), row(conv_ln_b),
        w_pw.reshape(D_CONV, D_CONV), row(b_pw))
    out = _attn_out_call(qkvg, bias, sinks.reshape(N_Q_HEADS), y_conv, x2d, mod,
                         w_out.reshape(D_MODEL, D_MODEL), row(ln_g), row(ln_b))
    return out.reshape(1, SEQ, D_MODEL)
```

```python
import math

import jax
import jax.numpy as jnp
import numpy as np
from jax.experimental import pallas as pl
from jax.experimental.pallas import tpu as pltpu

F32 = jnp.float32
BF16 = jnp.bfloat16

D_MODEL = 2048
SEQ = 8192
HEAD_DIM = 64
D_ATTN = 1024
D_CONV = 1024
N_Q_HEADS = 16
N_KV_HEADS = 4
GQA_GROUP = 4
D_KV = 256
WINDOW = 128
BLOCK = 128
CONV_WIDTH = 31
N_BUCKETS = 32
MAX_DISTANCE = 128
LN_EPS = 1e-5
DEPTH = 1
ALPHA = (2.0 * DEPTH) ** 0.25
D_IN = 2 * D_ATTN + 2 * D_KV + 3 * D_CONV

LANES = 128
SUBLANES = 8

COL_Q = 0
COL_K = COL_Q + D_ATTN
COL_V = COL_K + D_KV
COL_G_ATTN = COL_V + D_KV
COL_GLU_A = COL_G_ATTN + D_ATTN
COL_GLU_B = COL_GLU_A + D_CONV
COL_G_CONV = COL_GLU_B + D_CONV
HALF = 512

VMEM_LIMIT = 56 * 1024 * 1024


def _silu(v):
    return v * jax.nn.sigmoid(v)


MOD_STAGE_ROWS = 256


def _mod_kernel(c_ref, w_hbm, b_ref, o_ref, ca_ref, stage_ref, sem_ref):
    n_out = o_ref.shape[1]
    n = D_MODEL // MOD_STAGE_ROWS

    def chunk_copy(j):
        slot = j % 2
        return pltpu.make_async_copy(
            w_hbm.at[pl.ds(j * MOD_STAGE_ROWS, MOD_STAGE_ROWS), :],
            stage_ref.at[slot], sem_ref.at[slot])

    chunk_copy(0).start()
    ca = _silu(c_ref[...])
    ca_ref[...] = jnp.transpose(jnp.broadcast_to(ca, (LANES, D_MODEL)))
    acc = jnp.zeros((SUBLANES, n_out), F32)
    for j in range(n):
        if j + 1 < n:
            chunk_copy(j + 1).start()
        chunk_copy(j).wait()
        rows = slice(j * MOD_STAGE_ROWS, (j + 1) * MOD_STAGE_ROWS)
        prod = jnp.tile(ca_ref[rows, :], (1, n_out // LANES)) * stage_ref[j % 2]
        for r in range(0, MOD_STAGE_ROWS, SUBLANES):
            acc = acc + prod[r:r + SUBLANES, :]
    o_ref[...] = jnp.sum(acc, axis=0, keepdims=True) + b_ref[...]


def _mod_call(c_row, w_ada, b_ada):
    n = w_ada.shape[1]
    return pl.pallas_call(
        _mod_kernel,
        in_specs=[
            pl.BlockSpec((1, D_MODEL), lambda: (0, 0)),
            pl.BlockSpec(memory_space=pl.ANY),
            pl.BlockSpec((1, n), lambda: (0, 0)),
        ],
        out_specs=pl.BlockSpec((1, n), lambda: (0, 0)),
        out_shape=jax.ShapeDtypeStruct((1, n), F32),
        scratch_shapes=[
            pltpu.VMEM((D_MODEL, LANES), F32),
            pltpu.VMEM((2, MOD_STAGE_ROWS, n), F32),
            pltpu.SemaphoreType.DMA((2,)),
        ],
        compiler_params=pltpu.CompilerParams(vmem_limit_bytes=VMEM_LIMIT),
        name="mod",
    )(c_row, w_ada, b_ada)


LOG2E = math.log2(math.e)
Q_SCALE = LOG2E * HEAD_DIM ** -0.5
MASKED = -1e30


def _lane_buckets():
    m = np.arange(2 * BLOCK)
    dist = BLOCK - m
    ok = (dist >= 0) & (dist < WINDOW)
    max_exact = N_BUCKETS // 2
    d = np.maximum(dist, 1).astype(np.float64)
    large = max_exact + (np.log(d / max_exact) / math.log(MAX_DISTANCE / max_exact)
                         * (N_BUCKETS - max_exact)).astype(np.int32)
    large = np.minimum(large, N_BUCKETS - 1)
    bucket = np.where(dist < max_exact, dist, large)
    return np.where(ok, bucket, -1).astype(np.int32).reshape(1, -1)


def _bias_kernel(rb_ref, bm_ref, o_ref):
    bm = bm_ref[...]
    col = jax.lax.broadcasted_iota(jnp.int32, (BLOCK, 2 * BLOCK), 1)
    for h in range(N_Q_HEADS):
        t = jnp.full(bm.shape, MASKED, F32)
        for b in range(N_BUCKETS):
            t = jnp.where(bm == b, rb_ref[b, h] * LOG2E, t)
        x = pltpu.roll(jnp.broadcast_to(t, (BLOCK, 2 * BLOCK)), 0, 1, stride=1, stride_axis=0)
        o_ref[1, h] = x
        o_ref[0, h] = jnp.where(col >= BLOCK, x, MASKED)


def _bias_call(rel_bias):
    shape = (2, N_Q_HEADS, BLOCK, 2 * BLOCK)
    return pl.pallas_call(
        _bias_kernel,
        in_specs=[
            pl.BlockSpec(memory_space=pltpu.SMEM),
            pl.BlockSpec((1, 2 * BLOCK), lambda: (0, 0)),
        ],
        out_specs=pl.BlockSpec(shape, lambda: (0, 0, 0, 0)),
        out_shape=jax.ShapeDtypeStruct(shape, F32),
        name="bias",
    )(rel_bias, jnp.asarray(_lane_buckets()))


def _load_weight_bf16(w_hbm, w_ref, stage_ref, sem_ref, col_scale=None):
    rows, cols = w_ref.shape
    chunk_rows = stage_ref.shape[1]
    n = rows // chunk_rows

    def chunk_copy(j):
        slot = j % 2
        return pltpu.make_async_copy(
            w_hbm.at[pl.ds(j * chunk_rows, chunk_rows), :],
            stage_ref.at[slot, :, pl.ds(0, cols)], sem_ref.at[slot])

    chunk_copy(0).start()
    for j in range(n):
        if j + 1 < n:
            chunk_copy(j + 1).start()
        chunk_copy(j).wait()
        chunk = stage_ref[j % 2, :, 0:cols]
        if col_scale is not None:
            chunk = chunk * col_scale
        w_ref[j * chunk_rows:(j + 1) * chunk_rows, :] = chunk.astype(BF16)


def _token(v):
    t = v[0:SUBLANES, 0:LANES]
    for r in range(SUBLANES, v.shape[0], SUBLANES):
        t = t + v[r:r + SUBLANES, 0:LANES]
    return t


def _zero_of(token):
    return (pltpu.bitcast(token, jnp.uint32) >> 16) >> 16


def _copy_after(src_ref, dst_ref, tokens):
    zero = _zero_of(tokens[0])
    for t in tokens[1:]:
        zero = zero | _zero_of(t)
    bits = pltpu.bitcast(src_ref[...], jnp.uint32)
    reps = (bits.shape[0] // zero.shape[0], bits.shape[1] // zero.shape[1])
    dst_ref[...] = pltpu.bitcast(bits | jnp.tile(zero, reps), dst_ref.dtype)


ATT_TM = 4 * BLOCK
ATT_ROWS = BLOCK
OUT_STAGE_ROWS = 512


def _dup_halves(t):
    swapped = jnp.concatenate([t[:, HEAD_DIM:], t[:, :HEAD_DIM]], axis=1)
    lane = jax.lax.broadcasted_iota(jnp.int32, t.shape, 1)
    low_half = lane < HEAD_DIM
    return jnp.where(low_half, t, swapped), jnp.where(low_half, swapped, t)


def _attn_group(q_ref, g0_ref, g1_ref, k_dup, v_dup, bias_ref, sinks_ref, r0, b0, kv, ya_ref):
    nr = ATT_ROWS
    lane = jax.lax.broadcasted_iota(jnp.int32, (nr, LANES), 1)
    low_half = lane < HEAD_DIM
    rows = slice(r0, r0 + nr)
    tiles = [GQA_GROUP * kv // 2 + t for t in range(GQA_GROUP // 2)]
    stacked = []
    for tile in tiles:
        q_tile = q_ref[rows, tile * LANES:(tile + 1) * LANES]
        zero = jnp.zeros_like(q_tile)
        stacked += [jnp.where(low_half, q_tile, zero), jnp.where(low_half, zero, q_tile)]
    s_all = jax.lax.dot_general(jnp.concatenate(stacked, axis=0), k_dup,
                                (((1,), (1,)), ((), ())),
                                preferred_element_type=F32)
    probs, denoms = [], []
    for r in range(GQA_GROUP):
        head = GQA_GROUP * kv + r
        s = s_all[r * nr:(r + 1) * nr] + bias_ref[0, head, b0:b0 + nr, :]
        sink = sinks_ref[head] * LOG2E
        m = jnp.max(s, axis=1, keepdims=True)
        e = jnp.exp2(s - m)
        denoms.append(jnp.sum(e, axis=1, keepdims=True) + jnp.exp2(sink - m))
        probs.append(e.astype(BF16))
    o_all = jnp.dot(jnp.concatenate(probs, axis=0), v_dup,
                    preferred_element_type=F32)
    outs = [o_all[r * nr:(r + 1) * nr] / denoms[r] for r in range(GQA_GROUP)]
    for t, tile in enumerate(tiles):
        y = jnp.where(low_half, outs[2 * t], outs[2 * t + 1])
        g_ref = g0_ref if tile < HALF // LANES else g1_ref
        gl = (tile * LANES) % HALF
        gate = g_ref[rows, gl:gl + LANES].astype(F32)
        ya_ref[rows, tile * LANES:(tile + 1) * LANES] = (y * gate).astype(BF16)


def _attn_tile(sinks_ref, q_ref, g0_ref, g1_ref, k_ref, v_ref, kp_ref, vp_ref,
               bias_refs, ya_ref):
    n_lt = D_KV // LANES
    lt = lambda t: slice(t * LANES, (t + 1) * LANES)

    def block_operands(b):
        r0 = b * BLOCK
        if b == 0:
            k_prev = [kp_ref[:, lt(t)] for t in range(n_lt)]
            v_prev = [vp_ref[:, lt(t)] for t in range(n_lt)]
        else:
            k_prev = [k_ref[r0 - BLOCK:r0, lt(t)] for t in range(n_lt)]
            v_prev = [v_ref[r0 - BLOCK:r0, lt(t)] for t in range(n_lt)]
        k_rows = [jnp.concatenate([k_prev[t], k_ref[r0:r0 + BLOCK, lt(t)]], axis=0)
                  for t in range(n_lt)]
        v_rows = [jnp.concatenate([v_prev[t], v_ref[r0:r0 + BLOCK, lt(t)]], axis=0)
                  for t in range(n_lt)]
        return ([_dup_halves(kt) for kt in k_rows], [_dup_halves(vt) for vt in v_rows])

    for b in range(ATT_TM // BLOCK):
        k_dups, v_dups = block_operands(b)
        for b0 in range(0, BLOCK, ATT_ROWS):
            for kv in range(N_KV_HEADS):
                _attn_group(q_ref, g0_ref, g1_ref, k_dups[kv // 2][kv % 2],
                            v_dups[kv // 2][kv % 2], bias_refs[b], sinks_ref,
                            b * BLOCK + b0, b0, kv, ya_ref)


def _attn_out_kernel(sinks_ref, q_ref, g0_ref, g1_ref, k_ref, v_ref, kp_ref, vp_ref,
                     bias0_ref, bias1_ref, yc_ref, x_ref, gate_ref, wout_hbm,
                     lng_ref, lnb_ref, o_ref, ya_ref, wout_ref, stage_ref, sem_ref):
    i = pl.program_id(0)

    @pl.when(i == 0)
    def _():
        ya_ref[...] = jnp.zeros(ya_ref.shape, ya_ref.dtype)
        _load_weight_bf16(wout_hbm, wout_ref, stage_ref, sem_ref,
                          col_scale=gate_ref[...] * (1.0 / ALPHA))

    y = jnp.dot(ya_ref[...], wout_ref[0:D_ATTN, :], preferred_element_type=F32)
    y = y + jnp.dot(yc_ref[...], wout_ref[D_ATTN:D_MODEL, :], preferred_element_type=F32)
    z = x_ref[...] + y
    mu = jnp.mean(z, axis=1, keepdims=True)
    d = z - mu
    var = jnp.mean(d * d, axis=1, keepdims=True)
    o_ref[...] = d * jax.lax.rsqrt(var + LN_EPS / ALPHA ** 2) * lng_ref[...] + lnb_ref[...]

    _attn_tile(sinks_ref, q_ref, g0_ref, g1_ref, k_ref, v_ref, kp_ref, vp_ref,
               [bias0_ref] + [bias1_ref] * (ATT_TM // BLOCK - 1), ya_ref)


def _attn_out_call(qkvg, bias, sinks, y_conv, x2d, mod, w_out, ln_g, ln_b):
    tm = ATT_TM
    n = SEQ // tm
    clamp = lambda t: jnp.clip(t, 0, n - 1)
    cur = lambda i: clamp(i)
    lag = lambda i: clamp(i - 1)
    kv_prev = lambda i: jnp.maximum(cur(i) * (tm // BLOCK) - 1, 0)
    row = lambda i: (0, 0)
    return pl.pallas_call(
        _attn_out_kernel,
        grid=(n + 1,),
        in_specs=[
            pl.BlockSpec(memory_space=pltpu.SMEM),
            pl.BlockSpec((tm, D_ATTN), lambda i: (cur(i), COL_Q // D_ATTN)),
            pl.BlockSpec((tm, HALF), lambda i: (cur(i), COL_G_ATTN // HALF)),
            pl.BlockSpec((tm, HALF), lambda i: (cur(i), COL_G_ATTN // HALF + 1)),
            pl.BlockSpec((tm, D_KV), lambda i: (cur(i), COL_K // D_KV)),
            pl.BlockSpec((tm, D_KV), lambda i: (cur(i), COL_V // D_KV)),
            pl.BlockSpec((BLOCK, D_KV), lambda i: (kv_prev(i), COL_K // D_KV)),
            pl.BlockSpec((BLOCK, D_KV), lambda i: (kv_prev(i), COL_V // D_KV)),
            pl.BlockSpec((1, N_Q_HEADS, BLOCK, 2 * BLOCK),
                         lambda i: (jnp.minimum(cur(i), 1), 0, 0, 0)),
            pl.BlockSpec((1, N_Q_HEADS, BLOCK, 2 * BLOCK), lambda i: (1, 0, 0, 0)),
            pl.BlockSpec((tm, D_CONV), lambda i: (lag(i), 0)),
            pl.BlockSpec((tm, D_MODEL), lambda i: (lag(i), 0)),
            pl.BlockSpec((1, D_MODEL), lambda i: (0, 2)),
            pl.BlockSpec(memory_space=pl.ANY),
            pl.BlockSpec((1, D_MODEL), row),
            pl.BlockSpec((1, D_MODEL), row),
        ],
        out_specs=pl.BlockSpec((tm, D_MODEL), lambda i: (lag(i), 0)),
        out_shape=jax.ShapeDtypeStruct((SEQ, D_MODEL), F32),
        scratch_shapes=[
            pltpu.VMEM((tm, D_ATTN), BF16),
            pltpu.VMEM((D_MODEL, D_MODEL), BF16),
            pltpu.VMEM((2, OUT_STAGE_ROWS, D_MODEL), F32),
            pltpu.SemaphoreType.DMA((2,)),
        ],
        compiler_params=pltpu.CompilerParams(
            dimension_semantics=("arbitrary",), vmem_limit_bytes=VMEM_LIMIT),
        name="attn_out",
    )(sinks, qkvg, qkvg, qkvg, qkvg, qkvg, qkvg, qkvg, bias, bias, y_conv, x2d, mod,
      w_out, ln_g, ln_b)


IN_TM = 256
IN_CHUNK = 256
D_QKVG = COL_GLU_A
CONV_HALO = 32
IN_STAGE_ROWS = 256
TAP_ROWS = 32
SYNC_EVERY = 2
LAST_TAP_SYNC_CHUNK = 18


def _tap_piece(u_ref, cw_ref, cb_ref, acc_ref, c, r0):
    off = CONV_HALO - (CONV_WIDTH - 1)
    cs = slice(c * LANES, (c + 1) * LANES)
    acc = jnp.broadcast_to(cb_ref[:, cs], (TAP_ROWS, LANES))
    for k in range(CONV_WIDTH):
        acc = acc + cw_ref[0, k:k + 1, cs] * u_ref[c, r0 + off + k:r0 + off + k + TAP_ROWS, :]
    acc_ref[r0:r0 + TAP_ROWS, cs] = acc
    return _token(acc)


def _conv_tail(acc_ref, lng_ref, lnb_ref, wpw_ref, bpw_ref, gate_ref, yc_ref):
    v = acc_ref[...]
    mu = jnp.mean(v, axis=1, keepdims=True)
    d = v - mu
    var = jnp.mean(d * d, axis=1, keepdims=True)
    y = d * jax.lax.rsqrt(var + LN_EPS) * lng_ref[...] + lnb_ref[...]
    s = _silu(y).astype(BF16)
    z = jnp.dot(s, wpw_ref[...], preferred_element_type=F32) + bpw_ref[...]
    yc_ref[...] = (z * gate_ref[...]).astype(BF16)


def _inproj_conv_kernel(x_ref, shift_ref, scale_ref, w_hbm, cw_ref, cb_ref, lng_ref,
                        lnb_ref, wpw_hbm, bpw_ref, qkvg_ref, yc_ref,
                        h0_ref, h1_ref, u_ref, gc_ref, gcp_ref, acc_ref,
                        w_ref, wpw_ref, stage_ref, sem_ref):
    i = pl.program_id(0)
    tm = IN_TM
    pieces = [(c, r0) for c in range(D_CONV // LANES) for r0 in range(0, tm, TAP_ROWS)]

    @pl.when(i == 0)
    def _():
        u_ref[...] = jnp.zeros(u_ref.shape, u_ref.dtype)
        gc_ref[...] = jnp.zeros(gc_ref.shape, gc_ref.dtype)
        _load_weight_bf16(w_hbm, w_ref, stage_ref, sem_ref)
        _load_weight_bf16(wpw_hbm, wpw_ref, stage_ref, sem_ref)

    gcp_ref[...] = gc_ref[...]
    h = x_ref[...] * (1.0 + scale_ref[...]) + shift_ref[...]
    h0_ref[...] = h.astype(BF16)
    h_bufs = [h0_ref, h1_ref]
    state = {"cur": 0}

    def sync(tokens):
        src, dst = h_bufs[state["cur"]], h_bufs[1 - state["cur"]]
        _copy_after(src, dst, tokens)
        state["cur"] = 1 - state["cur"]

    def project(col):
        return jnp.dot(h_bufs[state["cur"]][...], w_ref[:, col:col + IN_CHUNK],
                       preferred_element_type=F32)

    cols = list(range(0, D_QKVG, IN_CHUNK)) + list(range(COL_G_CONV, D_IN, IN_CHUNK))
    for c in range(D_CONV // IN_CHUNK):
        cols += [COL_GLU_A + c * IN_CHUNK, COL_GLU_B + c * IN_CHUNK]
    sync_chunks = list(range(SYNC_EVERY, LAST_TAP_SYNC_CHUNK + 1, SYNC_EVERY))
    bounds = [round(s * len(pieces) / len(sync_chunks)) for s in range(len(sync_chunks) + 1)]
    glu_a = None
    pieces_done = 0
    for j, col in enumerate(cols):
        if j in sync_chunks:
            s_idx = sync_chunks.index(j)
            group = pieces[bounds[s_idx]:bounds[s_idx + 1]]
            sync([_tap_piece(u_ref, cw_ref, cb_ref, acc_ref, c, r0) for c, r0 in group])
            pieces_done = bounds[s_idx + 1]
            if pieces_done == len(pieces):
                _conv_tail(acc_ref, lng_ref, lnb_ref, wpw_ref, bpw_ref, gcp_ref, yc_ref)
        res = project(col)
        if col < COL_K:
            qkvg_ref[:, col:col + IN_CHUNK] = (res * Q_SCALE).astype(BF16)
        elif col < COL_G_ATTN:
            qkvg_ref[:, col:col + IN_CHUNK] = res.astype(BF16)
        elif col < D_QKVG:
            qkvg_ref[:, col:col + IN_CHUNK] = _silu(res).astype(BF16)
        elif col < COL_GLU_B:
            glu_a = res
        elif col < COL_G_CONV:
            u = glu_a * jax.nn.sigmoid(res)
            t0 = (col - COL_GLU_B) // LANES
            for t in range(t0, t0 + IN_CHUNK // LANES):
                assert pieces_done >= (t + 1) * (tm // TAP_ROWS)
                u_ref[t, 0:CONV_HALO, :] = u_ref[t, tm:tm + CONV_HALO, :]
                u_ref[t, CONV_HALO:CONV_HALO + tm, :] = u[:, (t - t0) * LANES:(t - t0 + 1) * LANES]
        else:
            gc_ref[:, col - COL_G_CONV:col - COL_G_CONV + IN_CHUNK] = _silu(res)


def _inproj_conv_call(x2d, mod, w_in, conv_w, conv_b, ln_g, ln_b, w_pw, b_pw):
    tm = IN_TM
    n = SEQ // tm
    cur = lambda i: jnp.minimum(i, n - 1)
    lag = lambda i: jnp.maximum(i - 1, 0)
    row = lambda i: (0, 0)
    return pl.pallas_call(
        _inproj_conv_kernel,
        grid=(n + 1,),
        in_specs=[
            pl.BlockSpec((tm, D_MODEL), lambda i: (cur(i), 0)),
            pl.BlockSpec((1, D_MODEL), lambda i: (0, 0)),
            pl.BlockSpec((1, D_MODEL), lambda i: (0, 1)),
            pl.BlockSpec(memory_space=pl.ANY),
            pl.BlockSpec((DEPTH, CONV_WIDTH, D_CONV), lambda i: (0, 0, 0)),
            pl.BlockSpec((1, D_CONV), row),
            pl.BlockSpec((1, D_CONV), row),
            pl.BlockSpec((1, D_CONV), row),
            pl.BlockSpec(memory_space=pl.ANY),
            pl.BlockSpec((1, D_CONV), row),
        ],
        out_specs=[
            pl.BlockSpec((tm, D_QKVG), lambda i: (cur(i), 0)),
            pl.BlockSpec((tm, D_CONV), lambda i: (lag(i), 0)),
        ],
        out_shape=[
            jax.ShapeDtypeStruct((SEQ, D_QKVG), BF16),
            jax.ShapeDtypeStruct((SEQ, D_CONV), BF16),
        ],
        scratch_shapes=[
            pltpu.VMEM((tm, D_MODEL), BF16),
            pltpu.VMEM((tm, D_MODEL), BF16),
            pltpu.VMEM((D_CONV // LANES, CONV_HALO + tm, LANES), F32),
            pltpu.VMEM((tm, D_CONV), F32),
            pltpu.VMEM((tm, D_CONV), F32),
            pltpu.VMEM((tm, D_CONV), F32),
            pltpu.VMEM((D_MODEL, D_IN), BF16),
            pltpu.VMEM((D_CONV, D_CONV), BF16),
            pltpu.VMEM((2, IN_STAGE_ROWS, D_IN), F32),
            pltpu.SemaphoreType.DMA((2,)),
        ],
        compiler_params=pltpu.CompilerParams(
            dimension_semantics=("arbitrary",), vmem_limit_bytes=VMEM_LIMIT),
        name="inproj_conv",
    )(x2d, mod, mod, w_in, conv_w, conv_b, ln_g, ln_b, w_pw, b_pw)


def kernel(x, c, w_ada, b_ada, w_in, rel_bias, sinks, conv_w, conv_b, conv_ln_g,
           conv_ln_b, w_pw, b_pw, w_out, ln_g, ln_b):
    assert x.shape == (1, SEQ, D_MODEL) and w_in.shape == (DEPTH, D_MODEL, D_IN)
    x2d = x.reshape(SEQ, D_MODEL)

    row = lambda v: v.reshape(1, -1)
    mod = _mod_call(row(c), w_ada.reshape(D_MODEL, 3 * D_MODEL), row(b_ada))
    bias = _bias_call(rel_bias)
    qkvg, y_conv = _inproj_conv_call(
        x2d, mod, w_in.reshape(D_MODEL, D_IN),
        conv_w, row(conv_b), row(conv_ln_g), row(conv_ln_b),
        w_pw.reshape(D_CONV, D_CONV), row(b_pw))
    out = _attn_out_call(qkvg, bias, sinks.reshape(N_Q_HEADS), y_conv, x2d, mod,
                         w_out.reshape(D_MODEL, D_MODEL), row(ln_g), row(ln_b))
    return out.reshape(1, SEQ, D_MODEL)
```

```python
import math

import jax
import jax.numpy as jnp
import numpy as np
from jax.experimental import pallas as pl
from jax.experimental.pallas import tpu as pltpu

F32 = jnp.float32
BF16 = jnp.bfloat16

D_MODEL = 2048
SEQ = 8192
HEAD_DIM = 64
D_ATTN = 1024
D_CONV = 1024
N_Q_HEADS = 16
N_KV_HEADS = 4
GQA_GROUP = 4
D_KV = 256
WINDOW = 128
BLOCK = 128
CONV_WIDTH = 31
N_BUCKETS = 32
MAX_DISTANCE = 128
LN_EPS = 1e-5
DEPTH = 1
ALPHA = (2.0 * DEPTH) ** 0.25
D_IN = 2 * D_ATTN + 2 * D_KV + 3 * D_CONV

LANES = 128
SUBLANES = 8

COL_Q = 0
COL_K = COL_Q + D_ATTN
COL_V = COL_K + D_KV
COL_G_ATTN = COL_V + D_KV
COL_GLU_A = COL_G_ATTN + D_ATTN
COL_GLU_B = COL_GLU_A + D_CONV
COL_G_CONV = COL_GLU_B + D_CONV
HALF = 512

VMEM_LIMIT = 56 * 1024 * 1024


def _silu(v):
    return v * jax.nn.sigmoid(v)


MOD_STAGE_ROWS = 256


def _mod_kernel(c_ref, w_hbm, b_ref, o_ref, ca_ref, stage_ref, sem_ref):
    n_out = o_ref.shape[1]
    n = D_MODEL // MOD_STAGE_ROWS

    def chunk_copy(j):
        slot = j % 2
        return pltpu.make_async_copy(
            w_hbm.at[pl.ds(j * MOD_STAGE_ROWS, MOD_STAGE_ROWS), :],
            stage_ref.at[slot], sem_ref.at[slot])

    chunk_copy(0).start()
    ca = _silu(c_ref[...])
    ca_ref[...] = jnp.transpose(jnp.broadcast_to(ca, (LANES, D_MODEL)))
    acc = jnp.zeros((SUBLANES, n_out), F32)
    for j in range(n):
        if j + 1 < n:
            chunk_copy(j + 1).start(priority=(j + 1) % 2)
        chunk_copy(j).wait()
        rows = slice(j * MOD_STAGE_ROWS, (j + 1) * MOD_STAGE_ROWS)
        prod = jnp.tile(ca_ref[rows, :], (1, n_out // LANES)) * stage_ref[j % 2]
        for r in range(0, MOD_STAGE_ROWS, SUBLANES):
            acc = acc + prod[r:r + SUBLANES, :]
    o_ref[...] = jnp.sum(acc, axis=0, keepdims=True) + b_ref[...]


def _mod_call(c_row, w_ada, b_ada):
    n = w_ada.shape[1]
    return pl.pallas_call(
        _mod_kernel,
        in_specs=[
            pl.BlockSpec((1, D_MODEL), lambda: (0, 0)),
            pl.BlockSpec(memory_space=pl.ANY),
            pl.BlockSpec((1, n), lambda: (0, 0)),
        ],
        out_specs=pl.BlockSpec((1, n), lambda: (0, 0)),
        out_shape=jax.ShapeDtypeStruct((1, n), F32),
        scratch_shapes=[
            pltpu.VMEM((D_MODEL, LANES), F32),
            pltpu.VMEM((2, MOD_STAGE_ROWS, n), F32),
            pltpu.SemaphoreType.DMA((2,)),
        ],
        compiler_params=pltpu.CompilerParams(vmem_limit_bytes=VMEM_LIMIT),
        name="mod",
    )(c_row, w_ada, b_ada)


LOG2E = math.log2(math.e)
Q_SCALE = LOG2E * HEAD_DIM ** -0.5
MASKED = -1e30


def _lane_buckets():
    m = np.arange(2 * BLOCK)
    dist = BLOCK - m
    ok = (dist >= 0) & (dist < WINDOW)
    max_exact = N_BUCKETS // 2
    d = np.maximum(dist, 1).astype(np.float64)
    large = max_exact + (np.log(d / max_exact) / math.log(MAX_DISTANCE / max_exact)
                         * (N_BUCKETS - max_exact)).astype(np.int32)
    large = np.minimum(large, N_BUCKETS - 1)
    bucket = np.where(dist < max_exact, dist, large)
    return np.where(ok, bucket, -1).astype(np.int32).reshape(1, -1)


def _bias_kernel(rb_ref, bm_ref, o_ref):
    bm = bm_ref[...]
    col = jax.lax.broadcasted_iota(jnp.int32, (BLOCK, 2 * BLOCK), 1)
    for h in range(N_Q_HEADS):
        t = jnp.full(bm.shape, MASKED, F32)
        for b in range(N_BUCKETS):
            t = jnp.where(bm == b, rb_ref[b, h] * LOG2E, t)
        x = pltpu.roll(jnp.broadcast_to(t, (BLOCK, 2 * BLOCK)), 0, 1, stride=1, stride_axis=0)
        o_ref[1, h] = x
        o_ref[0, h] = jnp.where(col >= BLOCK, x, MASKED)


def _bias_call(rel_bias):
    shape = (2, N_Q_HEADS, BLOCK, 2 * BLOCK)
    return pl.pallas_call(
        _bias_kernel,
        in_specs=[
            pl.BlockSpec(memory_space=pltpu.SMEM),
            pl.BlockSpec((1, 2 * BLOCK), lambda: (0, 0)),
        ],
        out_specs=pl.BlockSpec(shape, lambda: (0, 0, 0, 0)),
        out_shape=jax.ShapeDtypeStruct(shape, F32),
        name="bias",
    )(rel_bias, jnp.asarray(_lane_buckets()))


def _load_weight_bf16(w_hbm, w_ref, stage_ref, sem_ref, col_scale=None):
    rows, cols = w_ref.shape
    chunk_rows = stage_ref.shape[1]
    n = rows // chunk_rows

    def chunk_copy(j):
        slot = j % 2
        return pltpu.make_async_copy(
            w_hbm.at[pl.ds(j * chunk_rows, chunk_rows), :],
            stage_ref.at[slot, :, pl.ds(0, cols)], sem_ref.at[slot])

    chunk_copy(0).start()
    for j in range(n):
        if j + 1 < n:
            chunk_copy(j + 1).start(priority=(j + 1) % 2)
        chunk_copy(j).wait()
        chunk = stage_ref[j % 2, :, 0:cols]
        if col_scale is not None:
            chunk = chunk * col_scale
        w_ref[j * chunk_rows:(j + 1) * chunk_rows, :] = chunk.astype(BF16)


def _token(v):
    t = v[0:SUBLANES, 0:LANES]
    for r in range(SUBLANES, v.shape[0], SUBLANES):
        t = t + v[r:r + SUBLANES, 0:LANES]
    return t


def _zero_of(token):
    return (pltpu.bitcast(token, jnp.uint32) >> 16) >> 16


def _copy_after(src_ref, dst_ref, tokens):
    zero = _zero_of(tokens[0])
    for t in tokens[1:]:
        zero = zero | _zero_of(t)
    bits = pltpu.bitcast(src_ref[...], jnp.uint32)
    reps = (bits.shape[0] // zero.shape[0], bits.shape[1] // zero.shape[1])
    dst_ref[...] = pltpu.bitcast(bits | jnp.tile(zero, reps), dst_ref.dtype)


ATT_TM = 4 * BLOCK
ATT_ROWS = BLOCK
OUT_STAGE_ROWS = 512


def _dup_halves(t):
    swapped = jnp.concatenate([t[:, HEAD_DIM:], t[:, :HEAD_DIM]], axis=1)
    lane = jax.lax.broadcasted_iota(jnp.int32, t.shape, 1)
    low_half = lane < HEAD_DIM
    return jnp.where(low_half, t, swapped), jnp.where(low_half, swapped, t)


def _attn_group(q_ref, g0_ref, g1_ref, k_dup, v_dup, bias_ref, sinks_ref, r0, b0, kv, ya_ref):
    nr = ATT_ROWS
    lane = jax.lax.broadcasted_iota(jnp.int32, (nr, LANES), 1)
    low_half = lane < HEAD_DIM
    rows = slice(r0, r0 + nr)
    tiles = [GQA_GROUP * kv // 2 + t for t in range(GQA_GROUP // 2)]
    stacked = []
    for tile in tiles:
        q_tile = q_ref[rows, tile * LANES:(tile + 1) * LANES]
        zero = jnp.zeros_like(q_tile)
        stacked += [jnp.where(low_half, q_tile, zero), jnp.where(low_half, zero, q_tile)]
    s_all = jax.lax.dot_general(jnp.concatenate(stacked, axis=0), k_dup,
                                (((1,), (1,)), ((), ())),
                                preferred_element_type=F32)
    probs, denoms = [], []
    for r in range(GQA_GROUP):
        head = GQA_GROUP * kv + r
        s = s_all[r * nr:(r + 1) * nr] + bias_ref[0, head, b0:b0 + nr, :]
        sink = sinks_ref[head] * LOG2E
        m = jnp.max(s, axis=1, keepdims=True)
        e = jnp.exp2(s - m)
        denoms.append(jnp.sum(e, axis=1, keepdims=True) + jnp.exp2(sink - m))
        probs.append(e.astype(BF16))
    o_all = jnp.dot(jnp.concatenate(probs, axis=0), v_dup,
                    preferred_element_type=F32)
    outs = [o_all[r * nr:(r + 1) * nr] / denoms[r] for r in range(GQA_GROUP)]
    for t, tile in enumerate(tiles):
        y = jnp.where(low_half, outs[2 * t], outs[2 * t + 1])
        g_ref = g0_ref if tile < HALF // LANES else g1_ref
        gl = (tile * LANES) % HALF
        gate = g_ref[rows, gl:gl + LANES].astype(F32)
        ya_ref[rows, tile * LANES:(tile + 1) * LANES] = (y * gate).astype(BF16)


def _attn_tile(sinks_ref, q_ref, g0_ref, g1_ref, k_ref, v_ref, kp_ref, vp_ref,
               bias_refs, ya_ref):
    n_lt = D_KV // LANES
    lt = lambda t: slice(t * LANES, (t + 1) * LANES)

    def block_operands(b):
        r0 = b * BLOCK
        if b == 0:
            k_prev = [kp_ref[:, lt(t)] for t in range(n_lt)]
            v_prev = [vp_ref[:, lt(t)] for t in range(n_lt)]
        else:
            k_prev = [k_ref[r0 - BLOCK:r0, lt(t)] for t in range(n_lt)]
            v_prev = [v_ref[r0 - BLOCK:r0, lt(t)] for t in range(n_lt)]
        k_rows = [jnp.concatenate([k_prev[t], k_ref[r0:r0 + BLOCK, lt(t)]], axis=0)
                  for t in range(n_lt)]
        v_rows = [jnp.concatenate([v_prev[t], v_ref[r0:r0 + BLOCK, lt(t)]], axis=0)
                  for t in range(n_lt)]
        return ([_dup_halves(kt) for kt in k_rows], [_dup_halves(vt) for vt in v_rows])

    for b in range(ATT_TM // BLOCK):
        k_dups, v_dups = block_operands(b)
        for b0 in range(0, BLOCK, ATT_ROWS):
            for kv in range(N_KV_HEADS):
                _attn_group(q_ref, g0_ref, g1_ref, k_dups[kv // 2][kv % 2],
                            v_dups[kv // 2][kv % 2], bias_refs[b], sinks_ref,
                            b * BLOCK + b0, b0, kv, ya_ref)


def _attn_out_kernel(sinks_ref, q_ref, g0_ref, g1_ref, k_ref, v_ref, kp_ref, vp_ref,
                     bias0_ref, bias1_ref, yc_ref, x_ref, gate_ref, wout_hbm,
                     lng_ref, lnb_ref, o_ref, ya_ref, wout_ref, stage_ref, sem_ref):
    i = pl.program_id(0)

    @pl.when(i == 0)
    def _():
        ya_ref[...] = jnp.zeros(ya_ref.shape, ya_ref.dtype)
        _load_weight_bf16(wout_hbm, wout_ref, stage_ref, sem_ref,
                          col_scale=gate_ref[...] * (1.0 / ALPHA))

    y = jnp.dot(ya_ref[...], wout_ref[0:D_ATTN, :], preferred_element_type=F32)
    y = y + jnp.dot(yc_ref[...], wout_ref[D_ATTN:D_MODEL, :], preferred_element_type=F32)
    z = x_ref[...] + y
    mu = jnp.mean(z, axis=1, keepdims=True)
    d = z - mu
    var = jnp.mean(d * d, axis=1, keepdims=True)
    o_ref[...] = d * jax.lax.rsqrt(var + LN_EPS / ALPHA ** 2) * lng_ref[...] + lnb_ref[...]

    _attn_tile(sinks_ref, q_ref, g0_ref, g1_ref, k_ref, v_ref, kp_ref, vp_ref,
               [bias0_ref] + [bias1_ref] * (ATT_TM // BLOCK - 1), ya_ref)


def _attn_out_call(qkvg, bias, sinks, y_conv, x2d, mod, w_out, ln_g, ln_b):
    tm = ATT_TM
    n = SEQ // tm
    clamp = lambda t: jnp.clip(t, 0, n - 1)
    cur = lambda i: clamp(i)
    lag = lambda i: clamp(i - 1)
    kv_prev = lambda i: jnp.maximum(cur(i) * (tm // BLOCK) - 1, 0)
    row = lambda i: (0, 0)
    return pl.pallas_call(
        _attn_out_kernel,
        grid=(n + 1,),
        in_specs=[
            pl.BlockSpec(memory_space=pltpu.SMEM),
            pl.BlockSpec((tm, D_ATTN), lambda i: (cur(i), COL_Q // D_ATTN)),
            pl.BlockSpec((tm, HALF), lambda i: (cur(i), COL_G_ATTN // HALF)),
            pl.BlockSpec((tm, HALF), lambda i: (cur(i), COL_G_ATTN // HALF + 1)),
            pl.BlockSpec((tm, D_KV), lambda i: (cur(i), COL_K // D_KV)),
            pl.BlockSpec((tm, D_KV), lambda i: (cur(i), COL_V // D_KV)),
            pl.BlockSpec((BLOCK, D_KV), lambda i: (kv_prev(i), COL_K // D_KV)),
            pl.BlockSpec((BLOCK, D_KV), lambda i: (kv_prev(i), COL_V // D_KV)),
            pl.BlockSpec((1, N_Q_HEADS, BLOCK, 2 * BLOCK),
                         lambda i: (jnp.minimum(cur(i), 1), 0, 0, 0)),
            pl.BlockSpec((1, N_Q_HEADS, BLOCK, 2 * BLOCK), lambda i: (1, 0, 0, 0)),
            pl.BlockSpec((tm, D_CONV), lambda i: (lag(i), 0)),
            pl.BlockSpec((tm, D_MODEL), lambda i: (lag(i), 0)),
            pl.BlockSpec((1, D_MODEL), lambda i: (0, 2)),
            pl.BlockSpec(memory_space=pl.ANY),
            pl.BlockSpec((1, D_MODEL), row),
            pl.BlockSpec((1, D_MODEL), row),
        ],
        out_specs=pl.BlockSpec((tm, D_MODEL), lambda i: (lag(i), 0)),
        out_shape=jax.ShapeDtypeStruct((SEQ, D_MODEL), F32),
        scratch_shapes=[
            pltpu.VMEM((tm, D_ATTN), BF16),
            pltpu.VMEM((D_MODEL, D_MODEL), BF16),
            pltpu.VMEM((2, OUT_STAGE_ROWS, D_MODEL), F32),
            pltpu.SemaphoreType.DMA((2,)),
        ],
        compiler_params=pltpu.CompilerParams(
            dimension_semantics=("arbitrary",), vmem_limit_bytes=VMEM_LIMIT),
        name="attn_out",
    )(sinks, qkvg, qkvg, qkvg, qkvg, qkvg, qkvg, qkvg, bias, bias, y_conv, x2d, mod,
      w_out, ln_g, ln_b)


IN_TM = 256
IN_CHUNK = 256
D_QKVG = COL_GLU_A
CONV_HALO = 32
IN_STAGE_ROWS = 256
TAP_ROWS = 32
SYNC_EVERY = 2
LAST_TAP_SYNC_CHUNK = 18


def _tap_piece(u_ref, cw_ref, cb_ref, acc_ref, c, r0):
    off = CONV_HALO - (CONV_WIDTH - 1)
    cs = slice(c * LANES, (c + 1) * LANES)
    acc = jnp.broadcast_to(cb_ref[:, cs], (TAP_ROWS, LANES))
    for k in range(CONV_WIDTH):
        acc = acc + cw_ref[0, k:k + 1, cs] * u_ref[c, r0 + off + k:r0 + off + k + TAP_ROWS, :]
    acc_ref[r0:r0 + TAP_ROWS, cs] = acc
    return _token(acc)


def _conv_tail(acc_ref, lng_ref, lnb_ref, wpw_ref, bpw_ref, gate_ref, yc_ref):
    v = acc_ref[...]
    mu = jnp.mean(v, axis=1, keepdims=True)
    d = v - mu
    var = jnp.mean(d * d, axis=1, keepdims=True)
    y = d * jax.lax.rsqrt(var + LN_EPS) * lng_ref[...] + lnb_ref[...]
    s = _silu(y).astype(BF16)
    z = jnp.dot(s, wpw_ref[...], preferred_element_type=F32) + bpw_ref[...]
    yc_ref[...] = (z * gate_ref[...]).astype(BF16)


def _inproj_conv_kernel(x_ref, shift_ref, scale_ref, w_hbm, cw_ref, cb_ref, lng_ref,
                        lnb_ref, wpw_hbm, bpw_ref, qkvg_ref, yc_ref,
                        h0_ref, h1_ref, u_ref, gc_ref, gcp_ref, acc_ref,
                        w_ref, wpw_ref, stage_ref, sem_ref):
    i = pl.program_id(0)
    tm = IN_TM
    pieces = [(c, r0) for c in range(D_CONV // LANES) for r0 in range(0, tm, TAP_ROWS)]

    @pl.when(i == 0)
    def _():
        u_ref[...] = jnp.zeros(u_ref.shape, u_ref.dtype)
        gc_ref[...] = jnp.zeros(gc_ref.shape, gc_ref.dtype)
        _load_weight_bf16(w_hbm, w_ref, stage_ref, sem_ref)
        _load_weight_bf16(wpw_hbm, wpw_ref, stage_ref, sem_ref)

    gcp_ref[...] = gc_ref[...]
    h = x_ref[...] * (1.0 + scale_ref[...]) + shift_ref[...]
    h0_ref[...] = h.astype(BF16)
    h_bufs = [h0_ref, h1_ref]
    state = {"cur": 0}

    def sync(tokens):
        src, dst = h_bufs[state["cur"]], h_bufs[1 - state["cur"]]
        _copy_after(src, dst, tokens)
        state["cur"] = 1 - state["cur"]

    def project(col):
        return jnp.dot(h_bufs[state["cur"]][...], w_ref[:, col:col + IN_CHUNK],
                       preferred_element_type=F32)

    cols = list(range(0, D_QKVG, IN_CHUNK)) + list(range(COL_G_CONV, D_IN, IN_CHUNK))
    for c in range(D_CONV // IN_CHUNK):
        cols += [COL_GLU_A + c * IN_CHUNK, COL_GLU_B + c * IN_CHUNK]
    sync_chunks = list(range(SYNC_EVERY, LAST_TAP_SYNC_CHUNK + 1, SYNC_EVERY))
    bounds = [round(s * len(pieces) / len(sync_chunks)) for s in range(len(sync_chunks) + 1)]
    glu_a = None
    pieces_done = 0
    for j, col in enumerate(cols):
        if j in sync_chunks:
            s_idx = sync_chunks.index(j)
            group = pieces[bounds[s_idx]:bounds[s_idx + 1]]
            sync([_tap_piece(u_ref, cw_ref, cb_ref, acc_ref, c, r0) for c, r0 in group])
            pieces_done = bounds[s_idx + 1]
            if pieces_done == len(pieces):
                _conv_tail(acc_ref, lng_ref, lnb_ref, wpw_ref, bpw_ref, gcp_ref, yc_ref)
        res = project(col)
        if col < COL_K:
            qkvg_ref[:, col:col + IN_CHUNK] = (res * Q_SCALE).astype(BF16)
        elif col < COL_G_ATTN:
            qkvg_ref[:, col:col + IN_CHUNK] = res.astype(BF16)
        elif col < D_QKVG:
            qkvg_ref[:, col:col + IN_CHUNK] = _silu(res).astype(BF16)
        elif col < COL_GLU_B:
            glu_a = res
        elif col < COL_G_CONV:
            u = glu_a * jax.nn.sigmoid(res)
            t0 = (col - COL_GLU_B) // LANES
            for t in range(t0, t0 + IN_CHUNK // LANES):
                assert pieces_done >= (t + 1) * (tm // TAP_ROWS)
                u_ref[t, 0:CONV_HALO, :] = u_ref[t, tm:tm + CONV_HALO, :]
                u_ref[t, CONV_HALO:CONV_HALO + tm, :] = u[:, (t - t0) * LANES:(t - t0 + 1) * LANES]
        else:
            gc_ref[:, col - COL_G_CONV:col - COL_G_CONV + IN_CHUNK] = _silu(res)


def _inproj_conv_call(x2d, mod, w_in, conv_w, conv_b, ln_g, ln_b, w_pw, b_pw):
    tm = IN_TM
    n = SEQ // tm
    cur = lambda i: jnp.minimum(i, n - 1)
    lag = lambda i: jnp.maximum(i - 1, 0)
    row = lambda i: (0, 0)
    return pl.pallas_call(
        _inproj_conv_kernel,
        grid=(n + 1,),
        in_specs=[
            pl.BlockSpec((tm, D_MODEL), lambda i: (cur(i), 0)),
            pl.BlockSpec((1, D_MODEL), lambda i: (0, 0)),
            pl.BlockSpec((1, D_MODEL), lambda i: (0, 1)),
            pl.BlockSpec(memory_space=pl.ANY),
            pl.BlockSpec((DEPTH, CONV_WIDTH, D_CONV), lambda i: (0, 0, 0)),
            pl.BlockSpec((1, D_CONV), row),
            pl.BlockSpec((1, D_CONV), row),
            pl.BlockSpec((1, D_CONV), row),
            pl.BlockSpec(memory_space=pl.ANY),
            pl.BlockSpec((1, D_CONV), row),
        ],
        out_specs=[
            pl.BlockSpec((tm, D_QKVG), lambda i: (cur(i), 0)),
            pl.BlockSpec((tm, D_CONV), lambda i: (lag(i), 0)),
        ],
        out_shape=[
            jax.ShapeDtypeStruct((SEQ, D_QKVG), BF16),
            jax.ShapeDtypeStruct((SEQ, D_CONV), BF16),
        ],
        scratch_shapes=[
            pltpu.VMEM((tm, D_MODEL), BF16),
            pltpu.VMEM((tm, D_MODEL), BF16),
            pltpu.VMEM((D_CONV // LANES, CONV_HALO + tm, LANES), F32),
            pltpu.VMEM((tm, D_CONV), F32),
            pltpu.VMEM((tm, D_CONV), F32),
            pltpu.VMEM((tm, D_CONV), F32),
            pltpu.VMEM((D_MODEL, D_IN), BF16),
            pltpu.VMEM((D_CONV, D_CONV), BF16),
            pltpu.VMEM((2, IN_STAGE_ROWS, D_IN), F32),
            pltpu.SemaphoreType.DMA((2,)),
        ],
        compiler_params=pltpu.CompilerParams(
            dimension_semantics=("arbitrary",), vmem_limit_bytes=VMEM_LIMIT),
        name="inproj_conv",
    )(x2d, mod, mod, w_in, conv_w, conv_b, ln_g, ln_b, w_pw, b_pw)


def kernel(x, c, w_ada, b_ada, w_in, rel_bias, sinks, conv_w, conv_b, conv_ln_g,
           conv_ln_b, w_pw, b_pw, w_out, ln_g, ln_b):
    assert x.shape == (1, SEQ, D_MODEL) and w_in.shape == (DEPTH, D_MODEL, D_IN)
    x2d = x.reshape(SEQ, D_MODEL)

    row = lambda v: v.reshape(1, -1)
    mod = _mod_call(row(c), w_ada.reshape(D_MODEL, 3 * D_MODEL), row(b_ada))
    bias = _bias_call(rel_bias)
    qkvg, y_conv = _inproj_conv_call(
        x2d, mod, w_in.reshape(D_MODEL, D_IN),
        conv_w, row(conv_b), row(conv_ln_g), row(conv_ln_b),
        w_pw.reshape(D_CONV, D_CONV), row(b_pw))
    out = _attn_out_call(qkvg, bias, sinks.reshape(N_Q_HEADS), y_conv, x2d, mod,
                         w_out.reshape(D_MODEL, D_MODEL), row(ln_g), row(ln_b))
    return out.reshape(1, SEQ, D_MODEL)
```

```python
import math

import jax
import jax.numpy as jnp
import numpy as np
from jax.experimental import pallas as pl
from jax.experimental.pallas import tpu as pltpu

F32 = jnp.float32
BF16 = jnp.bfloat16

D_MODEL = 2048
SEQ = 8192
HEAD_DIM = 64
D_ATTN = 1024
D_CONV = 1024
N_Q_HEADS = 16
N_KV_HEADS = 4
GQA_GROUP = 4
D_KV = 256
WINDOW = 128
BLOCK = 128
CONV_WIDTH = 31
N_BUCKETS = 32
MAX_DISTANCE = 128
LN_EPS = 1e-5
DEPTH = 1
ALPHA = (2.0 * DEPTH) ** 0.25
D_IN = 2 * D_ATTN + 2 * D_KV + 3 * D_CONV

LANES = 128
SUBLANES = 8

COL_Q = 0
COL_K = COL_Q + D_ATTN
COL_V = COL_K + D_KV
COL_G_ATTN = COL_V + D_KV
COL_GLU_A = COL_G_ATTN + D_ATTN
COL_GLU_B = COL_GLU_A + D_CONV
COL_G_CONV = COL_GLU_B + D_CONV
HALF = 512

VMEM_LIMIT = 56 * 1024 * 1024


def _silu(v):
    return v * jax.nn.sigmoid(v)


MOD_STAGE_ROWS = 256


def _mod_kernel(c_ref, w_hbm, b_ref, o_ref, ca_ref, stage_ref, sem_ref):
    n_out = o_ref.shape[1]
    n = D_MODEL // MOD_STAGE_ROWS

    def chunk_copy(j):
        slot = j % 2
        return pltpu.make_async_copy(
            w_hbm.at[pl.ds(j * MOD_STAGE_ROWS, MOD_STAGE_ROWS), :],
            stage_ref.at[slot], sem_ref.at[slot])

    chunk_copy(0).start()
    ca = _silu(c_ref[...])
    ca_ref[...] = jnp.transpose(jnp.broadcast_to(ca, (LANES, D_MODEL)))
    acc = jnp.zeros((SUBLANES, n_out), F32)
    for j in range(n):
        if j + 1 < n:
            chunk_copy(j + 1).start()
        chunk_copy(j).wait()
        rows = slice(j * MOD_STAGE_ROWS, (j + 1) * MOD_STAGE_ROWS)
        prod = jnp.tile(ca_ref[rows, :], (1, n_out // LANES)) * stage_ref[j % 2]
        for r in range(0, MOD_STAGE_ROWS, SUBLANES):
            acc = acc + prod[r:r + SUBLANES, :]
    o_ref[...] = jnp.sum(acc, axis=0, keepdims=True) + b_ref[...]


def _mod_call(c_row, w_ada, b_ada):
    n = w_ada.shape[1]
    return pl.pallas_call(
        _mod_kernel,
        in_specs=[
            pl.BlockSpec((1, D_MODEL), lambda: (0, 0)),
            pl.BlockSpec(memory_space=pl.ANY),
            pl.BlockSpec((1, n), lambda: (0, 0)),
        ],
        out_specs=pl.BlockSpec((1, n), lambda: (0, 0)),
        out_shape=jax.ShapeDtypeStruct((1, n), F32),
        scratch_shapes=[
            pltpu.VMEM((D_MODEL, LANES), F32),
            pltpu.VMEM((2, MOD_STAGE_ROWS, n), F32),
            pltpu.SemaphoreType.DMA((2,)),
        ],
        compiler_params=pltpu.CompilerParams(vmem_limit_bytes=VMEM_LIMIT),
        name="mod",
    )(c_row, w_ada, b_ada)


LOG2E = math.log2(math.e)
Q_SCALE = LOG2E * HEAD_DIM ** -0.5
MASKED = -1e30


def _lane_buckets():
    m = np.arange(2 * BLOCK)
    dist = BLOCK - m
    ok = (dist >= 0) & (dist < WINDOW)
    max_exact = N_BUCKETS // 2
    d = np.maximum(dist, 1).astype(np.float64)
    large = max_exact + (np.log(d / max_exact) / math.log(MAX_DISTANCE / max_exact)
                         * (N_BUCKETS - max_exact)).astype(np.int32)
    large = np.minimum(large, N_BUCKETS - 1)
    bucket = np.where(dist < max_exact, dist, large)
    return np.where(ok, bucket, -1).astype(np.int32).reshape(1, -1)


def _bias_kernel(rb_ref, bm_ref, o_ref):
    bm = bm_ref[...]
    col = jax.lax.broadcasted_iota(jnp.int32, (BLOCK, 2 * BLOCK), 1)
    for h in range(N_Q_HEADS):
        t = jnp.full(bm.shape, MASKED, F32)
        for b in range(N_BUCKETS):
            t = jnp.where(bm == b, rb_ref[b, h] * LOG2E, t)
        x = pltpu.roll(jnp.broadcast_to(t, (BLOCK, 2 * BLOCK)), 0, 1, stride=1, stride_axis=0)
        o_ref[1, h] = x
        o_ref[0, h] = jnp.where(col >= BLOCK, x, MASKED)


def _bias_call(rel_bias):
    shape = (2, N_Q_HEADS, BLOCK, 2 * BLOCK)
    return pl.pallas_call(
        _bias_kernel,
        in_specs=[
            pl.BlockSpec(memory_space=pltpu.SMEM),
            pl.BlockSpec((1, 2 * BLOCK), lambda: (0, 0)),
        ],
        out_specs=pl.BlockSpec(shape, lambda: (0, 0, 0, 0)),
        out_shape=jax.ShapeDtypeStruct(shape, F32),
        name="bias",
    )(rel_bias, jnp.asarray(_lane_buckets()))


def _load_weight_bf16(w_hbm, w_ref, stage_ref, sem_ref, col_scale=None):
    rows, cols = w_ref.shape
    chunk_rows = stage_ref.shape[1]
    n = rows // chunk_rows

    def chunk_copy(j):
        slot = j % 2
        return pltpu.make_async_copy(
            w_hbm.at[pl.ds(j * chunk_rows, chunk_rows), :],
            stage_ref.at[slot, :, pl.ds(0, cols)], sem_ref.at[slot])

    chunk_copy(0).start()
    for j in range(n):
        if j + 1 < n:
            chunk_copy(j + 1).start()
        chunk_copy(j).wait()
        chunk = stage_ref[j % 2, :, 0:cols]
        if col_scale is not None:
            chunk = chunk * col_scale
        w_ref[j * chunk_rows:(j + 1) * chunk_rows, :] = chunk.astype(BF16)


def _token(v):
    t = v[0:SUBLANES, 0:LANES]
    for r in range(SUBLANES, v.shape[0], SUBLANES):
        t = t + v[r:r + SUBLANES, 0:LANES]
    return t


def _zero_of(token):
    return (pltpu.bitcast(token, jnp.uint32) >> 16) >> 16


def _copy_after(src_ref, dst_ref, tokens):
    zero = _zero_of(tokens[0])
    for t in tokens[1:]:
        zero = zero | _zero_of(t)
    bits = pltpu.bitcast(src_ref[...], jnp.uint32)
    reps = (bits.shape[0] // zero.shape[0], bits.shape[1] // zero.shape[1])
    dst_ref[...] = pltpu.bitcast(bits | jnp.tile(zero, reps), dst_ref.dtype)


ATT_TM = 4 * BLOCK
ATT_ROWS = BLOCK
OUT_STAGE_ROWS = 512


def _dup_halves(t):
    swapped = jnp.concatenate([t[:, HEAD_DIM:], t[:, :HEAD_DIM]], axis=1)
    lane = jax.lax.broadcasted_iota(jnp.int32, t.shape, 1)
    low_half = lane < HEAD_DIM
    return jnp.where(low_half, t, swapped), jnp.where(low_half, swapped, t)


def _attn_group(q_ref, g0_ref, g1_ref, k_dup, v_dup, bias_ref, sinks_ref, r0, b0, kv, ya_ref):
    nr = ATT_ROWS
    lane = jax.lax.broadcasted_iota(jnp.int32, (nr, LANES), 1)
    low_half = lane < HEAD_DIM
    rows = slice(r0, r0 + nr)
    tiles = [GQA_GROUP * kv // 2 + t for t in range(GQA_GROUP // 2)]
    stacked = []
    for tile in tiles:
        q_tile = q_ref[rows, tile * LANES:(tile + 1) * LANES]
        zero = jnp.zeros_like(q_tile)
        stacked += [jnp.where(low_half, q_tile, zero), jnp.where(low_half, zero, q_tile)]
    s_all = jax.lax.dot_general(jnp.concatenate(stacked, axis=0), k_dup,
                                (((1,), (1,)), ((), ())),
                                preferred_element_type=F32)
    probs, denoms = [], []
    for r in range(GQA_GROUP):
        head = GQA_GROUP * kv + r
        s = s_all[r * nr:(r + 1) * nr] + bias_ref[0, head, b0:b0 + nr, :]
        sink = sinks_ref[head] * LOG2E
        m = jnp.max(s, axis=1, keepdims=True)
        e = jnp.exp2(s - m)
        denoms.append(jnp.sum(e, axis=1, keepdims=True) + jnp.exp2(sink - m))
        probs.append(e.astype(BF16))
    o_all = jnp.dot(jnp.concatenate(probs, axis=0), v_dup,
                    preferred_element_type=F32)
    outs = [o_all[r * nr:(r + 1) * nr] / denoms[r] for r in range(GQA_GROUP)]
    for t, tile in enumerate(tiles):
        y = jnp.where(low_half, outs[2 * t], outs[2 * t + 1])
        g_ref = g0_ref if tile < HALF // LANES else g1_ref
        gl = (tile * LANES) % HALF
        gate = g_ref[rows, gl:gl + LANES].astype(F32)
        ya_ref[rows, tile * LANES:(tile + 1) * LANES] = (y * gate).astype(BF16)


def _attn_tile(sinks_ref, q_ref, g0_ref, g1_ref, k_ref, v_ref, kp_ref, vp_ref,
               bias_refs, ya_ref):
    n_lt = D_KV // LANES
    lt = lambda t: slice(t * LANES, (t + 1) * LANES)

    def block_operands(b):
        r0 = b * BLOCK
        if b == 0:
            k_prev = [kp_ref[:, lt(t)] for t in range(n_lt)]
            v_prev = [vp_ref[:, lt(t)] for t in range(n_lt)]
        else:
            k_prev = [k_ref[r0 - BLOCK:r0, lt(t)] for t in range(n_lt)]
            v_prev = [v_ref[r0 - BLOCK:r0, lt(t)] for t in range(n_lt)]
        k_rows = [jnp.concatenate([k_prev[t], k_ref[r0:r0 + BLOCK, lt(t)]], axis=0)
                  for t in range(n_lt)]
        v_rows = [jnp.concatenate([v_prev[t], v_ref[r0:r0 + BLOCK, lt(t)]], axis=0)
                  for t in range(n_lt)]
        return ([_dup_halves(kt) for kt in k_rows], [_dup_halves(vt) for vt in v_rows])

    for b in range(ATT_TM // BLOCK):
        k_dups, v_dups = block_operands(b)
        for b0 in range(0, BLOCK, ATT_ROWS):
            for kv in range(N_KV_HEADS):
                _attn_group(q_ref, g0_ref, g1_ref, k_dups[kv // 2][kv % 2],
                            v_dups[kv // 2][kv % 2], bias_refs[b], sinks_ref,
                            b * BLOCK + b0, b0, kv, ya_ref)


def _attn_out_kernel(sinks_ref, q_ref, g0_ref, g1_ref, k_ref, v_ref, kp_ref, vp_ref,
                     bias0_ref, bias1_ref, yc_ref, x_ref, gate_ref, wout_hbm,
                     lng_ref, lnb_ref, o_ref, ya_ref, wout_ref, stage_ref, sem_ref):
    i = pl.program_id(0)

    @pl.when(i == 0)
    def _():
        ya_ref[...] = jnp.zeros(ya_ref.shape, ya_ref.dtype)
        _load_weight_bf16(wout_hbm, wout_ref, stage_ref, sem_ref,
                          col_scale=gate_ref[...] * (1.0 / ALPHA))

    y = jnp.dot(ya_ref[...], wout_ref[0:D_ATTN, :], preferred_element_type=F32)
    y = y + jnp.dot(yc_ref[...], wout_ref[D_ATTN:D_MODEL, :], preferred_element_type=F32)
    z = x_ref[...] + y
    mu = jnp.mean(z, axis=1, keepdims=True)
    d = z - mu
    var = jnp.mean(d * d, axis=1, keepdims=True)
    o_ref[...] = d * jax.lax.rsqrt(var + LN_EPS / ALPHA ** 2) * lng_ref[...] + lnb_ref[...]

    _attn_tile(sinks_ref, q_ref, g0_ref, g1_ref, k_ref, v_ref, kp_ref, vp_ref,
               [bias0_ref] + [bias1_ref] * (ATT_TM // BLOCK - 1), ya_ref)


def _attn_out_call(qkvg, bias, sinks, y_conv, x2d, mod, w_out, ln_g, ln_b):
    tm = ATT_TM
    n = SEQ // tm
    clamp = lambda t: jnp.clip(t, 0, n - 1)
    cur = lambda i: clamp(i)
    lag = lambda i: clamp(i - 1)
    kv_prev = lambda i: jnp.maximum(cur(i) * (tm // BLOCK) - 1, 0)
    row = lambda i: (0, 0)
    return pl.pallas_call(
        _attn_out_kernel,
        grid=(n + 1,),
        in_specs=[
            pl.BlockSpec(memory_space=pltpu.SMEM),
            pl.BlockSpec((tm, D_ATTN), lambda i: (cur(i), COL_Q // D_ATTN)),
            pl.BlockSpec((tm, HALF), lambda i: (cur(i), COL_G_ATTN // HALF)),
            pl.BlockSpec((tm, HALF), lambda i: (cur(i), COL_G_ATTN // HALF + 1)),
            pl.BlockSpec((tm, D_KV), lambda i: (cur(i), COL_K // D_KV)),
            pl.BlockSpec((tm, D_KV), lambda i: (cur(i), COL_V // D_KV)),
            pl.BlockSpec((BLOCK, D_KV), lambda i: (kv_prev(i), COL_K // D_KV)),
            pl.BlockSpec((BLOCK, D_KV), lambda i: (kv_prev(i), COL_V // D_KV)),
            pl.BlockSpec((1, N_Q_HEADS, BLOCK, 2 * BLOCK),
                         lambda i: (jnp.minimum(cur(i), 1), 0, 0, 0)),
            pl.BlockSpec((1, N_Q_HEADS, BLOCK, 2 * BLOCK), lambda i: (1, 0, 0, 0)),
            pl.BlockSpec((tm, D_CONV), lambda i: (lag(i), 0)),
            pl.BlockSpec((tm, D_MODEL), lambda i: (lag(i), 0)),
            pl.BlockSpec((1, D_MODEL), lambda i: (0, 2)),
            pl.BlockSpec(memory_space=pl.ANY),
            pl.BlockSpec((1, D_MODEL), row),
            pl.BlockSpec((1, D_MODEL), row),
        ],
        out_specs=pl.BlockSpec((tm, D_MODEL), lambda i: (lag(i), 0)),
        out_shape=jax.ShapeDtypeStruct((SEQ, D_MODEL), F32),
        scratch_shapes=[
            pltpu.VMEM((tm, D_ATTN), BF16),
            pltpu.VMEM((D_MODEL, D_MODEL), BF16),
            pltpu.VMEM((2, OUT_STAGE_ROWS, D_MODEL), F32),
            pltpu.SemaphoreType.DMA((2,)),
        ],
        compiler_params=pltpu.CompilerParams(
            dimension_semantics=("arbitrary",), vmem_limit_bytes=VMEM_LIMIT),
        name="attn_out",
    )(sinks, qkvg, qkvg, qkvg, qkvg, qkvg, qkvg, qkvg, bias, bias, y_conv, x2d, mod,
      w_out, ln_g, ln_b)


IN_TM = 256
IN_CHUNK = 256
D_QKVG = COL_GLU_A
CONV_HALO = 32
IN_STAGE_ROWS = 256
TAP_ROWS = 32
SYNC_EVERY = 1
LAST_TAP_SYNC_CHUNK = 18


def _tap_piece(u_ref, cw_ref, cb_ref, acc_ref, c, r0):
    off = CONV_HALO - (CONV_WIDTH - 1)
    cs = slice(c * LANES, (c + 1) * LANES)
    acc = jnp.broadcast_to(cb_ref[:, cs], (TAP_ROWS, LANES))
    for k in range(CONV_WIDTH):
        acc = acc + cw_ref[0, k:k + 1, cs] * u_ref[c, r0 + off + k:r0 + off + k + TAP_ROWS, :]
    acc_ref[r0:r0 + TAP_ROWS, cs] = acc
    return _token(acc)


def _conv_tail(acc_ref, lng_ref, lnb_ref, wpw_ref, bpw_ref, gate_ref, yc_ref):
    v = acc_ref[...]
    mu = jnp.mean(v, axis=1, keepdims=True)
    d = v - mu
    var = jnp.mean(d * d, axis=1, keepdims=True)
    y = d * jax.lax.rsqrt(var + LN_EPS) * lng_ref[...] + lnb_ref[...]
    s = _silu(y).astype(BF16)
    z = jnp.dot(s, wpw_ref[...], preferred_element_type=F32) + bpw_ref[...]
    yc_ref[...] = (z * gate_ref[...]).astype(BF16)


def _inproj_conv_kernel(x_ref, shift_ref, scale_ref, w_hbm, cw_ref, cb_ref, lng_ref,
                        lnb_ref, wpw_hbm, bpw_ref, qkvg_ref, yc_ref,
                        h0_ref, h1_ref, u_ref, gc_ref, gcp_ref, acc_ref,
                        w_ref, wpw_ref, stage_ref, sem_ref):
    i = pl.program_id(0)
    tm = IN_TM
    pieces = [(c, r0) for c in range(D_CONV // LANES) for r0 in range(0, tm, TAP_ROWS)]

    @pl.when(i == 0)
    def _():
        u_ref[...] = jnp.zeros(u_ref.shape, u_ref.dtype)
        gc_ref[...] = jnp.zeros(gc_ref.shape, gc_ref.dtype)
        _load_weight_bf16(w_hbm, w_ref, stage_ref, sem_ref)
        _load_weight_bf16(wpw_hbm, wpw_ref, stage_ref, sem_ref)

    gcp_ref[...] = gc_ref[...]
    h = x_ref[...] * (1.0 + scale_ref[...]) + shift_ref[...]
    h0_ref[...] = h.astype(BF16)
    h_bufs = [h0_ref, h1_ref]
    state = {"cur": 0}

    def sync(tokens):
        src, dst = h_bufs[state["cur"]], h_bufs[1 - state["cur"]]
        _copy_after(src, dst, tokens)
        state["cur"] = 1 - state["cur"]

    def project(col):
        return jnp.dot(h_bufs[state["cur"]][...], w_ref[:, col:col + IN_CHUNK],
                       preferred_element_type=F32)

    cols = list(range(0, D_QKVG, IN_CHUNK)) + list(range(COL_G_CONV, D_IN, IN_CHUNK))
    for c in range(D_CONV // IN_CHUNK):
        cols += [COL_GLU_A + c * IN_CHUNK, COL_GLU_B + c * IN_CHUNK]
    sync_chunks = list(range(SYNC_EVERY, LAST_TAP_SYNC_CHUNK + 1, SYNC_EVERY))
    bounds = [round(s * len(pieces) / len(sync_chunks)) for s in range(len(sync_chunks) + 1)]
    glu_a = None
    pieces_done = 0
    for j, col in enumerate(cols):
        if j in sync_chunks:
            s_idx = sync_chunks.index(j)
            group = pieces[bounds[s_idx]:bounds[s_idx + 1]]
            sync([_tap_piece(u_ref, cw_ref, cb_ref, acc_ref, c, r0) for c, r0 in group])
            pieces_done = bounds[s_idx + 1]
            if pieces_done == len(pieces):
                _conv_tail(acc_ref, lng_ref, lnb_ref, wpw_ref, bpw_ref, gcp_ref, yc_ref)
        res = project(col)
        if col < COL_K:
            qkvg_ref[:, col:col + IN_CHUNK] = (res * Q_SCALE).astype(BF16)
        elif col < COL_G_ATTN:
            qkvg_ref[:, col:col + IN_CHUNK] = res.astype(BF16)
        elif col < D_QKVG:
            qkvg_ref[:, col:col + IN_CHUNK] = _silu(res).astype(BF16)
        elif col < COL_GLU_B:
            glu_a = res
        elif col < COL_G_CONV:
            u = glu_a * jax.nn.sigmoid(res)
            t0 = (col - COL_GLU_B) // LANES
            for t in range(t0, t0 + IN_CHUNK // LANES):
                assert pieces_done >= (t + 1) * (tm // TAP_ROWS)
                u_ref[t, 0:CONV_HALO, :] = u_ref[t, tm:tm + CONV_HALO, :]
                u_ref[t, CONV_HALO:CONV_HALO + tm, :] = u[:, (t - t0) * LANES:(t - t0 + 1) * LANES]
        else:
            gc_ref[:, col - COL_G_CONV:col - COL_G_CONV + IN_CHUNK] = _silu(res)


def _inproj_conv_call(x2d, mod, w_in, conv_w, conv_b, ln_g, ln_b, w_pw, b_pw):
    tm = IN_TM
    n = SEQ // tm
    cur = lambda i: jnp.minimum(i, n - 1)
    lag = lambda i: jnp.maximum(i - 1, 0)
    row = lambda i: (0, 0)
    return pl.pallas_call(
        _inproj_conv_kernel,
        grid=(n + 1,),
        in_specs=[
            pl.BlockSpec((tm, D_MODEL), lambda i: (cur(i), 0)),
            pl.BlockSpec((1, D_MODEL), lambda i: (0, 0)),
            pl.BlockSpec((1, D_MODEL), lambda i: (0, 1)),
            pl.BlockSpec(memory_space=pl.ANY),
            pl.BlockSpec((DEPTH, CONV_WIDTH, D_CONV), lambda i: (0, 0, 0)),
            pl.BlockSpec((1, D_CONV), row),
            pl.BlockSpec((1, D_CONV), row),
            pl.BlockSpec((1, D_CONV), row),
            pl.BlockSpec(memory_space=pl.ANY),
            pl.BlockSpec((1, D_CONV), row),
        ],
        out_specs=[
            pl.BlockSpec((tm, D_QKVG), lambda i: (cur(i), 0)),
            pl.BlockSpec((tm, D_CONV), lambda i: (lag(i), 0)),
        ],
        out_shape=[
            jax.ShapeDtypeStruct((SEQ, D_QKVG), BF16),
            jax.ShapeDtypeStruct((SEQ, D_CONV), BF16),
        ],
        scratch_shapes=[
            pltpu.VMEM((tm, D_MODEL), BF16),
            pltpu.VMEM((tm, D_MODEL), BF16),
            pltpu.VMEM((D_CONV // LANES, CONV_HALO + tm, LANES), F32),
            pltpu.VMEM((tm, D_CONV), F32),
            pltpu.VMEM((tm, D_CONV), F32),
            pltpu.VMEM((tm, D_CONV), F32),
            pltpu.VMEM((D_MODEL, D_IN), BF16),
            pltpu.VMEM((D_CONV, D_CONV), BF16),
            pltpu.VMEM((2, IN_STAGE_ROWS, D_IN), F32),
            pltpu.SemaphoreType.DMA((2,)),
        ],
        compiler_params=pltpu.CompilerParams(
            dimension_semantics=("arbitrary",), vmem_limit_bytes=VMEM_LIMIT),
        name="inproj_conv",
    )(x2d, mod, mod, w_in, conv_w, conv_b, ln_g, ln_b, w_pw, b_pw)


def kernel(x, c, w_ada, b_ada, w_in, rel_bias, sinks, conv_w, conv_b, conv_ln_g,
           conv_ln_b, w_pw, b_pw, w_out, ln_g, ln_b):
    assert x.shape == (1, SEQ, D_MODEL) and w_in.shape == (DEPTH, D_MODEL, D_IN)
    x2d = x.reshape(SEQ, D_MODEL)

    row = lambda v: v.reshape(1, -1)
    mod = _mod_call(row(c), w_ada.reshape(D_MODEL, 3 * D_MODEL), row(b_ada))
    bias = _bias_call(rel_bias)
    qkvg, y_conv = _inproj_conv_call(
        x2d, mod, w_in.reshape(D_MODEL, D_IN),
        conv_w, row(conv_b), row(conv_ln_g), row(conv_ln_b),
        w_pw.reshape(D_CONV, D_CONV), row(b_pw))
    out = _attn_out_call(qkvg, bias, sinks.reshape(N_Q_HEADS), y_conv, x2d, mod,
                         w_out.reshape(D_MODEL, D_MODEL), row(ln_g), row(ln_b))
    return out.reshape(1, SEQ, D_MODEL)
```

```python
import math

import jax
import jax.numpy as jnp
import numpy as np
from jax.experimental import pallas as pl
from jax.experimental.pallas import tpu as pltpu

F32 = jnp.float32
BF16 = jnp.bfloat16

D_MODEL = 2048
SEQ = 8192
HEAD_DIM = 64
D_ATTN = 1024
D_CONV = 1024
N_Q_HEADS = 16
N_KV_HEADS = 4
GQA_GROUP = 4
D_KV = 256
WINDOW = 128
BLOCK = 128
CONV_WIDTH = 31
N_BUCKETS = 32
MAX_DISTANCE = 128
LN_EPS = 1e-5
DEPTH = 1
ALPHA = (2.0 * DEPTH) ** 0.25
D_IN = 2 * D_ATTN + 2 * D_KV + 3 * D_CONV

LANES = 128
SUBLANES = 8

COL_Q = 0
COL_K = COL_Q + D_ATTN
COL_V = COL_K + D_KV
COL_G_ATTN = COL_V + D_KV
COL_GLU_A = COL_G_ATTN + D_ATTN
COL_GLU_B = COL_GLU_A + D_CONV
COL_G_CONV = COL_GLU_B + D_CONV
HALF = 512

VMEM_LIMIT = 56 * 1024 * 1024


def _silu(v):
    return v * jax.nn.sigmoid(v)


MOD_STAGE_ROWS = 256


def _mod_kernel(c_ref, w_hbm, b_ref, o_ref, ca_ref, stage_ref, sem_ref):
    n_out = o_ref.shape[1]
    n = D_MODEL // MOD_STAGE_ROWS

    def chunk_copy(j):
        slot = j % 2
        return pltpu.make_async_copy(
            w_hbm.at[pl.ds(j * MOD_STAGE_ROWS, MOD_STAGE_ROWS), :],
            stage_ref.at[slot], sem_ref.at[slot])

    chunk_copy(0).start()
    ca = _silu(c_ref[...])
    ca_ref[...] = jnp.transpose(jnp.broadcast_to(ca, (LANES, D_MODEL)))
    acc = jnp.zeros((SUBLANES, n_out), F32)
    for j in range(n):
        if j + 1 < n:
            chunk_copy(j + 1).start()
        chunk_copy(j).wait()
        rows = slice(j * MOD_STAGE_ROWS, (j + 1) * MOD_STAGE_ROWS)
        prod = jnp.tile(ca_ref[rows, :], (1, n_out // LANES)) * stage_ref[j % 2]
        for r in range(0, MOD_STAGE_ROWS, SUBLANES):
            acc = acc + prod[r:r + SUBLANES, :]
    o_ref[...] = jnp.sum(acc, axis=0, keepdims=True) + b_ref[...]


def _mod_call(c_row, w_ada, b_ada):
    n = w_ada.shape[1]
    return pl.pallas_call(
        _mod_kernel,
        in_specs=[
            pl.BlockSpec((1, D_MODEL), lambda: (0, 0)),
            pl.BlockSpec(memory_space=pl.ANY),
            pl.BlockSpec((1, n), lambda: (0, 0)),
        ],
        out_specs=pl.BlockSpec((1, n), lambda: (0, 0)),
        out_shape=jax.ShapeDtypeStruct((1, n), F32),
        scratch_shapes=[
            pltpu.VMEM((D_MODEL, LANES), F32),
            pltpu.VMEM((2, MOD_STAGE_ROWS, n), F32),
            pltpu.SemaphoreType.DMA((2,)),
        ],
        compiler_params=pltpu.CompilerParams(vmem_limit_bytes=VMEM_LIMIT),
        name="mod",
    )(c_row, w_ada, b_ada)


LOG2E = math.log2(math.e)
Q_SCALE = LOG2E * HEAD_DIM ** -0.5
MASKED = -1e30


def _lane_buckets():
    m = np.arange(2 * BLOCK)
    dist = BLOCK - m
    ok = (dist >= 0) & (dist < WINDOW)
    max_exact = N_BUCKETS // 2
    d = np.maximum(dist, 1).astype(np.float64)
    large = max_exact + (np.log(d / max_exact) / math.log(MAX_DISTANCE / max_exact)
                         * (N_BUCKETS - max_exact)).astype(np.int32)
    large = np.minimum(large, N_BUCKETS - 1)
    bucket = np.where(dist < max_exact, dist, large)
    return np.where(ok, bucket, -1).astype(np.int32).reshape(1, -1)


def _bias_kernel(rb_ref, bm_ref, o_ref):
    bm = bm_ref[...]
    col = jax.lax.broadcasted_iota(jnp.int32, (BLOCK, 2 * BLOCK), 1)
    for h in range(N_Q_HEADS):
        t = jnp.full(bm.shape, MASKED, F32)
        for b in range(N_BUCKETS):
            t = jnp.where(bm == b, rb_ref[b, h] * LOG2E, t)
        x = pltpu.roll(jnp.broadcast_to(t, (BLOCK, 2 * BLOCK)), 0, 1, stride=1, stride_axis=0)
        o_ref[1, h] = x
        o_ref[0, h] = jnp.where(col >= BLOCK, x, MASKED)


def _bias_call(rel_bias):
    shape = (2, N_Q_HEADS, BLOCK, 2 * BLOCK)
    return pl.pallas_call(
        _bias_kernel,
        in_specs=[
            pl.BlockSpec(memory_space=pltpu.SMEM),
            pl.BlockSpec((1, 2 * BLOCK), lambda: (0, 0)),
        ],
        out_specs=pl.BlockSpec(shape, lambda: (0, 0, 0, 0)),
        out_shape=jax.ShapeDtypeStruct(shape, F32),
        name="bias",
    )(rel_bias, jnp.asarray(_lane_buckets()))


def _load_weight_bf16(w_hbm, w_ref, stage_ref, sem_ref, col_scale=None):
    rows, cols = w_ref.shape
    chunk_rows = stage_ref.shape[1]
    n = rows // chunk_rows

    def chunk_copy(j):
        slot = j % 2
        return pltpu.make_async_copy(
            w_hbm.at[pl.ds(j * chunk_rows, chunk_rows), :],
            stage_ref.at[slot, :, pl.ds(0, cols)], sem_ref.at[slot])

    chunk_copy(0).start()
    for j in range(n):
        if j + 1 < n:
            chunk_copy(j + 1).start()
        chunk_copy(j).wait()
        chunk = stage_ref[j % 2, :, 0:cols]
        if col_scale is not None:
            chunk = chunk * col_scale
        w_ref[j * chunk_rows:(j + 1) * chunk_rows, :] = chunk.astype(BF16)


def _token(v):
    t = v[0:SUBLANES, 0:LANES]
    for r in range(SUBLANES, v.shape[0], SUBLANES):
        t = t + v[r:r + SUBLANES, 0:LANES]
    return t


def _zero_of(token):
    return (pltpu.bitcast(token, jnp.uint32) >> 16) >> 16


def _copy_after(src_ref, dst_ref, tokens):
    zero = _zero_of(tokens[0])
    for t in tokens[1:]:
        zero = zero | _zero_of(t)
    bits = pltpu.bitcast(src_ref[...], jnp.uint32)
    reps = (bits.shape[0] // zero.shape[0], bits.shape[1] // zero.shape[1])
    dst_ref[...] = pltpu.bitcast(bits | jnp.tile(zero, reps), dst_ref.dtype)


ATT_TM = 4 * BLOCK
ATT_ROWS = BLOCK
OUT_STAGE_ROWS = 512


def _dup_halves(t):
    swapped = jnp.concatenate([t[:, HEAD_DIM:], t[:, :HEAD_DIM]], axis=1)
    lane = jax.lax.broadcasted_iota(jnp.int32, t.shape, 1)
    low_half = lane < HEAD_DIM
    return jnp.where(low_half, t, swapped), jnp.where(low_half, swapped, t)


def _attn_group(q_ref, g0_ref, g1_ref, k_dup, v_dup, bias_ref, sinks_ref, r0, b0, kv, ya_ref):
    nr = ATT_ROWS
    lane = jax.lax.broadcasted_iota(jnp.int32, (nr, LANES), 1)
    low_half = lane < HEAD_DIM
    rows = slice(r0, r0 + nr)
    tiles = [GQA_GROUP * kv // 2 + t for t in range(GQA_GROUP // 2)]
    stacked = []
    for tile in tiles:
        q_tile = q_ref[rows, tile * LANES:(tile + 1) * LANES]
        zero = jnp.zeros_like(q_tile)
        stacked += [jnp.where(low_half, q_tile, zero), jnp.where(low_half, zero, q_tile)]
    s_all = jax.lax.dot_general(jnp.concatenate(stacked, axis=0), k_dup,
                                (((1,), (1,)), ((), ())),
                                preferred_element_type=F32)
    probs, denoms = [], []
    for r in range(GQA_GROUP):
        head = GQA_GROUP * kv + r
        s = s_all[r * nr:(r + 1) * nr] + bias_ref[0, head, b0:b0 + nr, :]
        sink = sinks_ref[head] * LOG2E
        m = jnp.max(s, axis=1, keepdims=True)
        e = jnp.exp2(s - m)
        denoms.append(jnp.sum(e, axis=1, keepdims=True) + jnp.exp2(sink - m))
        probs.append(e.astype(BF16))
    o_all = jnp.dot(jnp.concatenate(probs, axis=0), v_dup,
                    preferred_element_type=F32)
    outs = [o_all[r * nr:(r + 1) * nr] / denoms[r] for r in range(GQA_GROUP)]
    for t, tile in enumerate(tiles):
        y = jnp.where(low_half, outs[2 * t], outs[2 * t + 1])
        g_ref = g0_ref if tile < HALF // LANES else g1_ref
        gl = (tile * LANES) % HALF
        gate = g_ref[rows, gl:gl + LANES].astype(F32)
        ya_ref[rows, tile * LANES:(tile + 1) * LANES] = (y * gate).astype(BF16)


def _attn_tile(sinks_ref, q_ref, g0_ref, g1_ref, k_ref, v_ref, kp_ref, vp_ref,
               bias_refs, ya_ref):
    n_lt = D_KV // LANES
    lt = lambda t: slice(t * LANES, (t + 1) * LANES)

    def block_operands(b):
        r0 = b * BLOCK
        if b == 0:
            k_prev = [kp_ref[:, lt(t)] for t in range(n_lt)]
            v_prev = [vp_ref[:, lt(t)] for t in range(n_lt)]
        else:
            k_prev = [k_ref[r0 - BLOCK:r0, lt(t)] for t in range(n_lt)]
            v_prev = [v_ref[r0 - BLOCK:r0, lt(t)] for t in range(n_lt)]
        k_rows = [jnp.concatenate([k_prev[t], k_ref[r0:r0 + BLOCK, lt(t)]], axis=0)
                  for t in range(n_lt)]
        v_rows = [jnp.concatenate([v_prev[t], v_ref[r0:r0 + BLOCK, lt(t)]], axis=0)
                  for t in range(n_lt)]
        return ([_dup_halves(kt) for kt in k_rows], [_dup_halves(vt) for vt in v_rows])

    for b in range(ATT_TM // BLOCK):
        k_dups, v_dups = block_operands(b)
        for b0 in range(0, BLOCK, ATT_ROWS):
            for kv in range(N_KV_HEADS):
                _attn_group(q_ref, g0_ref, g1_ref, k_dups[kv // 2][kv % 2],
                            v_dups[kv // 2][kv % 2], bias_refs[b], sinks_ref,
                            b * BLOCK + b0, b0, kv, ya_ref)


def _attn_out_kernel(sinks_ref, q_ref, g0_ref, g1_ref, k_ref, v_ref, kp_ref, vp_ref,
                     bias0_ref, bias1_ref, yc_ref, x_ref, gate_ref, wout_hbm,
                     lng_ref, lnb_ref, o_ref, ya_ref, wout_ref, stage_ref, sem_ref):
    i = pl.program_id(0)

    @pl.when(i == 0)
    def _():
        ya_ref[...] = jnp.zeros(ya_ref.shape, ya_ref.dtype)
        _load_weight_bf16(wout_hbm, wout_ref, stage_ref, sem_ref,
                          col_scale=gate_ref[...] * (1.0 / ALPHA))

    y = jnp.dot(ya_ref[...], wout_ref[0:D_ATTN, :], preferred_element_type=F32)
    y = y + jnp.dot(yc_ref[...], wout_ref[D_ATTN:D_MODEL, :], preferred_element_type=F32)
    z = x_ref[...] + y
    mu = jnp.mean(z, axis=1, keepdims=True)
    d = z - mu
    var = jnp.mean(d * d, axis=1, keepdims=True)
    o_ref[...] = d * jax.lax.rsqrt(var + LN_EPS / ALPHA ** 2) * lng_ref[...] + lnb_ref[...]

    _attn_tile(sinks_ref, q_ref, g0_ref, g1_ref, k_ref, v_ref, kp_ref, vp_ref,
               [bias0_ref] + [bias1_ref] * (ATT_TM // BLOCK - 1), ya_ref)


def _attn_out_call(qkvg, bias, sinks, y_conv, x2d, mod, w_out, ln_g, ln_b):
    tm = ATT_TM
    n = SEQ // tm
    clamp = lambda t: jnp.clip(t, 0, n - 1)
    cur = lambda i: clamp(i)
    lag = lambda i: clamp(i - 1)
    kv_prev = lambda i: jnp.maximum(cur(i) * (tm // BLOCK) - 1, 0)
    row = lambda i: (0, 0)
    return pl.pallas_call(
        _attn_out_kernel,
        grid=(n + 1,),
        in_specs=[
            pl.BlockSpec(memory_space=pltpu.SMEM),
            pl.BlockSpec((tm, D_ATTN), lambda i: (cur(i), COL_Q // D_ATTN)),
            pl.BlockSpec((tm, HALF), lambda i: (cur(i), COL_G_ATTN // HALF)),
            pl.BlockSpec((tm, HALF), lambda i: (cur(i), COL_G_ATTN // HALF + 1)),
            pl.BlockSpec((tm, D_KV), lambda i: (cur(i), COL_K // D_KV)),
            pl.BlockSpec((tm, D_KV), lambda i: (cur(i), COL_V // D_KV)),
            pl.BlockSpec((BLOCK, D_KV), lambda i: (kv_prev(i), COL_K // D_KV)),
            pl.BlockSpec((BLOCK, D_KV), lambda i: (kv_prev(i), COL_V // D_KV)),
            pl.BlockSpec((1, N_Q_HEADS, BLOCK, 2 * BLOCK),
                         lambda i: (jnp.minimum(cur(i), 1), 0, 0, 0)),
            pl.BlockSpec((1, N_Q_HEADS, BLOCK, 2 * BLOCK), lambda i: (1, 0, 0, 0)),
            pl.BlockSpec((tm, D_CONV), lambda i: (lag(i), 0)),
            pl.BlockSpec((tm, D_MODEL), lambda i: (lag(i), 0)),
            pl.BlockSpec((1, D_MODEL), lambda i: (0, 2)),
            pl.BlockSpec(memory_space=pl.ANY),
            pl.BlockSpec((1, D_MODEL), row),
            pl.BlockSpec((1, D_MODEL), row),
        ],
        out_specs=pl.BlockSpec((tm, D_MODEL), lambda i: (lag(i), 0)),
        out_shape=jax.ShapeDtypeStruct((SEQ, D_MODEL), F32),
        scratch_shapes=[
            pltpu.VMEM((tm, D_ATTN), BF16),
            pltpu.VMEM((D_MODEL, D_MODEL), BF16),
            pltpu.VMEM((2, OUT_STAGE_ROWS, D_MODEL), F32),
            pltpu.SemaphoreType.DMA((2,)),
        ],
        compiler_params=pltpu.CompilerParams(
            dimension_semantics=("arbitrary",), vmem_limit_bytes=VMEM_LIMIT),
        name="attn_out",
    )(sinks, qkvg, qkvg, qkvg, qkvg, qkvg, qkvg, qkvg, bias, bias, y_conv, x2d, mod,
      w_out, ln_g, ln_b)


IN_TM = 256
IN_CHUNK = 256
D_QKVG = COL_GLU_A
CONV_HALO = 32
IN_STAGE_ROWS = 256
TAP_ROWS = 16
SYNC_EVERY = 1
LAST_TAP_SYNC_CHUNK = 18


def _tap_piece(u_ref, cw_ref, cb_ref, acc_ref, c, r0):
    off = CONV_HALO - (CONV_WIDTH - 1)
    cs = slice(c * LANES, (c + 1) * LANES)
    acc = jnp.broadcast_to(cb_ref[:, cs], (TAP_ROWS, LANES))
    for k in range(CONV_WIDTH):
        acc = acc + cw_ref[0, k:k + 1, cs] * u_ref[c, r0 + off + k:r0 + off + k + TAP_ROWS, :]
    acc_ref[r0:r0 + TAP_ROWS, cs] = acc
    return _token(acc)


def _conv_tail(acc_ref, lng_ref, lnb_ref, wpw_ref, bpw_ref, gate_ref, yc_ref):
    v = acc_ref[...]
    mu = jnp.mean(v, axis=1, keepdims=True)
    d = v - mu
    var = jnp.mean(d * d, axis=1, keepdims=True)
    y = d * jax.lax.rsqrt(var + LN_EPS) * lng_ref[...] + lnb_ref[...]
    s = _silu(y).astype(BF16)
    z = jnp.dot(s, wpw_ref[...], preferred_element_type=F32) + bpw_ref[...]
    yc_ref[...] = (z * gate_ref[...]).astype(BF16)


def _inproj_conv_kernel(x_ref, shift_ref, scale_ref, w_hbm, cw_ref, cb_ref, lng_ref,
                        lnb_ref, wpw_hbm, bpw_ref, qkvg_ref, yc_ref,
                        h0_ref, h1_ref, u_ref, gc_ref, gcp_ref, acc_ref,
                        w_ref, wpw_ref, stage_ref, sem_ref):
    i = pl.program_id(0)
    tm = IN_TM
    pieces = [(c, r0) for c in range(D_CONV // LANES) for r0 in range(0, tm, TAP_ROWS)]

    @pl.when(i == 0)
    def _():
        u_ref[...] = jnp.zeros(u_ref.shape, u_ref.dtype)
        gc_ref[...] = jnp.zeros(gc_ref.shape, gc_ref.dtype)
        _load_weight_bf16(w_hbm, w_ref, stage_ref, sem_ref)
        _load_weight_bf16(wpw_hbm, wpw_ref, stage_ref, sem_ref)

    gcp_ref[...] = gc_ref[...]
    h = x_ref[...] * (1.0 + scale_ref[...]) + shift_ref[...]
    h0_ref[...] = h.astype(BF16)
    h_bufs = [h0_ref, h1_ref]
    state = {"cur": 0}

    def sync(tokens):
        src, dst = h_bufs[state["cur"]], h_bufs[1 - state["cur"]]
        _copy_after(src, dst, tokens)
        state["cur"] = 1 - state["cur"]

    def project(col):
        return jnp.dot(h_bufs[state["cur"]][...], w_ref[:, col:col + IN_CHUNK],
                       preferred_element_type=F32)

    cols = list(range(0, D_QKVG, IN_CHUNK)) + list(range(COL_G_CONV, D_IN, IN_CHUNK))
    for c in range(D_CONV // IN_CHUNK):
        cols += [COL_GLU_A + c * IN_CHUNK, COL_GLU_B + c * IN_CHUNK]
    sync_chunks = list(range(SYNC_EVERY, LAST_TAP_SYNC_CHUNK + 1, SYNC_EVERY))
    bounds = [round(s * len(pieces) / len(sync_chunks)) for s in range(len(sync_chunks) + 1)]
    glu_a = None
    pieces_done = 0
    for j, col in enumerate(cols):
        if j in sync_chunks:
            s_idx = sync_chunks.index(j)
            group = pieces[bounds[s_idx]:bounds[s_idx + 1]]
            sync([_tap_piece(u_ref, cw_ref, cb_ref, acc_ref, c, r0) for c, r0 in group])
            pieces_done = bounds[s_idx + 1]
            if pieces_done == len(pieces):
                _conv_tail(acc_ref, lng_ref, lnb_ref, wpw_ref, bpw_ref, gcp_ref, yc_ref)
        res = project(col)
        if col < COL_K:
            qkvg_ref[:, col:col + IN_CHUNK] = (res * Q_SCALE).astype(BF16)
        elif col < COL_G_ATTN:
            qkvg_ref[:, col:col + IN_CHUNK] = res.astype(BF16)
        elif col < D_QKVG:
            qkvg_ref[:, col:col + IN_CHUNK] = _silu(res).astype(BF16)
        elif col < COL_GLU_B:
            glu_a = res
        elif col < COL_G_CONV:
            u = glu_a * jax.nn.sigmoid(res)
            t0 = (col - COL_GLU_B) // LANES
            for t in range(t0, t0 + IN_CHUNK // LANES):
                assert pieces_done >= (t + 1) * (tm // TAP_ROWS)
                u_ref[t, 0:CONV_HALO, :] = u_ref[t, tm:tm + CONV_HALO, :]
                u_ref[t, CONV_HALO:CONV_HALO + tm, :] = u[:, (t - t0) * LANES:(t - t0 + 1) * LANES]
        else:
            gc_ref[:, col - COL_G_CONV:col - COL_G_CONV + IN_CHUNK] = _silu(res)


def _inproj_conv_call(x2d, mod, w_in, conv_w, conv_b, ln_g, ln_b, w_pw, b_pw):
    tm = IN_TM
    n = SEQ // tm
    cur = lambda i: jnp.minimum(i, n - 1)
    lag = lambda i: jnp.maximum(i - 1, 0)
    row = lambda i: (0, 0)
    return pl.pallas_call(
        _inproj_conv_kernel,
        grid=(n + 1,),
        in_specs=[
            pl.BlockSpec((tm, D_MODEL), lambda i: (cur(i), 0)),
            pl.BlockSpec((1, D_MODEL), lambda i: (0, 0)),
            pl.BlockSpec((1, D_MODEL), lambda i: (0, 1)),
            pl.BlockSpec(memory_space=pl.ANY),
            pl.BlockSpec((DEPTH, CONV_WIDTH, D_CONV), lambda i: (0, 0, 0)),
            pl.BlockSpec((1, D_CONV), row),
            pl.BlockSpec((1, D_CONV), row),
            pl.BlockSpec((1, D_CONV), row),
            pl.BlockSpec(memory_space=pl.ANY),
            pl.BlockSpec((1, D_CONV), row),
        ],
        out_specs=[
            pl.BlockSpec((tm, D_QKVG), lambda i: (cur(i), 0)),
            pl.BlockSpec((tm, D_CONV), lambda i: (lag(i), 0)),
        ],
        out_shape=[
            jax.ShapeDtypeStruct((SEQ, D_QKVG), BF16),
            jax.ShapeDtypeStruct((SEQ, D_CONV), BF16),
        ],
        scratch_shapes=[
            pltpu.VMEM((tm, D_MODEL), BF16),
            pltpu.VMEM((tm, D_MODEL), BF16),
            pltpu.VMEM((D_CONV // LANES, CONV_HALO + tm, LANES), F32),
            pltpu.VMEM((tm, D_CONV), F32),
            pltpu.VMEM((tm, D_CONV), F32),
            pltpu.VMEM((tm, D_CONV), F32),
            pltpu.VMEM((D_MODEL, D_IN), BF16),
            pltpu.VMEM((D_CONV, D_CONV), BF16),
            pltpu.VMEM((2, IN_STAGE_ROWS, D_IN), F32),
            pltpu.SemaphoreType.DMA((2,)),
        ],
        compiler_params=pltpu.CompilerParams(
            dimension_semantics=("arbitrary",), vmem_limit_bytes=VMEM_LIMIT),
        name="inproj_conv",
    )(x2d, mod, mod, w_in, conv_w, conv_b, ln_g, ln_b, w_pw, b_pw)


def kernel(x, c, w_ada, b_ada, w_in, rel_bias, sinks, conv_w, conv_b, conv_ln_g,
           conv_ln_b, w_pw, b_pw, w_out, ln_g, ln_b):
    assert x.shape == (1, SEQ, D_MODEL) and w_in.shape == (DEPTH, D_MODEL, D_IN)
    x2d = x.reshape(SEQ, D_MODEL)

    row = lambda v: v.reshape(1, -1)
    mod = _mod_call(row(c), w_ada.reshape(D_MODEL, 3 * D_MODEL), row(b_ada))
    bias = _bias_call(rel_bias)
    qkvg, y_conv = _inproj_conv_call(
        x2d, mod, w_in.reshape(D_MODEL, D_IN),
        conv_w, row(conv_b), row(conv_ln_g), row(conv_ln_b),
        w_pw.reshape(D_CONV, D_CONV), row(b_pw))
    out = _attn_out_call(qkvg, bias, sinks.reshape(N_Q_HEADS), y_conv, x2d, mod,
                         w_out.reshape(D_MODEL, D_MODEL), row(ln_g), row(ln_b))
    return out.reshape(1, SEQ, D_MODEL)
```

```python
import math

import jax
import jax.numpy as jnp
import numpy as np
from jax.experimental import pallas as pl
from jax.experimental.pallas import tpu as pltpu

F32 = jnp.float32
BF16 = jnp.bfloat16

D_MODEL = 2048
SEQ = 8192
HEAD_DIM = 64
D_ATTN = 1024
D_CONV = 1024
N_Q_HEADS = 16
N_KV_HEADS = 4
GQA_GROUP = 4
D_KV = 256
WINDOW = 128
BLOCK = 128
CONV_WIDTH = 31
N_BUCKETS = 32
MAX_DISTANCE = 128
LN_EPS = 1e-5
DEPTH = 1
ALPHA = (2.0 * DEPTH) ** 0.25
D_IN = 2 * D_ATTN + 2 * D_KV + 3 * D_CONV

LANES = 128
SUBLANES = 8

COL_Q = 0
COL_K = COL_Q + D_ATTN
COL_V = COL_K + D_KV
COL_G_ATTN = COL_V + D_KV
COL_GLU_A = COL_G_ATTN + D_ATTN
COL_GLU_B = COL_GLU_A + D_CONV
COL_G_CONV = COL_GLU_B + D_CONV
HALF = 512

VMEM_LIMIT = 56 * 1024 * 1024


def _silu(v):
    return v * jax.nn.sigmoid(v)


MOD_STAGE_ROWS = 256


def _mod_kernel(c_ref, w_hbm, b_ref, o_ref, ca_ref, stage_ref, sem_ref):
    n_out = o_ref.shape[1]
    n = D_MODEL // MOD_STAGE_ROWS

    def chunk_copy(j):
        slot = j % 2
        return pltpu.make_async_copy(
            w_hbm.at[pl.ds(j * MOD_STAGE_ROWS, MOD_STAGE_ROWS), :],
            stage_ref.at[slot], sem_ref.at[slot])

    chunk_copy(0).start()
    ca = _silu(c_ref[...])
    ca_ref[...] = jnp.transpose(jnp.broadcast_to(ca, (LANES, D_MODEL)))
    acc = jnp.zeros((SUBLANES, n_out), F32)
    for j in range(n):
        if j + 1 < n:
            chunk_copy(j + 1).start()
        chunk_copy(j).wait()
        rows = slice(j * MOD_STAGE_ROWS, (j + 1) * MOD_STAGE_ROWS)
        prod = jnp.tile(ca_ref[rows, :], (1, n_out // LANES)) * stage_ref[j % 2]
        for r in range(0, MOD_STAGE_ROWS, SUBLANES):
            acc = acc + prod[r:r + SUBLANES, :]
    o_ref[...] = jnp.sum(acc, axis=0, keepdims=True) + b_ref[...]


def _mod_call(c_row, w_ada, b_ada):
    n = w_ada.shape[1]
    return pl.pallas_call(
        _mod_kernel,
        in_specs=[
            pl.BlockSpec((1, D_MODEL), lambda: (0, 0)),
            pl.BlockSpec(memory_space=pl.ANY),
            pl.BlockSpec((1, n), lambda: (0, 0)),
        ],
        out_specs=pl.BlockSpec((1, n), lambda: (0, 0)),
        out_shape=jax.ShapeDtypeStruct((1, n), F32),
        scratch_shapes=[
            pltpu.VMEM((D_MODEL, LANES), F32),
            pltpu.VMEM((2, MOD_STAGE_ROWS, n), F32),
            pltpu.SemaphoreType.DMA((2,)),
        ],
        compiler_params=pltpu.CompilerParams(vmem_limit_bytes=VMEM_LIMIT),
        name="mod",
    )(c_row, w_ada, b_ada)


LOG2E = math.log2(math.e)
Q_SCALE = LOG2E * HEAD_DIM ** -0.5
MASKED = -1e30


def _lane_buckets():
    m = np.arange(2 * BLOCK)
    dist = BLOCK - m
    ok = (dist >= 0) & (dist < WINDOW)
    max_exact = N_BUCKETS // 2
    d = np.maximum(dist, 1).astype(np.float64)
    large = max_exact + (np.log(d / max_exact) / math.log(MAX_DISTANCE / max_exact)
                         * (N_BUCKETS - max_exact)).astype(np.int32)
    large = np.minimum(large, N_BUCKETS - 1)
    bucket = np.where(dist < max_exact, dist, large)
    return np.where(ok, bucket, -1).astype(np.int32).reshape(1, -1)


def _bias_kernel(rb_ref, bm_ref, o_ref):
    bm = bm_ref[...]
    col = jax.lax.broadcasted_iota(jnp.int32, (BLOCK, 2 * BLOCK), 1)
    for h in range(N_Q_HEADS):
        t = jnp.full(bm.shape, MASKED, F32)
        for b in range(N_BUCKETS):
            t = jnp.where(bm == b, rb_ref[b, h] * LOG2E, t)
        x = pltpu.roll(jnp.broadcast_to(t, (BLOCK, 2 * BLOCK)), 0, 1, stride=1, stride_axis=0)
        o_ref[1, h] = x
        o_ref[0, h] = jnp.where(col >= BLOCK, x, MASKED)


def _bias_call(rel_bias):
    shape = (2, N_Q_HEADS, BLOCK, 2 * BLOCK)
    return pl.pallas_call(
        _bias_kernel,
        in_specs=[
            pl.BlockSpec(memory_space=pltpu.SMEM),
            pl.BlockSpec((1, 2 * BLOCK), lambda: (0, 0)),
        ],
        out_specs=pl.BlockSpec(shape, lambda: (0, 0, 0, 0)),
        out_shape=jax.ShapeDtypeStruct(shape, F32),
        name="bias",
    )(rel_bias, jnp.asarray(_lane_buckets()))


def _load_weight_bf16(w_hbm, w_ref, stage_ref, sem_ref, col_scale=None):
    rows, cols = w_ref.shape
    chunk_rows = stage_ref.shape[1]
    n = rows // chunk_rows

    def chunk_copy(j):
        slot = j % 2
        return pltpu.make_async_copy(
            w_hbm.at[pl.ds(j * chunk_rows, chunk_rows), :],
            stage_ref.at[slot, :, pl.ds(0, cols)], sem_ref.at[slot])

    chunk_copy(0).start()
    for j in range(n):
        if j + 1 < n:
            chunk_copy(j + 1).start()
        chunk_copy(j).wait()
        chunk = stage_ref[j % 2, :, 0:cols]
        if col_scale is not None:
            chunk = chunk * col_scale
        w_ref[j * chunk_rows:(j + 1) * chunk_rows, :] = chunk.astype(BF16)


def _token(v):
    t = v[0:SUBLANES, 0:LANES]
    for r in range(SUBLANES, v.shape[0], SUBLANES):
        t = t + v[r:r + SUBLANES, 0:LANES]
    return t


def _zero_of(token):
    return (pltpu.bitcast(token, jnp.uint32) >> 16) >> 16


def _copy_after(src_ref, dst_ref, tokens):
    zero = _zero_of(tokens[0])
    for t in tokens[1:]:
        zero = zero | _zero_of(t)
    bits = pltpu.bitcast(src_ref[...], jnp.uint32)
    reps = (bits.shape[0] // zero.shape[0], bits.shape[1] // zero.shape[1])
    dst_ref[...] = pltpu.bitcast(bits | jnp.tile(zero, reps), dst_ref.dtype)


ATT_TM = 4 * BLOCK
ATT_ROWS = BLOCK
OUT_STAGE_ROWS = 512


def _dup_halves(t):
    swapped = jnp.concatenate([t[:, HEAD_DIM:], t[:, :HEAD_DIM]], axis=1)
    lane = jax.lax.broadcasted_iota(jnp.int32, t.shape, 1)
    low_half = lane < HEAD_DIM
    return jnp.where(low_half, t, swapped), jnp.where(low_half, swapped, t)


def _attn_group(q_ref, g0_ref, g1_ref, k_dup, v_dup, bias_ref, sinks_ref, r0, b0, kv, ya_ref):
    nr = ATT_ROWS
    lane = jax.lax.broadcasted_iota(jnp.int32, (nr, LANES), 1)
    low_half = lane < HEAD_DIM
    rows = slice(r0, r0 + nr)
    tiles = [GQA_GROUP * kv // 2 + t for t in range(GQA_GROUP // 2)]
    stacked = []
    for tile in tiles:
        q_tile = q_ref[rows, tile * LANES:(tile + 1) * LANES]
        zero = jnp.zeros_like(q_tile)
        stacked += [jnp.where(low_half, q_tile, zero), jnp.where(low_half, zero, q_tile)]
    s_all = jax.lax.dot_general(jnp.concatenate(stacked, axis=0), k_dup,
                                (((1,), (1,)), ((), ())),
                                preferred_element_type=F32)
    probs, denoms = [], []
    for r in range(GQA_GROUP):
        head = GQA_GROUP * kv + r
        s = s_all[r * nr:(r + 1) * nr] + bias_ref[0, head, b0:b0 + nr, :]
        sink = sinks_ref[head] * LOG2E
        m = jnp.max(s, axis=1, keepdims=True)
        e = jnp.exp2(s - m)
        denoms.append(jnp.sum(e, axis=1, keepdims=True) + jnp.exp2(sink - m))
        probs.append(e.astype(BF16))
    o_all = jnp.dot(jnp.concatenate(probs, axis=0), v_dup,
                    preferred_element_type=F32)
    outs = [o_all[r * nr:(r + 1) * nr] / denoms[r] for r in range(GQA_GROUP)]
    for t, tile in enumerate(tiles):
        y = jnp.where(low_half, outs[2 * t], outs[2 * t + 1])
        g_ref = g0_ref if tile < HALF // LANES else g1_ref
        gl = (tile * LANES) % HALF
        gate = g_ref[rows, gl:gl + LANES].astype(F32)
        ya_ref[rows, tile * LANES:(tile + 1) * LANES] = (y * gate).astype(BF16)


def _attn_tile(sinks_ref, q_ref, g0_ref, g1_ref, k_ref, v_ref, kp_ref, vp_ref,
               bias_refs, ya_ref):
    n_lt = D_KV // LANES
    lt = lambda t: slice(t * LANES, (t + 1) * LANES)

    def block_operands(b):
        r0 = b * BLOCK
        if b == 0:
            k_prev = [kp_ref[:, lt(t)] for t in range(n_lt)]
            v_prev = [vp_ref[:, lt(t)] for t in range(n_lt)]
        else:
            k_prev = [k_ref[r0 - BLOCK:r0, lt(t)] for t in range(n_lt)]
            v_prev = [v_ref[r0 - BLOCK:r0, lt(t)] for t in range(n_lt)]
        k_rows = [jnp.concatenate([k_prev[t], k_ref[r0:r0 + BLOCK, lt(t)]], axis=0)
                  for t in range(n_lt)]
        v_rows = [jnp.concatenate([v_prev[t], v_ref[r0:r0 + BLOCK, lt(t)]], axis=0)
                  for t in range(n_lt)]
        return ([_dup_halves(kt) for kt in k_rows], [_dup_halves(vt) for vt in v_rows])

    for b in range(ATT_TM // BLOCK):
        k_dups, v_dups = block_operands(b)
        for b0 in range(0, BLOCK, ATT_ROWS):
            for kv in range(N_KV_HEADS):
                _attn_group(q_ref, g0_ref, g1_ref, k_dups[kv // 2][kv % 2],
                            v_dups[kv // 2][kv % 2], bias_refs[b], sinks_ref,
                            b * BLOCK + b0, b0, kv, ya_ref)


def _attn_out_kernel(sinks_ref, q_ref, g0_ref, g1_ref, k_ref, v_ref, kp_ref, vp_ref,
                     bias0_ref, bias1_ref, yc_ref, x_ref, gate_ref, wout_hbm,
                     lng_ref, lnb_ref, o_ref, ya_ref, wout_ref, stage_ref, sem_ref):
    i = pl.program_id(0)

    @pl.when(i == 0)
    def _():
        ya_ref[...] = jnp.zeros(ya_ref.shape, ya_ref.dtype)
        _load_weight_bf16(wout_hbm, wout_ref, stage_ref, sem_ref,
                          col_scale=gate_ref[...] * (1.0 / ALPHA))

    y = jnp.dot(ya_ref[...], wout_ref[0:D_ATTN, :], preferred_element_type=F32)
    y = y + jnp.dot(yc_ref[...], wout_ref[D_ATTN:D_MODEL, :], preferred_element_type=F32)
    z = x_ref[...] + y
    mu = jnp.mean(z, axis=1, keepdims=True)
    d = z - mu
    var = jnp.mean(d * d, axis=1, keepdims=True)
    o_ref[...] = d * jax.lax.rsqrt(var + LN_EPS / ALPHA ** 2) * lng_ref[...] + lnb_ref[...]

    _attn_tile(sinks_ref, q_ref, g0_ref, g1_ref, k_ref, v_ref, kp_ref, vp_ref,
               [bias0_ref] + [bias1_ref] * (ATT_TM // BLOCK - 1), ya_ref)


def _attn_out_call(qkvg, bias, sinks, y_conv, x2d, mod, w_out, ln_g, ln_b):
    tm = ATT_TM
    n = SEQ // tm
    clamp = lambda t: jnp.clip(t, 0, n - 1)
    cur = lambda i: clamp(i)
    lag = lambda i: clamp(i - 1)
    kv_prev = lambda i: jnp.maximum(cur(i) * (tm // BLOCK) - 1, 0)
    row = lambda i: (0, 0)
    return pl.pallas_call(
        _attn_out_kernel,
        grid=(n + 1,),
        in_specs=[
            pl.BlockSpec(memory_space=pltpu.SMEM),
            pl.BlockSpec((tm, D_ATTN), lambda i: (cur(i), COL_Q // D_ATTN)),
            pl.BlockSpec((tm, HALF), lambda i: (cur(i), COL_G_ATTN // HALF)),
            pl.BlockSpec((tm, HALF), lambda i: (cur(i), COL_G_ATTN // HALF + 1)),
            pl.BlockSpec((tm, D_KV), lambda i: (cur(i), COL_K // D_KV)),
            pl.BlockSpec((tm, D_KV), lambda i: (cur(i), COL_V // D_KV)),
            pl.BlockSpec((BLOCK, D_KV), lambda i: (kv_prev(i), COL_K // D_KV)),
            pl.BlockSpec((BLOCK, D_KV), lambda i: (kv_prev(i), COL_V // D_KV)),
            pl.BlockSpec((1, N_Q_HEADS, BLOCK, 2 * BLOCK),
                         lambda i: (jnp.minimum(cur(i), 1), 0, 0, 0)),
            pl.BlockSpec((1, N_Q_HEADS, BLOCK, 2 * BLOCK), lambda i: (1, 0, 0, 0)),
            pl.BlockSpec((tm, D_CONV), lambda i: (lag(i), 0)),
            pl.BlockSpec((tm, D_MODEL), lambda i: (lag(i), 0)),
            pl.BlockSpec((1, D_MODEL), lambda i: (0, 2)),
            pl.BlockSpec(memory_space=pl.ANY),
            pl.BlockSpec((1, D_MODEL), row),
            pl.BlockSpec((1, D_MODEL), row),
        ],
        out_specs=pl.BlockSpec((tm, D_MODEL), lambda i: (lag(i), 0)),
        out_shape=jax.ShapeDtypeStruct((SEQ, D_MODEL), F32),
        scratch_shapes=[
            pltpu.VMEM((tm, D_ATTN), BF16),
            pltpu.VMEM((D_MODEL, D_MODEL), BF16),
            pltpu.VMEM((2, OUT_STAGE_ROWS, D_MODEL), F32),
            pltpu.SemaphoreType.DMA((2,)),
        ],
        compiler_params=pltpu.CompilerParams(
            dimension_semantics=("arbitrary",), vmem_limit_bytes=VMEM_LIMIT),
        name="attn_out",
    )(sinks, qkvg, qkvg, qkvg, qkvg, qkvg, qkvg, qkvg, bias, bias, y_conv, x2d, mod,
      w_out, ln_g, ln_b)


IN_TM = 256
IN_CHUNK = 256
D_QKVG = COL_GLU_A
CONV_HALO = 32
IN_STAGE_ROWS = 256
TAP_ROWS = 32
SYNC_EVERY = 1
LAST_TAP_SYNC_CHUNK = 20


def _tap_piece(u_ref, cw_ref, cb_ref, acc_ref, c, r0):
    off = CONV_HALO - (CONV_WIDTH - 1)
    cs = slice(c * LANES, (c + 1) * LANES)
    acc = jnp.broadcast_to(cb_ref[:, cs], (TAP_ROWS, LANES))
    for k in range(CONV_WIDTH):
        acc = acc + cw_ref[0, k:k + 1, cs] * u_ref[c, r0 + off + k:r0 + off + k + TAP_ROWS, :]
    acc_ref[r0:r0 + TAP_ROWS, cs] = acc
    return _token(acc)


def _conv_tail(acc_ref, lng_ref, lnb_ref, wpw_ref, bpw_ref, gate_ref, yc_ref):
    v = acc_ref[...]
    mu = jnp.mean(v, axis=1, keepdims=True)
    d = v - mu
    var = jnp.mean(d * d, axis=1, keepdims=True)
    y = d * jax.lax.rsqrt(var + LN_EPS) * lng_ref[...] + lnb_ref[...]
    s = _silu(y).astype(BF16)
    z = jnp.dot(s, wpw_ref[...], preferred_element_type=F32) + bpw_ref[...]
    yc_ref[...] = (z * gate_ref[...]).astype(BF16)


def _inproj_conv_kernel(x_ref, shift_ref, scale_ref, w_hbm, cw_ref, cb_ref, lng_ref,
                        lnb_ref, wpw_hbm, bpw_ref, qkvg_ref, yc_ref,
                        h0_ref, h1_ref, u_ref, gc_ref, gcp_ref, acc_ref,
                        w_ref, wpw_ref, stage_ref, sem_ref):
    i = pl.program_id(0)
    tm = IN_TM
    pieces = [(c, r0) for c in range(D_CONV // LANES) for r0 in range(0, tm, TAP_ROWS)]

    @pl.when(i == 0)
    def _():
        u_ref[...] = jnp.zeros(u_ref.shape, u_ref.dtype)
        gc_ref[...] = jnp.zeros(gc_ref.shape, gc_ref.dtype)
        _load_weight_bf16(w_hbm, w_ref, stage_ref, sem_ref)
        _load_weight_bf16(wpw_hbm, wpw_ref, stage_ref, sem_ref)

    gcp_ref[...] = gc_ref[...]
    h = x_ref[...] * (1.0 + scale_ref[...]) + shift_ref[...]
    h0_ref[...] = h.astype(BF16)
    h_bufs = [h0_ref, h1_ref]
    state = {"cur": 0}

    def sync(tokens):
        src, dst = h_bufs[state["cur"]], h_bufs[1 - state["cur"]]
        _copy_after(src, dst, tokens)
        state["cur"] = 1 - state["cur"]

    def project(col):
        return jnp.dot(h_bufs[state["cur"]][...], w_ref[:, col:col + IN_CHUNK],
                       preferred_element_type=F32)

    cols = list(range(0, D_QKVG, IN_CHUNK)) + list(range(COL_G_CONV, D_IN, IN_CHUNK))
    for c in range(D_CONV // IN_CHUNK):
        cols += [COL_GLU_A + c * IN_CHUNK, COL_GLU_B + c * IN_CHUNK]
    sync_chunks = list(range(SYNC_EVERY, LAST_TAP_SYNC_CHUNK + 1, SYNC_EVERY))
    bounds = [round(s * len(pieces) / len(sync_chunks)) for s in range(len(sync_chunks) + 1)]
    glu_a = None
    pieces_done = 0
    for j, col in enumerate(cols):
        if j in sync_chunks:
            s_idx = sync_chunks.index(j)
            group = pieces[bounds[s_idx]:bounds[s_idx + 1]]
            sync([_tap_piece(u_ref, cw_ref, cb_ref, acc_ref, c, r0) for c, r0 in group])
            pieces_done = bounds[s_idx + 1]
            if pieces_done == len(pieces):
                _conv_tail(acc_ref, lng_ref, lnb_ref, wpw_ref, bpw_ref, gcp_ref, yc_ref)
        res = project(col)
        if col < COL_K:
            qkvg_ref[:, col:col + IN_CHUNK] = (res * Q_SCALE).astype(BF16)
        elif col < COL_G_ATTN:
            qkvg_ref[:, col:col + IN_CHUNK] = res.astype(BF16)
        elif col < D_QKVG:
            qkvg_ref[:, col:col + IN_CHUNK] = _silu(res).astype(BF16)
        elif col < COL_GLU_B:
            glu_a = res
        elif col < COL_G_CONV:
            u = glu_a * jax.nn.sigmoid(res)
            t0 = (col - COL_GLU_B) // LANES
            for t in range(t0, t0 + IN_CHUNK // LANES):
                assert pieces_done >= (t + 1) * (tm // TAP_ROWS)
                u_ref[t, 0:CONV_HALO, :] = u_ref[t, tm:tm + CONV_HALO, :]
                u_ref[t, CONV_HALO:CONV_HALO + tm, :] = u[:, (t - t0) * LANES:(t - t0 + 1) * LANES]
        else:
            gc_ref[:, col - COL_G_CONV:col - COL_G_CONV + IN_CHUNK] = _silu(res)


def _inproj_conv_call(x2d, mod, w_in, conv_w, conv_b, ln_g, ln_b, w_pw, b_pw):
    tm = IN_TM
    n = SEQ // tm
    cur = lambda i: jnp.minimum(i, n - 1)
    lag = lambda i: jnp.maximum(i - 1, 0)
    row = lambda i: (0, 0)
    return pl.pallas_call(
        _inproj_conv_kernel,
        grid=(n + 1,),
        in_specs=[
            pl.BlockSpec((tm, D_MODEL), lambda i: (cur(i), 0)),
            pl.BlockSpec((1, D_MODEL), lambda i: (0, 0)),
            pl.BlockSpec((1, D_MODEL), lambda i: (0, 1)),
            pl.BlockSpec(memory_space=pl.ANY),
            pl.BlockSpec((DEPTH, CONV_WIDTH, D_CONV), lambda i: (0, 0, 0)),
            pl.BlockSpec((1, D_CONV), row),
            pl.BlockSpec((1, D_CONV), row),
            pl.BlockSpec((1, D_CONV), row),
            pl.BlockSpec(memory_space=pl.ANY),
            pl.BlockSpec((1, D_CONV), row),
        ],
        out_specs=[
            pl.BlockSpec((tm, D_QKVG), lambda i: (cur(i), 0)),
            pl.BlockSpec((tm, D_CONV), lambda i: (lag(i), 0)),
        ],
        out_shape=[
            jax.ShapeDtypeStruct((SEQ, D_QKVG), BF16),
            jax.ShapeDtypeStruct((SEQ, D_CONV), BF16),
        ],
        scratch_shapes=[
            pltpu.VMEM((tm, D_MODEL), BF16),
            pltpu.VMEM((tm, D_MODEL), BF16),
            pltpu.VMEM((D_CONV // LANES, CONV_HALO + tm, LANES), F32),
            pltpu.VMEM((tm, D_CONV), F32),
            pltpu.VMEM((tm, D_CONV), F32),
            pltpu.VMEM((tm, D_CONV), F32),
            pltpu.VMEM((D_MODEL, D_IN), BF16),
            pltpu.VMEM((D_CONV, D_CONV), BF16),
            pltpu.VMEM((2, IN_STAGE_ROWS, D_IN), F32),
            pltpu.SemaphoreType.DMA((2,)),
        ],
        compiler_params=pltpu.CompilerParams(
            dimension_semantics=("arbitrary",), vmem_limit_bytes=VMEM_LIMIT),
        name="inproj_conv",
    )(x2d, mod, mod, w_in, conv_w, conv_b, ln_g, ln_b, w_pw, b_pw)


def kernel(x, c, w_ada, b_ada, w_in, rel_bias, sinks, conv_w, conv_b, conv_ln_g,
           conv_ln_b, w_pw, b_pw, w_out, ln_g, ln_b):
    assert x.shape == (1, SEQ, D_MODEL) and w_in.shape == (DEPTH, D_MODEL, D_IN)
    x2d = x.reshape(SEQ, D_MODEL)

    row = lambda v: v.reshape(1, -1)
    mod = _mod_call(row(c), w_ada.reshape(D_MODEL, 3 * D_MODEL), row(b_ada))
    bias = _bias_call(rel_bias)
    qkvg, y_conv = _inproj_conv_call(
        x2d, mod, w_in.reshape(D_MODEL, D_IN),
        conv_w, row(conv_b), row(conv_ln_g), row(conv_ln_b),
        w_pw.reshape(D_CONV, D_CONV), row(b_pw))
    out = _attn_out_call(qkvg, bias, sinks.reshape(N_Q_HEADS), y_conv, x2d, mod,
                         w_out.reshape(D_MODEL, D_MODEL), row(ln_g), row(ln_b))
    return out.reshape(1, SEQ, D_MODEL)
```

```python
import math

import jax
import jax.numpy as jnp
import numpy as np
from jax.experimental import pallas as pl
from jax.experimental.pallas import tpu as pltpu

F32 = jnp.float32
BF16 = jnp.bfloat16

D_MODEL = 2048
SEQ = 8192
HEAD_DIM = 64
D_ATTN = 1024
D_CONV = 1024
N_Q_HEADS = 16
N_KV_HEADS = 4
GQA_GROUP = 4
D_KV = 256
WINDOW = 128
BLOCK = 128
CONV_WIDTH = 31
N_BUCKETS = 32
MAX_DISTANCE = 128
LN_EPS = 1e-5
DEPTH = 1
ALPHA = (2.0 * DEPTH) ** 0.25
D_IN = 2 * D_ATTN + 2 * D_KV + 3 * D_CONV

LANES = 128
SUBLANES = 8

COL_Q = 0
COL_K = COL_Q + D_ATTN
COL_V = COL_K + D_KV
COL_G_ATTN = COL_V + D_KV
COL_GLU_A = COL_G_ATTN + D_ATTN
COL_GLU_B = COL_GLU_A + D_CONV
COL_G_CONV = COL_GLU_B + D_CONV
HALF = 512

VMEM_LIMIT = 56 * 1024 * 1024


def _silu(v):
    return v * jax.nn.sigmoid(v)


MOD_STAGE_ROWS = 256


def _mod_kernel(c_ref, w_hbm, b_ref, o_ref, ca_ref, stage_ref, sem_ref):
    n_out = o_ref.shape[1]
    n = D_MODEL // MOD_STAGE_ROWS

    def chunk_copy(j):
        slot = j % 2
        return pltpu.make_async_copy(
            w_hbm.at[pl.ds(j * MOD_STAGE_ROWS, MOD_STAGE_ROWS), :],
            stage_ref.at[slot], sem_ref.at[slot])

    chunk_copy(0).start()
    ca = _silu(c_ref[...])
    ca_ref[...] = jnp.transpose(jnp.broadcast_to(ca, (LANES, D_MODEL)))
    acc = jnp.zeros((SUBLANES, n_out), F32)
    for j in range(n):
        if j + 1 < n:
            chunk_copy(j + 1).start()
        chunk_copy(j).wait()
        rows = slice(j * MOD_STAGE_ROWS, (j + 1) * MOD_STAGE_ROWS)
        prod = jnp.tile(ca_ref[rows, :], (1, n_out // LANES)) * stage_ref[j % 2]
        for r in range(0, MOD_STAGE_ROWS, SUBLANES):
            acc = acc + prod[r:r + SUBLANES, :]
    o_ref[...] = jnp.sum(acc, axis=0, keepdims=True) + b_ref[...]


def _mod_call(c_row, w_ada, b_ada):
    n = w_ada.shape[1]
    return pl.pallas_call(
        _mod_kernel,
        in_specs=[
            pl.BlockSpec((1, D_MODEL), lambda: (0, 0)),
            pl.BlockSpec(memory_space=pl.ANY),
            pl.BlockSpec((1, n), lambda: (0, 0)),
        ],
        out_specs=pl.BlockSpec((1, n), lambda: (0, 0)),
        out_shape=jax.ShapeDtypeStruct((1, n), F32),
        scratch_shapes=[
            pltpu.VMEM((D_MODEL, LANES), F32),
            pltpu.VMEM((2, MOD_STAGE_ROWS, n), F32),
            pltpu.SemaphoreType.DMA((2,)),
        ],
        compiler_params=pltpu.CompilerParams(vmem_limit_bytes=VMEM_LIMIT),
        name="mod",
    )(c_row, w_ada, b_ada)


LOG2E = math.log2(math.e)
Q_SCALE = LOG2E * HEAD_DIM ** -0.5
MASKED = -1e30


def _lane_buckets():
    m = np.arange(2 * BLOCK)
    dist = BLOCK - m
    ok = (dist >= 0) & (dist < WINDOW)
    max_exact = N_BUCKETS // 2
    d = np.maximum(dist, 1).astype(np.float64)
    large = max_exact + (np.log(d / max_exact) / math.log(MAX_DISTANCE / max_exact)
                         * (N_BUCKETS - max_exact)).astype(np.int32)
    large = np.minimum(large, N_BUCKETS - 1)
    bucket = np.where(dist < max_exact, dist, large)
    return np.where(ok, bucket, -1).astype(np.int32).reshape(1, -1)


def _bias_kernel(rb_ref, bm_ref, o_ref):
    bm = bm_ref[...]
    col = jax.lax.broadcasted_iota(jnp.int32, (BLOCK, 2 * BLOCK), 1)
    for h in range(N_Q_HEADS):
        t = jnp.full(bm.shape, MASKED, F32)
        for b in range(N_BUCKETS):
            t = jnp.where(bm == b, rb_ref[b, h] * LOG2E, t)
        x = pltpu.roll(jnp.broadcast_to(t, (BLOCK, 2 * BLOCK)), 0, 1, stride=1, stride_axis=0)
        o_ref[1, h] = x
        o_ref[0, h] = jnp.where(col >= BLOCK, x, MASKED)


def _bias_call(rel_bias):
    shape = (2, N_Q_HEADS, BLOCK, 2 * BLOCK)
    return pl.pallas_call(
        _bias_kernel,
        in_specs=[
            pl.BlockSpec(memory_space=pltpu.SMEM),
            pl.BlockSpec((1, 2 * BLOCK), lambda: (0, 0)),
        ],
        out_specs=pl.BlockSpec(shape, lambda: (0, 0, 0, 0)),
        out_shape=jax.ShapeDtypeStruct(shape, F32),
        name="bias",
    )(rel_bias, jnp.asarray(_lane_buckets()))


def _load_weight_bf16(w_hbm, w_ref, stage_ref, sem_ref, col_scale=None):
    rows, cols = w_ref.shape
    chunk_rows = stage_ref.shape[1]
    n = rows // chunk_rows

    def chunk_copy(j):
        slot = j % 2
        return pltpu.make_async_copy(
            w_hbm.at[pl.ds(j * chunk_rows, chunk_rows), :],
            stage_ref.at[slot, :, pl.ds(0, cols)], sem_ref.at[slot])

    chunk_copy(0).start()
    for j in range(n):
        if j + 1 < n:
            chunk_copy(j + 1).start()
        chunk_copy(j).wait()
        chunk = stage_ref[j % 2, :, 0:cols]
        if col_scale is not None:
            chunk = chunk * col_scale
        w_ref[j * chunk_rows:(j + 1) * chunk_rows, :] = chunk.astype(BF16)


def _token(v):
    t = v[0:SUBLANES, 0:LANES]
    for r in range(SUBLANES, v.shape[0], SUBLANES):
        t = t + v[r:r + SUBLANES, 0:LANES]
    return t


def _zero_of(token):
    return (pltpu.bitcast(token, jnp.uint32) >> 16) >> 16


def _copy_after(src_ref, dst_ref, tokens):
    zero = _zero_of(tokens[0])
    for t in tokens[1:]:
        zero = zero | _zero_of(t)
    bits = pltpu.bitcast(src_ref[...], jnp.uint32)
    reps = (bits.shape[0] // zero.shape[0], bits.shape[1] // zero.shape[1])
    dst_ref[...] = pltpu.bitcast(bits | jnp.tile(zero, reps), dst_ref.dtype)


ATT_TM = 4 * BLOCK
ATT_ROWS = BLOCK
OUT_STAGE_ROWS = 512


def _dup_halves(t):
    swapped = jnp.concatenate([t[:, HEAD_DIM:], t[:, :HEAD_DIM]], axis=1)
    lane = jax.lax.broadcasted_iota(jnp.int32, t.shape, 1)
    low_half = lane < HEAD_DIM
    return jnp.where(low_half, t, swapped), jnp.where(low_half, swapped, t)


def _attn_group(q_ref, g0_ref, g1_ref, k_dup, v_dup, bias_ref, sinks_ref, r0, b0, kv, ya_ref):
    nr = ATT_ROWS
    lane = jax.lax.broadcasted_iota(jnp.int32, (nr, LANES), 1)
    low_half = lane < HEAD_DIM
    rows = slice(r0, r0 + nr)
    tiles = [GQA_GROUP * kv // 2 + t for t in range(GQA_GROUP // 2)]
    stacked = []
    for tile in tiles:
        q_tile = q_ref[rows, tile * LANES:(tile + 1) * LANES]
        zero = jnp.zeros_like(q_tile)
        stacked += [jnp.where(low_half, q_tile, zero), jnp.where(low_half, zero, q_tile)]
    s_all = jax.lax.dot_general(jnp.concatenate(stacked, axis=0), k_dup,
                                (((1,), (1,)), ((), ())),
                                preferred_element_type=F32)
    probs, denoms = [], []
    for r in range(GQA_GROUP):
        head = GQA_GROUP * kv + r
        s = s_all[r * nr:(r + 1) * nr] + bias_ref[0, head, b0:b0 + nr, :]
        sink = sinks_ref[head] * LOG2E
        m = jnp.max(s, axis=1, keepdims=True)
        e = jnp.exp2(s - m)
        denoms.append(jnp.sum(e, axis=1, keepdims=True) + jnp.exp2(sink - m))
        probs.append(e.astype(BF16))
    o_all = jnp.dot(jnp.concatenate(probs, axis=0), v_dup,
                    preferred_element_type=F32)
    outs = [o_all[r * nr:(r + 1) * nr] / denoms[r] for r in range(GQA_GROUP)]
    for t, tile in enumerate(tiles):
        y = jnp.where(low_half, outs[2 * t], outs[2 * t + 1])
        g_ref = g0_ref if tile < HALF // LANES else g1_ref
        gl = (tile * LANES) % HALF
        gate = g_ref[rows, gl:gl + LANES].astype(F32)
        ya_ref[rows, tile * LANES:(tile + 1) * LANES] = (y * gate).astype(BF16)


def _attn_tile(sinks_ref, q_ref, g0_ref, g1_ref, k_ref, v_ref, kp_ref, vp_ref,
               bias_refs, ya_ref):
    n_lt = D_KV // LANES
    lt = lambda t: slice(t * LANES, (t + 1) * LANES)

    def block_operands(b):
        r0 = b * BLOCK
        if b == 0:
            k_prev = [kp_ref[:, lt(t)] for t in range(n_lt)]
            v_prev = [vp_ref[:, lt(t)] for t in range(n_lt)]
        else:
            k_prev = [k_ref[r0 - BLOCK:r0, lt(t)] for t in range(n_lt)]
            v_prev = [v_ref[r0 - BLOCK:r0, lt(t)] for t in range(n_lt)]
        k_rows = [jnp.concatenate([k_prev[t], k_ref[r0:r0 + BLOCK, lt(t)]], axis=0)
                  for t in range(n_lt)]
        v_rows = [jnp.concatenate([v_prev[t], v_ref[r0:r0 + BLOCK, lt(t)]], axis=0)
                  for t in range(n_lt)]
        return ([_dup_halves(kt) for kt in k_rows], [_dup_halves(vt) for vt in v_rows])

    for b in range(ATT_TM // BLOCK):
        k_dups, v_dups = block_operands(b)
        for b0 in range(0, BLOCK, ATT_ROWS):
            for kv in range(N_KV_HEADS):
                _attn_group(q_ref, g0_ref, g1_ref, k_dups[kv // 2][kv % 2],
                            v_dups[kv // 2][kv % 2], bias_refs[b], sinks_ref,
                            b * BLOCK + b0, b0, kv, ya_ref)


def _attn_out_kernel(sinks_ref, q_ref, g0_ref, g1_ref, k_ref, v_ref, kp_ref, vp_ref,
                     bias0_ref, bias1_ref, yc_ref, x_ref, gate_ref, wout_hbm,
                     lng_ref, lnb_ref, o_ref, ya_ref, wout_ref, stage_ref, sem_ref):
    i = pl.program_id(0)

    @pl.when(i == 0)
    def _():
        ya_ref[...] = jnp.zeros(ya_ref.shape, ya_ref.dtype)
        _load_weight_bf16(wout_hbm, wout_ref, stage_ref, sem_ref,
                          col_scale=gate_ref[...] * (1.0 / ALPHA))

    y = jnp.dot(ya_ref[...], wout_ref[0:D_ATTN, :], preferred_element_type=F32)
    y = y + jnp.dot(yc_ref[...], wout_ref[D_ATTN:D_MODEL, :], preferred_element_type=F32)
    z = x_ref[...] + y
    mu = jnp.mean(z, axis=1, keepdims=True)
    d = z - mu
    var = jnp.mean(d * d, axis=1, keepdims=True)
    o_ref[...] = d * jax.lax.rsqrt(var + LN_EPS / ALPHA ** 2) * lng_ref[...] + lnb_ref[...]

    _attn_tile(sinks_ref, q_ref, g0_ref, g1_ref, k_ref, v_ref, kp_ref, vp_ref,
               [bias0_ref] + [bias1_ref] * (ATT_TM // BLOCK - 1), ya_ref)


def _attn_out_call(qkvg, bias, sinks, y_conv, x2d, mod, w_out, ln_g, ln_b):
    tm = ATT_TM
    n = SEQ // tm
    clamp = lambda t: jnp.clip(t, 0, n - 1)
    cur = lambda i: clamp(i)
    lag = lambda i: clamp(i - 1)
    kv_prev = lambda i: jnp.maximum(cur(i) * (tm // BLOCK) - 1, 0)
    row = lambda i: (0, 0)
    return pl.pallas_call(
        _attn_out_kernel,
        grid=(n + 1,),
        in_specs=[
            pl.BlockSpec(memory_space=pltpu.SMEM),
            pl.BlockSpec((tm, D_ATTN), lambda i: (cur(i), COL_Q // D_ATTN)),
            pl.BlockSpec((tm, HALF), lambda i: (cur(i), COL_G_ATTN // HALF)),
            pl.BlockSpec((tm, HALF), lambda i: (cur(i), COL_G_ATTN // HALF + 1)),
            pl.BlockSpec((tm, D_KV), lambda i: (cur(i), COL_K // D_KV)),
            pl.BlockSpec((tm, D_KV), lambda i: (cur(i), COL_V // D_KV)),
            pl.BlockSpec((BLOCK, D_KV), lambda i: (kv_prev(i), COL_K // D_KV)),
            pl.BlockSpec((BLOCK, D_KV), lambda i: (kv_prev(i), COL_V // D_KV)),
            pl.BlockSpec((1, N_Q_HEADS, BLOCK, 2 * BLOCK),
                         lambda i: (jnp.minimum(cur(i), 1), 0, 0, 0)),
            pl.BlockSpec((1, N_Q_HEADS, BLOCK, 2 * BLOCK), lambda i: (1, 0, 0, 0)),
            pl.BlockSpec((tm, D_CONV), lambda i: (lag(i), 0)),
            pl.BlockSpec((tm, D_MODEL), lambda i: (lag(i), 0)),
            pl.BlockSpec((1, D_MODEL), lambda i: (0, 2)),
            pl.BlockSpec(memory_space=pl.ANY),
            pl.BlockSpec((1, D_MODEL), row),
            pl.BlockSpec((1, D_MODEL), row),
        ],
        out_specs=pl.BlockSpec((tm, D_MODEL), lambda i: (lag(i), 0)),
        out_shape=jax.ShapeDtypeStruct((SEQ, D_MODEL), F32),
        scratch_shapes=[
            pltpu.VMEM((tm, D_ATTN), BF16),
            pltpu.VMEM((D_MODEL, D_MODEL), BF16),
            pltpu.VMEM((2, OUT_STAGE_ROWS, D_MODEL), F32),
            pltpu.SemaphoreType.DMA((2,)),
        ],
        compiler_params=pltpu.CompilerParams(
            dimension_semantics=("arbitrary",), vmem_limit_bytes=VMEM_LIMIT),
        name="attn_out",
    )(sinks, qkvg, qkvg, qkvg, qkvg, qkvg, qkvg, qkvg, bias, bias, y_conv, x2d, mod,
      w_out, ln_g, ln_b)


IN_TM = 256
IN_CHUNK = 256
D_QKVG = COL_GLU_A
CONV_HALO = 32
IN_STAGE_ROWS = 256
TAP_ROWS = 32
SYNC_EVERY = 1
LAST_TAP_SYNC_CHUNK = 21


def _tap_piece(u_ref, cw_ref, cb_ref, acc_ref, c, r0):
    off = CONV_HALO - (CONV_WIDTH - 1)
    cs = slice(c * LANES, (c + 1) * LANES)
    acc = jnp.broadcast_to(cb_ref[:, cs], (TAP_ROWS, LANES))
    for k in range(CONV_WIDTH):
        acc = acc + cw_ref[0, k:k + 1, cs] * u_ref[c, r0 + off + k:r0 + off + k + TAP_ROWS, :]
    acc_ref[r0:r0 + TAP_ROWS, cs] = acc
    return _token(acc)


def _conv_tail(acc_ref, lng_ref, lnb_ref, wpw_ref, bpw_ref, gate_ref, yc_ref):
    v = acc_ref[...]
    mu = jnp.mean(v, axis=1, keepdims=True)
    d = v - mu
    var = jnp.mean(d * d, axis=1, keepdims=True)
    y = d * jax.lax.rsqrt(var + LN_EPS) * lng_ref[...] + lnb_ref[...]
    s = _silu(y).astype(BF16)
    z = jnp.dot(s, wpw_ref[...], preferred_element_type=F32) + bpw_ref[...]
    yc_ref[...] = (z * gate_ref[...]).astype(BF16)


def _inproj_conv_kernel(x_ref, shift_ref, scale_ref, w_hbm, cw_ref, cb_ref, lng_ref,
                        lnb_ref, wpw_hbm, bpw_ref, qkvg_ref, yc_ref,
                        h0_ref, h1_ref, u_ref, gc_ref, gcp_ref, acc_ref,
                        w_ref, wpw_ref, stage_ref, sem_ref):
    i = pl.program_id(0)
    tm = IN_TM
    pieces = [(c, r0) for c in range(D_CONV // LANES) for r0 in range(0, tm, TAP_ROWS)]

    @pl.when(i == 0)
    def _():
        u_ref[...] = jnp.zeros(u_ref.shape, u_ref.dtype)
        gc_ref[...] = jnp.zeros(gc_ref.shape, gc_ref.dtype)
        _load_weight_bf16(w_hbm, w_ref, stage_ref, sem_ref)
        _load_weight_bf16(wpw_hbm, wpw_ref, stage_ref, sem_ref)

    gcp_ref[...] = gc_ref[...]
    h = x_ref[...] * (1.0 + scale_ref[...]) + shift_ref[...]
    h0_ref[...] = h.astype(BF16)
    h_bufs = [h0_ref, h1_ref]
    state = {"cur": 0}

    def sync(tokens):
        src, dst = h_bufs[state["cur"]], h_bufs[1 - state["cur"]]
        _copy_after(src, dst, tokens)
        state["cur"] = 1 - state["cur"]

    def project(col):
        return jnp.dot(h_bufs[state["cur"]][...], w_ref[:, col:col + IN_CHUNK],
                       preferred_element_type=F32)

    cols = list(range(0, D_QKVG, IN_CHUNK)) + list(range(COL_G_CONV, D_IN, IN_CHUNK))
    for c in range(D_CONV // IN_CHUNK):
        cols += [COL_GLU_A + c * IN_CHUNK, COL_GLU_B + c * IN_CHUNK]
    sync_chunks = list(range(SYNC_EVERY, LAST_TAP_SYNC_CHUNK + 1, SYNC_EVERY))
    bounds = [round(s * len(pieces) / len(sync_chunks)) for s in range(len(sync_chunks) + 1)]
    glu_a = None
    pieces_done = 0
    for j, col in enumerate(cols):
        if j in sync_chunks:
            s_idx = sync_chunks.index(j)
            group = pieces[bounds[s_idx]:bounds[s_idx + 1]]
            sync([_tap_piece(u_ref, cw_ref, cb_ref, acc_ref, c, r0) for c, r0 in group])
            pieces_done = bounds[s_idx + 1]
            if pieces_done == len(pieces):
                _conv_tail(acc_ref, lng_ref, lnb_ref, wpw_ref, bpw_ref, gcp_ref, yc_ref)
        res = project(col)
        if col < COL_K:
            qkvg_ref[:, col:col + IN_CHUNK] = (res * Q_SCALE).astype(BF16)
        elif col < COL_G_ATTN:
            qkvg_ref[:, col:col + IN_CHUNK] = res.astype(BF16)
        elif col < D_QKVG:
            qkvg_ref[:, col:col + IN_CHUNK] = _silu(res).astype(BF16)
        elif col < COL_GLU_B:
            glu_a = res
        elif col < COL_G_CONV:
            u = glu_a * jax.nn.sigmoid(res)
            t0 = (col - COL_GLU_B) // LANES
            for t in range(t0, t0 + IN_CHUNK // LANES):
                assert pieces_done >= (t + 1) * (tm // TAP_ROWS)
                u_ref[t, 0:CONV_HALO, :] = u_ref[t, tm:tm + CONV_HALO, :]
                u_ref[t, CONV_HALO:CONV_HALO + tm, :] = u[:, (t - t0) * LANES:(t - t0 + 1) * LANES]
        else:
            gc_ref[:, col - COL_G_CONV:col - COL_G_CONV + IN_CHUNK] = _silu(res)


def _inproj_conv_call(x2d, mod, w_in, conv_w, conv_b, ln_g, ln_b, w_pw, b_pw):
    tm = IN_TM
    n = SEQ // tm
    cur = lambda i: jnp.minimum(i, n - 1)
    lag = lambda i: jnp.maximum(i - 1, 0)
    row = lambda i: (0, 0)
    return pl.pallas_call(
        _inproj_conv_kernel,
        grid=(n + 1,),
        in_specs=[
            pl.BlockSpec((tm, D_MODEL), lambda i: (cur(i), 0)),
            pl.BlockSpec((1, D_MODEL), lambda i: (0, 0)),
            pl.BlockSpec((1, D_MODEL), lambda i: (0, 1)),
            pl.BlockSpec(memory_space=pl.ANY),
            pl.BlockSpec((DEPTH, CONV_WIDTH, D_CONV), lambda i: (0, 0, 0)),
            pl.BlockSpec((1, D_CONV), row),
            pl.BlockSpec((1, D_CONV), row),
            pl.BlockSpec((1, D_CONV), row),
            pl.BlockSpec(memory_space=pl.ANY),
            pl.BlockSpec((1, D_CONV), row),
        ],
        out_specs=[
            pl.BlockSpec((tm, D_QKVG), lambda i: (cur(i), 0)),
            pl.BlockSpec((tm, D_CONV), lambda i: (lag(i), 0)),
        ],
        out_shape=[
            jax.ShapeDtypeStruct((SEQ, D_QKVG), BF16),
            jax.ShapeDtypeStruct((SEQ, D_CONV), BF16),
        ],
        scratch_shapes=[
            pltpu.VMEM((tm, D_MODEL), BF16),
            pltpu.VMEM((tm, D_MODEL), BF16),
            pltpu.VMEM((D_CONV // LANES, CONV_HALO + tm, LANES), F32),
            pltpu.VMEM((tm, D_CONV), F32),
            pltpu.VMEM((tm, D_CONV), F32),
            pltpu.VMEM((tm, D_CONV), F32),
            pltpu.VMEM((D_MODEL, D_IN), BF16),
            pltpu.VMEM((D_CONV, D_CONV), BF16),
            pltpu.VMEM((2, IN_STAGE_ROWS, D_IN), F32),
            pltpu.SemaphoreType.DMA((2,)),
        ],
        compiler_params=pltpu.CompilerParams(
            dimension_semantics=("arbitrary",), vmem_limit_bytes=VMEM_LIMIT),
        name="inproj_conv",
    )(x2d, mod, mod, w_in, conv_w, conv_b, ln_g, ln_b, w_pw, b_pw)


def kernel(x, c, w_ada, b_ada, w_in, rel_bias, sinks, conv_w, conv_b, conv_ln_g,
           conv_ln_b, w_pw, b_pw, w_out, ln_g, ln_b):
    assert x.shape == (1, SEQ, D_MODEL) and w_in.shape == (DEPTH, D_MODEL, D_IN)
    x2d = x.reshape(SEQ, D_MODEL)

    row = lambda v: v.reshape(1, -1)
    mod = _mod_call(row(c), w_ada.reshape(D_MODEL, 3 * D_MODEL), row(b_ada))
    bias = _bias_call(rel_bias)
    qkvg, y_conv = _inproj_conv_call(
        x2d, mod, w_in.reshape(D_MODEL, D_IN),
        conv_w, row(conv_b), row(conv_ln_g), row(conv_ln_b),
        w_pw.reshape(D_CONV, D_CONV), row(b_pw))
    out = _attn_out_call(qkvg, bias, sinks.reshape(N_Q_HEADS), y_conv, x2d, mod,
                         w_out.reshape(D_MODEL, D_MODEL), row(ln_g), row(ln_b))
    return out.reshape(1, SEQ, D_MODEL)
```

```python
import math

import jax
import jax.numpy as jnp
import numpy as np
from jax.experimental import pallas as pl
from jax.experimental.pallas import tpu as pltpu

F32 = jnp.float32
BF16 = jnp.bfloat16

D_MODEL = 2048
SEQ = 8192
HEAD_DIM = 64
D_ATTN = 1024
D_CONV = 1024
N_Q_HEADS = 16
N_KV_HEADS = 4
GQA_GROUP = 4
D_KV = 256
WINDOW = 128
BLOCK = 128
CONV_WIDTH = 31
N_BUCKETS = 32
MAX_DISTANCE = 128
LN_EPS = 1e-5
DEPTH = 1
ALPHA = (2.0 * DEPTH) ** 0.25
D_IN = 2 * D_ATTN + 2 * D_KV + 3 * D_CONV

LANES = 128
SUBLANES = 8

COL_Q = 0
COL_K = COL_Q + D_ATTN
COL_V = COL_K + D_KV
COL_G_ATTN = COL_V + D_KV
COL_GLU_A = COL_G_ATTN + D_ATTN
COL_GLU_B = COL_GLU_A + D_CONV
COL_G_CONV = COL_GLU_B + D_CONV
HALF = 512

VMEM_LIMIT = 56 * 1024 * 1024


def _silu(v):
    return v * jax.nn.sigmoid(v)


MOD_STAGE_ROWS = 256


def _mod_kernel(c_ref, w_hbm, b_ref, o_ref, ca_ref, stage_ref, sem_ref):
    n_out = o_ref.shape[1]
    n = D_MODEL // MOD_STAGE_ROWS

    def chunk_copy(j):
        slot = j % 2
        return pltpu.make_async_copy(
            w_hbm.at[pl.ds(j * MOD_STAGE_ROWS, MOD_STAGE_ROWS), :],
            stage_ref.at[slot], sem_ref.at[slot])

    chunk_copy(0).start()
    ca = _silu(c_ref[...])
    ca_ref[...] = jnp.transpose(jnp.broadcast_to(ca, (LANES, D_MODEL)))
    acc = jnp.zeros((SUBLANES, n_out), F32)
    for j in range(n):
        if j + 1 < n:
            chunk_copy(j + 1).start()
        chunk_copy(j).wait()
        rows = slice(j * MOD_STAGE_ROWS, (j + 1) * MOD_STAGE_ROWS)
        prod = jnp.tile(ca_ref[rows, :], (1, n_out // LANES)) * stage_ref[j % 2]
        for r in range(0, MOD_STAGE_ROWS, SUBLANES):
            acc = acc + prod[r:r + SUBLANES, :]
    o_ref[...] = jnp.sum(acc, axis=0, keepdims=True) + b_ref[...]


def _mod_call(c_row, w_ada, b_ada):
    n = w_ada.shape[1]
    return pl.pallas_call(
        _mod_kernel,
        in_specs=[
            pl.BlockSpec((1, D_MODEL), lambda: (0, 0)),
            pl.BlockSpec(memory_space=pl.ANY),
            pl.BlockSpec((1, n), lambda: (0, 0)),
        ],
        out_specs=pl.BlockSpec((1, n), lambda: (0, 0)),
        out_shape=jax.ShapeDtypeStruct((1, n), F32),
        scratch_shapes=[
            pltpu.VMEM((D_MODEL, LANES), F32),
            pltpu.VMEM((2, MOD_STAGE_ROWS, n), F32),
            pltpu.SemaphoreType.DMA((2,)),
        ],
        compiler_params=pltpu.CompilerParams(vmem_limit_bytes=VMEM_LIMIT),
        name="mod",
    )(c_row, w_ada, b_ada)


LOG2E = math.log2(math.e)
Q_SCALE = LOG2E * HEAD_DIM ** -0.5
MASKED = -1e30


def _lane_buckets():
    m = np.arange(2 * BLOCK)
    dist = BLOCK - m
    ok = (dist >= 0) & (dist < WINDOW)
    max_exact = N_BUCKETS // 2
    d = np.maximum(dist, 1).astype(np.float64)
    large = max_exact + (np.log(d / max_exact) / math.log(MAX_DISTANCE / max_exact)
                         * (N_BUCKETS - max_exact)).astype(np.int32)
    large = np.minimum(large, N_BUCKETS - 1)
    bucket = np.where(dist < max_exact, dist, large)
    return np.where(ok, bucket, -1).astype(np.int32).reshape(1, -1)


def _bias_kernel(rb_ref, bm_ref, o_ref):
    bm = bm_ref[...]
    col = jax.lax.broadcasted_iota(jnp.int32, (BLOCK, 2 * BLOCK), 1)
    for h in range(N_Q_HEADS):
        t = jnp.full(bm.shape, MASKED, F32)
        for b in range(N_BUCKETS):
            t = jnp.where(bm == b, rb_ref[b, h] * LOG2E, t)
        x = pltpu.roll(jnp.broadcast_to(t, (BLOCK, 2 * BLOCK)), 0, 1, stride=1, stride_axis=0)
        o_ref[1, h] = x
        o_ref[0, h] = jnp.where(col >= BLOCK, x, MASKED)


def _bias_call(rel_bias):
    shape = (2, N_Q_HEADS, BLOCK, 2 * BLOCK)
    return pl.pallas_call(
        _bias_kernel,
        in_specs=[
            pl.BlockSpec(memory_space=pltpu.SMEM),
            pl.BlockSpec((1, 2 * BLOCK), lambda: (0, 0)),
        ],
        out_specs=pl.BlockSpec(shape, lambda: (0, 0, 0, 0)),
        out_shape=jax.ShapeDtypeStruct(shape, F32),
        name="bias",
    )(rel_bias, jnp.asarray(_lane_buckets()))


def _load_weight_bf16(w_hbm, w_ref, stage_ref, sem_ref, col_scale=None):
    rows, cols = w_ref.shape
    chunk_rows = stage_ref.shape[1]
    n = rows // chunk_rows

    def chunk_copy(j):
        slot = j % 2
        return pltpu.make_async_copy(
            w_hbm.at[pl.ds(j * chunk_rows, chunk_rows), :],
            stage_ref.at[slot, :, pl.ds(0, cols)], sem_ref.at[slot])

    chunk_copy(0).start()
    for j in range(n):
        if j + 1 < n:
            chunk_copy(j + 1).start()
        chunk_copy(j).wait()
        chunk = stage_ref[j % 2, :, 0:cols]
        if col_scale is not None:
            chunk = chunk * col_scale
        w_ref[j * chunk_rows:(j + 1) * chunk_rows, :] = chunk.astype(BF16)


def _token(v):
    t = v[0:SUBLANES, 0:LANES]
    for r in range(SUBLANES, v.shape[0], SUBLANES):
        t = t + v[r:r + SUBLANES, 0:LANES]
    return t


def _zero_of(token):
    return (pltpu.bitcast(token, jnp.uint32) >> 16) >> 16


def _copy_after(src_ref, dst_ref, tokens):
    zero = _zero_of(tokens[0])
    for t in tokens[1:]:
        zero = zero | _zero_of(t)
    bits = pltpu.bitcast(src_ref[...], jnp.uint32)
    reps = (bits.shape[0] // zero.shape[0], bits.shape[1] // zero.shape[1])
    dst_ref[...] = pltpu.bitcast(bits | jnp.tile(zero, reps), dst_ref.dtype)


ATT_TM = 4 * BLOCK
ATT_ROWS = BLOCK
OUT_STAGE_ROWS = 512


def _dup_halves(t):
    swapped = jnp.concatenate([t[:, HEAD_DIM:], t[:, :HEAD_DIM]], axis=1)
    lane = jax.lax.broadcasted_iota(jnp.int32, t.shape, 1)
    low_half = lane < HEAD_DIM
    return jnp.where(low_half, t, swapped), jnp.where(low_half, swapped, t)


def _attn_group(q_ref, g0_ref, g1_ref, k_dup, v_dup, bias_ref, sinks_ref, r0, b0, kv, ya_ref):
    nr = ATT_ROWS
    lane = jax.lax.broadcasted_iota(jnp.int32, (nr, LANES), 1)
    low_half = lane < HEAD_DIM
    rows = slice(r0, r0 + nr)
    tiles = [GQA_GROUP * kv // 2 + t for t in range(GQA_GROUP // 2)]
    stacked = []
    for tile in tiles:
        q_tile = q_ref[rows, tile * LANES:(tile + 1) * LANES]
        zero = jnp.zeros_like(q_tile)
        stacked += [jnp.where(low_half, q_tile, zero), jnp.where(low_half, zero, q_tile)]
    s_all = jax.lax.dot_general(jnp.concatenate(stacked, axis=0), k_dup,
                                (((1,), (1,)), ((), ())),
                                preferred_element_type=F32)
    probs, denoms = [], []
    for r in range(GQA_GROUP):
        head = GQA_GROUP * kv + r
        s = s_all[r * nr:(r + 1) * nr] + bias_ref[0, head, b0:b0 + nr, :]
        sink = sinks_ref[head] * LOG2E
        m = jnp.max(s, axis=1, keepdims=True)
        e = jnp.exp2(s - m)
        denoms.append(jnp.sum(e, axis=1, keepdims=True) + jnp.exp2(sink - m))
        probs.append(e.astype(BF16))
    o_all = jnp.dot(jnp.concatenate(probs, axis=0), v_dup,
                    preferred_element_type=F32)
    outs = [o_all[r * nr:(r + 1) * nr] / denoms[r] for r in range(GQA_GROUP)]
    for t, tile in enumerate(tiles):
        y = jnp.where(low_half, outs[2 * t], outs[2 * t + 1])
        g_ref = g0_ref if tile < HALF // LANES else g1_ref
        gl = (tile * LANES) % HALF
        gate = g_ref[rows, gl:gl + LANES].astype(F32)
        ya_ref[rows, tile * LANES:(tile + 1) * LANES] = (y * gate).astype(BF16)


def _attn_tile(sinks_ref, q_ref, g0_ref, g1_ref, k_ref, v_ref, kp_ref, vp_ref,
               bias_refs, ya_ref):
    n_lt = D_KV // LANES
    lt = lambda t: slice(t * LANES, (t + 1) * LANES)

    def block_operands(b):
        r0 = b * BLOCK
        if b == 0:
            k_prev = [kp_ref[:, lt(t)] for t in range(n_lt)]
            v_prev = [vp_ref[:, lt(t)] for t in range(n_lt)]
        else:
            k_prev = [k_ref[r0 - BLOCK:r0, lt(t)] for t in range(n_lt)]
            v_prev = [v_ref[r0 - BLOCK:r0, lt(t)] for t in range(n_lt)]
        k_rows = [jnp.concatenate([k_prev[t], k_ref[r0:r0 + BLOCK, lt(t)]], axis=0)
                  for t in range(n_lt)]
        v_rows = [jnp.concatenate([v_prev[t], v_ref[r0:r0 + BLOCK, lt(t)]], axis=0)
                  for t in range(n_lt)]
        return ([_dup_halves(kt) for kt in k_rows], [_dup_halves(vt) for vt in v_rows])

    for b in range(ATT_TM // BLOCK):
        k_dups, v_dups = block_operands(b)
        for b0 in range(0, BLOCK, ATT_ROWS):
            for kv in range(N_KV_HEADS):
                _attn_group(q_ref, g0_ref, g1_ref, k_dups[kv // 2][kv % 2],
                            v_dups[kv // 2][kv % 2], bias_refs[b], sinks_ref,
                            b * BLOCK + b0, b0, kv, ya_ref)


def _attn_out_kernel(sinks_ref, q_ref, g0_ref, g1_ref, k_ref, v_ref, kp_ref, vp_ref,
                     bias0_ref, bias1_ref, yc_ref, x_ref, gate_ref, wout_hbm,
                     lng_ref, lnb_ref, o_ref, ya_ref, wout_ref, stage_ref, sem_ref):
    i = pl.program_id(0)

    @pl.when(i == 0)
    def _():
        ya_ref[...] = jnp.zeros(ya_ref.shape, ya_ref.dtype)
        _load_weight_bf16(wout_hbm, wout_ref, stage_ref, sem_ref,
                          col_scale=gate_ref[...] * (1.0 / ALPHA))

    y = jnp.dot(ya_ref[...], wout_ref[0:D_ATTN, :], preferred_element_type=F32)
    y = y + jnp.dot(yc_ref[...], wout_ref[D_ATTN:D_MODEL, :], preferred_element_type=F32)
    z = x_ref[...] + y
    mu = jnp.mean(z, axis=1, keepdims=True)
    d = z - mu
    var = jnp.mean(d * d, axis=1, keepdims=True)
    o_ref[...] = d * jax.lax.rsqrt(var + LN_EPS / ALPHA ** 2) * lng_ref[...] + lnb_ref[...]

    _attn_tile(sinks_ref, q_ref, g0_ref, g1_ref, k_ref, v_ref, kp_ref, vp_ref,
               [bias0_ref] + [bias1_ref] * (ATT_TM // BLOCK - 1), ya_ref)


def _attn_out_call(qkvg, bias, sinks, y_conv, x2d, mod, w_out, ln_g, ln_b):
    tm = ATT_TM
    n = SEQ // tm
    clamp = lambda t: jnp.clip(t, 0, n - 1)
    cur = lambda i: clamp(i)
    lag = lambda i: clamp(i - 1)
    kv_prev = lambda i: jnp.maximum(cur(i) * (tm // BLOCK) - 1, 0)
    row = lambda i: (0, 0)
    return pl.pallas_call(
        _attn_out_kernel,
        grid=(n + 1,),
        in_specs=[
            pl.BlockSpec(memory_space=pltpu.SMEM),
            pl.BlockSpec((tm, D_ATTN), lambda i: (cur(i), COL_Q // D_ATTN)),
            pl.BlockSpec((tm, HALF), lambda i: (cur(i), COL_G_ATTN // HALF)),
            pl.BlockSpec((tm, HALF), lambda i: (cur(i), COL_G_ATTN // HALF + 1)),
            pl.BlockSpec((tm, D_KV), lambda i: (cur(i), COL_K // D_KV)),
            pl.BlockSpec((tm, D_KV), lambda i: (cur(i), COL_V // D_KV)),
            pl.BlockSpec((BLOCK, D_KV), lambda i: (kv_prev(i), COL_K // D_KV)),
            pl.BlockSpec((BLOCK, D_KV), lambda i: (kv_prev(i), COL_V // D_KV)),
            pl.BlockSpec((1, N_Q_HEADS, BLOCK, 2 * BLOCK),
                         lambda i: (jnp.minimum(cur(i), 1), 0, 0, 0)),
            pl.BlockSpec((1, N_Q_HEADS, BLOCK, 2 * BLOCK), lambda i: (1, 0, 0, 0)),
            pl.BlockSpec((tm, D_CONV), lambda i: (lag(i), 0)),
            pl.BlockSpec((tm, D_MODEL), lambda i: (lag(i), 0)),
            pl.BlockSpec((1, D_MODEL), lambda i: (0, 2)),
            pl.BlockSpec(memory_space=pl.ANY),
            pl.BlockSpec((1, D_MODEL), row),
            pl.BlockSpec((1, D_MODEL), row),
        ],
        out_specs=pl.BlockSpec((tm, D_MODEL), lambda i: (lag(i), 0)),
        out_shape=jax.ShapeDtypeStruct((SEQ, D_MODEL), F32),
        scratch_shapes=[
            pltpu.VMEM((tm, D_ATTN), BF16),
            pltpu.VMEM((D_MODEL, D_MODEL), BF16),
            pltpu.VMEM((2, OUT_STAGE_ROWS, D_MODEL), F32),
            pltpu.SemaphoreType.DMA((2,)),
        ],
        compiler_params=pltpu.CompilerParams(
            dimension_semantics=("arbitrary",), vmem_limit_bytes=VMEM_LIMIT),
        name="attn_out",
    )(sinks, qkvg, qkvg, qkvg, qkvg, qkvg, qkvg, qkvg, bias, bias, y_conv, x2d, mod,
      w_out, ln_g, ln_b)


IN_TM = 256
IN_CHUNK = 256
D_QKVG = COL_GLU_A
CONV_HALO = 32
IN_STAGE_ROWS = 256
TAP_ROWS = 32
SYNC_EVERY = 1
LAST_TAP_SYNC_CHUNK = 19


def _tap_piece(u_ref, cw_ref, cb_ref, acc_ref, c, r0):
    off = CONV_HALO - (CONV_WIDTH - 1)
    cs = slice(c * LANES, (c + 1) * LANES)
    acc = jnp.broadcast_to(cb_ref[:, cs], (TAP_ROWS, LANES))
    for k in range(CONV_WIDTH):
        acc = acc + cw_ref[0, k:k + 1, cs] * u_ref[c, r0 + off + k:r0 + off + k + TAP_ROWS, :]
    acc_ref[r0:r0 + TAP_ROWS, cs] = acc
    return _token(acc)


def _conv_tail(acc_ref, lng_ref, lnb_ref, wpw_ref, bpw_ref, gate_ref, yc_ref):
    v = acc_ref[...]
    mu = jnp.mean(v, axis=1, keepdims=True)
    d = v - mu
    var = jnp.mean(d * d, axis=1, keepdims=True)
    y = d * jax.lax.rsqrt(var + LN_EPS) * lng_ref[...] + lnb_ref[...]
    s = _silu(y).astype(BF16)
    z = jnp.dot(s, wpw_ref[...], preferred_element_type=F32) + bpw_ref[...]
    yc_ref[...] = (z * gate_ref[...]).astype(BF16)


def _inproj_conv_kernel(x_ref, shift_ref, scale_ref, w_hbm, cw_ref, cb_ref, lng_ref,
                        lnb_ref, wpw_hbm, bpw_ref, qkvg_ref, yc_ref,
                        h0_ref, h1_ref, u_ref, gc_ref, gcp_ref, acc_ref,
                        w_ref, wpw_ref, stage_ref, sem_ref):
    i = pl.program_id(0)
    tm = IN_TM
    pieces = [(c, r0) for c in range(D_CONV // LANES) for r0 in range(0, tm, TAP_ROWS)]

    @pl.when(i == 0)
    def _():
        u_ref[...] = jnp.zeros(u_ref.shape, u_ref.dtype)
        gc_ref[...] = jnp.zeros(gc_ref.shape, gc_ref.dtype)
        _load_weight_bf16(w_hbm, w_ref, stage_ref, sem_ref)
        _load_weight_bf16(wpw_hbm, wpw_ref, stage_ref, sem_ref)

    gcp_ref[...] = gc_ref[...]
    h = x_ref[...] * (1.0 + scale_ref[...]) + shift_ref[...]
    h0_ref[...] = h.astype(BF16)
    h_bufs = [h0_ref, h1_ref]
    state = {"cur": 0}

    def sync(tokens):
        src, dst = h_bufs[state["cur"]], h_bufs[1 - state["cur"]]
        _copy_after(src, dst, tokens)
        state["cur"] = 1 - state["cur"]

    def project(col):
        return jnp.dot(h_bufs[state["cur"]][...], w_ref[:, col:col + IN_CHUNK],
                       preferred_element_type=F32)

    cols = list(range(0, D_QKVG, IN_CHUNK)) + list(range(COL_G_CONV, D_IN, IN_CHUNK))
    for c in range(D_CONV // IN_CHUNK):
        cols += [COL_GLU_A + c * IN_CHUNK, COL_GLU_B + c * IN_CHUNK]
    sync_chunks = list(range(SYNC_EVERY, LAST_TAP_SYNC_CHUNK + 1, SYNC_EVERY))
    bounds = [round(s * len(pieces) / len(sync_chunks)) for s in range(len(sync_chunks) + 1)]
    glu_a = None
    pieces_done = 0
    for j, col in enumerate(cols):
        if j in sync_chunks:
            s_idx = sync_chunks.index(j)
            group = pieces[bounds[s_idx]:bounds[s_idx + 1]]
            sync([_tap_piece(u_ref, cw_ref, cb_ref, acc_ref, c, r0) for c, r0 in group])
            pieces_done = bounds[s_idx + 1]
            if pieces_done == len(pieces):
                _conv_tail(acc_ref, lng_ref, lnb_ref, wpw_ref, bpw_ref, gcp_ref, yc_ref)
        res = project(col)
        if col < COL_K:
            qkvg_ref[:, col:col + IN_CHUNK] = (res * Q_SCALE).astype(BF16)
        elif col < COL_G_ATTN:
            qkvg_ref[:, col:col + IN_CHUNK] = res.astype(BF16)
        elif col < D_QKVG:
            qkvg_ref[:, col:col + IN_CHUNK] = _silu(res).astype(BF16)
        elif col < COL_GLU_B:
            glu_a = res
        elif col < COL_G_CONV:
            u = glu_a * jax.nn.sigmoid(res)
            t0 = (col - COL_GLU_B) // LANES
            for t in range(t0, t0 + IN_CHUNK // LANES):
                assert pieces_done >= (t + 1) * (tm // TAP_ROWS)
                u_ref[t, 0:CONV_HALO, :] = u_ref[t, tm:tm + CONV_HALO, :]
                u_ref[t, CONV_HALO:CONV_HALO + tm, :] = u[:, (t - t0) * LANES:(t - t0 + 1) * LANES]
        else:
            gc_ref[:, col - COL_G_CONV:col - COL_G_CONV + IN_CHUNK] = _silu(res)


def _inproj_conv_call(x2d, mod, w_in, conv_w, conv_b, ln_g, ln_b, w_pw, b_pw):
    tm = IN_TM
    n = SEQ // tm
    cur = lambda i: jnp.minimum(i, n - 1)
    lag = lambda i: jnp.maximum(i - 1, 0)
    row = lambda i: (0, 0)
    return pl.pallas_call(
        _inproj_conv_kernel,
        grid=(n + 1,),
        in_specs=[
            pl.BlockSpec((tm, D_MODEL), lambda i: (cur(i), 0)),
            pl.BlockSpec((1, D_MODEL), lambda i: (0, 0)),
            pl.BlockSpec((1, D_MODEL), lambda i: (0, 1)),
            pl.BlockSpec(memory_space=pl.ANY),
            pl.BlockSpec((DEPTH, CONV_WIDTH, D_CONV), lambda i: (0, 0, 0)),
            pl.BlockSpec((1, D_CONV), row),
            pl.BlockSpec((1, D_CONV), row),
            pl.BlockSpec((1, D_CONV), row),
            pl.BlockSpec(memory_space=pl.ANY),
            pl.BlockSpec((1, D_CONV), row),
        ],
        out_specs=[
            pl.BlockSpec((tm, D_QKVG), lambda i: (cur(i), 0)),
            pl.BlockSpec((tm, D_CONV), lambda i: (lag(i), 0)),
        ],
        out_shape=[
            jax.ShapeDtypeStruct((SEQ, D_QKVG), BF16),
            jax.ShapeDtypeStruct((SEQ, D_CONV), BF16),
        ],
        scratch_shapes=[
            pltpu.VMEM((tm, D_MODEL), BF16),
            pltpu.VMEM((tm, D_MODEL), BF16),
            pltpu.VMEM((D_CONV // LANES, CONV_HALO + tm, LANES), F32),
            pltpu.VMEM((tm, D_CONV), F32),
            pltpu.VMEM((tm, D_CONV), F32),
            pltpu.VMEM((tm, D_CONV), F32),
            pltpu.VMEM((D_MODEL, D_IN), BF16),
            pltpu.VMEM((D_CONV, D_CONV), BF16),
            pltpu.VMEM((2, IN_STAGE_ROWS, D_IN), F32),
            pltpu.SemaphoreType.DMA((2,)),
        ],
        compiler_params=pltpu.CompilerParams(
            dimension_semantics=("arbitrary",), vmem_limit_bytes=VMEM_LIMIT),
        name="inproj_conv",
    )(x2d, mod, mod, w_in, conv_w, conv_b, ln_g, ln_b, w_pw, b_pw)


def kernel(x, c, w_ada, b_ada, w_in, rel_bias, sinks, conv_w, conv_b, conv_ln_g,
           conv_ln_b, w_pw, b_pw, w_out, ln_g, ln_b):
    assert x.shape == (1, SEQ, D_MODEL) and w_in.shape == (DEPTH, D_MODEL, D_IN)
    x2d = x.reshape(SEQ, D_MODEL)

    row = lambda v: v.reshape(1, -1)
    mod = _mod_call(row(c), w_ada.reshape(D_MODEL, 3 * D_MODEL), row(b_ada))
    bias = _bias_call(rel_bias)
    qkvg, y_conv = _inproj_conv_call(
        x2d, mod, w_in.reshape(D_MODEL, D_IN),
        conv_w, row(conv_b), row(conv_ln_g), row(conv_ln_b),
        w_pw.reshape(D_CONV, D_CONV), row(b_pw))
    out = _attn_out_call(qkvg, bias, sinks.reshape(N_Q_HEADS), y_conv, x2d, mod,
                         w_out.reshape(D_MODEL, D_MODEL), row(ln_g), row(ln_b))
    return out.reshape(1, SEQ, D_MODEL)
```

```python
import math

import jax
import jax.numpy as jnp
import numpy as np
from jax.experimental import pallas as pl
from jax.experimental.pallas import tpu as pltpu

F32 = jnp.float32
BF16 = jnp.bfloat16

D_MODEL = 2048
SEQ = 8192
HEAD_DIM = 64
D_ATTN = 1024
D_CONV = 1024
N_Q_HEADS = 16
N_KV_HEADS = 4
GQA_GROUP = 4
D_KV = 256
WINDOW = 128
BLOCK = 128
CONV_WIDTH = 31
N_BUCKETS = 32
MAX_DISTANCE = 128
LN_EPS = 1e-5
DEPTH = 1
ALPHA = (2.0 * DEPTH) ** 0.25
D_IN = 2 * D_ATTN + 2 * D_KV + 3 * D_CONV

LANES = 128
SUBLANES = 8

COL_Q = 0
COL_K = COL_Q + D_ATTN
COL_V = COL_K + D_KV
COL_G_ATTN = COL_V + D_KV
COL_GLU_A = COL_G_ATTN + D_ATTN
COL_GLU_B = COL_GLU_A + D_CONV
COL_G_CONV = COL_GLU_B + D_CONV
HALF = 512

VMEM_LIMIT = 56 * 1024 * 1024


def _silu(v):
    return v * jax.nn.sigmoid(v)


MOD_STAGE_ROWS = 256


def _mod_kernel(c_ref, w_hbm, b_ref, o_ref, ca_ref, stage_ref, sem_ref):
    n_out = o_ref.shape[1]
    n = D_MODEL // MOD_STAGE_ROWS

    def chunk_copy(j):
        slot = j % 2
        return pltpu.make_async_copy(
            w_hbm.at[pl.ds(j * MOD_STAGE_ROWS, MOD_STAGE_ROWS), :],
            stage_ref.at[slot], sem_ref.at[slot])

    chunk_copy(0).start()
    ca = _silu(c_ref[...])
    ca_ref[...] = jnp.transpose(jnp.broadcast_to(ca, (LANES, D_MODEL)))
    acc = jnp.zeros((SUBLANES, n_out), F32)
    for j in range(n):
        if j + 1 < n:
            chunk_copy(j + 1).start()
        chunk_copy(j).wait()
        rows = slice(j * MOD_STAGE_ROWS, (j + 1) * MOD_STAGE_ROWS)
        prod = jnp.tile(ca_ref[rows, :], (1, n_out // LANES)) * stage_ref[j % 2]
        for r in range(0, MOD_STAGE_ROWS, SUBLANES):
            acc = acc + prod[r:r + SUBLANES, :]
    o_ref[...] = jnp.sum(acc, axis=0, keepdims=True) + b_ref[...]


def _mod_call(c_row, w_ada, b_ada):
    n = w_ada.shape[1]
    return pl.pallas_call(
        _mod_kernel,
        in_specs=[
            pl.BlockSpec((1, D_MODEL), lambda: (0, 0)),
            pl.BlockSpec(memory_space=pl.ANY),
            pl.BlockSpec((1, n), lambda: (0, 0)),
        ],
        out_specs=pl.BlockSpec((1, n), lambda: (0, 0)),
        out_shape=jax.ShapeDtypeStruct((1, n), F32),
        scratch_shapes=[
            pltpu.VMEM((D_MODEL, LANES), F32),
            pltpu.VMEM((2, MOD_STAGE_ROWS, n), F32),
            pltpu.SemaphoreType.DMA((2,)),
        ],
        compiler_params=pltpu.CompilerParams(vmem_limit_bytes=VMEM_LIMIT),
        name="mod",
    )(c_row, w_ada, b_ada)


LOG2E = math.log2(math.e)
Q_SCALE = LOG2E * HEAD_DIM ** -0.5
MASKED = -1e30


def _lane_buckets():
    m = np.arange(2 * BLOCK)
    dist = BLOCK - m
    ok = (dist >= 0) & (dist < WINDOW)
    max_exact = N_BUCKETS // 2
    d = np.maximum(dist, 1).astype(np.float64)
    large = max_exact + (np.log(d / max_exact) / math.log(MAX_DISTANCE / max_exact)
                         * (N_BUCKETS - max_exact)).astype(np.int32)
    large = np.minimum(large, N_BUCKETS - 1)
    bucket = np.where(dist < max_exact, dist, large)
    return np.where(ok, bucket, -1).astype(np.int32).reshape(1, -1)


def _bias_kernel(rb_ref, bm_ref, o_ref):
    bm = bm_ref[...]
    col = jax.lax.broadcasted_iota(jnp.int32, (BLOCK, 2 * BLOCK), 1)
    for h in range(N_Q_HEADS):
        t = jnp.full(bm.shape, MASKED, F32)
        for b in range(N_BUCKETS):
            t = jnp.where(bm == b, rb_ref[b, h] * LOG2E, t)
        x = pltpu.roll(jnp.broadcast_to(t, (BLOCK, 2 * BLOCK)), 0, 1, stride=1, stride_axis=0)
        o_ref[1, h] = x
        o_ref[0, h] = jnp.where(col >= BLOCK, x, MASKED)


def _bias_call(rel_bias):
    shape = (2, N_Q_HEADS, BLOCK, 2 * BLOCK)
    return pl.pallas_call(
        _bias_kernel,
        in_specs=[
            pl.BlockSpec(memory_space=pltpu.SMEM),
            pl.BlockSpec((1, 2 * BLOCK), lambda: (0, 0)),
        ],
        out_specs=pl.BlockSpec(shape, lambda: (0, 0, 0, 0)),
        out_shape=jax.ShapeDtypeStruct(shape, F32),
        name="bias",
    )(rel_bias, jnp.asarray(_lane_buckets()))


def _load_weight_bf16(w_hbm, w_ref, stage_ref, sem_ref, col_scale=None):
    rows, cols = w_ref.shape
    chunk_rows = stage_ref.shape[1]
    n = rows // chunk_rows

    def chunk_copy(j):
        slot = j % 2
        return pltpu.make_async_copy(
            w_hbm.at[pl.ds(j * chunk_rows, chunk_rows), :],
            stage_ref.at[slot, :, pl.ds(0, cols)], sem_ref.at[slot])

    chunk_copy(0).start()
    for j in range(n):
        if j + 1 < n:
            chunk_copy(j + 1).start()
        chunk_copy(j).wait()
        chunk = stage_ref[j % 2, :, 0:cols]
        if col_scale is not None:
            chunk = chunk * col_scale
        w_ref[j * chunk_rows:(j + 1) * chunk_rows, :] = chunk.astype(BF16)


def _token(v):
    t = v[0:SUBLANES, 0:LANES]
    for r in range(SUBLANES, v.shape[0], SUBLANES):
        t = t + v[r:r + SUBLANES, 0:LANES]
    return t


def _zero_of(token):
    return (pltpu.bitcast(token, jnp.uint32) >> 16) >> 16


def _copy_after(src_ref, dst_ref, tokens):
    zero = _zero_of(tokens[0])
    for t in tokens[1:]:
        zero = zero | _zero_of(t)
    bits = pltpu.bitcast(src_ref[...], jnp.uint32)
    reps = (bits.shape[0] // zero.shape[0], bits.shape[1] // zero.shape[1])
    dst_ref[...] = pltpu.bitcast(bits | jnp.tile(zero, reps), dst_ref.dtype)


ATT_TM = 4 * BLOCK
ATT_ROWS = BLOCK
OUT_STAGE_ROWS = 512


def _dup_halves(t):
    swapped = jnp.concatenate([t[:, HEAD_DIM:], t[:, :HEAD_DIM]], axis=1)
    lane = jax.lax.broadcasted_iota(jnp.int32, t.shape, 1)
    low_half = lane < HEAD_DIM
    return jnp.where(low_half, t, swapped), jnp.where(low_half, swapped, t)


def _attn_group(q_ref, g0_ref, g1_ref, k_dup, v_dup, bias_ref, sinks_ref, r0, b0, kv, ya_ref):
    nr = ATT_ROWS
    lane = jax.lax.broadcasted_iota(jnp.int32, (nr, LANES), 1)
    low_half = lane < HEAD_DIM
    rows = slice(r0, r0 + nr)
    tiles = [GQA_GROUP * kv // 2 + t for t in range(GQA_GROUP // 2)]
    stacked = []
    for tile in tiles:
        q_tile = q_ref[rows, tile * LANES:(tile + 1) * LANES]
        zero = jnp.zeros_like(q_tile)
        stacked += [jnp.where(low_half, q_tile, zero), jnp.where(low_half, zero, q_tile)]
    s_all = jax.lax.dot_general(jnp.concatenate(stacked, axis=0), k_dup,
                                (((1,), (1,)), ((), ())),
                                preferred_element_type=F32)
    probs, denoms = [], []
    for r in range(GQA_GROUP):
        head = GQA_GROUP * kv + r
        s = s_all[r * nr:(r + 1) * nr] + bias_ref[0, head, b0:b0 + nr, :]
        sink = sinks_ref[head] * LOG2E
        m = jnp.max(s, axis=1, keepdims=True)
        e = jnp.exp2(s - m)
        denoms.append(jnp.sum(e, axis=1, keepdims=True) + jnp.exp2(sink - m))
        probs.append(e.astype(BF16))
    o_all = jnp.dot(jnp.concatenate(probs, axis=0), v_dup,
                    preferred_element_type=F32)
    outs = [o_all[r * nr:(r + 1) * nr] / denoms[r] for r in range(GQA_GROUP)]
    for t, tile in enumerate(tiles):
        y = jnp.where(low_half, outs[2 * t], outs[2 * t + 1])
        g_ref = g0_ref if tile < HALF // LANES else g1_ref
        gl = (tile * LANES) % HALF
        gate = g_ref[rows, gl:gl + LANES].astype(F32)
        ya_ref[rows, tile * LANES:(tile + 1) * LANES] = (y * gate).astype(BF16)


def _attn_tile(sinks_ref, q_ref, g0_ref, g1_ref, k_ref, v_ref, kp_ref, vp_ref,
               bias_refs, ya_ref):
    n_lt = D_KV // LANES
    lt = lambda t: slice(t * LANES, (t + 1) * LANES)

    def block_operands(b):
        r0 = b * BLOCK
        if b == 0:
            k_prev = [kp_ref[:, lt(t)] for t in range(n_lt)]
            v_prev = [vp_ref[:, lt(t)] for t in range(n_lt)]
        else:
            k_prev = [k_ref[r0 - BLOCK:r0, lt(t)] for t in range(n_lt)]
            v_prev = [v_ref[r0 - BLOCK:r0, lt(t)] for t in range(n_lt)]
        k_rows = [jnp.concatenate([k_prev[t], k_ref[r0:r0 + BLOCK, lt(t)]], axis=0)
                  for t in range(n_lt)]
        v_rows = [jnp.concatenate([v_prev[t], v_ref[r0:r0 + BLOCK, lt(t)]], axis=0)
                  for t in range(n_lt)]
        return ([_dup_halves(kt) for kt in k_rows], [_dup_halves(vt) for vt in v_rows])

    for b in range(ATT_TM // BLOCK):
        k_dups, v_dups = block_operands(b)
        for b0 in range(0, BLOCK, ATT_ROWS):
            for kv in range(N_KV_HEADS):
                _attn_group(q_ref, g0_ref, g1_ref, k_dups[kv // 2][kv % 2],
                            v_dups[kv // 2][kv % 2], bias_refs[b], sinks_ref,
                            b * BLOCK + b0, b0, kv, ya_ref)


def _attn_out_kernel(sinks_ref, q_ref, g0_ref, g1_ref, k_ref, v_ref, kp_ref, vp_ref,
                     bias0_ref, bias1_ref, yc_ref, x_ref, gate_ref, wout_hbm,
                     lng_ref, lnb_ref, o_ref, ya_ref, wout_ref, stage_ref, sem_ref):
    i = pl.program_id(0)

    @pl.when(i == 0)
    def _():
        ya_ref[...] = jnp.zeros(ya_ref.shape, ya_ref.dtype)
        _load_weight_bf16(wout_hbm, wout_ref, stage_ref, sem_ref,
                          col_scale=gate_ref[...] * (1.0 / ALPHA))

    y = jnp.dot(ya_ref[...], wout_ref[0:D_ATTN, :], preferred_element_type=F32)
    y = y + jnp.dot(yc_ref[...], wout_ref[D_ATTN:D_MODEL, :], preferred_element_type=F32)
    z = x_ref[...] + y
    mu = jnp.mean(z, axis=1, keepdims=True)
    d = z - mu
    var = jnp.mean(d * d, axis=1, keepdims=True)
    o_ref[...] = d * jax.lax.rsqrt(var + LN_EPS / ALPHA ** 2) * lng_ref[...] + lnb_ref[...]

    _attn_tile(sinks_ref, q_ref, g0_ref, g1_ref, k_ref, v_ref, kp_ref, vp_ref,
               [bias0_ref] + [bias1_ref] * (ATT_TM // BLOCK - 1), ya_ref)


def _attn_out_call(qkvg, bias, sinks, y_conv, x2d, mod, w_out, ln_g, ln_b):
    tm = ATT_TM
    n = SEQ // tm
    clamp = lambda t: jnp.clip(t, 0, n - 1)
    cur = lambda i: clamp(i)
    lag = lambda i: clamp(i - 1)
    kv_prev = lambda i: jnp.maximum(cur(i) * (tm // BLOCK) - 1, 0)
    row = lambda i: (0, 0)
    return pl.pallas_call(
        _attn_out_kernel,
        grid=(n + 1,),
        in_specs=[
            pl.BlockSpec(memory_space=pltpu.SMEM),
            pl.BlockSpec((tm, D_ATTN), lambda i: (cur(i), COL_Q // D_ATTN)),
            pl.BlockSpec((tm, HALF), lambda i: (cur(i), COL_G_ATTN // HALF)),
            pl.BlockSpec((tm, HALF), lambda i: (cur(i), COL_G_ATTN // HALF + 1)),
            pl.BlockSpec((tm, D_KV), lambda i: (cur(i), COL_K // D_KV)),
            pl.BlockSpec((tm, D_KV), lambda i: (cur(i), COL_V // D_KV)),
            pl.BlockSpec((BLOCK, D_KV), lambda i: (kv_prev(i), COL_K // D_KV)),
            pl.BlockSpec((BLOCK, D_KV), lambda i: (kv_prev(i), COL_V // D_KV)),
            pl.BlockSpec((1, N_Q_HEADS, BLOCK, 2 * BLOCK),
                         lambda i: (jnp.minimum(cur(i), 1), 0, 0, 0)),
            pl.BlockSpec((1, N_Q_HEADS, BLOCK, 2 * BLOCK), lambda i: (1, 0, 0, 0)),
            pl.BlockSpec((tm, D_CONV), lambda i: (lag(i), 0)),
            pl.BlockSpec((tm, D_MODEL), lambda i: (lag(i), 0)),
            pl.BlockSpec((1, D_MODEL), lambda i: (0, 2)),
            pl.BlockSpec(memory_space=pl.ANY),
            pl.BlockSpec((1, D_MODEL), row),
            pl.BlockSpec((1, D_MODEL), row),
        ],
        out_specs=pl.BlockSpec((tm, D_MODEL), lambda i: (lag(i), 0)),
        out_shape=jax.ShapeDtypeStruct((SEQ, D_MODEL), F32),
        scratch_shapes=[
            pltpu.VMEM((tm, D_ATTN), BF16),
            pltpu.VMEM((D_MODEL, D_MODEL), BF16),
            pltpu.VMEM((2, OUT_STAGE_ROWS, D_MODEL), F32),
            pltpu.SemaphoreType.DMA((2,)),
        ],
        compiler_params=pltpu.CompilerParams(
            dimension_semantics=("arbitrary",), vmem_limit_bytes=VMEM_LIMIT),
        name="attn_out",
    )(sinks, qkvg, qkvg, qkvg, qkvg, qkvg, qkvg, qkvg, bias, bias, y_conv, x2d, mod,
      w_out, ln_g, ln_b)


IN_TM = 256
IN_CHUNK = 256
D_QKVG = COL_GLU_A
CONV_HALO = 32
IN_STAGE_ROWS = 256
TAP_ROWS = 32
SYNC_EVERY = 1
LAST_TAP_SYNC_CHUNK = 20


def _tap_piece(u_ref, cw_ref, cb_ref, acc_ref, c, r0):
    off = CONV_HALO - (CONV_WIDTH - 1)
    cs = slice(c * LANES, (c + 1) * LANES)
    acc = jnp.broadcast_to(cb_ref[:, cs], (TAP_ROWS, LANES))
    for k in range(CONV_WIDTH):
        acc = acc + cw_ref[0, k:k + 1, cs] * u_ref[c, r0 + off + k:r0 + off + k + TAP_ROWS, :]
    acc_ref[r0:r0 + TAP_ROWS, cs] = acc
    return _token(acc)


def _conv_tail(acc_ref, lng_ref, lnb_ref, wpw_ref, bpw_ref, gate_ref, yc_ref):
    v = acc_ref[...]
    mu = jnp.mean(v, axis=1, keepdims=True)
    d = v - mu
    var = jnp.mean(d * d, axis=1, keepdims=True)
    y = d * jax.lax.rsqrt(var + LN_EPS) * lng_ref[...] + lnb_ref[...]
    s = _silu(y).astype(BF16)
    z = jnp.dot(s, wpw_ref[...], preferred_element_type=F32) + bpw_ref[...]
    yc_ref[...] = (z * gate_ref[...]).astype(BF16)


def _inproj_conv_kernel(x_ref, shift_ref, scale_ref, w_hbm, cw_ref, cb_ref, lng_ref,
                        lnb_ref, wpw_hbm, bpw_ref, qkvg_ref, yc_ref,
                        h0_ref, h1_ref, u_ref, gc_ref, gcp_ref, acc_ref,
                        w_ref, wpw_ref, stage_ref, sem_ref):
    i = pl.program_id(0)
    tm = IN_TM
    pieces = [(c, r0) for c in range(D_CONV // LANES) for r0 in range(0, tm, TAP_ROWS)]

    @pl.when(i == 0)
    def _():
        u_ref[...] = jnp.zeros(u_ref.shape, u_ref.dtype)
        gc_ref[...] = jnp.zeros(gc_ref.shape, gc_ref.dtype)
        _load_weight_bf16(w_hbm, w_ref, stage_ref, sem_ref)
        _load_weight_bf16(wpw_hbm, wpw_ref, stage_ref, sem_ref)

    gcp_ref[...] = gc_ref[...]
    h = x_ref[...] * (1.0 + scale_ref[...]) + shift_ref[...]
    h0_ref[...] = h.astype(BF16)
    h_bufs = [h0_ref, h1_ref]
    state = {"cur": 0}

    def sync(tokens):
        src, dst = h_bufs[state["cur"]], h_bufs[1 - state["cur"]]
        _copy_after(src, dst, tokens)
        state["cur"] = 1 - state["cur"]

    def project(col):
        return jnp.dot(h_bufs[state["cur"]][...], w_ref[:, col:col + IN_CHUNK],
                       preferred_element_type=F32)

    cols = ([col for col in range(0, D_QKVG, IN_CHUNK) if col != COL_V]
            + list(range(COL_G_CONV, D_IN, IN_CHUNK)))
    for c in range(D_CONV // IN_CHUNK):
        cols += [COL_GLU_A + c * IN_CHUNK, COL_GLU_B + c * IN_CHUNK]
    cols += [COL_V]
    sync_chunks = list(range(SYNC_EVERY, LAST_TAP_SYNC_CHUNK + 1, SYNC_EVERY))
    bounds = [round(s * len(pieces) / len(sync_chunks)) for s in range(len(sync_chunks) + 1)]
    glu_a = None
    pieces_done = 0
    for j, col in enumerate(cols):
        if j in sync_chunks:
            s_idx = sync_chunks.index(j)
            group = pieces[bounds[s_idx]:bounds[s_idx + 1]]
            sync([_tap_piece(u_ref, cw_ref, cb_ref, acc_ref, c, r0) for c, r0 in group])
            pieces_done = bounds[s_idx + 1]
            if pieces_done == len(pieces):
                _conv_tail(acc_ref, lng_ref, lnb_ref, wpw_ref, bpw_ref, gcp_ref, yc_ref)
        res = project(col)
        if col < COL_K:
            qkvg_ref[:, col:col + IN_CHUNK] = (res * Q_SCALE).astype(BF16)
        elif col < COL_G_ATTN:
            qkvg_ref[:, col:col + IN_CHUNK] = res.astype(BF16)
        elif col < D_QKVG:
            qkvg_ref[:, col:col + IN_CHUNK] = _silu(res).astype(BF16)
        elif col < COL_GLU_B:
            glu_a = res
        elif col < COL_G_CONV:
            u = glu_a * jax.nn.sigmoid(res)
            t0 = (col - COL_GLU_B) // LANES
            for t in range(t0, t0 + IN_CHUNK // LANES):
                assert pieces_done >= (t + 1) * (tm // TAP_ROWS)
                u_ref[t, 0:CONV_HALO, :] = u_ref[t, tm:tm + CONV_HALO, :]
                u_ref[t, CONV_HALO:CONV_HALO + tm, :] = u[:, (t - t0) * LANES:(t - t0 + 1) * LANES]
        else:
            gc_ref[:, col - COL_G_CONV:col - COL_G_CONV + IN_CHUNK] = _silu(res)


def _inproj_conv_call(x2d, mod, w_in, conv_w, conv_b, ln_g, ln_b, w_pw, b_pw):
    tm = IN_TM
    n = SEQ // tm
    cur = lambda i: jnp.minimum(i, n - 1)
    lag = lambda i: jnp.maximum(i - 1, 0)
    row = lambda i: (0, 0)
    return pl.pallas_call(
        _inproj_conv_kernel,
        grid=(n + 1,),
        in_specs=[
            pl.BlockSpec((tm, D_MODEL), lambda i: (cur(i), 0)),
            pl.BlockSpec((1, D_MODEL), lambda i: (0, 0)),
            pl.BlockSpec((1, D_MODEL), lambda i: (0, 1)),
            pl.BlockSpec(memory_space=pl.ANY),
            pl.BlockSpec((DEPTH, CONV_WIDTH, D_CONV), lambda i: (0, 0, 0)),
            pl.BlockSpec((1, D_CONV), row),
            pl.BlockSpec((1, D_CONV), row),
            pl.BlockSpec((1, D_CONV), row),
            pl.BlockSpec(memory_space=pl.ANY),
            pl.BlockSpec((1, D_CONV), row),
        ],
        out_specs=[
            pl.BlockSpec((tm, D_QKVG), lambda i: (cur(i), 0)),
            pl.BlockSpec((tm, D_CONV), lambda i: (lag(i), 0)),
        ],
        out_shape=[
            jax.ShapeDtypeStruct((SEQ, D_QKVG), BF16),
            jax.ShapeDtypeStruct((SEQ, D_CONV), BF16),
        ],
        scratch_shapes=[
            pltpu.VMEM((tm, D_MODEL), BF16),
            pltpu.VMEM((tm, D_MODEL), BF16),
            pltpu.VMEM((D_CONV // LANES, CONV_HALO + tm, LANES), F32),
            pltpu.VMEM((tm, D_CONV), F32),
            pltpu.VMEM((tm, D_CONV), F32),
            pltpu.VMEM((tm, D_CONV), F32),
            pltpu.VMEM((D_MODEL, D_IN), BF16),
            pltpu.VMEM((D_CONV, D_CONV), BF16),
            pltpu.VMEM((2, IN_STAGE_ROWS, D_IN), F32),
            pltpu.SemaphoreType.DMA((2,)),
        ],
        compiler_params=pltpu.CompilerParams(
            dimension_semantics=("arbitrary",), vmem_limit_bytes=VMEM_LIMIT),
        name="inproj_conv",
    )(x2d, mod, mod, w_in, conv_w, conv_b, ln_g, ln_b, w_pw, b_pw)


def kernel(x, c, w_ada, b_ada, w_in, rel_bias, sinks, conv_w, conv_b, conv_ln_g,
           conv_ln_b, w_pw, b_pw, w_out, ln_g, ln_b):
    assert x.shape == (1, SEQ, D_MODEL) and w_in.shape == (DEPTH, D_MODEL, D_IN)
    x2d = x.reshape(SEQ, D_MODEL)

    row = lambda v: v.reshape(1, -1)
    mod = _mod_call(row(c), w_ada.reshape(D_MODEL, 3 * D_MODEL), row(b_ada))
    bias = _bias_call(rel_bias)
    qkvg, y_conv = _inproj_conv_call(
        x2d, mod, w_in.reshape(D_MODEL, D_IN),
        conv_w, row(conv_b), row(conv_ln_g), row(conv_ln_b),
        w_pw.reshape(D_CONV, D_CONV), row(b_pw))
    out = _attn_out_call(qkvg, bias, sinks.reshape(N_Q_HEADS), y_conv, x2d, mod,
                         w_out.reshape(D_MODEL, D_MODEL), row(ln_g), row(ln_b))
    return out.reshape(1, SEQ, D_MODEL)
```

```python
import math

import jax
import jax.numpy as jnp
import numpy as np
from jax.experimental import pallas as pl
from jax.experimental.pallas import tpu as pltpu

F32 = jnp.float32
BF16 = jnp.bfloat16

D_MODEL = 2048
SEQ = 8192
HEAD_DIM = 64
D_ATTN = 1024
D_CONV = 1024
N_Q_HEADS = 16
N_KV_HEADS = 4
GQA_GROUP = 4
D_KV = 256
WINDOW = 128
BLOCK = 128
CONV_WIDTH = 31
N_BUCKETS = 32
MAX_DISTANCE = 128
LN_EPS = 1e-5
DEPTH = 1
ALPHA = (2.0 * DEPTH) ** 0.25
D_IN = 2 * D_ATTN + 2 * D_KV + 3 * D_CONV

LANES = 128
SUBLANES = 8

COL_Q = 0
COL_K = COL_Q + D_ATTN
COL_V = COL_K + D_KV
COL_G_ATTN = COL_V + D_KV
COL_GLU_A = COL_G_ATTN + D_ATTN
COL_GLU_B = COL_GLU_A + D_CONV
COL_G_CONV = COL_GLU_B + D_CONV
HALF = 512

VMEM_LIMIT = 56 * 1024 * 1024


def _silu(v):
    return v * jax.nn.sigmoid(v)


MOD_STAGE_ROWS = 256


def _mod_kernel(c_ref, w_hbm, b_ref, o_ref, ca_ref, stage_ref, sem_ref):
    n_out = o_ref.shape[1]
    n = D_MODEL // MOD_STAGE_ROWS

    def chunk_copy(j):
        slot = j % 2
        return pltpu.make_async_copy(
            w_hbm.at[pl.ds(j * MOD_STAGE_ROWS, MOD_STAGE_ROWS), :],
            stage_ref.at[slot], sem_ref.at[slot])

    chunk_copy(0).start()
    ca = _silu(c_ref[...])
    ca_ref[...] = jnp.transpose(jnp.broadcast_to(ca, (LANES, D_MODEL)))
    acc = jnp.zeros((SUBLANES, n_out), F32)
    for j in range(n):
        if j + 1 < n:
            chunk_copy(j + 1).start()
        chunk_copy(j).wait()
        rows = slice(j * MOD_STAGE_ROWS, (j + 1) * MOD_STAGE_ROWS)
        prod = jnp.tile(ca_ref[rows, :], (1, n_out // LANES)) * stage_ref[j % 2]
        for r in range(0, MOD_STAGE_ROWS, SUBLANES):
            acc = acc + prod[r:r + SUBLANES, :]
    o_ref[...] = jnp.sum(acc, axis=0, keepdims=True) + b_ref[...]


def _mod_call(c_row, w_ada, b_ada):
    n = w_ada.shape[1]
    return pl.pallas_call(
        _mod_kernel,
        in_specs=[
            pl.BlockSpec((1, D_MODEL), lambda: (0, 0)),
            pl.BlockSpec(memory_space=pl.ANY),
            pl.BlockSpec((1, n), lambda: (0, 0)),
        ],
        out_specs=pl.BlockSpec((1, n), lambda: (0, 0)),
        out_shape=jax.ShapeDtypeStruct((1, n), F32),
        scratch_shapes=[
            pltpu.VMEM((D_MODEL, LANES), F32),
            pltpu.VMEM((2, MOD_STAGE_ROWS, n), F32),
            pltpu.SemaphoreType.DMA((2,)),
        ],
        compiler_params=pltpu.CompilerParams(vmem_limit_bytes=VMEM_LIMIT),
        name="mod",
    )(c_row, w_ada, b_ada)


LOG2E = math.log2(math.e)
Q_SCALE = LOG2E * HEAD_DIM ** -0.5
MASKED = -1e30


def _lane_buckets():
    m = np.arange(2 * BLOCK)
    dist = BLOCK - m
    ok = (dist >= 0) & (dist < WINDOW)
    max_exact = N_BUCKETS // 2
    d = np.maximum(dist, 1).astype(np.float64)
    large = max_exact + (np.log(d / max_exact) / math.log(MAX_DISTANCE / max_exact)
                         * (N_BUCKETS - max_exact)).astype(np.int32)
    large = np.minimum(large, N_BUCKETS - 1)
    bucket = np.where(dist < max_exact, dist, large)
    return np.where(ok, bucket, -1).astype(np.int32).reshape(1, -1)


def _bias_kernel(rb_ref, bm_ref, o_ref):
    bm = bm_ref[...]
    col = jax.lax.broadcasted_iota(jnp.int32, (BLOCK, 2 * BLOCK), 1)
    for h in range(N_Q_HEADS):
        t = jnp.full(bm.shape, MASKED, F32)
        for b in range(N_BUCKETS):
            t = jnp.where(bm == b, rb_ref[b, h] * LOG2E, t)
        x = pltpu.roll(jnp.broadcast_to(t, (BLOCK, 2 * BLOCK)), 0, 1, stride=1, stride_axis=0)
        o_ref[1, h] = x
        o_ref[0, h] = jnp.where(col >= BLOCK, x, MASKED)


def _bias_call(rel_bias):
    shape = (2, N_Q_HEADS, BLOCK, 2 * BLOCK)
    return pl.pallas_call(
        _bias_kernel,
        in_specs=[
            pl.BlockSpec(memory_space=pltpu.SMEM),
            pl.BlockSpec((1, 2 * BLOCK), lambda: (0, 0)),
        ],
        out_specs=pl.BlockSpec(shape, lambda: (0, 0, 0, 0)),
        out_shape=jax.ShapeDtypeStruct(shape, F32),
        name="bias",
    )(rel_bias, jnp.asarray(_lane_buckets()))


def _load_weight_bf16(w_hbm, w_ref, stage_ref, sem_ref, col_scale=None):
    rows, cols = w_ref.shape
    chunk_rows = stage_ref.shape[1]
    n = rows // chunk_rows

    def chunk_copy(j):
        slot = j % 2
        return pltpu.make_async_copy(
            w_hbm.at[pl.ds(j * chunk_rows, chunk_rows), :],
            stage_ref.at[slot, :, pl.ds(0, cols)], sem_ref.at[slot])

    chunk_copy(0).start()
    for j in range(n):
        if j + 1 < n:
            chunk_copy(j + 1).start()
        chunk_copy(j).wait()
        chunk = stage_ref[j % 2, :, 0:cols]
        if col_scale is not None:
            chunk = chunk * col_scale
        w_ref[j * chunk_rows:(j + 1) * chunk_rows, :] = chunk.astype(BF16)


def _token(v):
    t = v[0:SUBLANES, 0:LANES]
    for r in range(SUBLANES, v.shape[0], SUBLANES):
        t = t + v[r:r + SUBLANES, 0:LANES]
    return t


def _zero_of(token):
    return (pltpu.bitcast(token, jnp.uint32) >> 16) >> 16


def _copy_after(src_ref, dst_ref, tokens):
    zero = _zero_of(tokens[0])
    for t in tokens[1:]:
        zero = zero | _zero_of(t)
    bits = pltpu.bitcast(src_ref[...], jnp.uint32)
    reps = (bits.shape[0] // zero.shape[0], bits.shape[1] // zero.shape[1])
    dst_ref[...] = pltpu.bitcast(bits | jnp.tile(zero, reps), dst_ref.dtype)


ATT_TM = 4 * BLOCK
ATT_ROWS = BLOCK
OUT_STAGE_ROWS = 512


def _dup_halves(t):
    swapped = jnp.concatenate([t[:, HEAD_DIM:], t[:, :HEAD_DIM]], axis=1)
    lane = jax.lax.broadcasted_iota(jnp.int32, t.shape, 1)
    low_half = lane < HEAD_DIM
    return jnp.where(low_half, t, swapped), jnp.where(low_half, swapped, t)


def _attn_group(q_ref, g0_ref, g1_ref, k_dup, v_dup, bias_ref, sinks_ref, r0, b0, kv, ya_ref):
    nr = ATT_ROWS
    lane = jax.lax.broadcasted_iota(jnp.int32, (nr, LANES), 1)
    low_half = lane < HEAD_DIM
    rows = slice(r0, r0 + nr)
    tiles = [GQA_GROUP * kv // 2 + t for t in range(GQA_GROUP // 2)]
    stacked = []
    for tile in tiles:
        q_tile = q_ref[rows, tile * LANES:(tile + 1) * LANES]
        zero = jnp.zeros_like(q_tile)
        stacked += [jnp.where(low_half, q_tile, zero), jnp.where(low_half, zero, q_tile)]
    s_all = jax.lax.dot_general(jnp.concatenate(stacked, axis=0), k_dup,
                                (((1,), (1,)), ((), ())),
                                preferred_element_type=F32)
    probs, denoms = [], []
    for r in range(GQA_GROUP):
        head = GQA_GROUP * kv + r
        s = s_all[r * nr:(r + 1) * nr] + bias_ref[0, head, b0:b0 + nr, :]
        sink = sinks_ref[head] * LOG2E
        m = jnp.max(s, axis=1, keepdims=True)
        e = jnp.exp2(s - m)
        denoms.append(jnp.sum(e, axis=1, keepdims=True) + jnp.exp2(sink - m))
        probs.append(e.astype(BF16))
    o_all = jnp.dot(jnp.concatenate(probs, axis=0), v_dup,
                    preferred_element_type=F32)
    outs = [o_all[r * nr:(r + 1) * nr] / denoms[r] for r in range(GQA_GROUP)]
    for t, tile in enumerate(tiles):
        y = jnp.where(low_half, outs[2 * t], outs[2 * t + 1])
        g_ref = g0_ref if tile < HALF // LANES else g1_ref
        gl = (tile * LANES) % HALF
        gate = g_ref[rows, gl:gl + LANES].astype(F32)
        ya_ref[rows, tile * LANES:(tile + 1) * LANES] = (y * gate).astype(BF16)


def _attn_tile(sinks_ref, q_ref, g0_ref, g1_ref, k_ref, v_ref, kp_ref, vp_ref,
               bias_refs, ya_ref):
    n_lt = D_KV // LANES
    lt = lambda t: slice(t * LANES, (t + 1) * LANES)

    def block_operands(b):
        r0 = b * BLOCK
        if b == 0:
            k_prev = [kp_ref[:, lt(t)] for t in range(n_lt)]
            v_prev = [vp_ref[:, lt(t)] for t in range(n_lt)]
        else:
            k_prev = [k_ref[r0 - BLOCK:r0, lt(t)] for t in range(n_lt)]
            v_prev = [v_ref[r0 - BLOCK:r0, lt(t)] for t in range(n_lt)]
        k_rows = [jnp.concatenate([k_prev[t], k_ref[r0:r0 + BLOCK, lt(t)]], axis=0)
                  for t in range(n_lt)]
        v_rows = [jnp.concatenate([v_prev[t], v_ref[r0:r0 + BLOCK, lt(t)]], axis=0)
                  for t in range(n_lt)]
        return ([_dup_halves(kt) for kt in k_rows], [_dup_halves(vt) for vt in v_rows])

    for b in range(ATT_TM // BLOCK):
        k_dups, v_dups = block_operands(b)
        for b0 in range(0, BLOCK, ATT_ROWS):
            for kv in range(N_KV_HEADS):
                _attn_group(q_ref, g0_ref, g1_ref, k_dups[kv // 2][kv % 2],
                            v_dups[kv // 2][kv % 2], bias_refs[b], sinks_ref,
                            b * BLOCK + b0, b0, kv, ya_ref)


def _attn_out_kernel(sinks_ref, q_ref, g0_ref, g1_ref, k_ref, v_ref, kp_ref, vp_ref,
                     bias0_ref, bias1_ref, yc_ref, x_ref, gate_ref, wout_hbm,
                     lng_ref, lnb_ref, o_ref, ya_ref, wout_ref, stage_ref, sem_ref):
    i = pl.program_id(0)

    @pl.when(i == 0)
    def _():
        ya_ref[...] = jnp.zeros(ya_ref.shape, ya_ref.dtype)
        _load_weight_bf16(wout_hbm, wout_ref, stage_ref, sem_ref,
                          col_scale=gate_ref[...] * (1.0 / ALPHA))

    y = jnp.dot(ya_ref[...], wout_ref[0:D_ATTN, :], preferred_element_type=F32)
    y = y + jnp.dot(yc_ref[...], wout_ref[D_ATTN:D_MODEL, :], preferred_element_type=F32)
    o_ref[...] = x_ref[...] + y
    mu = jnp.mean(o_ref[...], axis=1, keepdims=True)
    d = o_ref[...] - mu
    var = jnp.mean(d * d, axis=1, keepdims=True)
    rstd = jax.lax.rsqrt(var + LN_EPS / ALPHA ** 2)
    o_ref[...] = (o_ref[...] - mu) * rstd * lng_ref[...] + lnb_ref[...]

    _attn_tile(sinks_ref, q_ref, g0_ref, g1_ref, k_ref, v_ref, kp_ref, vp_ref,
               [bias0_ref] + [bias1_ref] * (ATT_TM // BLOCK - 1), ya_ref)


def _attn_out_call(qkvg, bias, sinks, y_conv, x2d, mod, w_out, ln_g, ln_b):
    tm = ATT_TM
    n = SEQ // tm
    clamp = lambda t: jnp.clip(t, 0, n - 1)
    cur = lambda i: clamp(i)
    lag = lambda i: clamp(i - 1)
    kv_prev = lambda i: jnp.maximum(cur(i) * (tm // BLOCK) - 1, 0)
    row = lambda i: (0, 0)
    return pl.pallas_call(
        _attn_out_kernel,
        grid=(n + 1,),
        in_specs=[
            pl.BlockSpec(memory_space=pltpu.SMEM),
            pl.BlockSpec((tm, D_ATTN), lambda i: (cur(i), COL_Q // D_ATTN)),
            pl.BlockSpec((tm, HALF), lambda i: (cur(i), COL_G_ATTN // HALF)),
            pl.BlockSpec((tm, HALF), lambda i: (cur(i), COL_G_ATTN // HALF + 1)),
            pl.BlockSpec((tm, D_KV), lambda i: (cur(i), COL_K // D_KV)),
            pl.BlockSpec((tm, D_KV), lambda i: (cur(i), COL_V // D_KV)),
            pl.BlockSpec((BLOCK, D_KV), lambda i: (kv_prev(i), COL_K // D_KV)),
            pl.BlockSpec((BLOCK, D_KV), lambda i: (kv_prev(i), COL_V // D_KV)),
            pl.BlockSpec((1, N_Q_HEADS, BLOCK, 2 * BLOCK),
                         lambda i: (jnp.minimum(cur(i), 1), 0, 0, 0)),
            pl.BlockSpec((1, N_Q_HEADS, BLOCK, 2 * BLOCK), lambda i: (1, 0, 0, 0)),
            pl.BlockSpec((tm, D_CONV), lambda i: (lag(i), 0)),
            pl.BlockSpec((tm, D_MODEL), lambda i: (lag(i), 0)),
            pl.BlockSpec((1, D_MODEL), lambda i: (0, 2)),
            pl.BlockSpec(memory_space=pl.ANY),
            pl.BlockSpec((1, D_MODEL), row),
            pl.BlockSpec((1, D_MODEL), row),
        ],
        out_specs=pl.BlockSpec((tm, D_MODEL), lambda i: (lag(i), 0)),
        out_shape=jax.ShapeDtypeStruct((SEQ, D_MODEL), F32),
        scratch_shapes=[
            pltpu.VMEM((tm, D_ATTN), BF16),
            pltpu.VMEM((D_MODEL, D_MODEL), BF16),
            pltpu.VMEM((2, OUT_STAGE_ROWS, D_MODEL), F32),
            pltpu.SemaphoreType.DMA((2,)),
        ],
        compiler_params=pltpu.CompilerParams(
            dimension_semantics=("arbitrary",), vmem_limit_bytes=VMEM_LIMIT),
        name="attn_out",
    )(sinks, qkvg, qkvg, qkvg, qkvg, qkvg, qkvg, qkvg, bias, bias, y_conv, x2d, mod,
      w_out, ln_g, ln_b)


IN_TM = 256
IN_CHUNK = 256
D_QKVG = COL_GLU_A
CONV_HALO = 32
IN_STAGE_ROWS = 256
TAP_ROWS = 32
SYNC_EVERY = 1
LAST_TAP_SYNC_CHUNK = 20


def _tap_piece(u_ref, cw_ref, cb_ref, acc_ref, c, r0):
    off = CONV_HALO - (CONV_WIDTH - 1)
    cs = slice(c * LANES, (c + 1) * LANES)
    acc = jnp.broadcast_to(cb_ref[:, cs], (TAP_ROWS, LANES))
    for k in range(CONV_WIDTH):
        acc = acc + cw_ref[0, k:k + 1, cs] * u_ref[c, r0 + off + k:r0 + off + k + TAP_ROWS, :]
    acc_ref[r0:r0 + TAP_ROWS, cs] = acc
    return _token(acc)


def _conv_tail(acc_ref, lng_ref, lnb_ref, wpw_ref, bpw_ref, gate_ref, yc_ref):
    v = acc_ref[...]
    mu = jnp.mean(v, axis=1, keepdims=True)
    d = v - mu
    var = jnp.mean(d * d, axis=1, keepdims=True)
    y = d * jax.lax.rsqrt(var + LN_EPS) * lng_ref[...] + lnb_ref[...]
    s = _silu(y).astype(BF16)
    z = jnp.dot(s, wpw_ref[...], preferred_element_type=F32) + bpw_ref[...]
    yc_ref[...] = (z * gate_ref[...]).astype(BF16)


def _inproj_conv_kernel(x_ref, shift_ref, scale_ref, w_hbm, cw_ref, cb_ref, lng_ref,
                        lnb_ref, wpw_hbm, bpw_ref, qkvg_ref, yc_ref,
                        h0_ref, h1_ref, u_ref, gc_ref, gcp_ref, acc_ref,
                        w_ref, wpw_ref, stage_ref, sem_ref):
    i = pl.program_id(0)
    tm = IN_TM
    pieces = [(c, r0) for c in range(D_CONV // LANES) for r0 in range(0, tm, TAP_ROWS)]

    @pl.when(i == 0)
    def _():
        u_ref[...] = jnp.zeros(u_ref.shape, u_ref.dtype)
        gc_ref[...] = jnp.zeros(gc_ref.shape, gc_ref.dtype)
        _load_weight_bf16(w_hbm, w_ref, stage_ref, sem_ref)
        _load_weight_bf16(wpw_hbm, wpw_ref, stage_ref, sem_ref)

    gcp_ref[...] = gc_ref[...]
    h = x_ref[...] * (1.0 + scale_ref[...]) + shift_ref[...]
    h0_ref[...] = h.astype(BF16)
    h_bufs = [h0_ref, h1_ref]
    state = {"cur": 0}

    def sync(tokens):
        src, dst = h_bufs[state["cur"]], h_bufs[1 - state["cur"]]
        _copy_after(src, dst, tokens)
        state["cur"] = 1 - state["cur"]

    def project(col):
        return jnp.dot(h_bufs[state["cur"]][...], w_ref[:, col:col + IN_CHUNK],
                       preferred_element_type=F32)

    cols = list(range(0, D_QKVG, IN_CHUNK)) + list(range(COL_G_CONV, D_IN, IN_CHUNK))
    for c in range(D_CONV // IN_CHUNK):
        cols += [COL_GLU_A + c * IN_CHUNK, COL_GLU_B + c * IN_CHUNK]
    sync_chunks = list(range(SYNC_EVERY, LAST_TAP_SYNC_CHUNK + 1, SYNC_EVERY))
    bounds = [round(s * len(pieces) / len(sync_chunks)) for s in range(len(sync_chunks) + 1)]
    glu_a = None
    pieces_done = 0
    for j, col in enumerate(cols):
        if j in sync_chunks:
            s_idx = sync_chunks.index(j)
            group = pieces[bounds[s_idx]:bounds[s_idx + 1]]
            sync([_tap_piece(u_ref, cw_ref, cb_ref, acc_ref, c, r0) for c, r0 in group])
            pieces_done = bounds[s_idx + 1]
            if pieces_done == len(pieces):
                _conv_tail(acc_ref, lng_ref, lnb_ref, wpw_ref, bpw_ref, gcp_ref, yc_ref)
        res = project(col)
        if col < COL_K:
            qkvg_ref[:, col:col + IN_CHUNK] = (res * Q_SCALE).astype(BF16)
        elif col < COL_G_ATTN:
            qkvg_ref[:, col:col + IN_CHUNK] = res.astype(BF16)
        elif col < D_QKVG:
            qkvg_ref[:, col:col + IN_CHUNK] = _silu(res).astype(BF16)
        elif col < COL_GLU_B:
            glu_a = res
        elif col < COL_G_CONV:
            u = glu_a * jax.nn.sigmoid(res)
            t0 = (col - COL_GLU_B) // LANES
            for t in range(t0, t0 + IN_CHUNK // LANES):
                assert pieces_done >= (t + 1) * (tm // TAP_ROWS)
                u_ref[t, 0:CONV_HALO, :] = u_ref[t, tm:tm + CONV_HALO, :]
                u_ref[t, CONV_HALO:CONV_HALO + tm, :] = u[:, (t - t0) * LANES:(t - t0 + 1) * LANES]
        else:
            gc_ref[:, col - COL_G_CONV:col - COL_G_CONV + IN_CHUNK] = _silu(res)


def _inproj_conv_call(x2d, mod, w_in, conv_w, conv_b, ln_g, ln_b, w_pw, b_pw):
    tm = IN_TM
    n = SEQ // tm
    cur = lambda i: jnp.minimum(i, n - 1)
    lag = lambda i: jnp.maximum(i - 1, 0)
    row = lambda i: (0, 0)
    return pl.pallas_call(
        _inproj_conv_kernel,
        grid=(n + 1,),
        in_specs=[
            pl.BlockSpec((tm, D_MODEL), lambda i: (cur(i), 0)),
            pl.BlockSpec((1, D_MODEL), lambda i: (0, 0)),
            pl.BlockSpec((1, D_MODEL), lambda i: (0, 1)),
            pl.BlockSpec(memory_space=pl.ANY),
            pl.BlockSpec((DEPTH, CONV_WIDTH, D_CONV), lambda i: (0, 0, 0)),
            pl.BlockSpec((1, D_CONV), row),
            pl.BlockSpec((1, D_CONV), row),
            pl.BlockSpec((1, D_CONV), row),
            pl.BlockSpec(memory_space=pl.ANY),
            pl.BlockSpec((1, D_CONV), row),
        ],
        out_specs=[
            pl.BlockSpec((tm, D_QKVG), lambda i: (cur(i), 0)),
            pl.BlockSpec((tm, D_CONV), lambda i: (lag(i), 0)),
        ],
        out_shape=[
            jax.ShapeDtypeStruct((SEQ, D_QKVG), BF16),
            jax.ShapeDtypeStruct((SEQ, D_CONV), BF16),
        ],
        scratch_shapes=[
            pltpu.VMEM((tm, D_MODEL), BF16),
            pltpu.VMEM((tm, D_MODEL), BF16),
            pltpu.VMEM((D_CONV // LANES, CONV_HALO + tm, LANES), F32),
            pltpu.VMEM((tm, D_CONV), F32),
            pltpu.VMEM((tm, D_CONV), F32),
            pltpu.VMEM((tm, D_CONV), F32),
            pltpu.VMEM((D_MODEL, D_IN), BF16),
            pltpu.VMEM((D_CONV, D_CONV), BF16),
            pltpu.VMEM((2, IN_STAGE_ROWS, D_IN), F32),
            pltpu.SemaphoreType.DMA((2,)),
        ],
        compiler_params=pltpu.CompilerParams(
            dimension_semantics=("arbitrary",), vmem_limit_bytes=VMEM_LIMIT),
        name="inproj_conv",
    )(x2d, mod, mod, w_in, conv_w, conv_b, ln_g, ln_b, w_pw, b_pw)


def kernel(x, c, w_ada, b_ada, w_in, rel_bias, sinks, conv_w, conv_b, conv_ln_g,
           conv_ln_b, w_pw, b_pw, w_out, ln_g, ln_b):
    assert x.shape == (1, SEQ, D_MODEL) and w_in.shape == (DEPTH, D_MODEL, D_IN)
    x2d = x.reshape(SEQ, D_MODEL)

    row = lambda v: v.reshape(1, -1)
    mod = _mod_call(row(c), w_ada.reshape(D_MODEL, 3 * D_MODEL), row(b_ada))
    bias = _bias_call(rel_bias)
    qkvg, y_conv = _inproj_conv_call(
        x2d, mod, w_in.reshape(D_MODEL, D_IN),
        conv_w, row(conv_b), row(conv_ln_g), row(conv_ln_b),
        w_pw.reshape(D_CONV, D_CONV), row(b_pw))
    out = _attn_out_call(qkvg, bias, sinks.reshape(N_Q_HEADS), y_conv, x2d, mod,
                         w_out.reshape(D_MODEL, D_MODEL), row(ln_g), row(ln_b))
    return out.reshape(1, SEQ, D_MODEL)
```

```python
import math

import jax
import jax.numpy as jnp
import numpy as np
from jax.experimental import pallas as pl
from jax.experimental.pallas import tpu as pltpu

F32 = jnp.float32
BF16 = jnp.bfloat16

D_MODEL = 2048
SEQ = 8192
HEAD_DIM = 64
D_ATTN = 1024
D_CONV = 1024
N_Q_HEADS = 16
N_KV_HEADS = 4
GQA_GROUP = 4
D_KV = 256
WINDOW = 128
BLOCK = 128
CONV_WIDTH = 31
N_BUCKETS = 32
MAX_DISTANCE = 128
LN_EPS = 1e-5
DEPTH = 1
ALPHA = (2.0 * DEPTH) ** 0.25
D_IN = 2 * D_ATTN + 2 * D_KV + 3 * D_CONV

LANES = 128
SUBLANES = 8

COL_Q = 0
COL_K = COL_Q + D_ATTN
COL_V = COL_K + D_KV
COL_G_ATTN = COL_V + D_KV
COL_GLU_A = COL_G_ATTN + D_ATTN
COL_GLU_B = COL_GLU_A + D_CONV
COL_G_CONV = COL_GLU_B + D_CONV
HALF = 512

VMEM_LIMIT = 56 * 1024 * 1024


def _silu(v):
    return v * jax.nn.sigmoid(v)


MOD_STAGE_ROWS = 256


def _mod_kernel(c_ref, w_hbm, b_ref, o_ref, ca_ref, stage_ref, sem_ref):
    n_out = o_ref.shape[1]
    n = D_MODEL // MOD_STAGE_ROWS

    def chunk_copy(j):
        slot = j % 2
        return pltpu.make_async_copy(
            w_hbm.at[pl.ds(j * MOD_STAGE_ROWS, MOD_STAGE_ROWS), :],
            stage_ref.at[slot], sem_ref.at[slot])

    chunk_copy(0).start()
    ca = _silu(c_ref[...])
    ca_ref[...] = jnp.transpose(jnp.broadcast_to(ca, (LANES, D_MODEL)))
    acc = jnp.zeros((SUBLANES, n_out), F32)
    for j in range(n):
        if j + 1 < n:
            chunk_copy(j + 1).start()
        chunk_copy(j).wait()
        rows = slice(j * MOD_STAGE_ROWS, (j + 1) * MOD_STAGE_ROWS)
        prod = jnp.tile(ca_ref[rows, :], (1, n_out // LANES)) * stage_ref[j % 2]
        for r in range(0, MOD_STAGE_ROWS, SUBLANES):
            acc = acc + prod[r:r + SUBLANES, :]
    o_ref[...] = jnp.sum(acc, axis=0, keepdims=True) + b_ref[...]


def _mod_call(c_row, w_ada, b_ada):
    n = w_ada.shape[1]
    return pl.pallas_call(
        _mod_kernel,
        in_specs=[
            pl.BlockSpec((1, D_MODEL), lambda: (0, 0)),
            pl.BlockSpec(memory_space=pl.ANY),
            pl.BlockSpec((1, n), lambda: (0, 0)),
        ],
        out_specs=pl.BlockSpec((1, n), lambda: (0, 0)),
        out_shape=jax.ShapeDtypeStruct((1, n), F32),
        scratch_shapes=[
            pltpu.VMEM((D_MODEL, LANES), F32),
            pltpu.VMEM((2, MOD_STAGE_ROWS, n), F32),
            pltpu.SemaphoreType.DMA((2,)),
        ],
        compiler_params=pltpu.CompilerParams(vmem_limit_bytes=VMEM_LIMIT),
        name="mod",
    )(c_row, w_ada, b_ada)


LOG2E = math.log2(math.e)
Q_SCALE = LOG2E * HEAD_DIM ** -0.5
MASKED = -1e30


def _lane_buckets():
    m = np.arange(2 * BLOCK)
    dist = BLOCK - m
    ok = (dist >= 0) & (dist < WINDOW)
    max_exact = N_BUCKETS // 2
    d = np.maximum(dist, 1).astype(np.float64)
    large = max_exact + (np.log(d / max_exact) / math.log(MAX_DISTANCE / max_exact)
                         * (N_BUCKETS - max_exact)).astype(np.int32)
    large = np.minimum(large, N_BUCKETS - 1)
    bucket = np.where(dist < max_exact, dist, large)
    return np.where(ok, bucket, -1).astype(np.int32).reshape(1, -1)


def _bias_kernel(rb_ref, bm_ref, o_ref):
    bm = bm_ref[...]
    col = jax.lax.broadcasted_iota(jnp.int32, (BLOCK, 2 * BLOCK), 1)
    for h in range(N_Q_HEADS):
        t = jnp.full(bm.shape, MASKED, F32)
        for b in range(N_BUCKETS):
            t = jnp.where(bm == b, rb_ref[b, h] * LOG2E, t)
        x = pltpu.roll(jnp.broadcast_to(t, (BLOCK, 2 * BLOCK)), 0, 1, stride=1, stride_axis=0)
        o_ref[1, h] = x
        o_ref[0, h] = jnp.where(col >= BLOCK, x, MASKED)


def _bias_call(rel_bias):
    shape = (2, N_Q_HEADS, BLOCK, 2 * BLOCK)
    return pl.pallas_call(
        _bias_kernel,
        in_specs=[
            pl.BlockSpec(memory_space=pltpu.SMEM),
            pl.BlockSpec((1, 2 * BLOCK), lambda: (0, 0)),
        ],
        out_specs=pl.BlockSpec(shape, lambda: (0, 0, 0, 0)),
        out_shape=jax.ShapeDtypeStruct(shape, F32),
        name="bias",
    )(rel_bias, jnp.asarray(_lane_buckets()))


def _load_weight_bf16(w_hbm, w_ref, stage_ref, sem_ref, col_scale=None):
    rows, cols = w_ref.shape
    chunk_rows = stage_ref.shape[1]
    n = rows // chunk_rows

    def chunk_copy(j):
        slot = j % 2
        return pltpu.make_async_copy(
            w_hbm.at[pl.ds(j * chunk_rows, chunk_rows), :],
            stage_ref.at[slot, :, pl.ds(0, cols)], sem_ref.at[slot])

    chunk_copy(0).start()
    for j in range(n):
        if j + 1 < n:
            chunk_copy(j + 1).start()
        chunk_copy(j).wait()
        chunk = stage_ref[j % 2, :, 0:cols]
        if col_scale is not None:
            chunk = chunk * col_scale
        w_ref[j * chunk_rows:(j + 1) * chunk_rows, :] = chunk.astype(BF16)


def _token(v):
    t = v[0:SUBLANES, 0:LANES]
    for r in range(SUBLANES, v.shape[0], SUBLANES):
        t = t + v[r:r + SUBLANES, 0:LANES]
    return t


def _zero_of(token):
    return (pltpu.bitcast(token, jnp.uint32) >> 16) >> 16


def _copy_after(src_ref, dst_ref, tokens):
    zero = _zero_of(tokens[0])
    for t in tokens[1:]:
        zero = zero | _zero_of(t)
    bits = pltpu.bitcast(src_ref[...], jnp.uint32)
    reps = (bits.shape[0] // zero.shape[0], bits.shape[1] // zero.shape[1])
    dst_ref[...] = pltpu.bitcast(bits | jnp.tile(zero, reps), dst_ref.dtype)


ATT_TM = 4 * BLOCK
ATT_ROWS = BLOCK
OUT_STAGE_ROWS = 512


def _dup_halves(t):
    swapped = jnp.concatenate([t[:, HEAD_DIM:], t[:, :HEAD_DIM]], axis=1)
    lane = jax.lax.broadcasted_iota(jnp.int32, t.shape, 1)
    low_half = lane < HEAD_DIM
    return jnp.where(low_half, t, swapped), jnp.where(low_half, swapped, t)


def _attn_group(q_ref, g0_ref, g1_ref, k_dup, v_dup, bias_ref, sinks_ref, r0, b0, kv, ya_ref):
    nr = ATT_ROWS
    lane = jax.lax.broadcasted_iota(jnp.int32, (nr, LANES), 1)
    low_half = lane < HEAD_DIM
    rows = slice(r0, r0 + nr)
    tiles = [GQA_GROUP * kv // 2 + t for t in range(GQA_GROUP // 2)]
    stacked = []
    for tile in tiles:
        q_tile = q_ref[rows, tile * LANES:(tile + 1) * LANES]
        zero = jnp.zeros_like(q_tile)
        stacked += [jnp.where(low_half, q_tile, zero), jnp.where(low_half, zero, q_tile)]
    s_all = jax.lax.dot_general(jnp.concatenate(stacked, axis=0), k_dup,
                                (((1,), (1,)), ((), ())),
                                preferred_element_type=F32)
    probs, denoms = [], []
    for r in range(GQA_GROUP):
        head = GQA_GROUP * kv + r
        s = s_all[r * nr:(r + 1) * nr] + bias_ref[0, head, b0:b0 + nr, :]
        sink = sinks_ref[head] * LOG2E
        m = jnp.max(s, axis=1, keepdims=True)
        e = jnp.exp2(s - m)
        denoms.append(jnp.sum(e, axis=1, keepdims=True) + jnp.exp2(sink - m))
        probs.append(e.astype(BF16))
    o_all = jnp.dot(jnp.concatenate(probs, axis=0), v_dup,
                    preferred_element_type=F32)
    outs = [o_all[r * nr:(r + 1) * nr] for r in range(GQA_GROUP)]
    for t, tile in enumerate(tiles):
        y = (jnp.where(low_half, outs[2 * t], outs[2 * t + 1])
             / jnp.where(low_half, denoms[2 * t], denoms[2 * t + 1]))
        g_ref = g0_ref if tile < HALF // LANES else g1_ref
        gl = (tile * LANES) % HALF
        gate = g_ref[rows, gl:gl + LANES].astype(F32)
        ya_ref[rows, tile * LANES:(tile + 1) * LANES] = (y * gate).astype(BF16)


def _attn_tile(sinks_ref, q_ref, g0_ref, g1_ref, k_ref, v_ref, kp_ref, vp_ref,
               bias_refs, ya_ref):
    n_lt = D_KV // LANES
    lt = lambda t: slice(t * LANES, (t + 1) * LANES)

    def block_operands(b):
        r0 = b * BLOCK
        if b == 0:
            k_prev = [kp_ref[:, lt(t)] for t in range(n_lt)]
            v_prev = [vp_ref[:, lt(t)] for t in range(n_lt)]
        else:
            k_prev = [k_ref[r0 - BLOCK:r0, lt(t)] for t in range(n_lt)]
            v_prev = [v_ref[r0 - BLOCK:r0, lt(t)] for t in range(n_lt)]
        k_rows = [jnp.concatenate([k_prev[t], k_ref[r0:r0 + BLOCK, lt(t)]], axis=0)
                  for t in range(n_lt)]
        v_rows = [jnp.concatenate([v_prev[t], v_ref[r0:r0 + BLOCK, lt(t)]], axis=0)
                  for t in range(n_lt)]
        return ([_dup_halves(kt) for kt in k_rows], [_dup_halves(vt) for vt in v_rows])

    for b in range(ATT_TM // BLOCK):
        k_dups, v_dups = block_operands(b)
        for b0 in range(0, BLOCK, ATT_ROWS):
            for kv in range(N_KV_HEADS):
                _attn_group(q_ref, g0_ref, g1_ref, k_dups[kv // 2][kv % 2],
                            v_dups[kv // 2][kv % 2], bias_refs[b], sinks_ref,
                            b * BLOCK + b0, b0, kv, ya_ref)


def _attn_out_kernel(sinks_ref, q_ref, g0_ref, g1_ref, k_ref, v_ref, kp_ref, vp_ref,
                     bias0_ref, bias1_ref, yc_ref, x_ref, gate_ref, wout_hbm,
                     lng_ref, lnb_ref, o_ref, ya_ref, wout_ref, stage_ref, sem_ref):
    i = pl.program_id(0)

    @pl.when(i == 0)
    def _():
        ya_ref[...] = jnp.zeros(ya_ref.shape, ya_ref.dtype)
        _load_weight_bf16(wout_hbm, wout_ref, stage_ref, sem_ref,
                          col_scale=gate_ref[...] * (1.0 / ALPHA))

    y = jnp.dot(ya_ref[...], wout_ref[0:D_ATTN, :], preferred_element_type=F32)
    y = y + jnp.dot(yc_ref[...], wout_ref[D_ATTN:D_MODEL, :], preferred_element_type=F32)
    z = x_ref[...] + y
    o_ref[...] = z
    mu = jnp.mean(z, axis=1, keepdims=True)
    d = o_ref[...] - mu
    var = jnp.mean(d * d, axis=1, keepdims=True)
    rstd = jax.lax.rsqrt(var + LN_EPS / ALPHA ** 2)
    o_ref[...] = (o_ref[...] - mu) * rstd * lng_ref[...] + lnb_ref[...]

    _attn_tile(sinks_ref, q_ref, g0_ref, g1_ref, k_ref, v_ref, kp_ref, vp_ref,
               [bias0_ref] + [bias1_ref] * (ATT_TM // BLOCK - 1), ya_ref)


def _attn_out_call(qkvg, bias, sinks, y_conv, x2d, mod, w_out, ln_g, ln_b):
    tm = ATT_TM
    n = SEQ // tm
    clamp = lambda t: jnp.clip(t, 0, n - 1)
    cur = lambda i: clamp(i)
    lag = lambda i: clamp(i - 1)
    kv_prev = lambda i: jnp.maximum(cur(i) * (tm // BLOCK) - 1, 0)
    row = lambda i: (0, 0)
    return pl.pallas_call(
        _attn_out_kernel,
        grid=(n + 1,),
        in_specs=[
            pl.BlockSpec(memory_space=pltpu.SMEM),
            pl.BlockSpec((tm, D_ATTN), lambda i: (cur(i), COL_Q // D_ATTN)),
            pl.BlockSpec((tm, HALF), lambda i: (cur(i), COL_G_ATTN // HALF)),
            pl.BlockSpec((tm, HALF), lambda i: (cur(i), COL_G_ATTN // HALF + 1)),
            pl.BlockSpec((tm, D_KV), lambda i: (cur(i), COL_K // D_KV)),
            pl.BlockSpec((tm, D_KV), lambda i: (cur(i), COL_V // D_KV)),
            pl.BlockSpec((BLOCK, D_KV), lambda i: (kv_prev(i), COL_K // D_KV)),
            pl.BlockSpec((BLOCK, D_KV), lambda i: (kv_prev(i), COL_V // D_KV)),
            pl.BlockSpec((1, N_Q_HEADS, BLOCK, 2 * BLOCK),
                         lambda i: (jnp.minimum(cur(i), 1), 0, 0, 0)),
            pl.BlockSpec((1, N_Q_HEADS, BLOCK, 2 * BLOCK), lambda i: (1, 0, 0, 0)),
            pl.BlockSpec((tm, D_CONV), lambda i: (lag(i), 0)),
            pl.BlockSpec((tm, D_MODEL), lambda i: (lag(i), 0)),
            pl.BlockSpec((1, D_MODEL), lambda i: (0, 2)),
            pl.BlockSpec(memory_space=pl.ANY),
            pl.BlockSpec((1, D_MODEL), row),
            pl.BlockSpec((1, D_MODEL), row),
        ],
        out_specs=pl.BlockSpec((tm, D_MODEL), lambda i: (lag(i), 0)),
        out_shape=jax.ShapeDtypeStruct((SEQ, D_MODEL), F32),
        scratch_shapes=[
            pltpu.VMEM((tm, D_ATTN), BF16),
            pltpu.VMEM((D_MODEL, D_MODEL), BF16),
            pltpu.VMEM((2, OUT_STAGE_ROWS, D_MODEL), F32),
            pltpu.SemaphoreType.DMA((2,)),
        ],
        compiler_params=pltpu.CompilerParams(
            dimension_semantics=("arbitrary",), vmem_limit_bytes=VMEM_LIMIT),
        name="attn_out",
    )(sinks, qkvg, qkvg, qkvg, qkvg, qkvg, qkvg, qkvg, bias, bias, y_conv, x2d, mod,
      w_out, ln_g, ln_b)


IN_TM = 256
IN_CHUNK = 256
D_QKVG = COL_GLU_A
CONV_HALO = 32
IN_STAGE_ROWS = 256
TAP_ROWS = 32
SYNC_EVERY = 1
LAST_TAP_SYNC_CHUNK = 20


def _tap_piece(u_ref, cw_ref, cb_ref, acc_ref, c, r0):
    off = CONV_HALO - (CONV_WIDTH - 1)
    cs = slice(c * LANES, (c + 1) * LANES)
    acc = jnp.broadcast_to(cb_ref[:, cs], (TAP_ROWS, LANES))
    for k in range(CONV_WIDTH):
        acc = acc + cw_ref[0, k:k + 1, cs] * u_ref[c, r0 + off + k:r0 + off + k + TAP_ROWS, :]
    acc_ref[r0:r0 + TAP_ROWS, cs] = acc
    return _token(acc)


def _conv_tail(acc_ref, lng_ref, lnb_ref, wpw_ref, bpw_ref, gate_ref, yc_ref):
    v = acc_ref[...]
    mu = jnp.mean(v, axis=1, keepdims=True)
    d = v - mu
    var = jnp.mean(d * d, axis=1, keepdims=True)
    y = d * jax.lax.rsqrt(var + LN_EPS) * lng_ref[...] + lnb_ref[...]
    s = _silu(y).astype(BF16)
    z = jnp.dot(s, wpw_ref[...], preferred_element_type=F32) + bpw_ref[...]
    yc_ref[...] = (z * gate_ref[...]).astype(BF16)


def _inproj_conv_kernel(x_ref, shift_ref, scale_ref, w_hbm, cw_ref, cb_ref, lng_ref,
                        lnb_ref, wpw_hbm, bpw_ref, qkvg_ref, yc_ref,
                        h0_ref, h1_ref, u_ref, gc_ref, gcp_ref, acc_ref,
                        w_ref, wpw_ref, stage_ref, sem_ref):
    i = pl.program_id(0)
    tm = IN_TM
    pieces = [(c, r0) for c in range(D_CONV // LANES) for r0 in range(0, tm, TAP_ROWS)]

    @pl.when(i == 0)
    def _():
        u_ref[...] = jnp.zeros(u_ref.shape, u_ref.dtype)
        gc_ref[...] = jnp.zeros(gc_ref.shape, gc_ref.dtype)
        _load_weight_bf16(w_hbm, w_ref, stage_ref, sem_ref)
        _load_weight_bf16(wpw_hbm, wpw_ref, stage_ref, sem_ref)

    gcp_ref[...] = gc_ref[...]
    h = x_ref[...] * (1.0 + scale_ref[...]) + shift_ref[...]
    h0_ref[...] = h.astype(BF16)
    h_bufs = [h0_ref, h1_ref]
    state = {"cur": 0}

    def sync(tokens):
        src, dst = h_bufs[state["cur"]], h_bufs[1 - state["cur"]]
        _copy_after(src, dst, tokens)
        state["cur"] = 1 - state["cur"]

    def project(col):
        return jnp.dot(h_bufs[state["cur"]][...], w_ref[:, col:col + IN_CHUNK],
                       preferred_element_type=F32)

    cols = list(range(0, D_QKVG, IN_CHUNK)) + list(range(COL_G_CONV, D_IN, IN_CHUNK))
    for c in range(D_CONV // IN_CHUNK):
        cols += [COL_GLU_A + c * IN_CHUNK, COL_GLU_B + c * IN_CHUNK]
    sync_chunks = list(range(SYNC_EVERY, LAST_TAP_SYNC_CHUNK + 1, SYNC_EVERY))
    bounds = [round(s * len(pieces) / len(sync_chunks)) for s in range(len(sync_chunks) + 1)]
    glu_a = None
    pieces_done = 0
    for j, col in enumerate(cols):
        if j in sync_chunks:
            s_idx = sync_chunks.index(j)
            group = pieces[bounds[s_idx]:bounds[s_idx + 1]]
            sync([_tap_piece(u_ref, cw_ref, cb_ref, acc_ref, c, r0) for c, r0 in group])
            pieces_done = bounds[s_idx + 1]
            if pieces_done == len(pieces):
                _conv_tail(acc_ref, lng_ref, lnb_ref, wpw_ref, bpw_ref, gcp_ref, yc_ref)
        res = project(col)
        if col < COL_K:
            qkvg_ref[:, col:col + IN_CHUNK] = (res * Q_SCALE).astype(BF16)
        elif col < COL_G_ATTN:
            qkvg_ref[:, col:col + IN_CHUNK] = res.astype(BF16)
        elif col < D_QKVG:
            qkvg_ref[:, col:col + IN_CHUNK] = _silu(res).astype(BF16)
        elif col < COL_GLU_B:
            glu_a = res
        elif col < COL_G_CONV:
            u = glu_a * jax.nn.sigmoid(res)
            t0 = (col - COL_GLU_B) // LANES
            for t in range(t0, t0 + IN_CHUNK // LANES):
                assert pieces_done >= (t + 1) * (tm // TAP_ROWS)
                u_ref[t, 0:CONV_HALO, :] = u_ref[t, tm:tm + CONV_HALO, :]
                u_ref[t, CONV_HALO:CONV_HALO + tm, :] = u[:, (t - t0) * LANES:(t - t0 + 1) * LANES]
        else:
            gc_ref[:, col - COL_G_CONV:col - COL_G_CONV + IN_CHUNK] = _silu(res)


def _inproj_conv_call(x2d, mod, w_in, conv_w, conv_b, ln_g, ln_b, w_pw, b_pw):
    tm = IN_TM
    n = SEQ // tm
    cur = lambda i: jnp.minimum(i, n - 1)
    lag = lambda i: jnp.maximum(i - 1, 0)
    row = lambda i: (0, 0)
    return pl.pallas_call(
        _inproj_conv_kernel,
        grid=(n + 1,),
        in_specs=[
            pl.BlockSpec((tm, D_MODEL), lambda i: (cur(i), 0)),
            pl.BlockSpec((1, D_MODEL), lambda i: (0, 0)),
            pl.BlockSpec((1, D_MODEL), lambda i: (0, 1)),
            pl.BlockSpec(memory_space=pl.ANY),
            pl.BlockSpec((DEPTH, CONV_WIDTH, D_CONV), lambda i: (0, 0, 0)),
            pl.BlockSpec((1, D_CONV), row),
            pl.BlockSpec((1, D_CONV), row),
            pl.BlockSpec((1, D_CONV), row),
            pl.BlockSpec(memory_space=pl.ANY),
            pl.BlockSpec((1, D_CONV), row),
        ],
        out_specs=[
            pl.BlockSpec((tm, D_QKVG), lambda i: (cur(i), 0)),
            pl.BlockSpec((tm, D_CONV), lambda i: (lag(i), 0)),
        ],
        out_shape=[
            jax.ShapeDtypeStruct((SEQ, D_QKVG), BF16),
            jax.ShapeDtypeStruct((SEQ, D_CONV), BF16),
        ],
        scratch_shapes=[
            pltpu.VMEM((tm, D_MODEL), BF16),
            pltpu.VMEM((tm, D_MODEL), BF16),
            pltpu.VMEM((D_CONV // LANES, CONV_HALO + tm, LANES), F32),
            pltpu.VMEM((tm, D_CONV), F32),
            pltpu.VMEM((tm, D_CONV), F32),
            pltpu.VMEM((tm, D_CONV), F32),
            pltpu.VMEM((D_MODEL, D_IN), BF16),
            pltpu.VMEM((D_CONV, D_CONV), BF16),
            pltpu.VMEM((2, IN_STAGE_ROWS, D_IN), F32),
            pltpu.SemaphoreType.DMA((2,)),
        ],
        compiler_params=pltpu.CompilerParams(
            dimension_semantics=("arbitrary",), vmem_limit_bytes=VMEM_LIMIT),
        name="inproj_conv",
    )(x2d, mod, mod, w_in, conv_w, conv_b, ln_g, ln_b, w_pw, b_pw)


def kernel(x, c, w_ada, b_ada, w_in, rel_bias, sinks, conv_w, conv_b, conv_ln_g,
           conv_ln_b, w_pw, b_pw, w_out, ln_g, ln_b):
    assert x.shape == (1, SEQ, D_MODEL) and w_in.shape == (DEPTH, D_MODEL, D_IN)
    x2d = x.reshape(SEQ, D_MODEL)

    row = lambda v: v.reshape(1, -1)
    mod = _mod_call(row(c), w_ada.reshape(D_MODEL, 3 * D_MODEL), row(b_ada))
    bias = _bias_call(rel_bias)
    qkvg, y_conv = _inproj_conv_call(
        x2d, mod, w_in.reshape(D_MODEL, D_IN),
        conv_w, row(conv_b), row(conv_ln_g), row(conv_ln_b),
        w_pw.reshape(D_CONV, D_CONV), row(b_pw))
    out = _attn_out_call(qkvg, bias, sinks.reshape(N_Q_HEADS), y_conv, x2d, mod,
                         w_out.reshape(D_MODEL, D_MODEL), row(ln_g), row(ln_b))
    return out.reshape(1, SEQ, D_MODEL)
```

```python
import math

import jax
import jax.numpy as jnp
import numpy as np
from jax.experimental import pallas as pl
from jax.experimental.pallas import tpu as pltpu

F32 = jnp.float32
BF16 = jnp.bfloat16

D_MODEL = 2048
SEQ = 8192
HEAD_DIM = 64
D_ATTN = 1024
D_CONV = 1024
N_Q_HEADS = 16
N_KV_HEADS = 4
GQA_GROUP = 4
D_KV = 256
WINDOW = 128
BLOCK = 128
CONV_WIDTH = 31
N_BUCKETS = 32
MAX_DISTANCE = 128
LN_EPS = 1e-5
DEPTH = 1
ALPHA = (2.0 * DEPTH) ** 0.25
D_IN = 2 * D_ATTN + 2 * D_KV + 3 * D_CONV

LANES = 128
SUBLANES = 8

COL_Q = 0
COL_K = COL_Q + D_ATTN
COL_V = COL_K + D_KV
COL_G_ATTN = COL_V + D_KV
COL_GLU_A = COL_G_ATTN + D_ATTN
COL_GLU_B = COL_GLU_A + D_CONV
COL_G_CONV = COL_GLU_B + D_CONV
HALF = 512

VMEM_LIMIT = 56 * 1024 * 1024


def _silu(v):
    return v * jax.nn.sigmoid(v)


MOD_STAGE_ROWS = 256


def _mod_kernel(c_ref, w_hbm, b_ref, o_ref, ca_ref, stage_ref, sem_ref):
    n_out = o_ref.shape[1]
    n = D_MODEL // MOD_STAGE_ROWS

    def chunk_copy(j):
        slot = j % 2
        return pltpu.make_async_copy(
            w_hbm.at[pl.ds(j * MOD_STAGE_ROWS, MOD_STAGE_ROWS), :],
            stage_ref.at[slot], sem_ref.at[slot])

    chunk_copy(0).start()
    ca = _silu(c_ref[...])
    ca_ref[...] = jnp.transpose(jnp.broadcast_to(ca, (LANES, D_MODEL)))
    acc = jnp.zeros((SUBLANES, n_out), F32)
    for j in range(n):
        if j + 1 < n:
            chunk_copy(j + 1).start()
        chunk_copy(j).wait()
        rows = slice(j * MOD_STAGE_ROWS, (j + 1) * MOD_STAGE_ROWS)
        prod = jnp.tile(ca_ref[rows, :], (1, n_out // LANES)) * stage_ref[j % 2]
        for r in range(0, MOD_STAGE_ROWS, SUBLANES):
            acc = acc + prod[r:r + SUBLANES, :]
    o_ref[...] = jnp.sum(acc, axis=0, keepdims=True) + b_ref[...]


def _mod_call(c_row, w_ada, b_ada):
    n = w_ada.shape[1]
    return pl.pallas_call(
        _mod_kernel,
        in_specs=[
            pl.BlockSpec((1, D_MODEL), lambda: (0, 0)),
            pl.BlockSpec(memory_space=pl.ANY),
            pl.BlockSpec((1, n), lambda: (0, 0)),
        ],
        out_specs=pl.BlockSpec((1, n), lambda: (0, 0)),
        out_shape=jax.ShapeDtypeStruct((1, n), F32),
        scratch_shapes=[
            pltpu.VMEM((D_MODEL, LANES), F32),
            pltpu.VMEM((2, MOD_STAGE_ROWS, n), F32),
            pltpu.SemaphoreType.DMA((2,)),
        ],
        compiler_params=pltpu.CompilerParams(vmem_limit_bytes=VMEM_LIMIT),
        name="mod",
    )(c_row, w_ada, b_ada)


LOG2E = math.log2(math.e)
Q_SCALE = LOG2E * HEAD_DIM ** -0.5
MASKED = -1e30


def _lane_buckets():
    m = np.arange(2 * BLOCK)
    dist = BLOCK - m
    ok = (dist >= 0) & (dist < WINDOW)
    max_exact = N_BUCKETS // 2
    d = np.maximum(dist, 1).astype(np.float64)
    large = max_exact + (np.log(d / max_exact) / math.log(MAX_DISTANCE / max_exact)
                         * (N_BUCKETS - max_exact)).astype(np.int32)
    large = np.minimum(large, N_BUCKETS - 1)
    bucket = np.where(dist < max_exact, dist, large)
    return np.where(ok, bucket, -1).astype(np.int32).reshape(1, -1)


def _bias_kernel(rb_ref, bm_ref, o_ref):
    bm = bm_ref[...]
    col = jax.lax.broadcasted_iota(jnp.int32, (BLOCK, 2 * BLOCK), 1)
    for h in range(N_Q_HEADS):
        t = jnp.full(bm.shape, MASKED, F32)
        for b in range(N_BUCKETS):
            t = jnp.where(bm == b, rb_ref[b, h] * LOG2E, t)
        x = pltpu.roll(jnp.broadcast_to(t, (BLOCK, 2 * BLOCK)), 0, 1, stride=1, stride_axis=0)
        o_ref[1, h] = x
        o_ref[0, h] = jnp.where(col >= BLOCK, x, MASKED)


def _bias_call(rel_bias):
    shape = (2, N_Q_HEADS, BLOCK, 2 * BLOCK)
    return pl.pallas_call(
        _bias_kernel,
        in_specs=[
            pl.BlockSpec(memory_space=pltpu.SMEM),
            pl.BlockSpec((1, 2 * BLOCK), lambda: (0, 0)),
        ],
        out_specs=pl.BlockSpec(shape, lambda: (0, 0, 0, 0)),
        out_shape=jax.ShapeDtypeStruct(shape, F32),
        name="bias",
    )(rel_bias, jnp.asarray(_lane_buckets()))


def _load_weight_bf16(w_hbm, w_ref, stage_ref, sem_ref, col_scale=None):
    rows, cols = w_ref.shape
    chunk_rows = stage_ref.shape[1]
    n = rows // chunk_rows

    def chunk_copy(j):
        slot = j % 2
        return pltpu.make_async_copy(
            w_hbm.at[pl.ds(j * chunk_rows, chunk_rows), :],
            stage_ref.at[slot, :, pl.ds(0, cols)], sem_ref.at[slot])

    chunk_copy(0).start()
    for j in range(n):
        if j + 1 < n:
            chunk_copy(j + 1).start()
        chunk_copy(j).wait()
        chunk = stage_ref[j % 2, :, 0:cols]
        if col_scale is not None:
            chunk = chunk * col_scale
        w_ref[j * chunk_rows:(j + 1) * chunk_rows, :] = chunk.astype(BF16)


def _token(v):
    t = v[0:SUBLANES, 0:LANES]
    for r in range(SUBLANES, v.shape[0], SUBLANES):
        t = t + v[r:r + SUBLANES, 0:LANES]
    return t


def _zero_of(token):
    return (pltpu.bitcast(token, jnp.uint32) >> 16) >> 16


def _copy_after(src_ref, dst_ref, tokens):
    zero = _zero_of(tokens[0])
    for t in tokens[1:]:
        zero = zero | _zero_of(t)
    bits = pltpu.bitcast(src_ref[...], jnp.uint32)
    reps = (bits.shape[0] // zero.shape[0], bits.shape[1] // zero.shape[1])
    dst_ref[...] = pltpu.bitcast(bits | jnp.tile(zero, reps), dst_ref.dtype)


ATT_TM = 4 * BLOCK
ATT_ROWS = BLOCK
OUT_STAGE_ROWS = 512


def _dup_halves(t):
    swapped = jnp.concatenate([t[:, HEAD_DIM:], t[:, :HEAD_DIM]], axis=1)
    lane = jax.lax.broadcasted_iota(jnp.int32, t.shape, 1)
    low_half = lane < HEAD_DIM
    return jnp.where(low_half, t, swapped), jnp.where(low_half, swapped, t)


def _attn_group(q_ref, g0_ref, g1_ref, k_dup, v_dup, bias_ref, sinks_ref, r0, b0, kv, ya_ref):
    nr = ATT_ROWS
    lane = jax.lax.broadcasted_iota(jnp.int32, (nr, LANES), 1)
    low_half = lane < HEAD_DIM
    rows = slice(r0, r0 + nr)
    tiles = [GQA_GROUP * kv // 2 + t for t in range(GQA_GROUP // 2)]
    stacked = []
    for tile in tiles:
        q_tile = q_ref[rows, tile * LANES:(tile + 1) * LANES]
        zero = jnp.zeros_like(q_tile)
        stacked += [jnp.where(low_half, q_tile, zero), jnp.where(low_half, zero, q_tile)]
    s_all = jax.lax.dot_general(jnp.concatenate(stacked, axis=0), k_dup,
                                (((1,), (1,)), ((), ())),
                                preferred_element_type=F32)
    probs, sums, maxes, sink_vals = [], [], [], []
    for r in range(GQA_GROUP):
        head = GQA_GROUP * kv + r
        s = s_all[r * nr:(r + 1) * nr] + bias_ref[0, head, b0:b0 + nr, :]
        sink_vals.append(sinks_ref[head] * LOG2E)
        m = jnp.max(s, axis=1, keepdims=True)
        e = jnp.exp2(s - m)
        sums.append(jnp.sum(e, axis=1, keepdims=True))
        maxes.append(m)
        probs.append(e.astype(BF16))
    denoms = []
    for t in range(GQA_GROUP // 2):
        pick = lambda a, b: jnp.where(low_half, a, b)
        denoms.append(pick(sums[2 * t], sums[2 * t + 1])
                      + jnp.exp2(pick(sink_vals[2 * t], sink_vals[2 * t + 1])
                                 - pick(maxes[2 * t], maxes[2 * t + 1])))
    o_all = jnp.dot(jnp.concatenate(probs, axis=0), v_dup,
                    preferred_element_type=F32)
    outs = [o_all[r * nr:(r + 1) * nr] for r in range(GQA_GROUP)]
    for t, tile in enumerate(tiles):
        y = jnp.where(low_half, outs[2 * t], outs[2 * t + 1]) / denoms[t]
        g_ref = g0_ref if tile < HALF // LANES else g1_ref
        gl = (tile * LANES) % HALF
        gate = g_ref[rows, gl:gl + LANES].astype(F32)
        ya_ref[rows, tile * LANES:(tile + 1) * LANES] = (y * gate).astype(BF16)


def _attn_tile(sinks_ref, q_ref, g0_ref, g1_ref, k_ref, v_ref, kp_ref, vp_ref,
               bias_refs, ya_ref):
    n_lt = D_KV // LANES
    lt = lambda t: slice(t * LANES, (t + 1) * LANES)

    def block_operands(b):
        r0 = b * BLOCK
        if b == 0:
            k_prev = [kp_ref[:, lt(t)] for t in range(n_lt)]
            v_prev = [vp_ref[:, lt(t)] for t in range(n_lt)]
        else:
            k_prev = [k_ref[r0 - BLOCK:r0, lt(t)] for t in range(n_lt)]
            v_prev = [v_ref[r0 - BLOCK:r0, lt(t)] for t in range(n_lt)]
        k_rows = [jnp.concatenate([k_prev[t], k_ref[r0:r0 + BLOCK, lt(t)]], axis=0)
                  for t in range(n_lt)]
        v_rows = [jnp.concatenate([v_prev[t], v_ref[r0:r0 + BLOCK, lt(t)]], axis=0)
                  for t in range(n_lt)]
        return ([_dup_halves(kt) for kt in k_rows], [_dup_halves(vt) for vt in v_rows])

    for b in range(ATT_TM // BLOCK):
        k_dups, v_dups = block_operands(b)
        for b0 in range(0, BLOCK, ATT_ROWS):
            for kv in range(N_KV_HEADS):
                _attn_group(q_ref, g0_ref, g1_ref, k_dups[kv // 2][kv % 2],
                            v_dups[kv // 2][kv % 2], bias_refs[b], sinks_ref,
                            b * BLOCK + b0, b0, kv, ya_ref)


def _attn_out_kernel(sinks_ref, q_ref, g0_ref, g1_ref, k_ref, v_ref, kp_ref, vp_ref,
                     bias0_ref, bias1_ref, yc_ref, x_ref, gate_ref, wout_hbm,
                     lng_ref, lnb_ref, o_ref, ya_ref, wout_ref, stage_ref, sem_ref):
    i = pl.program_id(0)

    @pl.when(i == 0)
    def _():
        ya_ref[...] = jnp.zeros(ya_ref.shape, ya_ref.dtype)
        _load_weight_bf16(wout_hbm, wout_ref, stage_ref, sem_ref,
                          col_scale=gate_ref[...] * (1.0 / ALPHA))

    y = jnp.dot(ya_ref[...], wout_ref[0:D_ATTN, :], preferred_element_type=F32)
    y = y + jnp.dot(yc_ref[...], wout_ref[D_ATTN:D_MODEL, :], preferred_element_type=F32)
    z = x_ref[...] + y
    o_ref[...] = z
    mu = jnp.mean(z, axis=1, keepdims=True)
    d = o_ref[...] - mu
    var = jnp.mean(d * d, axis=1, keepdims=True)
    rstd = jax.lax.rsqrt(var + LN_EPS / ALPHA ** 2)
    o_ref[...] = (o_ref[...] - mu) * rstd * lng_ref[...] + lnb_ref[...]

    _attn_tile(sinks_ref, q_ref, g0_ref, g1_ref, k_ref, v_ref, kp_ref, vp_ref,
               [bias0_ref] + [bias1_ref] * (ATT_TM // BLOCK - 1), ya_ref)


def _attn_out_call(qkvg, bias, sinks, y_conv, x2d, mod, w_out, ln_g, ln_b):
    tm = ATT_TM
    n = SEQ // tm
    clamp = lambda t: jnp.clip(t, 0, n - 1)
    cur = lambda i: clamp(i)
    lag = lambda i: clamp(i - 1)
    kv_prev = lambda i: jnp.maximum(cur(i) * (tm // BLOCK) - 1, 0)
    row = lambda i: (0, 0)
    return pl.pallas_call(
        _attn_out_kernel,
        grid=(n + 1,),
        in_specs=[
            pl.BlockSpec(memory_space=pltpu.SMEM),
            pl.BlockSpec((tm, D_ATTN), lambda i: (cur(i), COL_Q // D_ATTN)),
            pl.BlockSpec((tm, HALF), lambda i: (cur(i), COL_G_ATTN // HALF)),
            pl.BlockSpec((tm, HALF), lambda i: (cur(i), COL_G_ATTN // HALF + 1)),
            pl.BlockSpec((tm, D_KV), lambda i: (cur(i), COL_K // D_KV)),
            pl.BlockSpec((tm, D_KV), lambda i: (cur(i), COL_V // D_KV)),
            pl.BlockSpec((BLOCK, D_KV), lambda i: (kv_prev(i), COL_K // D_KV)),
            pl.BlockSpec((BLOCK, D_KV), lambda i: (kv_prev(i), COL_V // D_KV)),
            pl.BlockSpec((1, N_Q_HEADS, BLOCK, 2 * BLOCK),
                         lambda i: (jnp.minimum(cur(i), 1), 0, 0, 0)),
            pl.BlockSpec((1, N_Q_HEADS, BLOCK, 2 * BLOCK), lambda i: (1, 0, 0, 0)),
            pl.BlockSpec((tm, D_CONV), lambda i: (lag(i), 0)),
            pl.BlockSpec((tm, D_MODEL), lambda i: (lag(i), 0)),
            pl.BlockSpec((1, D_MODEL), lambda i: (0, 2)),
            pl.BlockSpec(memory_space=pl.ANY),
            pl.BlockSpec((1, D_MODEL), row),
            pl.BlockSpec((1, D_MODEL), row),
        ],
        out_specs=pl.BlockSpec((tm, D_MODEL), lambda i: (lag(i), 0)),
        out_shape=jax.ShapeDtypeStruct((SEQ, D_MODEL), F32),
        scratch_shapes=[
            pltpu.VMEM((tm, D_ATTN), BF16),
            pltpu.VMEM((D_MODEL, D_MODEL), BF16),
            pltpu.VMEM((2, OUT_STAGE_ROWS, D_MODEL), F32),
            pltpu.SemaphoreType.DMA((2,)),
        ],
        compiler_params=pltpu.CompilerParams(
            dimension_semantics=("arbitrary",), vmem_limit_bytes=VMEM_LIMIT),
        name="attn_out",
    )(sinks, qkvg, qkvg, qkvg, qkvg, qkvg, qkvg, qkvg, bias, bias, y_conv, x2d, mod,
      w_out, ln_g, ln_b)


IN_TM = 256
IN_CHUNK = 256
D_QKVG = COL_GLU_A
CONV_HALO = 32
IN_STAGE_ROWS = 256
TAP_ROWS = 32
SYNC_EVERY = 1
LAST_TAP_SYNC_CHUNK = 20


def _tap_piece(u_ref, cw_ref, cb_ref, acc_ref, c, r0):
    off = CONV_HALO - (CONV_WIDTH - 1)
    cs = slice(c * LANES, (c + 1) * LANES)
    acc = jnp.broadcast_to(cb_ref[:, cs], (TAP_ROWS, LANES))
    for k in range(CONV_WIDTH):
        acc = acc + cw_ref[0, k:k + 1, cs] * u_ref[c, r0 + off + k:r0 + off + k + TAP_ROWS, :]
    acc_ref[r0:r0 + TAP_ROWS, cs] = acc
    return _token(acc)


def _conv_tail(acc_ref, lng_ref, lnb_ref, wpw_ref, bpw_ref, gate_ref, yc_ref):
    v = acc_ref[...]
    mu = jnp.mean(v, axis=1, keepdims=True)
    d = v - mu
    var = jnp.mean(d * d, axis=1, keepdims=True)
    y = d * jax.lax.rsqrt(var + LN_EPS) * lng_ref[...] + lnb_ref[...]
    s = _silu(y).astype(BF16)
    z = jnp.dot(s, wpw_ref[...], preferred_element_type=F32) + bpw_ref[...]
    yc_ref[...] = (z * gate_ref[...]).astype(BF16)


def _inproj_conv_kernel(x_ref, shift_ref, scale_ref, w_hbm, cw_ref, cb_ref, lng_ref,
                        lnb_ref, wpw_hbm, bpw_ref, qkvg_ref, yc_ref,
                        h0_ref, h1_ref, u_ref, gc_ref, gcp_ref, acc_ref,
                        w_ref, wpw_ref, stage_ref, sem_ref):
    i = pl.program_id(0)
    tm = IN_TM
    pieces = [(c, r0) for c in range(D_CONV // LANES) for r0 in range(0, tm, TAP_ROWS)]

    @pl.when(i == 0)
    def _():
        u_ref[...] = jnp.zeros(u_ref.shape, u_ref.dtype)
        gc_ref[...] = jnp.zeros(gc_ref.shape, gc_ref.dtype)
        _load_weight_bf16(w_hbm, w_ref, stage_ref, sem_ref)
        _load_weight_bf16(wpw_hbm, wpw_ref, stage_ref, sem_ref)

    gcp_ref[...] = gc_ref[...]
    h = x_ref[...] * (1.0 + scale_ref[...]) + shift_ref[...]
    h0_ref[...] = h.astype(BF16)
    h_bufs = [h0_ref, h1_ref]
    state = {"cur": 0}

    def sync(tokens):
        src, dst = h_bufs[state["cur"]], h_bufs[1 - state["cur"]]
        _copy_after(src, dst, tokens)
        state["cur"] = 1 - state["cur"]

    def project(col):
        return jnp.dot(h_bufs[state["cur"]][...], w_ref[:, col:col + IN_CHUNK],
                       preferred_element_type=F32)

    cols = list(range(0, D_QKVG, IN_CHUNK)) + list(range(COL_G_CONV, D_IN, IN_CHUNK))
    for c in range(D_CONV // IN_CHUNK):
        cols += [COL_GLU_A + c * IN_CHUNK, COL_GLU_B + c * IN_CHUNK]
    sync_chunks = list(range(SYNC_EVERY, LAST_TAP_SYNC_CHUNK + 1, SYNC_EVERY))
    bounds = [round(s * len(pieces) / len(sync_chunks)) for s in range(len(sync_chunks) + 1)]
    glu_a = None
    pieces_done = 0
    for j, col in enumerate(cols):
        if j in sync_chunks:
            s_idx = sync_chunks.index(j)
            group = pieces[bounds[s_idx]:bounds[s_idx + 1]]
            sync([_tap_piece(u_ref, cw_ref, cb_ref, acc_ref, c, r0) for c, r0 in group])
            pieces_done = bounds[s_idx + 1]
            if pieces_done == len(pieces):
                _conv_tail(acc_ref, lng_ref, lnb_ref, wpw_ref, bpw_ref, gcp_ref, yc_ref)
        res = project(col)
        if col < COL_K:
            qkvg_ref[:, col:col + IN_CHUNK] = (res * Q_SCALE).astype(BF16)
        elif col < COL_G_ATTN:
            qkvg_ref[:, col:col + IN_CHUNK] = res.astype(BF16)
        elif col < D_QKVG:
            qkvg_ref[:, col:col + IN_CHUNK] = _silu(res).astype(BF16)
        elif col < COL_GLU_B:
            glu_a = res
        elif col < COL_G_CONV:
            u = glu_a * jax.nn.sigmoid(res)
            t0 = (col - COL_GLU_B) // LANES
            for t in range(t0, t0 + IN_CHUNK // LANES):
                assert pieces_done >= (t + 1) * (tm // TAP_ROWS)
                u_ref[t, 0:CONV_HALO, :] = u_ref[t, tm:tm + CONV_HALO, :]
                u_ref[t, CONV_HALO:CONV_HALO + tm, :] = u[:, (t - t0) * LANES:(t - t0 + 1) * LANES]
        else:
            gc_ref[:, col - COL_G_CONV:col - COL_G_CONV + IN_CHUNK] = _silu(res)


def _inproj_conv_call(x2d, mod, w_in, conv_w, conv_b, ln_g, ln_b, w_pw, b_pw):
    tm = IN_TM
    n = SEQ // tm
    cur = lambda i: jnp.minimum(i, n - 1)
    lag = lambda i: jnp.maximum(i - 1, 0)
    row = lambda i: (0, 0)
    return pl.pallas_call(
        _inproj_conv_kernel,
        grid=(n + 1,),
        in_specs=[
            pl.BlockSpec((tm, D_MODEL), lambda i: (cur(i), 0)),
            pl.BlockSpec((1, D_MODEL), lambda i: (0, 0)),
            pl.BlockSpec((1, D_MODEL), lambda i: (0, 1)),
            pl.BlockSpec(memory_space=pl.ANY),
            pl.BlockSpec((DEPTH, CONV_WIDTH, D_CONV), lambda i: (0, 0, 0)),
            pl.BlockSpec((1, D_CONV), row),
            pl.BlockSpec((1, D_CONV), row),
            pl.BlockSpec((1, D_CONV), row),
            pl.BlockSpec(memory_space=pl.ANY),
            pl.BlockSpec((1, D_CONV), row),
        ],
        out_specs=[
            pl.BlockSpec((tm, D_QKVG), lambda i: (cur(i), 0)),
            pl.BlockSpec((tm, D_CONV), lambda i: (lag(i), 0)),
        ],
        out_shape=[
            jax.ShapeDtypeStruct((SEQ, D_QKVG), BF16),
            jax.ShapeDtypeStruct((SEQ, D_CONV), BF16),
        ],
        scratch_shapes=[
            pltpu.VMEM((tm, D_MODEL), BF16),
            pltpu.VMEM((tm, D_MODEL), BF16),
            pltpu.VMEM((D_CONV // LANES, CONV_HALO + tm, LANES), F32),
            pltpu.VMEM((tm, D_CONV), F32),
            pltpu.VMEM((tm, D_CONV), F32),
            pltpu.VMEM((tm, D_CONV), F32),
            pltpu.VMEM((D_MODEL, D_IN), BF16),
            pltpu.VMEM((D_CONV, D_CONV), BF16),
            pltpu.VMEM((2, IN_STAGE_ROWS, D_IN), F32),
            pltpu.SemaphoreType.DMA((2,)),
        ],
        compiler_params=pltpu.CompilerParams(
            dimension_semantics=("arbitrary",), vmem_limit_bytes=VMEM_LIMIT),
        name="inproj_conv",
    )(x2d, mod, mod, w_in, conv_w, conv_b, ln_g, ln_b, w_pw, b_pw)


def kernel(x, c, w_ada, b_ada, w_in, rel_bias, sinks, conv_w, conv_b, conv_ln_g,
           conv_ln_b, w_pw, b_pw, w_out, ln_g, ln_b):
    assert x.shape == (1, SEQ, D_MODEL) and w_in.shape == (DEPTH, D_MODEL, D_IN)
    x2d = x.reshape(SEQ, D_MODEL)

    row = lambda v: v.reshape(1, -1)
    mod = _mod_call(row(c), w_ada.reshape(D_MODEL, 3 * D_MODEL), row(b_ada))
    bias = _bias_call(rel_bias)
    qkvg, y_conv = _inproj_conv_call(
        x2d, mod, w_in.reshape(D_MODEL, D_IN),
        conv_w, row(conv_b), row(conv_ln_g), row(conv_ln_b),
        w_pw.reshape(D_CONV, D_CONV), row(b_pw))
    out = _attn_out_call(qkvg, bias, sinks.reshape(N_Q_HEADS), y_conv, x2d, mod,
                         w_out.reshape(D_MODEL, D_MODEL), row(ln_g), row(ln_b))
    return out.reshape(1, SEQ, D_MODEL)
```

```python
import math

import jax
import jax.numpy as jnp
import numpy as np
from jax.experimental import pallas as pl
from jax.experimental.pallas import tpu as pltpu

F32 = jnp.float32
BF16 = jnp.bfloat16

D_MODEL = 2048
SEQ = 8192
HEAD_DIM = 64
D_ATTN = 1024
D_CONV = 1024
N_Q_HEADS = 16
N_KV_HEADS = 4
GQA_GROUP = 4
D_KV = 256
WINDOW = 128
BLOCK = 128
CONV_WIDTH = 31
N_BUCKETS = 32
MAX_DISTANCE = 128
LN_EPS = 1e-5
DEPTH = 1
ALPHA = (2.0 * DEPTH) ** 0.25
D_IN = 2 * D_ATTN + 2 * D_KV + 3 * D_CONV

LANES = 128
SUBLANES = 8

COL_Q = 0
COL_K = COL_Q + D_ATTN
COL_V = COL_K + D_KV
COL_G_ATTN = COL_V + D_KV
COL_GLU_A = COL_G_ATTN + D_ATTN
COL_GLU_B = COL_GLU_A + D_CONV
COL_G_CONV = COL_GLU_B + D_CONV
HALF = 512

VMEM_LIMIT = 56 * 1024 * 1024


def _silu(v):
    return v * jax.nn.sigmoid(v)


MOD_STAGE_ROWS = 256


def _mod_kernel(c_ref, w_hbm, b_ref, o_ref, ca_ref, stage_ref, sem_ref):
    n_out = o_ref.shape[1]
    n = D_MODEL // MOD_STAGE_ROWS

    def chunk_copy(j):
        slot = j % 2
        return pltpu.make_async_copy(
            w_hbm.at[pl.ds(j * MOD_STAGE_ROWS, MOD_STAGE_ROWS), :],
            stage_ref.at[slot], sem_ref.at[slot])

    chunk_copy(0).start()
    ca = _silu(c_ref[...])
    ca_ref[...] = jnp.transpose(jnp.broadcast_to(ca, (LANES, D_MODEL)))
    acc = jnp.zeros((SUBLANES, n_out), F32)
    for j in range(n):
        if j + 1 < n:
            chunk_copy(j + 1).start()
        chunk_copy(j).wait()
        rows = slice(j * MOD_STAGE_ROWS, (j + 1) * MOD_STAGE_ROWS)
        prod = jnp.tile(ca_ref[rows, :], (1, n_out // LANES)) * stage_ref[j % 2]
        for r in range(0, MOD_STAGE_ROWS, SUBLANES):
            acc = acc + prod[r:r + SUBLANES, :]
    o_ref[...] = jnp.sum(acc, axis=0, keepdims=True) + b_ref[...]


def _mod_call(c_row, w_ada, b_ada):
    n = w_ada.shape[1]
    return pl.pallas_call(
        _mod_kernel,
        in_specs=[
            pl.BlockSpec((1, D_MODEL), lambda: (0, 0)),
            pl.BlockSpec(memory_space=pl.ANY),
            pl.BlockSpec((1, n), lambda: (0, 0)),
        ],
        out_specs=pl.BlockSpec((1, n), lambda: (0, 0)),
        out_shape=jax.ShapeDtypeStruct((1, n), F32),
        scratch_shapes=[
            pltpu.VMEM((D_MODEL, LANES), F32),
            pltpu.VMEM((2, MOD_STAGE_ROWS, n), F32),
            pltpu.SemaphoreType.DMA((2,)),
        ],
        compiler_params=pltpu.CompilerParams(vmem_limit_bytes=VMEM_LIMIT),
        name="mod",
    )(c_row, w_ada, b_ada)


LOG2E = math.log2(math.e)
Q_SCALE = LOG2E * HEAD_DIM ** -0.5
MASKED = -1e30


def _lane_buckets():
    m = np.arange(2 * BLOCK)
    dist = BLOCK - m
    ok = (dist >= 0) & (dist < WINDOW)
    max_exact = N_BUCKETS // 2
    d = np.maximum(dist, 1).astype(np.float64)
    large = max_exact + (np.log(d / max_exact) / math.log(MAX_DISTANCE / max_exact)
                         * (N_BUCKETS - max_exact)).astype(np.int32)
    large = np.minimum(large, N_BUCKETS - 1)
    bucket = np.where(dist < max_exact, dist, large)
    return np.where(ok, bucket, -1).astype(np.int32).reshape(1, -1)


def _bias_kernel(rb_ref, bm_ref, o_ref):
    bm = bm_ref[...]
    col = jax.lax.broadcasted_iota(jnp.int32, (BLOCK, 2 * BLOCK), 1)
    for h in range(N_Q_HEADS):
        t = jnp.full(bm.shape, MASKED, F32)
        for b in range(N_BUCKETS):
            t = jnp.where(bm == b, rb_ref[b, h] * LOG2E, t)
        x = pltpu.roll(jnp.broadcast_to(t, (BLOCK, 2 * BLOCK)), 0, 1, stride=1, stride_axis=0)
        o_ref[1, h] = x
        o_ref[0, h] = jnp.where(col >= BLOCK, x, MASKED)


def _bias_call(rel_bias):
    shape = (2, N_Q_HEADS, BLOCK, 2 * BLOCK)
    return pl.pallas_call(
        _bias_kernel,
        in_specs=[
            pl.BlockSpec(memory_space=pltpu.SMEM),
            pl.BlockSpec((1, 2 * BLOCK), lambda: (0, 0)),
        ],
        out_specs=pl.BlockSpec(shape, lambda: (0, 0, 0, 0)),
        out_shape=jax.ShapeDtypeStruct(shape, F32),
        name="bias",
    )(rel_bias, jnp.asarray(_lane_buckets()))


def _load_weight_bf16(w_hbm, w_ref, stage_ref, sem_ref, col_scale=None):
    rows, cols = w_ref.shape
    chunk_rows = stage_ref.shape[1]
    n = rows // chunk_rows

    def chunk_copy(j):
        slot = j % 2
        return pltpu.make_async_copy(
            w_hbm.at[pl.ds(j * chunk_rows, chunk_rows), :],
            stage_ref.at[slot, :, pl.ds(0, cols)], sem_ref.at[slot])

    chunk_copy(0).start()
    for j in range(n):
        if j + 1 < n:
            chunk_copy(j + 1).start()
        chunk_copy(j).wait()
        chunk = stage_ref[j % 2, :, 0:cols]
        if col_scale is not None:
            chunk = chunk * col_scale
        w_ref[j * chunk_rows:(j + 1) * chunk_rows, :] = chunk.astype(BF16)


def _token(v):
    t = v[0:SUBLANES, 0:LANES]
    for r in range(SUBLANES, v.shape[0], SUBLANES):
        t = t + v[r:r + SUBLANES, 0:LANES]
    return t


def _zero_of(token):
    return (pltpu.bitcast(token, jnp.uint32) >> 16) >> 16


def _copy_after(src_ref, dst_ref, tokens):
    zero = _zero_of(tokens[0])
    for t in tokens[1:]:
        zero = zero | _zero_of(t)
    bits = pltpu.bitcast(src_ref[...], jnp.uint32)
    reps = (bits.shape[0] // zero.shape[0], bits.shape[1] // zero.shape[1])
    dst_ref[...] = pltpu.bitcast(bits | jnp.tile(zero, reps), dst_ref.dtype)


ATT_TM = 4 * BLOCK
ATT_ROWS = BLOCK
OUT_STAGE_ROWS = 512


def _dup_halves(t):
    swapped = jnp.concatenate([t[:, HEAD_DIM:], t[:, :HEAD_DIM]], axis=1)
    lane = jax.lax.broadcasted_iota(jnp.int32, t.shape, 1)
    low_half = lane < HEAD_DIM
    return jnp.where(low_half, t, swapped), jnp.where(low_half, swapped, t)


def _attn_group(q_ref, g0_ref, g1_ref, k_dup, v_dup, bias_ref, sinks_ref, r0, b0, kv, ya_ref):
    nr = ATT_ROWS
    lane = jax.lax.broadcasted_iota(jnp.int32, (nr, LANES), 1)
    low_half = lane < HEAD_DIM
    rows = slice(r0, r0 + nr)
    tiles = [GQA_GROUP * kv // 2 + t for t in range(GQA_GROUP // 2)]
    stacked = []
    for tile in tiles:
        q_tile = q_ref[rows, tile * LANES:(tile + 1) * LANES]
        zero = jnp.zeros_like(q_tile)
        stacked += [jnp.where(low_half, q_tile, zero), jnp.where(low_half, zero, q_tile)]
    s_all = jax.lax.dot_general(jnp.concatenate(stacked, axis=0), k_dup,
                                (((1,), (1,)), ((), ())),
                                preferred_element_type=F32)
    probs, sums, maxes, sink_vals = [], [], [], []
    for r in range(GQA_GROUP):
        head = GQA_GROUP * kv + r
        s = s_all[r * nr:(r + 1) * nr] + bias_ref[0, head, b0:b0 + nr, :]
        sink_vals.append(sinks_ref[head] * LOG2E)
        m = jnp.max(s, axis=1, keepdims=True)
        e = jnp.exp2(s - m)
        sums.append(jnp.sum(e, axis=1, keepdims=True))
        maxes.append(m)
        probs.append(e.astype(BF16))
    denoms = []
    for t in range(GQA_GROUP // 2):
        pick = lambda a, b: jnp.where(low_half, a, b)
        denoms.append(pick(sums[2 * t], sums[2 * t + 1])
                      + jnp.exp2(pick(sink_vals[2 * t], sink_vals[2 * t + 1])
                                 - pick(maxes[2 * t], maxes[2 * t + 1])))
    o_all = jnp.dot(jnp.concatenate(probs, axis=0), v_dup,
                    preferred_element_type=F32)
    outs = [o_all[r * nr:(r + 1) * nr] for r in range(GQA_GROUP)]
    for t, tile in enumerate(tiles):
        y = jnp.where(low_half, outs[2 * t], outs[2 * t + 1]) / denoms[t]
        g_ref = g0_ref if tile < HALF // LANES else g1_ref
        gl = (tile * LANES) % HALF
        gate = g_ref[rows, gl:gl + LANES].astype(F32)
        ya_ref[rows, tile * LANES:(tile + 1) * LANES] = (y * gate).astype(BF16)


def _attn_tile(sinks_ref, q_ref, g0_ref, g1_ref, k_ref, v_ref, kp_ref, vp_ref,
               bias_refs, ya_ref):
    n_lt = D_KV // LANES
    lt = lambda t: slice(t * LANES, (t + 1) * LANES)

    def block_operands(b):
        r0 = b * BLOCK
        if b == 0:
            k_prev = [kp_ref[:, lt(t)] for t in range(n_lt)]
            v_prev = [vp_ref[:, lt(t)] for t in range(n_lt)]
        else:
            k_prev = [k_ref[r0 - BLOCK:r0, lt(t)] for t in range(n_lt)]
            v_prev = [v_ref[r0 - BLOCK:r0, lt(t)] for t in range(n_lt)]
        k_rows = [jnp.concatenate([k_prev[t], k_ref[r0:r0 + BLOCK, lt(t)]], axis=0)
                  for t in range(n_lt)]
        v_rows = [jnp.concatenate([v_prev[t], v_ref[r0:r0 + BLOCK, lt(t)]], axis=0)
                  for t in range(n_lt)]
        return ([_dup_halves(kt) for kt in k_rows], [_dup_halves(vt) for vt in v_rows])

    for b in range(ATT_TM // BLOCK):
        k_dups, v_dups = block_operands(b)
        for b0 in range(0, BLOCK, ATT_ROWS):
            for kv in range(N_KV_HEADS):
                _attn_group(q_ref, g0_ref, g1_ref, k_dups[kv // 2][kv % 2],
                            v_dups[kv // 2][kv % 2], bias_refs[b], sinks_ref,
                            b * BLOCK + b0, b0, kv, ya_ref)


def _attn_out_kernel(sinks_ref, q_ref, g0_ref, g1_ref, k_ref, v_ref, kp_ref, vp_ref,
                     bias0_ref, bias1_ref, yc_ref, x_ref, gate_ref, wout_hbm,
                     lng_ref, lnb_ref, o_ref, ya_ref, wout_ref, stage_ref, sem_ref):
    i = pl.program_id(0)

    @pl.when(i == 0)
    def _():
        ya_ref[...] = jnp.zeros(ya_ref.shape, ya_ref.dtype)
        _load_weight_bf16(wout_hbm, wout_ref, stage_ref, sem_ref,
                          col_scale=gate_ref[...] * (1.0 / ALPHA))

    y = jnp.dot(jnp.concatenate([ya_ref[...], yc_ref[...]], axis=1), wout_ref[...],
                preferred_element_type=F32)
    z = x_ref[...] + y
    o_ref[...] = z
    mu = jnp.mean(z, axis=1, keepdims=True)
    d = o_ref[...] - mu
    var = jnp.mean(d * d, axis=1, keepdims=True)
    rstd = jax.lax.rsqrt(var + LN_EPS / ALPHA ** 2)
    o_ref[...] = (o_ref[...] - mu) * rstd * lng_ref[...] + lnb_ref[...]

    _attn_tile(sinks_ref, q_ref, g0_ref, g1_ref, k_ref, v_ref, kp_ref, vp_ref,
               [bias0_ref] + [bias1_ref] * (ATT_TM // BLOCK - 1), ya_ref)


def _attn_out_call(qkvg, bias, sinks, y_conv, x2d, mod, w_out, ln_g, ln_b):
    tm = ATT_TM
    n = SEQ // tm
    clamp = lambda t: jnp.clip(t, 0, n - 1)
    cur = lambda i: clamp(i)
    lag = lambda i: clamp(i - 1)
    kv_prev = lambda i: jnp.maximum(cur(i) * (tm // BLOCK) - 1, 0)
    row = lambda i: (0, 0)
    return pl.pallas_call(
        _attn_out_kernel,
        grid=(n + 1,),
        in_specs=[
            pl.BlockSpec(memory_space=pltpu.SMEM),
            pl.BlockSpec((tm, D_ATTN), lambda i: (cur(i), COL_Q // D_ATTN)),
            pl.BlockSpec((tm, HALF), lambda i: (cur(i), COL_G_ATTN // HALF)),
            pl.BlockSpec((tm, HALF), lambda i: (cur(i), COL_G_ATTN // HALF + 1)),
            pl.BlockSpec((tm, D_KV), lambda i: (cur(i), COL_K // D_KV)),
            pl.BlockSpec((tm, D_KV), lambda i: (cur(i), COL_V // D_KV)),
            pl.BlockSpec((BLOCK, D_KV), lambda i: (kv_prev(i), COL_K // D_KV)),
            pl.BlockSpec((BLOCK, D_KV), lambda i: (kv_prev(i), COL_V // D_KV)),
            pl.BlockSpec((1, N_Q_HEADS, BLOCK, 2 * BLOCK),
                         lambda i: (jnp.minimum(cur(i), 1), 0, 0, 0)),
            pl.BlockSpec((1, N_Q_HEADS, BLOCK, 2 * BLOCK), lambda i: (1, 0, 0, 0)),
            pl.BlockSpec((tm, D_CONV), lambda i: (lag(i), 0)),
            pl.BlockSpec((tm, D_MODEL), lambda i: (lag(i), 0)),
            pl.BlockSpec((1, D_MODEL), lambda i: (0, 2)),
            pl.BlockSpec(memory_space=pl.ANY),
            pl.BlockSpec((1, D_MODEL), row),
            pl.BlockSpec((1, D_MODEL), row),
        ],
        out_specs=pl.BlockSpec((tm, D_MODEL), lambda i: (lag(i), 0)),
        out_shape=jax.ShapeDtypeStruct((SEQ, D_MODEL), F32),
        scratch_shapes=[
            pltpu.VMEM((tm, D_ATTN), BF16),
            pltpu.VMEM((D_MODEL, D_MODEL), BF16),
            pltpu.VMEM((2, OUT_STAGE_ROWS, D_MODEL), F32),
            pltpu.SemaphoreType.DMA((2,)),
        ],
        compiler_params=pltpu.CompilerParams(
            dimension_semantics=("arbitrary",), vmem_limit_bytes=VMEM_LIMIT),
        name="attn_out",
    )(sinks, qkvg, qkvg, qkvg, qkvg, qkvg, qkvg, qkvg, bias, bias, y_conv, x2d, mod,
      w_out, ln_g, ln_b)


IN_TM = 256
IN_CHUNK = 256
D_QKVG = COL_GLU_A
CONV_HALO = 32
IN_STAGE_ROWS = 256
TAP_ROWS = 32
SYNC_EVERY = 1
LAST_TAP_SYNC_CHUNK = 20


def _tap_piece(u_ref, cw_ref, cb_ref, acc_ref, c, r0):
    off = CONV_HALO - (CONV_WIDTH - 1)
    cs = slice(c * LANES, (c + 1) * LANES)
    acc = jnp.broadcast_to(cb_ref[:, cs], (TAP_ROWS, LANES))
    for k in range(CONV_WIDTH):
        acc = acc + cw_ref[0, k:k + 1, cs] * u_ref[c, r0 + off + k:r0 + off + k + TAP_ROWS, :]
    acc_ref[r0:r0 + TAP_ROWS, cs] = acc
    return _token(acc)


def _conv_tail(acc_ref, lng_ref, lnb_ref, wpw_ref, bpw_ref, gate_ref, yc_ref):
    v = acc_ref[...]
    mu = jnp.mean(v, axis=1, keepdims=True)
    d = v - mu
    var = jnp.mean(d * d, axis=1, keepdims=True)
    y = d * jax.lax.rsqrt(var + LN_EPS) * lng_ref[...] + lnb_ref[...]
    s = _silu(y).astype(BF16)
    z = jnp.dot(s, wpw_ref[...], preferred_element_type=F32) + bpw_ref[...]
    yc_ref[...] = (z * gate_ref[...]).astype(BF16)


def _inproj_conv_kernel(x_ref, shift_ref, scale_ref, w_hbm, cw_ref, cb_ref, lng_ref,
                        lnb_ref, wpw_hbm, bpw_ref, qkvg_ref, yc_ref,
                        h0_ref, h1_ref, u_ref, gc_ref, gcp_ref, acc_ref,
                        w_ref, wpw_ref, stage_ref, sem_ref):
    i = pl.program_id(0)
    tm = IN_TM
    pieces = [(c, r0) for c in range(D_CONV // LANES) for r0 in range(0, tm, TAP_ROWS)]

    @pl.when(i == 0)
    def _():
        u_ref[...] = jnp.zeros(u_ref.shape, u_ref.dtype)
        gc_ref[...] = jnp.zeros(gc_ref.shape, gc_ref.dtype)
        _load_weight_bf16(w_hbm, w_ref, stage_ref, sem_ref)
        _load_weight_bf16(wpw_hbm, wpw_ref, stage_ref, sem_ref)

    gcp_ref[...] = gc_ref[...]
    h = x_ref[...] * (1.0 + scale_ref[...]) + shift_ref[...]
    h0_ref[...] = h.astype(BF16)
    h_bufs = [h0_ref, h1_ref]
    state = {"cur": 0}

    def sync(tokens):
        src, dst = h_bufs[state["cur"]], h_bufs[1 - state["cur"]]
        _copy_after(src, dst, tokens)
        state["cur"] = 1 - state["cur"]

    def project(col):
        return jnp.dot(h_bufs[state["cur"]][...], w_ref[:, col:col + IN_CHUNK],
                       preferred_element_type=F32)

    cols = list(range(0, D_QKVG, IN_CHUNK)) + list(range(COL_G_CONV, D_IN, IN_CHUNK))
    for c in range(D_CONV // IN_CHUNK):
        cols += [COL_GLU_A + c * IN_CHUNK, COL_GLU_B + c * IN_CHUNK]
    sync_chunks = list(range(SYNC_EVERY, LAST_TAP_SYNC_CHUNK + 1, SYNC_EVERY))
    bounds = [round(s * len(pieces) / len(sync_chunks)) for s in range(len(sync_chunks) + 1)]
    glu_a = None
    pieces_done = 0
    for j, col in enumerate(cols):
        if j in sync_chunks:
            s_idx = sync_chunks.index(j)
            group = pieces[bounds[s_idx]:bounds[s_idx + 1]]
            sync([_tap_piece(u_ref, cw_ref, cb_ref, acc_ref, c, r0) for c, r0 in group])
            pieces_done = bounds[s_idx + 1]
            if pieces_done == len(pieces):
                _conv_tail(acc_ref, lng_ref, lnb_ref, wpw_ref, bpw_ref, gcp_ref, yc_ref)
        res = project(col)
        if col < COL_K:
            qkvg_ref[:, col:col + IN_CHUNK] = (res * Q_SCALE).astype(BF16)
        elif col < COL_G_ATTN:
            qkvg_ref[:, col:col + IN_CHUNK] = res.astype(BF16)
        elif col < D_QKVG:
            qkvg_ref[:, col:col + IN_CHUNK] = _silu(res).astype(BF16)
        elif col < COL_GLU_B:
            glu_a = res
        elif col < COL_G_CONV:
            u = glu_a * jax.nn.sigmoid(res)
            t0 = (col - COL_GLU_B) // LANES
            for t in range(t0, t0 + IN_CHUNK // LANES):
                assert pieces_done >= (t + 1) * (tm // TAP_ROWS)
                u_ref[t, 0:CONV_HALO, :] = u_ref[t, tm:tm + CONV_HALO, :]
                u_ref[t, CONV_HALO:CONV_HALO + tm, :] = u[:, (t - t0) * LANES:(t - t0 + 1) * LANES]
        else:
            gc_ref[:, col - COL_G_CONV:col - COL_G_CONV + IN_CHUNK] = _silu(res)


def _inproj_conv_call(x2d, mod, w_in, conv_w, conv_b, ln_g, ln_b, w_pw, b_pw):
    tm = IN_TM
    n = SEQ // tm
    cur = lambda i: jnp.minimum(i, n - 1)
    lag = lambda i: jnp.maximum(i - 1, 0)
    row = lambda i: (0, 0)
    return pl.pallas_call(
        _inproj_conv_kernel,
        grid=(n + 1,),
        in_specs=[
            pl.BlockSpec((tm, D_MODEL), lambda i: (cur(i), 0)),
            pl.BlockSpec((1, D_MODEL), lambda i: (0, 0)),
            pl.BlockSpec((1, D_MODEL), lambda i: (0, 1)),
            pl.BlockSpec(memory_space=pl.ANY),
            pl.BlockSpec((DEPTH, CONV_WIDTH, D_CONV), lambda i: (0, 0, 0)),
            pl.BlockSpec((1, D_CONV), row),
            pl.BlockSpec((1, D_CONV), row),
            pl.BlockSpec((1, D_CONV), row),
            pl.BlockSpec(memory_space=pl.ANY),
            pl.BlockSpec((1, D_CONV), row),
        ],
        out_specs=[
            pl.BlockSpec((tm, D_QKVG), lambda i: (cur(i), 0)),
            pl.BlockSpec((tm, D_CONV), lambda i: (lag(i), 0)),
        ],
        out_shape=[
            jax.ShapeDtypeStruct((SEQ, D_QKVG), BF16),
            jax.ShapeDtypeStruct((SEQ, D_CONV), BF16),
        ],
        scratch_shapes=[
            pltpu.VMEM((tm, D_MODEL), BF16),
            pltpu.VMEM((tm, D_MODEL), BF16),
            pltpu.VMEM((D_CONV // LANES, CONV_HALO + tm, LANES), F32),
            pltpu.VMEM((tm, D_CONV), F32),
            pltpu.VMEM((tm, D_CONV), F32),
            pltpu.VMEM((tm, D_CONV), F32),
            pltpu.VMEM((D_MODEL, D_IN), BF16),
            pltpu.VMEM((D_CONV, D_CONV), BF16),
            pltpu.VMEM((2, IN_STAGE_ROWS, D_IN), F32),
            pltpu.SemaphoreType.DMA((2,)),
        ],
        compiler_params=pltpu.CompilerParams(
            dimension_semantics=("arbitrary",), vmem_limit_bytes=VMEM_LIMIT),
        name="inproj_conv",
    )(x2d, mod, mod, w_in, conv_w, conv_b, ln_g, ln_b, w_pw, b_pw)


def kernel(x, c, w_ada, b_ada, w_in, rel_bias, sinks, conv_w, conv_b, conv_ln_g,
           conv_ln_b, w_pw, b_pw, w_out, ln_g, ln_b):
    assert x.shape == (1, SEQ, D_MODEL) and w_in.shape == (DEPTH, D_MODEL, D_IN)
    x2d = x.reshape(SEQ, D_MODEL)

    row = lambda v: v.reshape(1, -1)
    mod = _mod_call(row(c), w_ada.reshape(D_MODEL, 3 * D_MODEL), row(b_ada))
    bias = _bias_call(rel_bias)
    qkvg, y_conv = _inproj_conv_call(
        x2d, mod, w_in.reshape(D_MODEL, D_IN),
        conv_w, row(conv_b), row(conv_ln_g), row(conv_ln_b),
        w_pw.reshape(D_CONV, D_CONV), row(b_pw))
    out = _attn_out_call(qkvg, bias, sinks.reshape(N_Q_HEADS), y_conv, x2d, mod,
                         w_out.reshape(D_MODEL, D_MODEL), row(ln_g), row(ln_b))
    return out.reshape(1, SEQ, D_MODEL)
```

```python
import math

import jax
import jax.numpy as jnp
import numpy as np
from jax.experimental import pallas as pl
from jax.experimental.pallas import tpu as pltpu

F32 = jnp.float32
BF16 = jnp.bfloat16

D_MODEL = 2048
SEQ = 8192
HEAD_DIM = 64
D_ATTN = 1024
D_CONV = 1024
N_Q_HEADS = 16
N_KV_HEADS = 4
GQA_GROUP = 4
D_KV = 256
WINDOW = 128
BLOCK = 128
CONV_WIDTH = 31
N_BUCKETS = 32
MAX_DISTANCE = 128
LN_EPS = 1e-5
DEPTH = 1
ALPHA = (2.0 * DEPTH) ** 0.25
D_IN = 2 * D_ATTN + 2 * D_KV + 3 * D_CONV

LANES = 128
SUBLANES = 8

COL_Q = 0
COL_K = COL_Q + D_ATTN
COL_V = COL_K + D_KV
COL_G_ATTN = COL_V + D_KV
COL_GLU_A = COL_G_ATTN + D_ATTN
COL_GLU_B = COL_GLU_A + D_CONV
COL_G_CONV = COL_GLU_B + D_CONV
HALF = 512

VMEM_LIMIT = 56 * 1024 * 1024


def _silu(v):
    return v * jax.nn.sigmoid(v)


MOD_STAGE_ROWS = 256


def _mod_kernel(c_ref, w_hbm, b_ref, o_ref, ca_ref, stage_ref, sem_ref):
    n_out = o_ref.shape[1]
    n = D_MODEL // MOD_STAGE_ROWS

    def chunk_copy(j):
        slot = j % 2
        return pltpu.make_async_copy(
            w_hbm.at[pl.ds(j * MOD_STAGE_ROWS, MOD_STAGE_ROWS), :],
            stage_ref.at[slot], sem_ref.at[slot])

    chunk_copy(0).start()
    ca = _silu(c_ref[...])
    ca_ref[...] = jnp.transpose(jnp.broadcast_to(ca, (LANES, D_MODEL)))
    acc = jnp.zeros((SUBLANES, n_out), F32)
    for j in range(n):
        if j + 1 < n:
            chunk_copy(j + 1).start()
        chunk_copy(j).wait()
        rows = slice(j * MOD_STAGE_ROWS, (j + 1) * MOD_STAGE_ROWS)
        prod = jnp.tile(ca_ref[rows, :], (1, n_out // LANES)) * stage_ref[j % 2]
        for r in range(0, MOD_STAGE_ROWS, SUBLANES):
            acc = acc + prod[r:r + SUBLANES, :]
    o_ref[...] = jnp.sum(acc, axis=0, keepdims=True) + b_ref[...]


def _mod_call(c_row, w_ada, b_ada):
    n = w_ada.shape[1]
    return pl.pallas_call(
        _mod_kernel,
        in_specs=[
            pl.BlockSpec((1, D_MODEL), lambda: (0, 0)),
            pl.BlockSpec(memory_space=pl.ANY),
            pl.BlockSpec((1, n), lambda: (0, 0)),
        ],
        out_specs=pl.BlockSpec((1, n), lambda: (0, 0)),
        out_shape=jax.ShapeDtypeStruct((1, n), F32),
        scratch_shapes=[
            pltpu.VMEM((D_MODEL, LANES), F32),
            pltpu.VMEM((2, MOD_STAGE_ROWS, n), F32),
            pltpu.SemaphoreType.DMA((2,)),
        ],
        compiler_params=pltpu.CompilerParams(vmem_limit_bytes=VMEM_LIMIT),
        name="mod",
    )(c_row, w_ada, b_ada)


LOG2E = math.log2(math.e)
Q_SCALE = LOG2E * HEAD_DIM ** -0.5
MASKED = -1e30


def _lane_buckets():
    m = np.arange(2 * BLOCK)
    dist = BLOCK - m
    ok = (dist >= 0) & (dist < WINDOW)
    max_exact = N_BUCKETS // 2
    d = np.maximum(dist, 1).astype(np.float64)
    large = max_exact + (np.log(d / max_exact) / math.log(MAX_DISTANCE / max_exact)
                         * (N_BUCKETS - max_exact)).astype(np.int32)
    large = np.minimum(large, N_BUCKETS - 1)
    bucket = np.where(dist < max_exact, dist, large)
    return np.where(ok, bucket, -1).astype(np.int32).reshape(1, -1)


def _bias_kernel(rb_ref, bm_ref, o_ref):
    bm = bm_ref[...]
    col = jax.lax.broadcasted_iota(jnp.int32, (BLOCK, 2 * BLOCK), 1)
    for h in range(N_Q_HEADS):
        t = jnp.full(bm.shape, MASKED, F32)
        for b in range(N_BUCKETS):
            t = jnp.where(bm == b, rb_ref[b, h] * LOG2E, t)
        x = pltpu.roll(jnp.broadcast_to(t, (BLOCK, 2 * BLOCK)), 0, 1, stride=1, stride_axis=0)
        o_ref[1, h] = x
        o_ref[0, h] = jnp.where(col >= BLOCK, x, MASKED)


def _bias_call(rel_bias):
    shape = (2, N_Q_HEADS, BLOCK, 2 * BLOCK)
    return pl.pallas_call(
        _bias_kernel,
        in_specs=[
            pl.BlockSpec(memory_space=pltpu.SMEM),
            pl.BlockSpec((1, 2 * BLOCK), lambda: (0, 0)),
        ],
        out_specs=pl.BlockSpec(shape, lambda: (0, 0, 0, 0)),
        out_shape=jax.ShapeDtypeStruct(shape, F32),
        name="bias",
    )(rel_bias, jnp.asarray(_lane_buckets()))


def _load_weight_bf16(w_hbm, w_ref, stage_ref, sem_ref, col_scale=None):
    rows, cols = w_ref.shape
    chunk_rows = stage_ref.shape[1]
    n = rows // chunk_rows

    def chunk_copy(j):
        slot = j % 2
        return pltpu.make_async_copy(
            w_hbm.at[pl.ds(j * chunk_rows, chunk_rows), :],
            stage_ref.at[slot, :, pl.ds(0, cols)], sem_ref.at[slot])

    chunk_copy(0).start()
    for j in range(n):
        if j + 1 < n:
            chunk_copy(j + 1).start()
        chunk_copy(j).wait()
        chunk = stage_ref[j % 2, :, 0:cols]
        if col_scale is not None:
            chunk = chunk * col_scale
        w_ref[j * chunk_rows:(j + 1) * chunk_rows, :] = chunk.astype(BF16)


def _token(v):
    t = v[0:SUBLANES, 0:LANES]
    for r in range(SUBLANES, v.shape[0], SUBLANES):
        t = t + v[r:r + SUBLANES, 0:LANES]
    return t


def _zero_of(token):
    return (pltpu.bitcast(token, jnp.uint32) >> 16) >> 16


def _copy_after(src_ref, dst_ref, tokens):
    zero = _zero_of(tokens[0])
    for t in tokens[1:]:
        zero = zero | _zero_of(t)
    bits = pltpu.bitcast(src_ref[...], jnp.uint32)
    reps = (bits.shape[0] // zero.shape[0], bits.shape[1] // zero.shape[1])
    dst_ref[...] = pltpu.bitcast(bits | jnp.tile(zero, reps), dst_ref.dtype)


ATT_TM = 4 * BLOCK
ATT_ROWS = BLOCK
OUT_STAGE_ROWS = 512


def _dup_halves(t):
    swapped = jnp.concatenate([t[:, HEAD_DIM:], t[:, :HEAD_DIM]], axis=1)
    lane = jax.lax.broadcasted_iota(jnp.int32, t.shape, 1)
    low_half = lane < HEAD_DIM
    return jnp.where(low_half, t, swapped), jnp.where(low_half, swapped, t)


def _attn_group(q_ref, g0_ref, g1_ref, k_dup, v_dup, bias_ref, sinks_ref, r0, b0, kv, ya_ref,
                s_ref, p_ref, slot):
    nr = ATT_ROWS
    lane = jax.lax.broadcasted_iota(jnp.int32, (nr, LANES), 1)
    low_half = lane < HEAD_DIM
    rows = slice(r0, r0 + nr)
    tiles = [GQA_GROUP * kv // 2 + t for t in range(GQA_GROUP // 2)]
    stacked = []
    for tile in tiles:
        q_tile = q_ref[rows, tile * LANES:(tile + 1) * LANES]
        zero = jnp.zeros_like(q_tile)
        stacked += [jnp.where(low_half, q_tile, zero), jnp.where(low_half, zero, q_tile)]
    s_ref[slot] = jax.lax.dot_general(jnp.concatenate(stacked, axis=0), k_dup,
                                      (((1,), (1,)), ((), ())),
                                      preferred_element_type=F32)
    sums, maxes, sink_vals = [], [], []
    for r in range(GQA_GROUP):
        head = GQA_GROUP * kv + r
        hr = slice(r * nr, (r + 1) * nr)
        s = s_ref[slot, hr, :] + bias_ref[0, head, b0:b0 + nr, :]
        sink_vals.append(sinks_ref[head] * LOG2E)
        m = jnp.max(s, axis=1, keepdims=True)
        e = jnp.exp2(s - m)
        sums.append(jnp.sum(e, axis=1, keepdims=True))
        maxes.append(m)
        p_ref[slot, hr, :] = e.astype(BF16)
    denoms = []
    for t in range(GQA_GROUP // 2):
        pick = lambda a, b: jnp.where(low_half, a, b)
        denoms.append(pick(sums[2 * t], sums[2 * t + 1])
                      + jnp.exp2(pick(sink_vals[2 * t], sink_vals[2 * t + 1])
                                 - pick(maxes[2 * t], maxes[2 * t + 1])))
    o_all = jnp.dot(p_ref[slot], v_dup, preferred_element_type=F32)
    outs = [o_all[r * nr:(r + 1) * nr] for r in range(GQA_GROUP)]
    for t, tile in enumerate(tiles):
        y = jnp.where(low_half, outs[2 * t], outs[2 * t + 1]) / denoms[t]
        g_ref = g0_ref if tile < HALF // LANES else g1_ref
        gl = (tile * LANES) % HALF
        gate = g_ref[rows, gl:gl + LANES].astype(F32)
        ya_ref[rows, tile * LANES:(tile + 1) * LANES] = (y * gate).astype(BF16)


def _attn_tile(sinks_ref, q_ref, g0_ref, g1_ref, k_ref, v_ref, kp_ref, vp_ref,
               bias_refs, ya_ref, s_ref, p_ref):
    n_lt = D_KV // LANES
    lt = lambda t: slice(t * LANES, (t + 1) * LANES)

    def block_operands(b):
        r0 = b * BLOCK
        if b == 0:
            k_prev = [kp_ref[:, lt(t)] for t in range(n_lt)]
            v_prev = [vp_ref[:, lt(t)] for t in range(n_lt)]
        else:
            k_prev = [k_ref[r0 - BLOCK:r0, lt(t)] for t in range(n_lt)]
            v_prev = [v_ref[r0 - BLOCK:r0, lt(t)] for t in range(n_lt)]
        k_rows = [jnp.concatenate([k_prev[t], k_ref[r0:r0 + BLOCK, lt(t)]], axis=0)
                  for t in range(n_lt)]
        v_rows = [jnp.concatenate([v_prev[t], v_ref[r0:r0 + BLOCK, lt(t)]], axis=0)
                  for t in range(n_lt)]
        return ([_dup_halves(kt) for kt in k_rows], [_dup_halves(vt) for vt in v_rows])

    for b in range(ATT_TM // BLOCK):
        k_dups, v_dups = block_operands(b)
        for b0 in range(0, BLOCK, ATT_ROWS):
            for kv in range(N_KV_HEADS):
                _attn_group(q_ref, g0_ref, g1_ref, k_dups[kv // 2][kv % 2],
                            v_dups[kv // 2][kv % 2], bias_refs[b], sinks_ref,
                            b * BLOCK + b0, b0, kv, ya_ref, s_ref, p_ref, kv % 2)


def _attn_out_kernel(sinks_ref, q_ref, g0_ref, g1_ref, k_ref, v_ref, kp_ref, vp_ref,
                     bias0_ref, bias1_ref, yc_ref, x_ref, gate_ref, wout_hbm,
                     lng_ref, lnb_ref, o_ref, ya_ref, wout_ref, stage_ref, sem_ref,
                     s_ref, p_ref):
    i = pl.program_id(0)

    @pl.when(i == 0)
    def _():
        ya_ref[...] = jnp.zeros(ya_ref.shape, ya_ref.dtype)
        _load_weight_bf16(wout_hbm, wout_ref, stage_ref, sem_ref,
                          col_scale=gate_ref[...] * (1.0 / ALPHA))

    y = jnp.dot(jnp.concatenate([ya_ref[...], yc_ref[...]], axis=1), wout_ref[...],
                preferred_element_type=F32)
    z = x_ref[...] + y
    o_ref[...] = z
    mu = jnp.mean(z, axis=1, keepdims=True)
    d = o_ref[...] - mu
    var = jnp.mean(d * d, axis=1, keepdims=True)
    rstd = jax.lax.rsqrt(var + LN_EPS / ALPHA ** 2)
    o_ref[...] = (o_ref[...] - mu) * rstd * lng_ref[...] + lnb_ref[...]

    _attn_tile(sinks_ref, q_ref, g0_ref, g1_ref, k_ref, v_ref, kp_ref, vp_ref,
               [bias0_ref] + [bias1_ref] * (ATT_TM // BLOCK - 1), ya_ref, s_ref, p_ref)


def _attn_out_call(qkvg, bias, sinks, y_conv, x2d, mod, w_out, ln_g, ln_b):
    tm = ATT_TM
    n = SEQ // tm
    clamp = lambda t: jnp.clip(t, 0, n - 1)
    cur = lambda i: clamp(i)
    lag = lambda i: clamp(i - 1)
    kv_prev = lambda i: jnp.maximum(cur(i) * (tm // BLOCK) - 1, 0)
    row = lambda i: (0, 0)
    return pl.pallas_call(
        _attn_out_kernel,
        grid=(n + 1,),
        in_specs=[
            pl.BlockSpec(memory_space=pltpu.SMEM),
            pl.BlockSpec((tm, D_ATTN), lambda i: (cur(i), COL_Q // D_ATTN)),
            pl.BlockSpec((tm, HALF), lambda i: (cur(i), COL_G_ATTN // HALF)),
            pl.BlockSpec((tm, HALF), lambda i: (cur(i), COL_G_ATTN // HALF + 1)),
            pl.BlockSpec((tm, D_KV), lambda i: (cur(i), COL_K // D_KV)),
            pl.BlockSpec((tm, D_KV), lambda i: (cur(i), COL_V // D_KV)),
            pl.BlockSpec((BLOCK, D_KV), lambda i: (kv_prev(i), COL_K // D_KV)),
            pl.BlockSpec((BLOCK, D_KV), lambda i: (kv_prev(i), COL_V // D_KV)),
            pl.BlockSpec((1, N_Q_HEADS, BLOCK, 2 * BLOCK),
                         lambda i: (jnp.minimum(cur(i), 1), 0, 0, 0)),
            pl.BlockSpec((1, N_Q_HEADS, BLOCK, 2 * BLOCK), lambda i: (1, 0, 0, 0)),
            pl.BlockSpec((tm, D_CONV), lambda i: (lag(i), 0)),
            pl.BlockSpec((tm, D_MODEL), lambda i: (lag(i), 0)),
            pl.BlockSpec((1, D_MODEL), lambda i: (0, 2)),
            pl.BlockSpec(memory_space=pl.ANY),
            pl.BlockSpec((1, D_MODEL), row),
            pl.BlockSpec((1, D_MODEL), row),
        ],
        out_specs=pl.BlockSpec((tm, D_MODEL), lambda i: (lag(i), 0)),
        out_shape=jax.ShapeDtypeStruct((SEQ, D_MODEL), F32),
        scratch_shapes=[
            pltpu.VMEM((tm, D_ATTN), BF16),
            pltpu.VMEM((D_MODEL, D_MODEL), BF16),
            pltpu.VMEM((2, OUT_STAGE_ROWS, D_MODEL), F32),
            pltpu.SemaphoreType.DMA((2,)),
            pltpu.VMEM((2, GQA_GROUP * ATT_ROWS, 2 * BLOCK), F32),
            pltpu.VMEM((2, GQA_GROUP * ATT_ROWS, 2 * BLOCK), BF16),
        ],
        compiler_params=pltpu.CompilerParams(
            dimension_semantics=("arbitrary",), vmem_limit_bytes=VMEM_LIMIT),
        name="attn_out",
    )(sinks, qkvg, qkvg, qkvg, qkvg, qkvg, qkvg, qkvg, bias, bias, y_conv, x2d, mod,
      w_out, ln_g, ln_b)


IN_TM = 256
IN_CHUNK = 256
D_QKVG = COL_GLU_A
CONV_HALO = 32
IN_STAGE_ROWS = 256
TAP_ROWS = 32
SYNC_EVERY = 1
LAST_TAP_SYNC_CHUNK = 20


def _tap_piece(u_ref, cw_ref, cb_ref, acc_ref, c, r0):
    off = CONV_HALO - (CONV_WIDTH - 1)
    cs = slice(c * LANES, (c + 1) * LANES)
    acc = jnp.broadcast_to(cb_ref[:, cs], (TAP_ROWS, LANES))
    for k in range(CONV_WIDTH):
        acc = acc + cw_ref[0, k:k + 1, cs] * u_ref[c, r0 + off + k:r0 + off + k + TAP_ROWS, :]
    acc_ref[r0:r0 + TAP_ROWS, cs] = acc
    return _token(acc)


def _conv_tail(acc_ref, lng_ref, lnb_ref, wpw_ref, bpw_ref, gate_ref, yc_ref):
    v = acc_ref[...]
    mu = jnp.mean(v, axis=1, keepdims=True)
    d = v - mu
    var = jnp.mean(d * d, axis=1, keepdims=True)
    y = d * jax.lax.rsqrt(var + LN_EPS) * lng_ref[...] + lnb_ref[...]
    s = _silu(y).astype(BF16)
    z = jnp.dot(s, wpw_ref[...], preferred_element_type=F32) + bpw_ref[...]
    yc_ref[...] = (z * gate_ref[...]).astype(BF16)


def _inproj_conv_kernel(x_ref, shift_ref, scale_ref, w_hbm, cw_ref, cb_ref, lng_ref,
                        lnb_ref, wpw_hbm, bpw_ref, qkvg_ref, yc_ref,
                        h0_ref, h1_ref, u_ref, gc_ref, gcp_ref, acc_ref,
                        w_ref, wpw_ref, stage_ref, sem_ref):
    i = pl.program_id(0)
    tm = IN_TM
    pieces = [(c, r0) for c in range(D_CONV // LANES) for r0 in range(0, tm, TAP_ROWS)]

    @pl.when(i == 0)
    def _():
        u_ref[...] = jnp.zeros(u_ref.shape, u_ref.dtype)
        gc_ref[...] = jnp.zeros(gc_ref.shape, gc_ref.dtype)
        _load_weight_bf16(w_hbm, w_ref, stage_ref, sem_ref)
        _load_weight_bf16(wpw_hbm, wpw_ref, stage_ref, sem_ref)

    gcp_ref[...] = gc_ref[...]
    h = x_ref[...] * (1.0 + scale_ref[...]) + shift_ref[...]
    h0_ref[...] = h.astype(BF16)
    h_bufs = [h0_ref, h1_ref]
    state = {"cur": 0}

    def sync(tokens):
        src, dst = h_bufs[state["cur"]], h_bufs[1 - state["cur"]]
        _copy_after(src, dst, tokens)
        state["cur"] = 1 - state["cur"]

    def project(col):
        return jnp.dot(h_bufs[state["cur"]][...], w_ref[:, col:col + IN_CHUNK],
                       preferred_element_type=F32)

    cols = list(range(0, D_QKVG, IN_CHUNK)) + list(range(COL_G_CONV, D_IN, IN_CHUNK))
    for c in range(D_CONV // IN_CHUNK):
        cols += [COL_GLU_A + c * IN_CHUNK, COL_GLU_B + c * IN_CHUNK]
    sync_chunks = list(range(SYNC_EVERY, LAST_TAP_SYNC_CHUNK + 1, SYNC_EVERY))
    bounds = [round(s * len(pieces) / len(sync_chunks)) for s in range(len(sync_chunks) + 1)]
    glu_a = None
    pieces_done = 0
    for j, col in enumerate(cols):
        if j in sync_chunks:
            s_idx = sync_chunks.index(j)
            group = pieces[bounds[s_idx]:bounds[s_idx + 1]]
            sync([_tap_piece(u_ref, cw_ref, cb_ref, acc_ref, c, r0) for c, r0 in group])
            pieces_done = bounds[s_idx + 1]
            if pieces_done == len(pieces):
                _conv_tail(acc_ref, lng_ref, lnb_ref, wpw_ref, bpw_ref, gcp_ref, yc_ref)
        res = project(col)
        if col < COL_K:
            qkvg_ref[:, col:col + IN_CHUNK] = (res * Q_SCALE).astype(BF16)
        elif col < COL_G_ATTN:
            qkvg_ref[:, col:col + IN_CHUNK] = res.astype(BF16)
        elif col < D_QKVG:
            qkvg_ref[:, col:col + IN_CHUNK] = _silu(res).astype(BF16)
        elif col < COL_GLU_B:
            glu_a = res
        elif col < COL_G_CONV:
            u = glu_a * jax.nn.sigmoid(res)
            t0 = (col - COL_GLU_B) // LANES
            for t in range(t0, t0 + IN_CHUNK // LANES):
                assert pieces_done >= (t + 1) * (tm // TAP_ROWS)
                u_ref[t, 0:CONV_HALO, :] = u_ref[t, tm:tm + CONV_HALO, :]
                u_ref[t, CONV_HALO:CONV_HALO + tm, :] = u[:, (t - t0) * LANES:(t - t0 + 1) * LANES]
        else:
            gc_ref[:, col - COL_G_CONV:col - COL_G_CONV + IN_CHUNK] = _silu(res)


def _inproj_conv_call(x2d, mod, w_in, conv_w, conv_b, ln_g, ln_b, w_pw, b_pw):
    tm = IN_TM
    n = SEQ // tm
    cur = lambda i: jnp.minimum(i, n - 1)
    lag = lambda i: jnp.maximum(i - 1, 0)
    row = lambda i: (0, 0)
    return pl.pallas_call(
        _inproj_conv_kernel,
        grid=(n + 1,),
        in_specs=[
            pl.BlockSpec((tm, D_MODEL), lambda i: (cur(i), 0)),
            pl.BlockSpec((1, D_MODEL), lambda i: (0, 0)),
            pl.BlockSpec((1, D_MODEL), lambda i: (0, 1)),
            pl.BlockSpec(memory_space=pl.ANY),
            pl.BlockSpec((DEPTH, CONV_WIDTH, D_CONV), lambda i: (0, 0, 0)),
            pl.BlockSpec((1, D_CONV), row),
            pl.BlockSpec((1, D_CONV), row),
            pl.BlockSpec((1, D_CONV), row),
            pl.BlockSpec(memory_space=pl.ANY),
            pl.BlockSpec((1, D_CONV), row),
        ],
        out_specs=[
            pl.BlockSpec((tm, D_QKVG), lambda i: (cur(i), 0)),
            pl.BlockSpec((tm, D_CONV), lambda i: (lag(i), 0)),
        ],
        out_shape=[
            jax.ShapeDtypeStruct((SEQ, D_QKVG), BF16),
            jax.ShapeDtypeStruct((SEQ, D_CONV), BF16),
        ],
        scratch_shapes=[
            pltpu.VMEM((tm, D_MODEL), BF16),
            pltpu.VMEM((tm, D_MODEL), BF16),
            pltpu.VMEM((D_CONV // LANES, CONV_HALO + tm, LANES), F32),
            pltpu.VMEM((tm, D_CONV), F32),
            pltpu.VMEM((tm, D_CONV), F32),
            pltpu.VMEM((tm, D_CONV), F32),
            pltpu.VMEM((D_MODEL, D_IN), BF16),
            pltpu.VMEM((D_CONV, D_CONV), BF16),
            pltpu.VMEM((2, IN_STAGE_ROWS, D_IN), F32),
            pltpu.SemaphoreType.DMA((2,)),
        ],
        compiler_params=pltpu.CompilerParams(
            dimension_semantics=("arbitrary",), vmem_limit_bytes=VMEM_LIMIT),
        name="inproj_conv",
    )(x2d, mod, mod, w_in, conv_w, conv_b, ln_g, ln_b, w_pw, b_pw)


def kernel(x, c, w_ada, b_ada, w_in, rel_bias, sinks, conv_w, conv_b, conv_ln_g,
           conv_ln_b, w_pw, b_pw, w_out, ln_g, ln_b):
    assert x.shape == (1, SEQ, D_MODEL) and w_in.shape == (DEPTH, D_MODEL, D_IN)
    x2d = x.reshape(SEQ, D_MODEL)

    row = lambda v: v.reshape(1, -1)
    mod = _mod_call(row(c), w_ada.reshape(D_MODEL, 3 * D_MODEL), row(b_ada))
    bias = _bias_call(rel_bias)
    qkvg, y_conv = _inproj_conv_call(
        x2d, mod, w_in.reshape(D_MODEL, D_IN),
        conv_w, row(conv_b), row(conv_ln_g), row(conv_ln_b),
        w_pw.reshape(D_CONV, D_CONV), row(b_pw))
    out = _attn_out_call(qkvg, bias, sinks.reshape(N_Q_HEADS), y_conv, x2d, mod,
                         w_out.reshape(D_MODEL, D_MODEL), row(ln_g), row(ln_b))
    return out.reshape(1, SEQ, D_MODEL)
```

```python
import math

import jax
import jax.numpy as jnp
import numpy as np
from jax.experimental import pallas as pl
from jax.experimental.pallas import tpu as pltpu

F32 = jnp.float32
BF16 = jnp.bfloat16

D_MODEL = 2048
SEQ = 8192
HEAD_DIM = 64
D_ATTN = 1024
D_CONV = 1024
N_Q_HEADS = 16
N_KV_HEADS = 4
GQA_GROUP = 4
D_KV = 256
WINDOW = 128
BLOCK = 128
CONV_WIDTH = 31
N_BUCKETS = 32
MAX_DISTANCE = 128
LN_EPS = 1e-5
DEPTH = 1
ALPHA = (2.0 * DEPTH) ** 0.25
D_IN = 2 * D_ATTN + 2 * D_KV + 3 * D_CONV

LANES = 128
SUBLANES = 8

COL_Q = 0
COL_K = COL_Q + D_ATTN
COL_V = COL_K + D_KV
COL_G_ATTN = COL_V + D_KV
COL_GLU_A = COL_G_ATTN + D_ATTN
COL_GLU_B = COL_GLU_A + D_CONV
COL_G_CONV = COL_GLU_B + D_CONV
HALF = 512

VMEM_LIMIT = 56 * 1024 * 1024


def _silu(v):
    return v * jax.nn.sigmoid(v)


MOD_STAGE_ROWS = 256


def _mod_kernel(c_ref, w_hbm, b_ref, o_ref, ca_ref, stage_ref, sem_ref):
    n_out = o_ref.shape[1]
    n = D_MODEL // MOD_STAGE_ROWS

    def chunk_copy(j):
        slot = j % 2
        return pltpu.make_async_copy(
            w_hbm.at[pl.ds(j * MOD_STAGE_ROWS, MOD_STAGE_ROWS), :],
            stage_ref.at[slot], sem_ref.at[slot])

    chunk_copy(0).start()
    ca = _silu(c_ref[...])
    ca_ref[...] = jnp.transpose(jnp.broadcast_to(ca, (LANES, D_MODEL)))
    acc = jnp.zeros((SUBLANES, n_out), F32)
    for j in range(n):
        if j + 1 < n:
            chunk_copy(j + 1).start()
        chunk_copy(j).wait()
        rows = slice(j * MOD_STAGE_ROWS, (j + 1) * MOD_STAGE_ROWS)
        prod = jnp.tile(ca_ref[rows, :], (1, n_out // LANES)) * stage_ref[j % 2]
        for r in range(0, MOD_STAGE_ROWS, SUBLANES):
            acc = acc + prod[r:r + SUBLANES, :]
    o_ref[...] = jnp.sum(acc, axis=0, keepdims=True) + b_ref[...]


def _mod_call(c_row, w_ada, b_ada):
    n = w_ada.shape[1]
    return pl.pallas_call(
        _mod_kernel,
        in_specs=[
            pl.BlockSpec((1, D_MODEL), lambda: (0, 0)),
            pl.BlockSpec(memory_space=pl.ANY),
            pl.BlockSpec((1, n), lambda: (0, 0)),
        ],
        out_specs=pl.BlockSpec((1, n), lambda: (0, 0)),
        out_shape=jax.ShapeDtypeStruct((1, n), F32),
        scratch_shapes=[
            pltpu.VMEM((D_MODEL, LANES), F32),
            pltpu.VMEM((2, MOD_STAGE_ROWS, n), F32),
            pltpu.SemaphoreType.DMA((2,)),
        ],
        compiler_params=pltpu.CompilerParams(vmem_limit_bytes=VMEM_LIMIT),
        name="mod",
    )(c_row, w_ada, b_ada)


LOG2E = math.log2(math.e)
Q_SCALE = LOG2E * HEAD_DIM ** -0.5
MASKED = -1e30


def _lane_buckets():
    m = np.arange(2 * BLOCK)
    dist = BLOCK - m
    ok = (dist >= 0) & (dist < WINDOW)
    max_exact = N_BUCKETS // 2
    d = np.maximum(dist, 1).astype(np.float64)
    large = max_exact + (np.log(d / max_exact) / math.log(MAX_DISTANCE / max_exact)
                         * (N_BUCKETS - max_exact)).astype(np.int32)
    large = np.minimum(large, N_BUCKETS - 1)
    bucket = np.where(dist < max_exact, dist, large)
    return np.where(ok, bucket, -1).astype(np.int32).reshape(1, -1)


def _bias_kernel(rb_ref, bm_ref, o_ref):
    bm = bm_ref[...]
    col = jax.lax.broadcasted_iota(jnp.int32, (BLOCK, 2 * BLOCK), 1)
    for h in range(N_Q_HEADS):
        t = jnp.full(bm.shape, MASKED, F32)
        for b in range(N_BUCKETS):
            t = jnp.where(bm == b, rb_ref[b, h] * LOG2E, t)
        x = pltpu.roll(jnp.broadcast_to(t, (BLOCK, 2 * BLOCK)), 0, 1, stride=1, stride_axis=0)
        o_ref[1, h] = x
        o_ref[0, h] = jnp.where(col >= BLOCK, x, MASKED)


def _bias_call(rel_bias):
    shape = (2, N_Q_HEADS, BLOCK, 2 * BLOCK)
    return pl.pallas_call(
        _bias_kernel,
        in_specs=[
            pl.BlockSpec(memory_space=pltpu.SMEM),
            pl.BlockSpec((1, 2 * BLOCK), lambda: (0, 0)),
        ],
        out_specs=pl.BlockSpec(shape, lambda: (0, 0, 0, 0)),
        out_shape=jax.ShapeDtypeStruct(shape, F32),
        name="bias",
    )(rel_bias, jnp.asarray(_lane_buckets()))


def _load_weight_bf16(w_hbm, w_ref, stage_ref, sem_ref, col_scale=None):
    rows, cols = w_ref.shape
    chunk_rows = stage_ref.shape[1]
    n = rows // chunk_rows

    def chunk_copy(j):
        slot = j % 2
        return pltpu.make_async_copy(
            w_hbm.at[pl.ds(j * chunk_rows, chunk_rows), :],
            stage_ref.at[slot, :, pl.ds(0, cols)], sem_ref.at[slot])

    chunk_copy(0).start()
    for j in range(n):
        if j + 1 < n:
            chunk_copy(j + 1).start()
        chunk_copy(j).wait()
        chunk = stage_ref[j % 2, :, 0:cols]
        if col_scale is not None:
            chunk = chunk * col_scale
        w_ref[j * chunk_rows:(j + 1) * chunk_rows, :] = chunk.astype(BF16)


def _token(v):
    t = v[0:SUBLANES, 0:LANES]
    for r in range(SUBLANES, v.shape[0], SUBLANES):
        t = t + v[r:r + SUBLANES, 0:LANES]
    return t


def _zero_of(token):
    return (pltpu.bitcast(token, jnp.uint32) >> 16) >> 16


def _copy_after(src_ref, dst_ref, tokens):
    zero = _zero_of(tokens[0])
    for t in tokens[1:]:
        zero = zero | _zero_of(t)
    bits = pltpu.bitcast(src_ref[...], jnp.uint32)
    reps = (bits.shape[0] // zero.shape[0], bits.shape[1] // zero.shape[1])
    dst_ref[...] = pltpu.bitcast(bits | jnp.tile(zero, reps), dst_ref.dtype)


ATT_TM = 4 * BLOCK
ATT_ROWS = BLOCK
OUT_STAGE_ROWS = 256


def _dup_halves(t):
    swapped = jnp.concatenate([t[:, HEAD_DIM:], t[:, :HEAD_DIM]], axis=1)
    lane = jax.lax.broadcasted_iota(jnp.int32, t.shape, 1)
    low_half = lane < HEAD_DIM
    return jnp.where(low_half, t, swapped), jnp.where(low_half, swapped, t)


GROUP_LAG = 2


def _attn_group(q_ref, g0_ref, g1_ref, k_dup, v_dup, bias_ref, sinks_ref, r0, b0, kv, ya_ref,
                after=None):
    nr = ATT_ROWS
    lane = jax.lax.broadcasted_iota(jnp.int32, (nr, LANES), 1)
    low_half = lane < HEAD_DIM
    rows = slice(r0, r0 + nr)
    tiles = [GQA_GROUP * kv // 2 + t for t in range(GQA_GROUP // 2)]
    stacked = []
    for tile in tiles:
        q_tile = q_ref[rows, tile * LANES:(tile + 1) * LANES]
        if after is not None:
            bits = pltpu.bitcast(q_tile, jnp.uint32)
            zero_bits = jnp.tile(_zero_of(after), (bits.shape[0] // SUBLANES, 1))
            q_tile = pltpu.bitcast(bits | zero_bits, BF16)
        zero = jnp.zeros_like(q_tile)
        stacked += [jnp.where(low_half, q_tile, zero), jnp.where(low_half, zero, q_tile)]
    s_all = jax.lax.dot_general(jnp.concatenate(stacked, axis=0), k_dup,
                                (((1,), (1,)), ((), ())),
                                preferred_element_type=F32)
    probs, sums, maxes, sink_vals = [], [], [], []
    for r in range(GQA_GROUP):
        head = GQA_GROUP * kv + r
        s = s_all[r * nr:(r + 1) * nr] + bias_ref[0, head, b0:b0 + nr, :]
        sink_vals.append(sinks_ref[head] * LOG2E)
        m = jnp.max(s, axis=1, keepdims=True)
        e = jnp.exp2(s - m)
        sums.append(jnp.sum(e, axis=1, keepdims=True))
        maxes.append(m)
        probs.append(e.astype(BF16))
    denoms = []
    for t in range(GQA_GROUP // 2):
        pick = lambda a, b: jnp.where(low_half, a, b)
        denoms.append(pick(sums[2 * t], sums[2 * t + 1])
                      + jnp.exp2(pick(sink_vals[2 * t], sink_vals[2 * t + 1])
                                 - pick(maxes[2 * t], maxes[2 * t + 1])))
    o_all = jnp.dot(jnp.concatenate(probs, axis=0), v_dup,
                    preferred_element_type=F32)
    outs = [o_all[r * nr:(r + 1) * nr] for r in range(GQA_GROUP)]
    for t, tile in enumerate(tiles):
        y = jnp.where(low_half, outs[2 * t], outs[2 * t + 1]) / denoms[t]
        g_ref = g0_ref if tile < HALF // LANES else g1_ref
        gl = (tile * LANES) % HALF
        gate = g_ref[rows, gl:gl + LANES].astype(F32)
        ya_ref[rows, tile * LANES:(tile + 1) * LANES] = (y * gate).astype(BF16)
    return _token(outs[GQA_GROUP - 1])


def _attn_tile(sinks_ref, q_ref, g0_ref, g1_ref, k_ref, v_ref, kp_ref, vp_ref,
               bias_refs, ya_ref):
    n_lt = D_KV // LANES
    lt = lambda t: slice(t * LANES, (t + 1) * LANES)

    def block_operands(b):
        r0 = b * BLOCK
        if b == 0:
            k_prev = [kp_ref[:, lt(t)] for t in range(n_lt)]
            v_prev = [vp_ref[:, lt(t)] for t in range(n_lt)]
        else:
            k_prev = [k_ref[r0 - BLOCK:r0, lt(t)] for t in range(n_lt)]
            v_prev = [v_ref[r0 - BLOCK:r0, lt(t)] for t in range(n_lt)]
        k_rows = [jnp.concatenate([k_prev[t], k_ref[r0:r0 + BLOCK, lt(t)]], axis=0)
                  for t in range(n_lt)]
        v_rows = [jnp.concatenate([v_prev[t], v_ref[r0:r0 + BLOCK, lt(t)]], axis=0)
                  for t in range(n_lt)]
        return ([_dup_halves(kt) for kt in k_rows], [_dup_halves(vt) for vt in v_rows])

    tokens = []
    for b in range(ATT_TM // BLOCK):
        k_dups, v_dups = block_operands(b)
        for b0 in range(0, BLOCK, ATT_ROWS):
            for kv in range(N_KV_HEADS):
                after = tokens[-GROUP_LAG] if len(tokens) >= GROUP_LAG else None
                tokens.append(_attn_group(q_ref, g0_ref, g1_ref, k_dups[kv // 2][kv % 2],
                                          v_dups[kv // 2][kv % 2], bias_refs[b], sinks_ref,
                                          b * BLOCK + b0, b0, kv, ya_ref, after))


def _attn_out_kernel(sinks_ref, q_ref, g0_ref, g1_ref, k_ref, v_ref, kp_ref, vp_ref,
                     bias0_ref, bias1_ref, yc_ref, x_ref, gate_ref, wout_hbm,
                     lng_ref, lnb_ref, o_ref, ya_ref, wout_ref, stage_ref, sem_ref):
    i = pl.program_id(0)

    @pl.when(i == 0)
    def _():
        ya_ref[...] = jnp.zeros(ya_ref.shape, ya_ref.dtype)
        _load_weight_bf16(wout_hbm, wout_ref, stage_ref, sem_ref,
                          col_scale=gate_ref[...] * (1.0 / ALPHA))

    y = jnp.dot(jnp.concatenate([ya_ref[...], yc_ref[...]], axis=1), wout_ref[...],
                preferred_element_type=F32)
    z = x_ref[...] + y
    o_ref[...] = z
    mu = jnp.mean(z, axis=1, keepdims=True)
    d = o_ref[...] - mu
    var = jnp.mean(d * d, axis=1, keepdims=True)
    rstd = jax.lax.rsqrt(var + LN_EPS / ALPHA ** 2)
    o_ref[...] = (o_ref[...] - mu) * rstd * lng_ref[...] + lnb_ref[...]

    _attn_tile(sinks_ref, q_ref, g0_ref, g1_ref, k_ref, v_ref, kp_ref, vp_ref,
               [bias0_ref] + [bias1_ref] * (ATT_TM // BLOCK - 1), ya_ref)


def _attn_out_call(qkvg, bias, sinks, y_conv, x2d, mod, w_out, ln_g, ln_b):
    tm = ATT_TM
    n = SEQ // tm
    clamp = lambda t: jnp.clip(t, 0, n - 1)
    cur = lambda i: clamp(i)
    lag = lambda i: clamp(i - 1)
    kv_prev = lambda i: jnp.maximum(cur(i) * (tm // BLOCK) - 1, 0)
    row = lambda i: (0, 0)
    return pl.pallas_call(
        _attn_out_kernel,
        grid=(n + 1,),
        in_specs=[
            pl.BlockSpec(memory_space=pltpu.SMEM),
            pl.BlockSpec((tm, D_ATTN), lambda i: (cur(i), COL_Q // D_ATTN)),
            pl.BlockSpec((tm, HALF), lambda i: (cur(i), COL_G_ATTN // HALF)),
            pl.BlockSpec((tm, HALF), lambda i: (cur(i), COL_G_ATTN // HALF + 1)),
            pl.BlockSpec((tm, D_KV), lambda i: (cur(i), COL_K // D_KV)),
            pl.BlockSpec((tm, D_KV), lambda i: (cur(i), COL_V // D_KV)),
            pl.BlockSpec((BLOCK, D_KV), lambda i: (kv_prev(i), COL_K // D_KV)),
            pl.BlockSpec((BLOCK, D_KV), lambda i: (kv_prev(i), COL_V // D_KV)),
            pl.BlockSpec((1, N_Q_HEADS, BLOCK, 2 * BLOCK),
                         lambda i: (jnp.minimum(cur(i), 1), 0, 0, 0)),
            pl.BlockSpec((1, N_Q_HEADS, BLOCK, 2 * BLOCK), lambda i: (1, 0, 0, 0)),
            pl.BlockSpec((tm, D_CONV), lambda i: (lag(i), 0)),
            pl.BlockSpec((tm, D_MODEL), lambda i: (lag(i), 0)),
            pl.BlockSpec((1, D_MODEL), lambda i: (0, 2)),
            pl.BlockSpec(memory_space=pl.ANY),
            pl.BlockSpec((1, D_MODEL), row),
            pl.BlockSpec((1, D_MODEL), row),
        ],
        out_specs=pl.BlockSpec((tm, D_MODEL), lambda i: (lag(i), 0)),
        out_shape=jax.ShapeDtypeStruct((SEQ, D_MODEL), F32),
        scratch_shapes=[
            pltpu.VMEM((tm, D_ATTN), BF16),
            pltpu.VMEM((D_MODEL, D_MODEL), BF16),
            pltpu.VMEM((2, OUT_STAGE_ROWS, D_MODEL), F32),
            pltpu.SemaphoreType.DMA((2,)),
        ],
        compiler_params=pltpu.CompilerParams(
            dimension_semantics=("arbitrary",), vmem_limit_bytes=VMEM_LIMIT),
        name="attn_out",
    )(sinks, qkvg, qkvg, qkvg, qkvg, qkvg, qkvg, qkvg, bias, bias, y_conv, x2d, mod,
      w_out, ln_g, ln_b)


IN_TM = 256
IN_CHUNK = 256
D_QKVG = COL_GLU_A
CONV_HALO = 32
IN_STAGE_ROWS = 256
TAP_ROWS = 32
SYNC_EVERY = 1
LAST_TAP_SYNC_CHUNK = 20


def _tap_piece(u_ref, cw_ref, cb_ref, acc_ref, c, r0):
    off = CONV_HALO - (CONV_WIDTH - 1)
    cs = slice(c * LANES, (c + 1) * LANES)
    acc = jnp.broadcast_to(cb_ref[:, cs], (TAP_ROWS, LANES))
    for k in range(CONV_WIDTH):
        acc = acc + cw_ref[0, k:k + 1, cs] * u_ref[c, r0 + off + k:r0 + off + k + TAP_ROWS, :]
    acc_ref[r0:r0 + TAP_ROWS, cs] = acc
    return _token(acc)


def _conv_tail(acc_ref, lng_ref, lnb_ref, wpw_ref, bpw_ref, gate_ref, yc_ref):
    v = acc_ref[...]
    mu = jnp.mean(v, axis=1, keepdims=True)
    d = v - mu
    var = jnp.mean(d * d, axis=1, keepdims=True)
    y = d * jax.lax.rsqrt(var + LN_EPS) * lng_ref[...] + lnb_ref[...]
    s = _silu(y).astype(BF16)
    z = jnp.dot(s, wpw_ref[...], preferred_element_type=F32) + bpw_ref[...]
    yc_ref[...] = (z * gate_ref[...]).astype(BF16)


def _inproj_conv_kernel(x_ref, shift_ref, scale_ref, w_hbm, cw_ref, cb_ref, lng_ref,
                        lnb_ref, wpw_hbm, bpw_ref, qkvg_ref, yc_ref,
                        h0_ref, h1_ref, u_ref, gc_ref, gcp_ref, acc_ref,
                        w_ref, wpw_ref, stage_ref, sem_ref):
    i = pl.program_id(0)
    tm = IN_TM
    pieces = [(c, r0) for c in range(D_CONV // LANES) for r0 in range(0, tm, TAP_ROWS)]

    @pl.when(i == 0)
    def _():
        u_ref[...] = jnp.zeros(u_ref.shape, u_ref.dtype)
        gc_ref[...] = jnp.zeros(gc_ref.shape, gc_ref.dtype)
        _load_weight_bf16(w_hbm, w_ref, stage_ref, sem_ref)
        _load_weight_bf16(wpw_hbm, wpw_ref, stage_ref, sem_ref)

    gcp_ref[...] = gc_ref[...]
    h = x_ref[...] * (1.0 + scale_ref[...]) + shift_ref[...]
    h0_ref[...] = h.astype(BF16)
    h_bufs = [h0_ref, h1_ref]
    state = {"cur": 0}

    def sync(tokens):
        src, dst = h_bufs[state["cur"]], h_bufs[1 - state["cur"]]
        _copy_after(src, dst, tokens)
        state["cur"] = 1 - state["cur"]

    def project(col):
        return jnp.dot(h_bufs[state["cur"]][...], w_ref[:, col:col + IN_CHUNK],
                       preferred_element_type=F32)

    cols = list(range(0, D_QKVG, IN_CHUNK)) + list(range(COL_G_CONV, D_IN, IN_CHUNK))
    for c in range(D_CONV // IN_CHUNK):
        cols += [COL_GLU_A + c * IN_CHUNK, COL_GLU_B + c * IN_CHUNK]
    sync_chunks = list(range(SYNC_EVERY, LAST_TAP_SYNC_CHUNK + 1, SYNC_EVERY))
    bounds = [round(s * len(pieces) / len(sync_chunks)) for s in range(len(sync_chunks) + 1)]
    glu_a = None
    pieces_done = 0
    for j, col in enumerate(cols):
        if j in sync_chunks:
            s_idx = sync_chunks.index(j)
            group = pieces[bounds[s_idx]:bounds[s_idx + 1]]
            sync([_tap_piece(u_ref, cw_ref, cb_ref, acc_ref, c, r0) for c, r0 in group])
            pieces_done = bounds[s_idx + 1]
            if pieces_done == len(pieces):
                _conv_tail(acc_ref, lng_ref, lnb_ref, wpw_ref, bpw_ref, gcp_ref, yc_ref)
        res = project(col)
        if col < COL_K:
            qkvg_ref[:, col:col + IN_CHUNK] = (res * Q_SCALE).astype(BF16)
        elif col < COL_G_ATTN:
            qkvg_ref[:, col:col + IN_CHUNK] = res.astype(BF16)
        elif col < D_QKVG:
            qkvg_ref[:, col:col + IN_CHUNK] = _silu(res).astype(BF16)
        elif col < COL_GLU_B:
            glu_a = res
        elif col < COL_G_CONV:
            u = glu_a * jax.nn.sigmoid(res)
            t0 = (col - COL_GLU_B) // LANES
            for t in range(t0, t0 + IN_CHUNK // LANES):
                assert pieces_done >= (t + 1) * (tm // TAP_ROWS)
                u_ref[t, 0:CONV_HALO, :] = u_ref[t, tm:tm + CONV_HALO, :]
                u_ref[t, CONV_HALO:CONV_HALO + tm, :] = u[:, (t - t0) * LANES:(t - t0 + 1) * LANES]
        else:
            gc_ref[:, col - COL_G_CONV:col - COL_G_CONV + IN_CHUNK] = _silu(res)


def _inproj_conv_call(x2d, mod, w_in, conv_w, conv_b, ln_g, ln_b, w_pw, b_pw):
    tm = IN_TM
    n = SEQ // tm
    cur = lambda i: jnp.minimum(i, n - 1)
    lag = lambda i: jnp.maximum(i - 1, 0)
    row = lambda i: (0, 0)
    return pl.pallas_call(
        _inproj_conv_kernel,
        grid=(n + 1,),
        in_specs=[
            pl.BlockSpec((tm, D_MODEL), lambda i: (cur(i), 0)),
            pl.BlockSpec((1, D_MODEL), lambda i: (0, 0)),
            pl.BlockSpec((1, D_MODEL), lambda i: (0, 1)),
            pl.BlockSpec(memory_space=pl.ANY),
            pl.BlockSpec((DEPTH, CONV_WIDTH, D_CONV), lambda i: (0, 0, 0)),
            pl.BlockSpec((1, D_CONV), row),
            pl.BlockSpec((1, D_CONV), row),
            pl.BlockSpec((1, D_CONV), row),
            pl.BlockSpec(memory_space=pl.ANY),
            pl.BlockSpec((1, D_CONV), row),
        ],
        out_specs=[
            pl.BlockSpec((tm, D_QKVG), lambda i: (cur(i), 0)),
            pl.BlockSpec((tm, D_CONV), lambda i: (lag(i), 0)),
        ],
        out_shape=[
            jax.ShapeDtypeStruct((SEQ, D_QKVG), BF16),
            jax.ShapeDtypeStruct((SEQ, D_CONV), BF16),
        ],
        scratch_shapes=[
            pltpu.VMEM((tm, D_MODEL), BF16),
            pltpu.VMEM((tm, D_MODEL), BF16),
            pltpu.VMEM((D_CONV // LANES, CONV_HALO + tm, LANES), F32),
            pltpu.VMEM((tm, D_CONV), F32),
            pltpu.VMEM((tm, D_CONV), F32),
            pltpu.VMEM((tm, D_CONV), F32),
            pltpu.VMEM((D_MODEL, D_IN), BF16),
            pltpu.VMEM((D_CONV, D_CONV), BF16),
            pltpu.VMEM((2, IN_STAGE_ROWS, D_IN), F32),
            pltpu.SemaphoreType.DMA((2,)),
        ],
        compiler_params=pltpu.CompilerParams(
            dimension_semantics=("arbitrary",), vmem_limit_bytes=VMEM_LIMIT),
        name="inproj_conv",
    )(x2d, mod, mod, w_in, conv_w, conv_b, ln_g, ln_b, w_pw, b_pw)


def kernel(x, c, w_ada, b_ada, w_in, rel_bias, sinks, conv_w, conv_b, conv_ln_g,
           conv_ln_b, w_pw, b_pw, w_out, ln_g, ln_b):
    assert x.shape == (1, SEQ, D_MODEL) and w_in.shape == (DEPTH, D_MODEL, D_IN)
    x2d = x.reshape(SEQ, D_MODEL)

    row = lambda v: v.reshape(1, -1)
    mod = _mod_call(row(c), w_ada.reshape(D_MODEL, 3 * D_MODEL), row(b_ada))
    bias = _bias_call(rel_bias)
    qkvg, y_conv = _inproj_conv_call(
        x2d, mod, w_in.reshape(D_MODEL, D_IN),
        conv_w, row(conv_b), row(conv_ln_g), row(conv_ln_b),
        w_pw.reshape(D_CONV, D_CONV), row(b_pw))
    out = _attn_out_call(qkvg, bias, sinks.reshape(N_Q_HEADS), y_conv, x2d, mod,
                         w_out.reshape(D_MODEL, D_MODEL), row(ln_g), row(ln_b))
    return out.reshape(1, SEQ, D_MODEL)
```
